```python
import math
import jax, jax.numpy as jnp
from jax import lax
import numpy as np

D_MODEL = 2048
BATCH = 32
SEQ = 256
DEPTH = 4
DEC_BATCH = 8
DEC_SEQ = 2048
PAST_LEN = 512

GRID_W = 64
HEAD_DIM = 64
RWKV_W = 3 * D_MODEL // 8
RWKV_HEADS = RWKV_W // HEAD_DIM
HYENA_W = D_MODEL // 4
ATTN_W = 3 * D_MODEL // 8
ATTN_HEADS = ATTN_W // HEAD_DIM
ATTN_KV_HEADS = ATTN_HEADS // 3
GQA_GROUP = ATTN_HEADS // ATTN_KV_HEADS
KV_W = ATTN_KV_HEADS * HEAD_DIM
DECAY_LORA = 64
A_LORA = 64
GN_EPS = 64e-5
HYENA_ORDER = 2
HYENA_SHORT = 3
FILTER_EMB = 33
FILTER_HID = 64
HY_MIN_DECAY = math.log(1e-2) / 1.5
HY_MAX_DECAY = math.log(1e-2) / 0.3
WINDOW = 128
BLOCK = 128
ROPE_BASE = 10000.0
D_FF = 5504
FFN_CONV = 3
LN_EPS = 1e-5
DN_ALPHA = (2 * DEPTH) ** 0.25
DN_BETA = (8 * DEPTH) ** -0.25
IN_WIDTHS = (RWKV_W, RWKV_W, RWKV_W, RWKV_W, (HYENA_ORDER + 1) * HYENA_W, ATTN_W, KV_W, KV_W)
IN_W = sum(IN_WIDTHS)
IN_SPLITS = tuple(int(s) for s in np.cumsum(IN_WIDTHS)[:-1])

kernel_name = 'hybrid_rwkv_hyena_swa_diffusion_step'

F32 = jnp.float32


def layer_norm(x, g, b):
    xf = x.astype(F32)
    mu = jnp.mean(xf, -1, keepdims=True)
    var = jnp.mean(jnp.square(xf - mu), -1, keepdims=True)
    return ((xf - mu) * lax.rsqrt(var + LN_EPS)).astype(x.dtype) * g + b


def dwconv_centred(u, w):
    K = w.shape[0]
    pad = K // 2
    T = u.shape[1]
    up = jnp.pad(u, ((0, 0), (pad, pad), (0, 0)))
    return sum(up[:, j:j + T] * w[j] for j in range(K))


def rwkv_scan(r, decay, k, v, kk, a, init, reverse):
    def step(S, inp):
        r_t, w_t, k_t, v_t, kk_t, a_t = inp
        sa = jnp.einsum('bhvk,bhk->bhv', S, -kk_t)
        S = (S * w_t[:, :, None, :] + sa[..., None] * (kk_t * a_t)[:, :, None, :]
             + v_t[..., None] * k_t[:, :, None, :])
        return S, jnp.einsum('bhvk,bhk->bhv', S, r_t)
    xs = tuple(jnp.swapaxes(t, 0, 1) for t in (r, decay, k, v, kk, a))
    S, o = lax.scan(step, init.astype(F32), xs, reverse=reverse)
    return jnp.swapaxes(o, 0, 1), S


def rwkv_mixer(h, r, k, v, g, init, w0, w1, w2, a0, a1, a2, k_k, k_a, r_k, lnx_g, lnx_b):
    B, T, _ = h.shape
    heads = lambda t: t.reshape(B, T, RWKV_HEADS, HEAD_DIM).astype(F32)
    kk = heads(k * k_k)
    kk = kk / jnp.maximum(jnp.sqrt(jnp.sum(kk * kk, -1, keepdims=True)), 1e-12)
    rh, vh, kh = heads(r), heads(v), heads(k)
    k_a_h = k_a.reshape(RWKV_HEADS, HEAD_DIM).astype(F32)
    out = 0.0
    bonus = 0.0
    finals = []
    for d in range(2):
        w = -jax.nn.softplus(-(w0[d] + jnp.tanh(h @ w1[d]) @ w2[d])) - 0.5
        decay = jnp.exp(-jnp.exp(heads(w)))
        a = heads(jax.nn.sigmoid(a0[d] + (h @ a1[d]) @ a2[d]))
        kd = kh * (1.0 + (a - 1.0) * k_a_h)
        o, S = rwkv_scan(rh, decay, kd, vh, kk, a, init[d], reverse=(d == 1))
        out = out + o
        bonus = bonus + jnp.sum(rh * kd * r_k.astype(F32), -1, keepdims=True) * vh
        finals.append(S)
    mu = jnp.mean(out, -1, keepdims=True)
    var = jnp.mean(jnp.square(out - mu), -1, keepdims=True)
    on = ((out - mu) * lax.rsqrt(var + GN_EPS)).reshape(B, T, RWKV_W) * lnx_g + lnx_b
    y = (on + bonus.reshape(B, T, RWKV_W)) * jax.nn.sigmoid(g.astype(F32))
    return y.astype(h.dtype), jnp.stack(finals, 0)


def hyena_filters(L, f_w1, f_b1, f_freq1, f_w2, f_b2, f_freq2, f_w3):
    t = jnp.linspace(0.0, 1.0, L, dtype=F32)[:, None]
    bands = (FILTER_EMB - 1) // 2
    t_res = jnp.arange(L, dtype=F32)[:, None]
    f = jnp.linspace(1e-4, bands - 1, bands, dtype=F32)[None, :]
    ang = 2.0 * math.pi * t_res * f / L
    z = jnp.concatenate([t, jnp.cos(ang), jnp.sin(ang)], -1)
    hm = jnp.sin(f_freq1 * (z @ f_w1 + f_b1))
    hm = jnp.sin(f_freq2 * (hm @ f_w2 + f_b2))
    hf = (hm @ f_w3).astype(F32).reshape(L, HYENA_ORDER, 2, HYENA_W)
    deltas = jnp.abs(jnp.linspace(HY_MIN_DECAY, HY_MAX_DECAY, HYENA_W, dtype=F32))
    hf = hf * jnp.exp(-t * deltas)[:, None, None, :]
    causal = hf[:, :, 0]
    anti = hf[1:, :, 1][::-1]
    zero = jnp.zeros((1, HYENA_ORDER, HYENA_W), F32)
    return jnp.concatenate([causal, zero, anti], 0)


def fft_conv(u, kern, bias):
    L = u.shape[1]
    uf = jnp.fft.rfft(u.astype(F32), n=2 * L, axis=1)
    kf = jnp.fft.rfft(kern, n=2 * L, axis=0)
    y = jnp.fft.irfft(uf * kf[None], n=2 * L, axis=1)[:, :L]
    return (y + u.astype(F32) * bias).astype(u.dtype)


def hyena_mixer(u, short_w, f_w1, f_b1, f_freq1, f_w2, f_b2, f_freq2, f_w3, hy_bias):
    L = u.shape[1]
    u = dwconv_centred(u, short_w)
    streams = jnp.split(u, HYENA_ORDER + 1, axis=-1)
    kern = hyena_filters(L, f_w1, f_b1, f_freq1, f_w2, f_b2, f_freq2, f_w3)
    z = streams[0]
    for n in range(HYENA_ORDER):
        z = streams[n + 1] * fft_conv(z, kern[:, n], hy_bias[n])
    return z


def axial_rope(T):
    rows = T // GRID_W
    row = jnp.repeat(jnp.arange(rows), GRID_W).astype(F32)
    col = jnp.tile(jnp.arange(GRID_W), rows).astype(F32)
    nf = HEAD_DIM // 4
    inv = ROPE_BASE ** (-jnp.arange(nf, dtype=F32) / nf)
    ang = jnp.concatenate([row[:, None] * inv, col[:, None] * inv], -1)
    return jnp.cos(ang), jnp.sin(ang)


def apply_rope(x, cos, sin):
    half = HEAD_DIM // 2
    x1 = x[..., :half].astype(F32)
    x2 = x[..., half:].astype(F32)
    c = cos[None, :, None, :]
    s = sin[None, :, None, :]
    return jnp.concatenate([x1 * c - x2 * s, x1 * s + x2 * c], -1).astype(x.dtype)


def sink_softmax_av(s, v, sink):
    snk = sink.astype(F32).reshape(1, ATTN_KV_HEADS, GQA_GROUP, 1, 1)
    m = jnp.maximum(jnp.max(s, -1, keepdims=True), snk)
    p = jnp.exp(s - m)
    p = p / (jnp.sum(p, -1, keepdims=True) + jnp.exp(snk - m))
    return jnp.einsum('bhgqk,bkhd->bqhgd', p.astype(v.dtype), v)


def context_attention(q, k, v, sink):
    B, S = q.shape[:2]
    nb = S // BLOCK
    qb = jnp.swapaxes(q.reshape(B, nb, BLOCK, ATTN_KV_HEADS, GQA_GROUP, HEAD_DIM), 0, 1)
    scale = HEAD_DIM ** -0.5
    def block(qblk):
        s = jnp.einsum('bqhgd,bkhd->bhgqk', qblk, k).astype(F32) * scale
        return sink_softmax_av(s, v, sink)
    o = lax.map(block, qb)
    return jnp.swapaxes(o, 0, 1).reshape(B, S, ATTN_W)


def latent_attention(q, k, v, ctx_k, ctx_v, sink):
    B, T = q.shape[:2]
    nb = T // BLOCK
    qg = q.reshape(B, T, ATTN_KV_HEADS, GQA_GROUP, HEAD_DIM)
    kp = jnp.pad(k, ((0, 0), (BLOCK, BLOCK), (0, 0), (0, 0)))
    vp = jnp.pad(v, ((0, 0), (BLOCK, BLOCK), (0, 0), (0, 0)))
    ctx_k = ctx_k.astype(q.dtype)
    ctx_v = ctx_v.astype(v.dtype)
    scale = HEAD_DIM ** -0.5
    def block(i):
        start = i * BLOCK
        qb = lax.dynamic_slice_in_dim(qg, start, BLOCK, axis=1)
        kb = lax.dynamic_slice_in_dim(kp, start, 3 * BLOCK, axis=1)
        vb = lax.dynamic_slice_in_dim(vp, start, 3 * BLOCK, axis=1)
        qpos = start + jnp.arange(BLOCK)
        kpos = start - BLOCK + jnp.arange(3 * BLOCK)
        ok = ((jnp.abs(qpos[:, None] - kpos[None, :]) <= WINDOW)
              & (kpos >= 0)[None, :] & (kpos < T)[None, :])
        s_loc = jnp.where(ok, jnp.einsum('bqhgd,bkhd->bhgqk', qb, kb).astype(F32) * scale, -jnp.inf)
        s_ctx = jnp.einsum('bqhgd,bkhd->bhgqk', qb, ctx_k).astype(F32) * scale
        return sink_softmax_av(jnp.concatenate([s_loc, s_ctx], -1),
                               jnp.concatenate([vb, ctx_v], 1), sink)
    o = lax.map(block, jnp.arange(nb))
    return jnp.swapaxes(o, 0, 1).reshape(B, T, ATTN_W)


def conv_ffn(h, w_up, conv_w, w_down):
    u = dwconv_centred(h @ w_up, conv_w)
    a, b = jnp.split(u, 2, axis=-1)
    return (jax.nn.silu(a) * b) @ w_down


def trunk_layer(x, cond, p, rwkv_init, ctx_k=None, ctx_v=None):
    B, T, _ = x.shape
    mod = (jax.nn.silu(cond) @ p['w_mod'] + p['b_mod'])[:, None, :]
    shift1, scale1, gate1, shift2, scale2, gate2 = jnp.split(mod, 6, axis=-1)
    h = x * (1.0 + scale1) + shift1
    proj = h @ p['w_in']
    r, k, v, g, hy_u, q, ak, av = jnp.split(proj, IN_SPLITS, axis=-1)
    o_a, S = rwkv_mixer(h, r, k, v, g, rwkv_init, p['rwkv_w0'], p['rwkv_w1'], p['rwkv_w2'],
                        p['rwkv_a0'], p['rwkv_a1'], p['rwkv_a2'], p['rwkv_k_k'], p['rwkv_k_a'],
                        p['rwkv_r_k'], p['rwkv_lnx_g'], p['rwkv_lnx_b'])
    o_b = hyena_mixer(hy_u, p['hy_short_w'], p['hy_f_w1'], p['hy_f_b1'], p['hy_f_freq1'],
                      p['hy_f_w2'], p['hy_f_b2'], p['hy_f_freq2'], p['hy_f_w3'], p['hy_bias'])
    q = q.reshape(B, T, ATTN_HEADS, HEAD_DIM)
    ak = ak.reshape(B, T, ATTN_KV_HEADS, HEAD_DIM)
    av = av.reshape(B, T, ATTN_KV_HEADS, HEAD_DIM)
    if ctx_k is None:
        o_c = context_attention(q, ak, av, p['attn_sink'])
        ctx_out = (S, ak, av)
    else:
        cos, sin = axial_rope(T)
        o_c = latent_attention(apply_rope(q, cos, sin), apply_rope(ak, cos, sin), av,
                               ctx_k, ctx_v, p['attn_sink'])
        ctx_out = None
    mix = jnp.concatenate([o_a.astype(x.dtype), o_b.astype(x.dtype), o_c.astype(x.dtype)], -1) @ p['w_out']
    x = layer_norm(DN_ALPHA * x + gate1 * mix, p['ln1_g'], p['ln1_b'])
    h2 = x * (1.0 + scale2) + shift2
    f = conv_ffn(h2, p['ffn_w_up'], p['ffn_conv_w'], p['ffn_w_down'])
    x = layer_norm(DN_ALPHA * x + gate2 * f, p['ln2_g'], p['ln2_b'])
    return x, ctx_out


def setup_inputs(seed: int = 0) -> dict:
    key = jax.random.key(seed)
    ks = iter(jax.random.split(key, 64))
    def nrm(shape, s):
        return s * jax.random.normal(next(ks), shape, F32)
    L = DEPTH
    D = D_MODEL
    return {
        'x_prompt': nrm((BATCH, SEQ, D), 1.0),
        'x_sample': nrm((DEC_BATCH, DEC_SEQ, D), 1.0),
        'c': nrm((DEC_BATCH, D), 1.0),
        'state_rwkv': nrm((DEC_BATCH, DEPTH, 2, RWKV_HEADS, HEAD_DIM, HEAD_DIM), 0.5),
        'cache_k': nrm((DEC_BATCH, DEPTH, PAST_LEN, ATTN_KV_HEADS, HEAD_DIM), 1.0),
        'cache_v': nrm((DEC_BATCH, DEPTH, PAST_LEN, ATTN_KV_HEADS, HEAD_DIM), 1.0),
        'c_ctx': nrm((D,), 1.0),
        'w_mod': nrm((L, D, 6 * D), 0.5 * D ** -0.5),
        'b_mod': nrm((L, 6 * D), 0.02),
        'w_in': nrm((L, D, IN_W), D ** -0.5),
        'rwkv_w0': jax.random.uniform(next(ks), (L, 2, RWKV_W), F32, -6.0, -1.0),
        'rwkv_w1': nrm((L, 2, D, DECAY_LORA), D ** -0.5),
        'rwkv_w2': nrm((L, 2, DECAY_LORA, RWKV_W), 0.1 * DECAY_LORA ** -0.5),
        'rwkv_a0': nrm((L, 2, RWKV_W), 0.1),
        'rwkv_a1': nrm((L, 2, D, A_LORA), D ** -0.5),
        'rwkv_a2': nrm((L, 2, A_LORA, RWKV_W), 0.1 * A_LORA ** -0.5),
        'rwkv_k_k': 0.85 + nrm((L, RWKV_W), 0.02),
        'rwkv_k_a': 1.0 + nrm((L, RWKV_W), 0.02),
        'rwkv_r_k': nrm((L, RWKV_HEADS, HEAD_DIM), 0.1),
        'rwkv_lnx_g': 1.0 + nrm((L, RWKV_W), 0.02),
        'rwkv_lnx_b': nrm((L, RWKV_W), 0.02),
        'hy_short_w': nrm((L, HYENA_SHORT, (HYENA_ORDER + 1) * HYENA_W), HYENA_SHORT ** -0.5),
        'hy_f_w1': nrm((L, FILTER_EMB, FILTER_HID), FILTER_EMB ** -0.5),
        'hy_f_b1': nrm((L, FILTER_HID), 0.1),
        'hy_f_freq1': 1.0 + nrm((L, FILTER_HID), 0.02),
        'hy_f_w2': nrm((L, FILTER_HID, FILTER_HID), FILTER_HID ** -0.5),
        'hy_f_b2': nrm((L, FILTER_HID), 0.1),
        'hy_f_freq2': 1.0 + nrm((L, FILTER_HID), 0.02),
        'hy_f_w3': nrm((L, FILTER_HID, HYENA_ORDER * 2 * HYENA_W), 0.05 * FILTER_HID ** -0.5),
        'hy_bias': nrm((L, HYENA_ORDER, HYENA_W), 0.5),
        'attn_sink': nrm((L, ATTN_HEADS), 0.5),
        'w_out': nrm((L, D, D), DN_BETA * D ** -0.5),
        'ln1_g': 1.0 + nrm((L, D), 0.02),
        'ln1_b': nrm((L, D), 0.02),
        'ffn_w_up': nrm((L, D, 2 * D_FF), D ** -0.5),
        'ffn_conv_w': nrm((L, FFN_CONV, 2 * D_FF), FFN_CONV ** -0.5),
        'ffn_w_down': nrm((L, D_FF, D), DN_BETA * D_FF ** -0.5),
        'ln2_g': 1.0 + nrm((L, D), 0.02),
        'ln2_b': nrm((L, D), 0.02),
    }


def reference(x_prompt, x_sample, c, state_rwkv, cache_k, cache_v, c_ctx,
              w_mod, b_mod, w_in,
              rwkv_w0, rwkv_w1, rwkv_w2, rwkv_a0, rwkv_a1, rwkv_a2,
              rwkv_k_k, rwkv_k_a, rwkv_r_k, rwkv_lnx_g, rwkv_lnx_b,
              hy_short_w, hy_f_w1, hy_f_b1, hy_f_freq1, hy_f_w2, hy_f_b2, hy_f_freq2, hy_f_w3, hy_bias,
              attn_sink, w_out, ln1_g, ln1_b, ffn_w_up, ffn_conv_w, ffn_w_down, ln2_g, ln2_b):
    params = dict(w_mod=w_mod, b_mod=b_mod, w_in=w_in,
                  rwkv_w0=rwkv_w0, rwkv_w1=rwkv_w1, rwkv_w2=rwkv_w2,
                  rwkv_a0=rwkv_a0, rwkv_a1=rwkv_a1, rwkv_a2=rwkv_a2,
                  rwkv_k_k=rwkv_k_k, rwkv_k_a=rwkv_k_a, rwkv_r_k=rwkv_r_k,
                  rwkv_lnx_g=rwkv_lnx_g, rwkv_lnx_b=rwkv_lnx_b,
                  hy_short_w=hy_short_w, hy_f_w1=hy_f_w1, hy_f_b1=hy_f_b1, hy_f_freq1=hy_f_freq1,
                  hy_f_w2=hy_f_w2, hy_f_b2=hy_f_b2, hy_f_freq2=hy_f_freq2, hy_f_w3=hy_f_w3,
                  hy_bias=hy_bias, attn_sink=attn_sink, w_out=w_out, ln1_g=ln1_g, ln1_b=ln1_b,
                  ffn_w_up=ffn_w_up, ffn_conv_w=ffn_conv_w, ffn_w_down=ffn_w_down,
                  ln2_g=ln2_g, ln2_b=ln2_b)

    xc = x_prompt
    Bp = x_prompt.shape[0]
    zero_state = jnp.zeros((2, Bp, RWKV_HEADS, HEAD_DIM, HEAD_DIM), F32)
    states, keys, vals = [], [], []
    for l in range(DEPTH):
        p = {n: a[l] for n, a in params.items()}
        xc, (S, kc, vc) = trunk_layer(xc, c_ctx[None, :], p, zero_state)
        states.append(jnp.moveaxis(S, 0, 1))
        keys.append(kc)
        vals.append(vc)
    y_prompt = xc
    new_state_rwkv = jnp.stack(states, 1)
    new_cache_k = jnp.stack(keys, 1)
    new_cache_v = jnp.stack(vals, 1)

    xs = x_sample
    for l in range(DEPTH):
        p = {n: a[l] for n, a in params.items()}
        init = jnp.moveaxis(state_rwkv[:, l], 1, 0)
        xs, _ = trunk_layer(xs, c, p, init, cache_k[:, l], cache_v[:, l])
    y_sample = xs

    return (y_prompt, y_sample, new_state_rwkv, new_cache_k, new_cache_v)
```

```python
import functools
import math

import numpy as np
import jax
import jax.numpy as jnp
from jax import lax
from jax.experimental import pallas as pl
from jax.experimental.pallas import tpu as pltpu

F32 = jnp.float32
BF16 = jnp.bfloat16

D_MODEL = 2048
DEPTH = 4
GRID_W = 64
HEAD_DIM = 64
RWKV_W = 768
RWKV_HEADS = 12
HYENA_W = 512
ATTN_W = 768
ATTN_HEADS = 12
ATTN_KV_HEADS = 4
GQA_GROUP = 3
KV_W = 256
LORA = 64
GN_EPS = 64e-5
FILTER_EMB = 33
FILTER_HID = 64
HY_MIN_DECAY = math.log(1e-2) / 1.5
HY_MAX_DECAY = math.log(1e-2) / 0.3
WINDOW = 128
BLOCK = 128
ROPE_BASE = 10000.0
D_FF = 5504
LN_EPS = 1e-5
DN_ALPHA = (2 * DEPTH) ** 0.25

C_R, C_K, C_V, C_G = 0, 768, 1536, 2304
C_HY = 3072
C_Q, C_AK, C_AV = 4608, 5376, 5632
C_LORA = 5888
IN_W_EXT = 6144

MOD_BLK = 256
CHUNK = 64
LANES = 128
D_FF_PAD = 5632
VMEM_LIMIT = 56 * 1024 * 1024


def _dot(a, b):
    return jnp.dot(a.astype(BF16), b.astype(BF16), preferred_element_type=F32)


def _dot_nt(a, b):
    return lax.dot_general(a.astype(BF16), b.astype(BF16), (((1,), (1,)), ((), ())),
                           preferred_element_type=F32)


def _split(x):
    hi = x.astype(BF16)
    lo = (x - hi.astype(F32)).astype(BF16)
    return hi, lo


def _dot3_dims(a, b, dims):
    ah, al = _split(a)
    bh, bl = _split(b)
    dg = functools.partial(lax.dot_general, dimension_numbers=dims, preferred_element_type=F32)
    return dg(ah, bh) + (dg(ah, bl) + dg(al, bh))


_NN = (((1,), (0,)), ((), ()))
_NT = (((1,), (1,)), ((), ()))
_TN = (((0,), (0,)), ((), ()))


def _dot3(a, b):
    return _dot3_dims(a, b, _NN)


def _dot3_nt(a, b):
    return _dot3_dims(a, b, _NT)


def _dot3_tn(a, b):
    return _dot3_dims(a, b, _TN)


def _sigmoid(x):
    return 1.0 / (1.0 + jnp.exp(-x))


def _softplus(x):
    return jnp.maximum(x, 0.0) + jnp.log(1.0 + jnp.exp(-jnp.abs(x)))


def _mod_kernel(c_ref, w_ref, b_ref, o_ref):
    c = c_ref[...]
    s = c * _sigmoid(c)
    o_ref[...] = _dot(s, w_ref[...]) + b_ref[...]


def _modulation(cond, w_mod, b_mod):
    L = w_mod.shape[0]
    R = cond.shape[0]
    tn = 1024
    return pl.pallas_call(
        _mod_kernel,
        grid=(L, 6 * D_MODEL // tn),
        in_specs=[pl.BlockSpec((R, D_MODEL), lambda l, j: (0, 0)),
                  pl.BlockSpec((None, D_MODEL, tn), lambda l, j: (l, 0, j)),
                  pl.BlockSpec((None, 1, tn), lambda l, j: (l, 0, j))],
        out_specs=pl.BlockSpec((None, R, tn), lambda l, j: (l, 0, j)),
        out_shape=jax.ShapeDtypeStruct((L, R, 6 * D_MODEL), F32),
        compiler_params=pltpu.CompilerParams(dimension_semantics=("arbitrary", "arbitrary"),
                                             vmem_limit_bytes=VMEM_LIMIT),
        name="modulation",
    )(cond, w_mod, b_mod.reshape(L, 1, 6 * D_MODEL))


def _inproj_kernel(x_ref, mod_ref, w_ref, o_ref, h_scr, *, nsub):
    @pl.when(pl.program_id(1) == 0)
    def _():
        for s in range(nsub):
            rows = slice(s * MOD_BLK, (s + 1) * MOD_BLK)
            shift = mod_ref[s, 0:1, :]
            scale = mod_ref[s, 1:2, :]
            h_scr[rows, :] = (x_ref[rows, :] * (1.0 + scale) + shift).astype(BF16)

    o_ref[...] = jnp.dot(h_scr[...], w_ref[...], preferred_element_type=F32)


def _in_projection(x, mod, w_ext, tm=512, tn=512):
    N = x.shape[0]
    nsub = tm // MOD_BLK
    return pl.pallas_call(
        functools.partial(_inproj_kernel, nsub=nsub),
        grid=(N // tm, IN_W_EXT // tn),
        in_specs=[pl.BlockSpec((tm, D_MODEL), lambda i, j: (i, 0)),
                  pl.BlockSpec((nsub, 6, D_MODEL), lambda i, j: (i, 0, 0)),
                  pl.BlockSpec((D_MODEL, tn), lambda i, j: (0, j))],
        out_specs=pl.BlockSpec((tm, tn), lambda i, j: (i, j)),
        out_shape=jax.ShapeDtypeStruct((N, IN_W_EXT), F32),
        scratch_shapes=[pltpu.VMEM((tm, D_MODEL), BF16)],
        compiler_params=pltpu.CompilerParams(dimension_semantics=("arbitrary", "arbitrary"),
                                             vmem_limit_bytes=VMEM_LIMIT),
        name="in_projection",
    )(x, mod, w_ext)


def _rwkv_kernel(*refs, T, has_init):
    (r_ref, k_ref, v_ref, g_ref, lo_ref, w0_ref, w2_ref, a0_ref, a2_ref,
     kk_ref, ka_ref, rk_ref, lng_ref, lnb_ref) = refs[:14]
    pos = 14
    init_ref = None
    if has_init:
        init_ref = refs[pos]
        pos += 1
    y_ref, sfin_ref = refs[pos:pos + 2]
    rp_scr, ov_scr, m_scr, n_scr, o_scr, bonus_scr, s_scr = refs[pos + 2:]

    nC = T // CHUNK
    C = CHUNK
    H = HEAD_DIM
    row = lax.broadcasted_iota(jnp.int32, (C, C), 0)
    col = lax.broadcasted_iota(jnp.int32, (C, C), 1)
    eye = row == col
    lrow = lax.broadcasted_iota(jnp.int32, (LANES, LANES), 0) // H
    lcol = lax.broadcasted_iota(jnp.int32, (LANES, LANES), 1) // H
    head_ones = jnp.where(lrow == lcol, 1.0, 0.0).astype(F32)

    kkw = kk_ref[...]
    kaw = ka_ref[...]
    rkw = rk_ref[...]

    def phase_a(c, carry):
        rows = pl.ds(pl.multiple_of(c * C, C), C)
        r = r_ref[rows, :]
        k = k_ref[rows, :]
        v = v_ref[rows, :]
        lo = lo_ref[rows, :]
        kkraw = k * kkw
        ss = _dot3(kkraw * kkraw, head_ones)
        kkn = kkraw / jnp.maximum(jnp.sqrt(ss), 1e-12)
        bonus = jnp.zeros((C, LANES), F32)
        for d in range(2):
            strict = (col < row) if d == 0 else (col > row)
            incl = (col <= row) if d == 0 else (col >= row)
            hw = lo[:, d * LORA:(d + 1) * LORA]
            ha = lo[:, 2 * LORA + d * LORA:2 * LORA + (d + 1) * LORA]
            wl = w0_ref[d] + _dot(jnp.tanh(hw), w2_ref[d])
            ld = -jnp.exp(-_softplus(-wl) - 0.5)
            alpha = _sigmoid(a0_ref[d] + _dot(ha, a2_ref[d]))
            kd = k * (1.0 + (alpha - 1.0) * kaw)
            bb = kkn * alpha
            bonus = bonus + _dot3(r * kd * rkw, head_ones) * v
            g_incl = _dot3(jnp.where(incl, 1.0, 0.0).astype(F32), ld)
            g_excl = g_incl - ld
            g_end = g_incl[C - 1:C, :] if d == 0 else g_incl[0:1, :]
            rt = r * jnp.exp(g_incl)
            at = -kkn * jnp.exp(g_excl)
            ginv = jnp.exp(-g_incl)
            kt = kd * ginv
            bt = bb * ginv
            dec_end = jnp.exp(g_end - g_incl)
            kend = kd * dec_end
            bend = bb * dec_end
            gend = jnp.exp(g_end)
            for hh in range(2):
                sl = slice(hh * H, (hh + 1) * H)
                ar = jnp.concatenate([at[:, sl], rt[:, sl]], axis=0)
                bk = jnp.concatenate([bt[:, sl], kt[:, sl]], axis=0)
                a = _dot3_nt(ar, bk)
                a_ab = jnp.where(strict, a[:C, :C], 0.0)
                a_ak = jnp.where(strict, a[:C, C:], 0.0)
                a_rb = jnp.where(incl, a[C:, :C], 0.0)
                a_rk = jnp.where(incl, a[C:, C:], 0.0)
                vh = v[:, sl]
                rhs = jnp.concatenate([at[:, sl], _dot3(a_ak, vh)], axis=1)
                p = a_ab
                for it in range(6):
                    rhs = rhs + _dot3(p, rhs)
                    if it < 5:
                        p = _dot3(p, p)
                x = _dot3(a_rb, rhs)
                rp = rt[:, sl] + x[:, :H]
                ov = x[:, H:] + _dot3(a_rk, vh)
                m = jnp.where(eye, jnp.broadcast_to(gend[:, sl], (H, H)), 0.0) + _dot3_tn(rhs[:, :H], bend[:, sl])
                n = _dot3_tn(jnp.concatenate([rhs[:, H:], vh], axis=0),
                             jnp.concatenate([bend[:, sl], kend[:, sl]], axis=0))
                rp_scr[d, c, hh] = rp
                ov_scr[d, c, hh] = ov
                m_scr[d, c, hh] = m
                n_scr[d, c, hh] = n
        bonus_scr[rows, :] = bonus
        o_scr[rows, :] = jnp.zeros((C, LANES), F32)
        return carry

    lax.fori_loop(0, nC, phase_a, 0)

    for d in range(2):
        for hh in range(2):
            if has_init:
                s_scr[d, hh] = init_ref[d, hh]
            else:
                s_scr[d, hh] = jnp.zeros((H, H), F32)

    def phase_b(i, carry):
        for d in range(2):
            c = i if d == 0 else nC - 1 - i
            rows = pl.ds(pl.multiple_of(c * C, C), C)
            outs = []
            for hh in range(2):
                s = s_scr[d, hh]
                outs.append(_dot3_nt(rp_scr[d, c, hh], s) + ov_scr[d, c, hh])
                s_scr[d, hh] = _dot3(s, m_scr[d, c, hh]) + n_scr[d, c, hh]
            o_scr[rows, :] = o_scr[rows, :] + jnp.concatenate(outs, axis=1)
        return carry

    lax.fori_loop(0, nC, phase_b, 0)

    for d in range(2):
        for hh in range(2):
            sfin_ref[d, hh] = s_scr[d, hh]

    lng = lng_ref[...]
    lnb = lnb_ref[...]

    def phase_c(c, carry):
        rows = pl.ds(pl.multiple_of(c * C, C), C)
        o = o_scr[rows, :]
        mu = _dot3(o, head_ones) * (1.0 / H)
        dlt = o - mu
        var = _dot3(dlt * dlt, head_ones) * (1.0 / H)
        on = dlt * lax.rsqrt(var + GN_EPS) * lng + lnb
        y_ref[rows, :] = (on + bonus_scr[rows, :]) * _sigmoid(g_ref[rows, :])
        return carry

    lax.fori_loop(0, nC, phase_c, 0)


def _rwkv(proj, p, init, *, T, n_seq, row_blk0):
    nC = T // CHUNK
    npair = RWKV_HEADS // 2
    has_init = init is not None
    cb = lambda off: (lambda b, j: (b + row_blk0, off // LANES + j))
    vec = lambda b, j: (0, j)
    in_specs = [pl.BlockSpec((T, LANES), cb(C_R)),
                pl.BlockSpec((T, LANES), cb(C_K)),
                pl.BlockSpec((T, LANES), cb(C_V)),
                pl.BlockSpec((T, LANES), cb(C_G)),
                pl.BlockSpec((T, 4 * LORA), lambda b, j: (b + row_blk0, C_LORA // (4 * LORA))),
                pl.BlockSpec((2, 1, LANES), lambda b, j: (0, 0, j)),
                pl.BlockSpec((2, LORA, LANES), lambda b, j: (0, 0, j)),
                pl.BlockSpec((2, 1, LANES), lambda b, j: (0, 0, j)),
                pl.BlockSpec((2, LORA, LANES), lambda b, j: (0, 0, j)),
                pl.BlockSpec((1, LANES), vec), pl.BlockSpec((1, LANES), vec),
                pl.BlockSpec((1, LANES), vec), pl.BlockSpec((1, LANES), vec),
                pl.BlockSpec((1, LANES), vec)]
    args = [proj, proj, proj, proj, proj,
            p['rwkv_w0'].reshape(2, 1, RWKV_W), p['rwkv_w2'], p['rwkv_a0'].reshape(2, 1, RWKV_W), p['rwkv_a2'],
            p['rwkv_k_k'].reshape(1, RWKV_W), p['rwkv_k_a'].reshape(1, RWKV_W),
            p['rwkv_r_k'].reshape(1, RWKV_W), p['rwkv_lnx_g'].reshape(1, RWKV_W),
            p['rwkv_lnx_b'].reshape(1, RWKV_W)]
    st_spec = pl.BlockSpec((None, 2, 2, HEAD_DIM, HEAD_DIM), lambda b, j: (b, 0, j, 0, 0))
    if has_init:
        in_specs.append(st_spec)
        args.append(init)
    sc = lambda: pltpu.VMEM((2, nC, 2, CHUNK, HEAD_DIM), F32)
    return pl.pallas_call(
        functools.partial(_rwkv_kernel, T=T, has_init=has_init),
        grid=(n_seq, npair),
        in_specs=in_specs,
        out_specs=[pl.BlockSpec((T, LANES), lambda b, j: (b, j)), st_spec],
        out_shape=[jax.ShapeDtypeStruct((n_seq * T, RWKV_W), F32),
                   jax.ShapeDtypeStruct((n_seq, 2, RWKV_HEADS, HEAD_DIM, HEAD_DIM), F32)],
        scratch_shapes=[sc(), sc(), sc(), sc(),
                        pltpu.VMEM((T, LANES), F32), pltpu.VMEM((T, LANES), F32),
                        pltpu.VMEM((2, 2, HEAD_DIM, HEAD_DIM), F32)],
        compiler_params=pltpu.CompilerParams(dimension_semantics=("arbitrary", "arbitrary"),
                                             vmem_limit_bytes=VMEM_LIMIT),
        name="rwkv_T%d" % T,
    )(*args)


def _dft_mats(L):
    f = np.arange(L, dtype=np.int64)
    ph = (np.outer(f, f) % (2 * L)).astype(np.float64) * (math.pi / L)
    return np.cos(ph), -np.sin(ph)


def _filter_features(L):
    t = np.linspace(0.0, 1.0, L, dtype=np.float32)[:, None]
    bands = (FILTER_EMB - 1) // 2
    t_res = np.arange(L, dtype=np.float32)[:, None]
    f = np.linspace(1e-4, bands - 1, bands, dtype=np.float32)[None, :]
    ang = (2.0 * math.pi * t_res * f / L).astype(np.float32)
    z = np.concatenate([t, np.cos(ang), np.sin(ang)], -1).astype(np.float32)
    zp = np.zeros((L, FILTER_HID), np.float32)
    zp[:, :FILTER_EMB] = z
    deltas = np.abs(np.linspace(HY_MIN_DECAY, HY_MAX_DECAY, HYENA_W, dtype=np.float32))
    return zp, t.astype(np.float32), deltas[None, :].astype(np.float32)


def _hyfilter_kernel(z_ref, t_ref, dl_ref, w1_ref, b1_ref, q1_ref, w2_ref, b2_ref, q2_ref, wc_ref, wa_ref,
                     hs_ref, hd_ref, knyq_ref, *, L):
    hm = jnp.sin(q1_ref[...] * (_dot3(z_ref[...], w1_ref[...]) + b1_ref[...]))
    hm = jnp.sin(q2_ref[...] * (_dot3(hm, w2_ref[...]) + b2_ref[...]))
    win = jnp.exp(-t_ref[...] * dl_ref[...])
    lag = lax.broadcasted_iota(jnp.int32, (L, 1), 0)
    alt = jnp.where(lag % 2 == 0, 1.0, -1.0).astype(F32)
    hc = _dot3(hm, wc_ref[...]) * win
    ha = jnp.where(lag == 0, 0.0, _dot3(hm, wa_ref[...]) * win)
    hs_ref[...] = hc + ha
    hd_ref[...] = hc - ha
    knyq_ref[...] = jnp.sum((hc + ha) * alt, axis=0, keepdims=True) * (1.0 / (2 * L))


def _hyspec_kernel(cm_ref, sm_ref, hs_ref, hd_ref, kre_ref, kim_ref, *, L, tf):
    f = pl.program_id(2) * tf + lax.broadcasted_iota(jnp.int32, (tf, 1), 0)
    wf = jnp.where(f == 0, 1.0, 2.0).astype(F32) * (1.0 / (2 * L))
    kre_ref[...] = _dot3(cm_ref[...], hs_ref[...]) * wf
    kim_ref[...] = _dot3(sm_ref[...], hd_ref[...]) * wf


def _hyena_filter(p, L, cm, sm, cw=128):
    zp, t, dl = _filter_features(L)
    w1 = jnp.zeros((FILTER_HID, FILTER_HID), F32).at[:FILTER_EMB].set(p['hy_f_w1'])
    r2 = lambda a: a.reshape(1, -1)
    ncb = HYENA_W // cw
    full = lambda shape: pl.BlockSpec(shape, lambda n, j: (0,) * len(shape))
    shp = jax.ShapeDtypeStruct((2, L, HYENA_W), F32)
    hs, hd, knyq = pl.pallas_call(
        functools.partial(_hyfilter_kernel, L=L),
        grid=(2, ncb),
        in_specs=[full((L, FILTER_HID)), full((L, 1)),
                  pl.BlockSpec((1, cw), lambda n, j: (0, j)),
                  full((FILTER_HID, FILTER_HID)), full((1, FILTER_HID)), full((1, FILTER_HID)),
                  full((FILTER_HID, FILTER_HID)), full((1, FILTER_HID)), full((1, FILTER_HID)),
                  pl.BlockSpec((FILTER_HID, cw), lambda n, j: (0, n * 2 * ncb + j)),
                  pl.BlockSpec((FILTER_HID, cw), lambda n, j: (0, n * 2 * ncb + ncb + j))],
        out_specs=[pl.BlockSpec((None, L, cw), lambda n, j: (n, 0, j)),
                   pl.BlockSpec((None, L, cw), lambda n, j: (n, 0, j)),
                   pl.BlockSpec((None, 1, cw), lambda n, j: (n, 0, j))],
        out_shape=[shp, shp, jax.ShapeDtypeStruct((2, 1, HYENA_W), F32)],
        compiler_params=pltpu.CompilerParams(dimension_semantics=("arbitrary", "arbitrary"),
                                             vmem_limit_bytes=VMEM_LIMIT),
        name="hyena_filter_L%d" % L,
    )(jnp.asarray(zp), jnp.asarray(t), jnp.asarray(dl), w1, r2(p['hy_f_b1']), r2(p['hy_f_freq1']),
      p['hy_f_w2'], r2(p['hy_f_b2']), r2(p['hy_f_freq2']), p['hy_f_w3'], p['hy_f_w3'])
    tf = min(256, L)
    sw = 256
    kre, kim = pl.pallas_call(
        functools.partial(_hyspec_kernel, L=L, tf=tf),
        grid=(2, HYENA_W // sw, L // tf),
        in_specs=[pl.BlockSpec((tf, L), lambda n, j, i: (i, 0)),
                  pl.BlockSpec((tf, L), lambda n, j, i: (i, 0)),
                  pl.BlockSpec((None, L, sw), lambda n, j, i: (n, 0, j)),
                  pl.BlockSpec((None, L, sw), lambda n, j, i: (n, 0, j))],
        out_specs=[pl.BlockSpec((None, tf, sw), lambda n, j, i: (n, i, j)),
                   pl.BlockSpec((None, tf, sw), lambda n, j, i: (n, i, j))],
        out_shape=[shp, shp],
        compiler_params=pltpu.CompilerParams(dimension_semantics=("arbitrary",) * 3,
                                             vmem_limit_bytes=VMEM_LIMIT),
        name="hyena_spectrum_L%d" % L,
    )(cm, sm, hs, hd)
    return kre, kim, knyq


def _hyena_kernel(v_ref, x1_ref, x2_ref, sw_ref, bias_ref, cm_ref, sm_ref, kre_ref, kim_ref, knyq_ref,
                  o_ref, zb_scr, zf_scr, xs_scr, yre_scr, yim_scr, *, L, rb):
    nrb = L // rb
    t = lax.broadcasted_iota(jnp.int32, (L, 1), 0)
    first = t == 0
    last = t == L - 1
    alt_all = jnp.where(t % 2 == 0, 1.0, -1.0).astype(F32)
    tb = lax.broadcasted_iota(jnp.int32, (rb, 1), 0)
    alt_blk = jnp.where(tb % 2 == 0, 1.0, -1.0).astype(F32)
    sw = sw_ref[...]

    def short(ref, s):
        u = ref[...]
        prev = jnp.where(first, 0.0, pltpu.roll(u, 1, 0))
        nxt = jnp.where(last, 0.0, pltpu.roll(u, L - 1, 0))
        return prev * sw[0, s][None, :] + u * sw[1, s][None, :] + nxt * sw[2, s][None, :]

    z0 = short(v_ref, 0)
    zf_scr[...] = z0
    zb_scr[...] = z0.astype(BF16)
    unyq = jnp.sum(z0 * alt_all, axis=0, keepdims=True)

    for n in range(2):
        xs_scr[...] = short(x1_ref if n == 0 else x2_ref, n + 1)

        def fwd(fb, carry):
            rows = pl.ds(pl.multiple_of(fb * rb, rb), rb)
            zb = zb_scr[...]
            ure = jnp.dot(cm_ref[rows, :], zb, preferred_element_type=F32)
            uim = jnp.dot(sm_ref[rows, :], zb, preferred_element_type=F32)
            kre = kre_ref[n, rows, :]
            kim = kim_ref[n, rows, :]
            yre_scr[rows, :] = (ure * kre - uim * kim).astype(BF16)
            yim_scr[rows, :] = (ure * kim + uim * kre).astype(BF16)
            return carry

        lax.fori_loop(0, nrb, fwd, 0)
        ynyq = unyq * knyq_ref[n]
        bias = bias_ref[n]

        def inv(ib, acc):
            rows = pl.ds(pl.multiple_of(ib * rb, rb), rb)
            y = (jnp.dot(cm_ref[rows, :], yre_scr[...], preferred_element_type=F32)
                 + jnp.dot(sm_ref[rows, :], yim_scr[...], preferred_element_type=F32))
            y = y + alt_blk * ynyq + zf_scr[rows, :] * bias
            znew = xs_scr[rows, :] * y
            if n == 0:
                zf_scr[rows, :] = znew
                zb_scr[rows, :] = znew.astype(BF16)
                acc = acc + jnp.sum(znew * alt_blk, axis=0, keepdims=True)
            else:
                o_ref[rows, :] = znew
            return acc

        unyq = lax.fori_loop(0, nrb, inv, jnp.zeros_like(unyq))


def _hyena(proj, p, filt, cm, sm, *, L, n_seq, row_blk0, cw=256):
    kre, kim, knyq = filt
    ncb = HYENA_W // cw
    cb = lambda s: (lambda j, b: (b + row_blk0, (C_HY + s * HYENA_W) // cw + j))
    once = pl.Buffered(1)
    sw = p['hy_short_w'].reshape(3, 3, HYENA_W)
    rb = min(256, L)
    return pl.pallas_call(
        functools.partial(_hyena_kernel, L=L, rb=rb),
        grid=(ncb, n_seq),
        in_specs=[pl.BlockSpec((L, cw), cb(0)), pl.BlockSpec((L, cw), cb(1)), pl.BlockSpec((L, cw), cb(2)),
                  pl.BlockSpec((3, 3, cw), lambda j, b: (0, 0, j)),
                  pl.BlockSpec((2, 1, cw), lambda j, b: (0, 0, j)),
                  pl.BlockSpec((L, L), lambda j, b: (0, 0), pipeline_mode=once),
                  pl.BlockSpec((L, L), lambda j, b: (0, 0), pipeline_mode=once),
                  pl.BlockSpec((2, L, cw), lambda j, b: (0, 0, j), pipeline_mode=once),
                  pl.BlockSpec((2, L, cw), lambda j, b: (0, 0, j), pipeline_mode=once),
                  pl.BlockSpec((2, 1, cw), lambda j, b: (0, 0, j))],
        out_specs=pl.BlockSpec((L, cw), lambda j, b: (b, j)),
        out_shape=jax.ShapeDtypeStruct((n_seq * L, HYENA_W), F32),
        scratch_shapes=[pltpu.VMEM((L, cw), BF16), pltpu.VMEM((L, cw), F32), pltpu.VMEM((L, cw), F32),
                        pltpu.VMEM((L, cw), BF16), pltpu.VMEM((L, cw), BF16)],
        compiler_params=pltpu.CompilerParams(dimension_semantics=("arbitrary", "arbitrary"),
                                             vmem_limit_bytes=VMEM_LIMIT),
        name="hyena_L%d" % L,
    )(proj, proj, proj, sw, p['hy_bias'].reshape(2, 1, HYENA_W), cm.astype(BF16), sm.astype(BF16),
      kre, kim, knyq)


def _softmax_av(s, v, snk):
    m = jnp.maximum(jnp.max(s, axis=-1, keepdims=True), snk)
    pe = jnp.exp(s - m)
    den = jnp.sum(pe, axis=-1, keepdims=True) + jnp.exp(snk - m)
    return _dot(pe, v) / den


def _ctx_attn_kernel(q_ref, k_ref, v_ref, sink_ref, o_ref, *, S):
    scale = HEAD_DIM ** -0.5
    outs = []
    for h in range(ATTN_KV_HEADS):
        kh = k_ref[:, h * HEAD_DIM:(h + 1) * HEAD_DIM]
        vh = v_ref[:, h * HEAD_DIM:(h + 1) * HEAD_DIM]
        for g in range(GQA_GROUP):
            a = h * GQA_GROUP + g
            qa = q_ref[:, a * HEAD_DIM:(a + 1) * HEAD_DIM]
            s = _dot_nt(qa, kh) * scale
            snk = jnp.broadcast_to(sink_ref[0:1, a:a + 1], (S, 1))
            outs.append(_softmax_av(s, vh, snk))
    o_ref[...] = jnp.concatenate(outs, axis=1)


def _ctx_attention(proj, sink, *, S, n_seq):
    return pl.pallas_call(
        functools.partial(_ctx_attn_kernel, S=S),
        grid=(n_seq,),
        in_specs=[pl.BlockSpec((S, ATTN_W), lambda b: (b, C_Q // ATTN_W)),
                  pl.BlockSpec((S, KV_W), lambda b: (b, C_AK // KV_W)),
                  pl.BlockSpec((S, KV_W), lambda b: (b, C_AV // KV_W)),
                  pl.BlockSpec((1, LANES), lambda b: (0, 0))],
        out_specs=pl.BlockSpec((S, ATTN_W), lambda b: (b, 0)),
        out_shape=jax.ShapeDtypeStruct((n_seq * S, ATTN_W), F32),
        compiler_params=pltpu.CompilerParams(dimension_semantics=("arbitrary",),
                                             vmem_limit_bytes=VMEM_LIMIT),
        name="ctx_attention",
    )(proj, proj, proj, sink)


def _rope(x, cos, sin_signed, first_half):
    n = x.shape[-1]
    partner = jnp.where(first_half, pltpu.roll(x, n - HEAD_DIM // 2, 1), pltpu.roll(x, HEAD_DIM // 2, 1))
    return x * cos + partner * sin_signed


def _lat_attn_kernel(q_ref, kp_ref, kc_ref, kn_ref, vp_ref, vc_ref, vn_ref, ck_ref, cv_ref,
                     cq_ref, sq_ref, cp_ref, sp_ref, cc_ref, sc_ref, cn_ref, sn_ref, sink_ref, o_ref, *, T):
    i = pl.program_id(1)
    scale = HEAD_DIM ** -0.5
    fh_q = (lax.broadcasted_iota(jnp.int32, (1, ATTN_W), 1) % HEAD_DIM) < HEAD_DIM // 2
    fh_k = (lax.broadcasted_iota(jnp.int32, (1, KV_W), 1) % HEAD_DIM) < HEAD_DIM // 2
    q = _rope(q_ref[...], cq_ref[...], sq_ref[...], fh_q)
    kl = jnp.concatenate([_rope(kp_ref[...], cp_ref[...], sp_ref[...], fh_k),
                          _rope(kc_ref[...], cc_ref[...], sc_ref[...], fh_k),
                          _rope(kn_ref[...], cn_ref[...], sn_ref[...], fh_k)], axis=0)
    vl = jnp.concatenate([vp_ref[...], vc_ref[...], vn_ref[...]], axis=0)
    qpos = i * BLOCK + lax.broadcasted_iota(jnp.int32, (BLOCK, 3 * BLOCK), 0)
    kpos = (i - 1) * BLOCK + lax.broadcasted_iota(jnp.int32, (BLOCK, 3 * BLOCK), 1)
    ok = (jnp.abs(qpos - kpos) <= WINDOW) & (kpos >= 0) & (kpos < T)
    outs = []
    for h in range(ATTN_KV_HEADS):
        hs = slice(h * HEAD_DIM, (h + 1) * HEAD_DIM)
        klh, vlh = kl[:, hs], vl[:, hs]
        kch, vch = ck_ref[:, hs], cv_ref[:, hs]
        for g in range(GQA_GROUP):
            a = h * GQA_GROUP + g
            qa = q[:, a * HEAD_DIM:(a + 1) * HEAD_DIM]
            s_loc = jnp.where(ok, _dot_nt(qa, klh) * scale, -jnp.inf)
            s_ctx = _dot_nt(qa, kch) * scale
            snk = jnp.broadcast_to(sink_ref[0:1, a:a + 1], (BLOCK, 1))
            m = jnp.maximum(jnp.maximum(jnp.max(s_loc, axis=-1, keepdims=True),
                                        jnp.max(s_ctx, axis=-1, keepdims=True)), snk)
            p_loc = jnp.exp(s_loc - m)
            p_ctx = jnp.exp(s_ctx - m)
            den = (jnp.sum(p_loc, axis=-1, keepdims=True) + jnp.sum(p_ctx, axis=-1, keepdims=True)
                   + jnp.exp(snk - m))
            outs.append((_dot(p_loc, vlh) + _dot(p_ctx, vch)) / den)
    o_ref[...] = jnp.concatenate(outs, axis=1)


def _rope_tables(T):
    rows = T // GRID_W
    row = np.repeat(np.arange(rows), GRID_W).astype(np.float32)
    col = np.tile(np.arange(GRID_W), rows).astype(np.float32)
    nf = HEAD_DIM // 4
    inv = (ROPE_BASE ** (-np.arange(nf, dtype=np.float32) / nf)).astype(np.float32)
    ang = np.concatenate([row[:, None] * inv, col[:, None] * inv], -1).astype(np.float32)
    cos = np.cos(ang).astype(np.float32)
    sin = np.sin(ang).astype(np.float32)
    cos_h = np.concatenate([cos, cos], -1)
    sin_h = np.concatenate([-sin, sin], -1)
    return cos_h, sin_h


def _lat_attention(proj, cache_k, cache_v, sink, *, T, n_seq, row_blk0):
    nb = T // BLOCK
    P = cache_k.shape[1]
    cos_h, sin_h = _rope_tables(T)
    cq = jnp.asarray(np.tile(cos_h, (1, ATTN_HEADS)))
    sq = jnp.asarray(np.tile(sin_h, (1, ATTN_HEADS)))
    ck = jnp.asarray(np.tile(cos_h, (1, ATTN_KV_HEADS)))
    sk = jnp.asarray(np.tile(sin_h, (1, ATTN_KV_HEADS)))
    r0 = row_blk0 * nb
    prv = lambda i: jnp.maximum(i - 1, 0)
    nxt = lambda i: jnp.minimum(i + 1, nb - 1)
    kcol, vcol = C_AK // KV_W, C_AV // KV_W
    kv = lambda col, f: pl.BlockSpec((BLOCK, KV_W), lambda b, i: (r0 + b * nb + f(i), col))
    tab = lambda w, f: pl.BlockSpec((BLOCK, w), lambda b, i: (f(i), 0))
    cur = lambda i: i
    return pl.pallas_call(
        functools.partial(_lat_attn_kernel, T=T),
        grid=(n_seq, nb),
        in_specs=[pl.BlockSpec((BLOCK, ATTN_W), lambda b, i: (r0 + b * nb + i, C_Q // ATTN_W)),
                  kv(kcol, prv), kv(kcol, cur), kv(kcol, nxt),
                  kv(vcol, prv), kv(vcol, cur), kv(vcol, nxt),
                  pl.BlockSpec((None, P, KV_W), lambda b, i: (b, 0, 0)),
                  pl.BlockSpec((None, P, KV_W), lambda b, i: (b, 0, 0)),
                  tab(ATTN_W, cur), tab(ATTN_W, cur),
                  tab(KV_W, prv), tab(KV_W, prv), tab(KV_W, cur), tab(KV_W, cur), tab(KV_W, nxt), tab(KV_W, nxt),
                  pl.BlockSpec((1, LANES), lambda b, i: (0, 0))],
        out_specs=pl.BlockSpec((BLOCK, ATTN_W), lambda b, i: (b * nb + i, 0)),
        out_shape=jax.ShapeDtypeStruct((n_seq * T, ATTN_W), F32),
        compiler_params=pltpu.CompilerParams(dimension_semantics=("arbitrary", "arbitrary"),
                                             vmem_limit_bytes=VMEM_LIMIT),
        name="latent_attention",
    )(proj, proj, proj, proj, proj, proj, proj, cache_k, cache_v, cq, sq, ck, sk, ck, sk, ck, sk, sink)


def _layer_norm(y, g, b):
    mu = jnp.mean(y, axis=-1, keepdims=True)
    d = y - mu
    var = jnp.mean(d * d, axis=-1, keepdims=True)
    return d * lax.rsqrt(var + LN_EPS) * g + b


def _outproj_kernel(a_ref, b_ref, c_ref, x_ref, mod_ref, wa_ref, wb_ref, wc_ref, g_ref, be_ref,
                    x1_ref, h2_ref, *, nsub):
    m = (jnp.dot(a_ref[...].astype(BF16), wa_ref[...], preferred_element_type=F32)
         + jnp.dot(b_ref[...].astype(BF16), wb_ref[...], preferred_element_type=F32)
         + jnp.dot(c_ref[...].astype(BF16), wc_ref[...], preferred_element_type=F32))
    for s in range(nsub):
        rows = slice(s * MOD_BLK, (s + 1) * MOD_BLK)
        gate1 = mod_ref[s, 2:3, :]
        shift2 = mod_ref[s, 3:4, :]
        scale2 = mod_ref[s, 4:5, :]
        x1 = _layer_norm(DN_ALPHA * x_ref[rows, :] + gate1 * m[rows, :], g_ref[...], be_ref[...])
        x1_ref[rows, :] = x1
        h2_ref[rows, :] = (x1 * (1.0 + scale2) + shift2).astype(BF16)


def _out_projection(oa, ob, oc, x, mod, w_out, g, b, tm=256):
    N = x.shape[0]
    nsub = tm // MOD_BLK
    row = lambda w: pl.BlockSpec((tm, w), lambda i: (i, 0))
    full = lambda r: pl.BlockSpec((r, D_MODEL), lambda i: (0, 0))
    wa, wb, wc = w_out[:RWKV_W], w_out[RWKV_W:RWKV_W + HYENA_W], w_out[RWKV_W + HYENA_W:]
    return pl.pallas_call(
        functools.partial(_outproj_kernel, nsub=nsub),
        grid=(N // tm,),
        in_specs=[row(RWKV_W), row(HYENA_W), row(ATTN_W), row(D_MODEL),
                  pl.BlockSpec((nsub, 6, D_MODEL), lambda i: (i, 0, 0)),
                  full(RWKV_W), full(HYENA_W), full(ATTN_W), full(1), full(1)],
        out_specs=[row(D_MODEL), row(D_MODEL)],
        out_shape=[jax.ShapeDtypeStruct((N, D_MODEL), F32), jax.ShapeDtypeStruct((N, D_MODEL), BF16)],
        compiler_params=pltpu.CompilerParams(dimension_semantics=("arbitrary",),
                                             vmem_limit_bytes=VMEM_LIMIT),
        name="out_projection",
    )(oa, ob, oc, x, mod, wa, wb, wc, g.reshape(1, D_MODEL), b.reshape(1, D_MODEL))


HALO = 16


def _ffn_kernel(hp_ref, h_ref, hn_ref, x1_ref, mod_ref, wu_ref, cw_ref, wd_ref, g_ref, be_ref,
                o_ref, hx_scr, u_scr, acc_scr, *, tm, tf, nsub, n_ctx, s_ctx, s_lat):
    i = pl.program_id(0)
    j = pl.program_id(1)

    @pl.when(j == 0)
    def _():
        hx_scr[0:HALO, :] = hp_ref[...]
        hx_scr[HALO:HALO + tm, :] = h_ref[...]
        hx_scr[HALO + tm:, :] = hn_ref[...]
        acc_scr[...] = jnp.zeros_like(acc_scr)

    u_scr[...] = jnp.dot(hx_scr[...], wu_ref[...], preferred_element_type=F32)
    grow = i * tm + lax.broadcasted_iota(jnp.int32, (tm, 1), 0)
    pos = jnp.where(grow < n_ctx, grow % s_ctx, (grow - n_ctx) % s_lat)
    slen = jnp.where(grow < n_ctx, s_ctx, s_lat)
    mprev = jnp.where(pos == 0, 0.0, 1.0).astype(F32)
    mnext = jnp.where(pos == slen - 1, 0.0, 1.0).astype(F32)
    up = u_scr[HALO - 1:HALO - 1 + tm, :] * mprev
    uc = u_scr[HALO:HALO + tm, :]
    un = u_scr[HALO + 1:HALO + 1 + tm, :] * mnext
    cv = up * cw_ref[0:1, :] + uc * cw_ref[1:2, :] + un * cw_ref[2:3, :]
    a = cv[:, :tf]
    b = cv[:, tf:]
    act = (a * _sigmoid(a) * b).astype(BF16)
    acc_scr[...] += jnp.dot(act, wd_ref[...], preferred_element_type=F32)

    @pl.when(j == pl.num_programs(1) - 1)
    def _():
        for s in range(nsub):
            rows = slice(s * MOD_BLK, (s + 1) * MOD_BLK)
            gate2 = mod_ref[s, 5:6, :]
            o_ref[rows, :] = _layer_norm(DN_ALPHA * x1_ref[rows, :] + gate2 * acc_scr[rows, :],
                                         g_ref[...], be_ref[...])


def _ffn(h2, x1, mod, wu, cw, wd, g, b, *, n_ctx, s_ctx, s_lat, tm=512, tf=256):
    N = x1.shape[0]
    nsub = tm // MOD_BLK
    nh = tm // HALO
    last_h = N // HALO - 1
    return pl.pallas_call(
        functools.partial(_ffn_kernel, tm=tm, tf=tf, nsub=nsub, n_ctx=n_ctx, s_ctx=s_ctx, s_lat=s_lat),
        grid=(N // tm, D_FF_PAD // tf),
        in_specs=[pl.BlockSpec((HALO, D_MODEL), lambda i, j: (jnp.maximum(i * nh - 1, 0), 0)),
                  pl.BlockSpec((tm, D_MODEL), lambda i, j: (i, 0)),
                  pl.BlockSpec((HALO, D_MODEL), lambda i, j: (jnp.minimum((i + 1) * nh, last_h), 0)),
                  pl.BlockSpec((tm, D_MODEL), lambda i, j: (i, 0)),
                  pl.BlockSpec((nsub, 6, D_MODEL), lambda i, j: (i, 0, 0)),
                  pl.BlockSpec((D_MODEL, 2 * tf), lambda i, j: (0, j)),
                  pl.BlockSpec((3, 2 * tf), lambda i, j: (0, j)),
                  pl.BlockSpec((tf, D_MODEL), lambda i, j: (j, 0)),
                  pl.BlockSpec((1, D_MODEL), lambda i, j: (0, 0)),
                  pl.BlockSpec((1, D_MODEL), lambda i, j: (0, 0))],
        out_specs=pl.BlockSpec((tm, D_MODEL), lambda i, j: (i, 0)),
        out_shape=jax.ShapeDtypeStruct((N, D_MODEL), F32),
        scratch_shapes=[pltpu.VMEM((tm + 2 * HALO, D_MODEL), BF16),
                        pltpu.VMEM((tm + 2 * HALO, 2 * tf), F32),
                        pltpu.VMEM((tm, D_MODEL), F32)],
        compiler_params=pltpu.CompilerParams(dimension_semantics=("arbitrary", "arbitrary"),
                                             vmem_limit_bytes=VMEM_LIMIT),
        name="conv_ffn",
    )(h2, h2, h2, x1, mod, wu, cw, wd, g.reshape(1, D_MODEL), b.reshape(1, D_MODEL))


def _prep_ffn_weights(w_up, conv_w, w_down, tf=256):
    pad = D_FF_PAD - D_FF
    nb = D_FF_PAD // tf

    def inter(w):
        R = w.shape[0]
        w = jnp.pad(w.reshape(R, 2, D_FF), ((0, 0), (0, 0), (0, pad)))
        return w.reshape(R, 2, nb, tf).transpose(0, 2, 1, 3).reshape(R, 2 * D_FF_PAD)

    wu = inter(w_up).astype(BF16)
    cw = inter(conv_w)
    wd = jnp.pad(w_down, ((0, pad), (0, 0))).astype(BF16)
    return wu, cw, wd


def kernel(x_prompt, x_sample, c, state_rwkv, cache_k, cache_v, c_ctx, w_mod, b_mod, w_in, rwkv_w0, rwkv_w1, rwkv_w2, rwkv_a0, rwkv_a1, rwkv_a2, rwkv_k_k, rwkv_k_a, rwkv_r_k, rwkv_lnx_g, rwkv_lnx_b, hy_short_w, hy_f_w1, hy_f_b1, hy_f_freq1, hy_f_w2, hy_f_b2, hy_f_freq2, hy_f_w3, hy_bias, attn_sink, w_out, ln1_g, ln1_b, ffn_w_up, ffn_conv_w, ffn_w_down, ln2_g, ln2_b):
    params = dict(rwkv_w0=rwkv_w0, rwkv_w2=rwkv_w2, rwkv_a0=rwkv_a0, rwkv_a2=rwkv_a2,
                  rwkv_k_k=rwkv_k_k, rwkv_k_a=rwkv_k_a, rwkv_r_k=rwkv_r_k,
                  rwkv_lnx_g=rwkv_lnx_g, rwkv_lnx_b=rwkv_lnx_b,
                  hy_short_w=hy_short_w, hy_f_w1=hy_f_w1, hy_f_b1=hy_f_b1, hy_f_freq1=hy_f_freq1,
                  hy_f_w2=hy_f_w2, hy_f_b2=hy_f_b2, hy_f_freq2=hy_f_freq2, hy_f_w3=hy_f_w3,
                  hy_bias=hy_bias)
    L = w_in.shape[0]
    Bc, Sc, D = x_prompt.shape
    Bl, Tl, _ = x_sample.shape
    P = cache_k.shape[2]
    n_ctx = Bc * Sc
    n_lat = Bl * Tl
    N = n_ctx + n_lat
    assert Sc % MOD_BLK == 0 and Tl % MOD_BLK == 0 and n_ctx % Tl == 0

    n_cond = 1 + Bl
    cond = jnp.zeros((16, D), F32).at[0].set(c_ctx).at[1:n_cond].set(c)
    mod_all = _modulation(cond, w_mod, b_mod)
    mod_ctx = jnp.broadcast_to(mod_all[:, 0:1, :], (L, n_ctx // MOD_BLK, 6 * D))
    mod_lat = jnp.broadcast_to(mod_all[:, 1:n_cond, None, :], (L, Bl, Tl // MOD_BLK, 6 * D))
    mod_blk = jnp.concatenate([mod_ctx, mod_lat.reshape(L, n_lat // MOD_BLK, 6 * D)], axis=1)
    mod_blk = mod_blk.reshape(L, N // MOD_BLK, 6, D)

    cm_c, sm_c = (jnp.asarray(m, F32) for m in _dft_mats(Sc))
    cm_l, sm_l = (jnp.asarray(m, F32) for m in _dft_mats(Tl))
    sink_pad = jnp.zeros((L, 1, LANES), F32).at[:, 0, :ATTN_HEADS].set(attn_sink)

    x = jnp.concatenate([x_prompt.reshape(n_ctx, D), x_sample.reshape(n_lat, D)], axis=0)
    states, keys, vals = [], [], []
    for l in range(L):
        p = {n: a[l] for n, a in params.items()}
        w_ext = jnp.concatenate([w_in[l], rwkv_w1[l, 0], rwkv_w1[l, 1], rwkv_a1[l, 0], rwkv_a1[l, 1]],
                                axis=1).astype(BF16)
        proj = _in_projection(x, mod_blk[l], w_ext)
        keys.append(proj[:n_ctx, C_AK:C_AK + KV_W].reshape(Bc, Sc, ATTN_KV_HEADS, HEAD_DIM))
        vals.append(proj[:n_ctx, C_AV:C_AV + KV_W].reshape(Bc, Sc, ATTN_KV_HEADS, HEAD_DIM))

        ya_c, s_fin = _rwkv(proj, p, None, T=Sc, n_seq=Bc, row_blk0=0)
        ya_l, _ = _rwkv(proj, p, state_rwkv[:, l], T=Tl, n_seq=Bl, row_blk0=n_ctx // Tl)
        states.append(s_fin)

        yb_c = _hyena(proj, p, _hyena_filter(p, Sc, cm_c, sm_c), cm_c, sm_c, L=Sc, n_seq=Bc, row_blk0=0)
        yb_l = _hyena(proj, p, _hyena_filter(p, Tl, cm_l, sm_l), cm_l, sm_l, L=Tl, n_seq=Bl,
                      row_blk0=n_ctx // Tl)

        yc_c = _ctx_attention(proj, sink_pad[l], S=Sc, n_seq=Bc)
        yc_l = _lat_attention(proj, cache_k[:, l].reshape(Bl, P, KV_W), cache_v[:, l].reshape(Bl, P, KV_W),
                              sink_pad[l], T=Tl, n_seq=Bl, row_blk0=n_ctx // Tl)

        oa = jnp.concatenate([ya_c, ya_l], axis=0)
        ob = jnp.concatenate([yb_c, yb_l], axis=0)
        oc = jnp.concatenate([yc_c, yc_l], axis=0)
        x1, h2 = _out_projection(oa, ob, oc, x, mod_blk[l], w_out[l].astype(BF16), ln1_g[l], ln1_b[l])
        wu, cw, wd = _prep_ffn_weights(ffn_w_up[l], ffn_conv_w[l], ffn_w_down[l])
        x = _ffn(h2, x1, mod_blk[l], wu, cw, wd, ln2_g[l], ln2_b[l], n_ctx=n_ctx, s_ctx=Sc, s_lat=Tl)

    y_prompt = x[:n_ctx].reshape(Bc, Sc, D)
    y_sample = x[n_ctx:].reshape(Bl, Tl, D)
    new_state = jnp.stack(states, 1)
    new_k = jnp.stack(keys, 1)
    new_v = jnp.stack(vals, 1)
    return (y_prompt, y_sample, new_state, new_k, new_v)
```

```python
import functools
import math

import numpy as np
import jax
import jax.numpy as jnp
from jax import lax
from jax.experimental import pallas as pl
from jax.experimental.pallas import tpu as pltpu

F32 = jnp.float32
BF16 = jnp.bfloat16

D_MODEL = 2048
DEPTH = 4
GRID_W = 64
HEAD_DIM = 64
RWKV_W = 768
RWKV_HEADS = 12
HYENA_W = 512
ATTN_W = 768
ATTN_HEADS = 12
ATTN_KV_HEADS = 4
GQA_GROUP = 3
KV_W = 256
LORA = 64
GN_EPS = 64e-5
FILTER_EMB = 33
FILTER_HID = 64
HY_MIN_DECAY = math.log(1e-2) / 1.5
HY_MAX_DECAY = math.log(1e-2) / 0.3
WINDOW = 128
BLOCK = 128
ROPE_BASE = 10000.0
D_FF = 5504
LN_EPS = 1e-5
DN_ALPHA = (2 * DEPTH) ** 0.25

C_R, C_K, C_V, C_G = 0, 768, 1536, 2304
C_HY = 3072
C_Q, C_AK, C_AV = 4608, 5376, 5632
C_LORA = 5888
IN_W_EXT = 6144

MOD_BLK = 256
CHUNK = 64
RWKV_CB = 4
LANES = 128
D_FF_PAD = 5632
VMEM_LIMIT = 56 * 1024 * 1024


def _dot(a, b):
    return jnp.dot(a.astype(BF16), b.astype(BF16), preferred_element_type=F32)


def _dot_nt(a, b):
    return lax.dot_general(a.astype(BF16), b.astype(BF16), (((1,), (1,)), ((), ())),
                           preferred_element_type=F32)


def _split(x):
    hi = x.astype(BF16)
    lo = (x - hi.astype(F32)).astype(BF16)
    return hi, lo


def _dot3_dims(a, b, dims):
    ah, al = _split(a)
    bh, bl = _split(b)
    dg = functools.partial(lax.dot_general, dimension_numbers=dims, preferred_element_type=F32)
    return dg(ah, bh) + (dg(ah, bl) + dg(al, bh))


_NN = (((1,), (0,)), ((), ()))
_NT = (((1,), (1,)), ((), ()))
_TN = (((0,), (0,)), ((), ()))


def _dot3(a, b):
    return _dot3_dims(a, b, _NN)


def _dot3_nt(a, b):
    return _dot3_dims(a, b, _NT)


def _dot3_tn(a, b):
    return _dot3_dims(a, b, _TN)


def _sigmoid(x):
    return 1.0 / (1.0 + jnp.exp(-x))


def _softplus(x):
    return jnp.maximum(x, 0.0) + jnp.log(1.0 + jnp.exp(-jnp.abs(x)))


def _mod_kernel(c_ref, w_ref, b_ref, o_ref):
    c = c_ref[...]
    s = c * _sigmoid(c)
    o_ref[...] = _dot(s, w_ref[...]) + b_ref[...]


def _modulation(cond, w_mod, b_mod):
    L = w_mod.shape[0]
    R = cond.shape[0]
    tn = 1024
    return pl.pallas_call(
        _mod_kernel,
        grid=(L, 6 * D_MODEL // tn),
        in_specs=[pl.BlockSpec((R, D_MODEL), lambda l, j: (0, 0)),
                  pl.BlockSpec((None, D_MODEL, tn), lambda l, j: (l, 0, j)),
                  pl.BlockSpec((None, 1, tn), lambda l, j: (l, 0, j))],
        out_specs=pl.BlockSpec((None, R, tn), lambda l, j: (l, 0, j)),
        out_shape=jax.ShapeDtypeStruct((L, R, 6 * D_MODEL), F32),
        compiler_params=pltpu.CompilerParams(dimension_semantics=("arbitrary", "arbitrary"),
                                             vmem_limit_bytes=VMEM_LIMIT),
        name="modulation",
    )(cond, w_mod, b_mod.reshape(L, 1, 6 * D_MODEL))


def _inproj_kernel(x_ref, mod_ref, w_ref, o_ref, h_scr, *, nsub):
    @pl.when(pl.program_id(1) == 0)
    def _():
        for s in range(nsub):
            rows = slice(s * MOD_BLK, (s + 1) * MOD_BLK)
            shift = mod_ref[s, 0:1, :]
            scale = mod_ref[s, 1:2, :]
            h_scr[rows, :] = (x_ref[rows, :] * (1.0 + scale) + shift).astype(BF16)

    o_ref[...] = jnp.dot(h_scr[...], w_ref[...], preferred_element_type=F32)


def _in_projection(x, mod, w_ext, tm=512, tn=512):
    N = x.shape[0]
    nsub = tm // MOD_BLK
    return pl.pallas_call(
        functools.partial(_inproj_kernel, nsub=nsub),
        grid=(N // tm, IN_W_EXT // tn),
        in_specs=[pl.BlockSpec((tm, D_MODEL), lambda i, j: (i, 0)),
                  pl.BlockSpec((nsub, 6, D_MODEL), lambda i, j: (i, 0, 0)),
                  pl.BlockSpec((D_MODEL, tn), lambda i, j: (0, j))],
        out_specs=pl.BlockSpec((tm, tn), lambda i, j: (i, j)),
        out_shape=jax.ShapeDtypeStruct((N, IN_W_EXT), F32),
        scratch_shapes=[pltpu.VMEM((tm, D_MODEL), BF16)],
        compiler_params=pltpu.CompilerParams(dimension_semantics=("arbitrary", "arbitrary"),
                                             vmem_limit_bytes=VMEM_LIMIT),
        name="in_projection",
    )(x, mod, w_ext)


PREC = dict(seg=2, cum=2, a=1, neu=1, post=1, mn=1, b_o=1, b_s=3, gn=2)


def _dotp(a, b, dims, passes):
    dg = functools.partial(lax.dot_general, dimension_numbers=dims, preferred_element_type=F32)
    if passes == 1:
        return dg(a.astype(BF16), b.astype(BF16))
    ah, al = _split(a)
    if passes == 2:
        bh = b.astype(BF16)
        return dg(ah, bh) + dg(al, bh)
    bh, bl = _split(b)
    return dg(ah, bh) + (dg(ah, bl) + dg(al, bh))


def _dotp_rhs(a, b, passes):
    dg = functools.partial(lax.dot_general, dimension_numbers=_NN, preferred_element_type=F32)
    ab = a.astype(BF16)
    if passes == 1:
        return dg(ab, b.astype(BF16))
    bh, bl = _split(b)
    return dg(ab, bh) + dg(ab, bl)


def _rwkv_kernel(*refs, T, has_init, cb):
    (r_ref, k_ref, v_ref, g_ref, lo_ref, w0_ref, w2_ref, a0_ref, a2_ref,
     kk_ref, ka_ref, rk_ref, lng_ref, lnb_ref) = refs[:14]
    pos = 14
    init_ref = None
    if has_init:
        init_ref = refs[pos]
        pos += 1
    y_ref, sfin_ref = refs[pos:pos + 2]
    rp_scr, ov_scr, m_scr, n_scr, o_scr, bonus_scr, s_scr = refs[pos + 2:]

    nC = T // CHUNK
    C = CHUNK
    H = HEAD_DIM
    row = lax.broadcasted_iota(jnp.int32, (C, C), 0)
    col = lax.broadcasted_iota(jnp.int32, (C, C), 1)
    eye = row == col
    lrow = lax.broadcasted_iota(jnp.int32, (LANES, LANES), 0) // H
    lcol = lax.broadcasted_iota(jnp.int32, (LANES, LANES), 1) // H
    head_ones = jnp.where(lrow == lcol, 1.0, 0.0).astype(F32)
    strict_m = [col < row, col > row]
    incl_m = [col <= row, col >= row]
    tri = [jnp.where(m, 1.0, 0.0).astype(F32) for m in incl_m]
    P = PREC

    kkw = kk_ref[...]
    kaw = ka_ref[...]
    rkw = rk_ref[...]

    def phase_a(ci, carry):
        units = []
        for cc in range(cb):
            c = ci * cb + cc
            rows = pl.ds(pl.multiple_of(c * C, C), C)
            r = r_ref[rows, :]
            k = k_ref[rows, :]
            v = v_ref[rows, :]
            lo = lo_ref[rows, :]
            kkraw = k * kkw
            ss = _dotp(kkraw * kkraw, head_ones, _NN, P['seg'])
            kkn = kkraw / jnp.maximum(jnp.sqrt(ss), 1e-12)
            bonus = jnp.zeros((C, LANES), F32)
            for d in range(2):
                hw = lo[:, d * LORA:(d + 1) * LORA]
                ha = lo[:, 2 * LORA + d * LORA:2 * LORA + (d + 1) * LORA]
                wl = w0_ref[d] + _dot(jnp.tanh(hw), w2_ref[d])
                ld = -jnp.exp(-_softplus(-wl) - 0.5)
                alpha = _sigmoid(a0_ref[d] + _dot(ha, a2_ref[d]))
                kd = k * (1.0 + (alpha - 1.0) * kaw)
                bb = kkn * alpha
                bonus = bonus + _dotp(r * kd * rkw, head_ones, _NN, P['seg']) * v
                g_incl = _dotp_rhs(tri[d], ld, P['cum'])
                g_excl = g_incl - ld
                g_end = g_incl[C - 1:C, :] if d == 0 else g_incl[0:1, :]
                rt = r * jnp.exp(g_incl)
                at = -kkn * jnp.exp(g_excl)
                ginv = jnp.exp(-g_incl)
                kt = kd * ginv
                bt = bb * ginv
                dec_end = jnp.exp(g_end - g_incl)
                kend = kd * dec_end
                bend = bb * dec_end
                gend = jnp.exp(g_end)
                for hh in range(2):
                    sl = slice(hh * H, (hh + 1) * H)
                    units.append(dict(d=d, c=c, hh=hh, at=at[:, sl], rt=rt[:, sl], bt=bt[:, sl], kt=kt[:, sl],
                                      bend=bend[:, sl], kend=kend[:, sl], v=v[:, sl], gend=gend[:, sl]))
            bonus_scr[rows, :] = bonus
            o_scr[rows, :] = jnp.zeros((C, LANES), F32)

        for u in units:
            ar = jnp.concatenate([u['at'], u['rt']], axis=0)
            bk = jnp.concatenate([u['bt'], u['kt']], axis=0)
            u['a'] = _dotp(ar, bk, _NT, P['a'])
        for u in units:
            a = u['a']
            st, ic = strict_m[u['d']], incl_m[u['d']]
            u['p'] = jnp.where(st, a[:C, :C], 0.0)
            u['a_ak'] = jnp.where(st, a[:C, C:], 0.0)
            u['a_rb'] = jnp.where(ic, a[C:, :C], 0.0)
            u['a_rk'] = jnp.where(ic, a[C:, C:], 0.0)
        for u in units:
            u['rhs'] = jnp.concatenate([u['at'], _dotp(u['a_ak'], u['v'], _NN, P['post'])], axis=1)
            u['ov'] = _dotp(u['a_rk'], u['v'], _NN, P['post'])
        for it in range(6):
            upd = [_dotp(u['p'], u['rhs'], _NN, P['neu']) for u in units]
            if it < 5:
                sq = [_dotp(u['p'], u['p'], _NN, P['neu']) for u in units]
            for i, u in enumerate(units):
                u['rhs'] = u['rhs'] + upd[i]
                if it < 5:
                    u['p'] = sq[i]
        xs = [_dotp(u['a_rb'], u['rhs'], _NN, P['post']) for u in units]
        ms = [_dotp(u['rhs'][:, :H], u['bend'], _TN, P['mn']) for u in units]
        ns = [_dotp(jnp.concatenate([u['rhs'][:, H:], u['v']], axis=0),
                    jnp.concatenate([u['bend'], u['kend']], axis=0), _TN, P['mn']) for u in units]
        for i, u in enumerate(units):
            d, c, hh = u['d'], u['c'], u['hh']
            rp_scr[d, c, hh] = u['rt'] + xs[i][:, :H]
            ov_scr[d, c, hh] = xs[i][:, H:] + u['ov']
            m_scr[d, c, hh] = jnp.where(eye, jnp.broadcast_to(u['gend'], (H, H)), 0.0) + ms[i]
            n_scr[d, c, hh] = ns[i]
        return carry

    lax.fori_loop(0, nC // cb, phase_a, 0)

    for d in range(2):
        for hh in range(2):
            if has_init:
                s_scr[d, hh] = init_ref[d, hh]
            else:
                s_scr[d, hh] = jnp.zeros((H, H), F32)

    def phase_b(i, carry):
        idx = [(d, i if d == 0 else nC - 1 - i, hh) for d in range(2) for hh in range(2)]
        ss = [s_scr[d, hh] for d, c, hh in idx]
        os_ = [_dotp(rp_scr[d, c, hh], ss[j], _NT, P['b_o']) for j, (d, c, hh) in enumerate(idx)]
        sn = [_dotp(ss[j], m_scr[d, c, hh], _NN, P['b_s']) for j, (d, c, hh) in enumerate(idx)]
        for j, (d, c, hh) in enumerate(idx):
            s_scr[d, hh] = sn[j] + n_scr[d, c, hh]
        for d in range(2):
            c = i if d == 0 else nC - 1 - i
            rows = pl.ds(pl.multiple_of(c * C, C), C)
            o = jnp.concatenate([os_[2 * d] + ov_scr[d, c, 0], os_[2 * d + 1] + ov_scr[d, c, 1]], axis=1)
            o_scr[rows, :] = o_scr[rows, :] + o
        return carry

    lax.fori_loop(0, nC, phase_b, 0)

    for d in range(2):
        for hh in range(2):
            sfin_ref[d, hh] = s_scr[d, hh]

    lng = lng_ref[...]
    lnb = lnb_ref[...]

    rb = min(T, 256)

    def phase_c(c, carry):
        rows = pl.ds(pl.multiple_of(c * rb, rb), rb)
        o = o_scr[rows, :]
        st = _dotp(jnp.concatenate([o, o * o], axis=0), head_ones, _NN, P['gn']) * (1.0 / H)
        mu = st[:rb]
        var = st[rb:] - mu * mu
        on = (o - mu) * lax.rsqrt(var + GN_EPS) * lng + lnb
        y_ref[rows, :] = (on + bonus_scr[rows, :]) * _sigmoid(g_ref[rows, :])
        return carry

    lax.fori_loop(0, T // rb, phase_c, 0)


def _rwkv(proj, p, init, *, T, n_seq, row_blk0):
    nC = T // CHUNK
    npair = RWKV_HEADS // 2
    has_init = init is not None
    cb = lambda off: (lambda b, j: (b + row_blk0, off // LANES + j))
    vec = lambda b, j: (0, j)
    in_specs = [pl.BlockSpec((T, LANES), cb(C_R)),
                pl.BlockSpec((T, LANES), cb(C_K)),
                pl.BlockSpec((T, LANES), cb(C_V)),
                pl.BlockSpec((T, LANES), cb(C_G)),
                pl.BlockSpec((T, 4 * LORA), lambda b, j: (b + row_blk0, C_LORA // (4 * LORA))),
                pl.BlockSpec((2, 1, LANES), lambda b, j: (0, 0, j)),
                pl.BlockSpec((2, LORA, LANES), lambda b, j: (0, 0, j)),
                pl.BlockSpec((2, 1, LANES), lambda b, j: (0, 0, j)),
                pl.BlockSpec((2, LORA, LANES), lambda b, j: (0, 0, j)),
                pl.BlockSpec((1, LANES), vec), pl.BlockSpec((1, LANES), vec),
                pl.BlockSpec((1, LANES), vec), pl.BlockSpec((1, LANES), vec),
                pl.BlockSpec((1, LANES), vec)]
    args = [proj, proj, proj, proj, proj,
            p['rwkv_w0'].reshape(2, 1, RWKV_W), p['rwkv_w2'], p['rwkv_a0'].reshape(2, 1, RWKV_W), p['rwkv_a2'],
            p['rwkv_k_k'].reshape(1, RWKV_W), p['rwkv_k_a'].reshape(1, RWKV_W),
            p['rwkv_r_k'].reshape(1, RWKV_W), p['rwkv_lnx_g'].reshape(1, RWKV_W),
            p['rwkv_lnx_b'].reshape(1, RWKV_W)]
    st_spec = pl.BlockSpec((None, 2, 2, HEAD_DIM, HEAD_DIM), lambda b, j: (b, 0, j, 0, 0))
    if has_init:
        in_specs.append(st_spec)
        args.append(init)
    sc = lambda: pltpu.VMEM((2, nC, 2, CHUNK, HEAD_DIM), F32)
    return pl.pallas_call(
        functools.partial(_rwkv_kernel, T=T, has_init=has_init, cb=min(RWKV_CB, nC)),
        grid=(n_seq, npair),
        in_specs=in_specs,
        out_specs=[pl.BlockSpec((T, LANES), lambda b, j: (b, j)), st_spec],
        out_shape=[jax.ShapeDtypeStruct((n_seq * T, RWKV_W), F32),
                   jax.ShapeDtypeStruct((n_seq, 2, RWKV_HEADS, HEAD_DIM, HEAD_DIM), F32)],
        scratch_shapes=[sc(), sc(), sc(), sc(),
                        pltpu.VMEM((T, LANES), F32), pltpu.VMEM((T, LANES), F32),
                        pltpu.VMEM((2, 2, HEAD_DIM, HEAD_DIM), F32)],
        compiler_params=pltpu.CompilerParams(dimension_semantics=("arbitrary", "arbitrary"),
                                             vmem_limit_bytes=VMEM_LIMIT),
        name="rwkv_T%d" % T,
    )(*args)


def _dft_mats(L):
    f = np.arange(L, dtype=np.int64)
    ph = (np.outer(f, f) % (2 * L)).astype(np.float64) * (math.pi / L)
    return np.cos(ph), -np.sin(ph)


def _filter_features(L):
    t = np.linspace(0.0, 1.0, L, dtype=np.float32)[:, None]
    bands = (FILTER_EMB - 1) // 2
    t_res = np.arange(L, dtype=np.float32)[:, None]
    f = np.linspace(1e-4, bands - 1, bands, dtype=np.float32)[None, :]
    ang = (2.0 * math.pi * t_res * f / L).astype(np.float32)
    z = np.concatenate([t, np.cos(ang), np.sin(ang)], -1).astype(np.float32)
    zp = np.zeros((L, FILTER_HID), np.float32)
    zp[:, :FILTER_EMB] = z
    deltas = np.abs(np.linspace(HY_MIN_DECAY, HY_MAX_DECAY, HYENA_W, dtype=np.float32))
    return zp, t.astype(np.float32), deltas[None, :].astype(np.float32)


def _hyfilter_kernel(z_ref, t_ref, dl_ref, w1_ref, b1_ref, q1_ref, w2_ref, b2_ref, q2_ref, wc_ref, wa_ref,
                     hs_ref, hd_ref, knyq_ref, *, L):
    hm = jnp.sin(q1_ref[...] * (_dot3(z_ref[...], w1_ref[...]) + b1_ref[...]))
    hm = jnp.sin(q2_ref[...] * (_dot3(hm, w2_ref[...]) + b2_ref[...]))
    win = jnp.exp(-t_ref[...] * dl_ref[...])
    lag = lax.broadcasted_iota(jnp.int32, (L, 1), 0)
    alt = jnp.where(lag % 2 == 0, 1.0, -1.0).astype(F32)
    hc = _dot3(hm, wc_ref[...]) * win
    ha = jnp.where(lag == 0, 0.0, _dot3(hm, wa_ref[...]) * win)
    hs_ref[...] = hc + ha
    hd_ref[...] = hc - ha
    knyq_ref[...] = jnp.sum((hc + ha) * alt, axis=0, keepdims=True) * (1.0 / (2 * L))


def _hyspec_kernel(cm_ref, sm_ref, hs_ref, hd_ref, kre_ref, kim_ref, *, L, tf):
    f = pl.program_id(2) * tf + lax.broadcasted_iota(jnp.int32, (tf, 1), 0)
    wf = jnp.where(f == 0, 1.0, 2.0).astype(F32) * (1.0 / (2 * L))
    kre_ref[...] = _dot3(cm_ref[...], hs_ref[...]) * wf
    kim_ref[...] = _dot3(sm_ref[...], hd_ref[...]) * wf


def _hyena_filter(p, L, cm, sm, cw=128):
    zp, t, dl = _filter_features(L)
    w1 = jnp.zeros((FILTER_HID, FILTER_HID), F32).at[:FILTER_EMB].set(p['hy_f_w1'])
    r2 = lambda a: a.reshape(1, -1)
    ncb = HYENA_W // cw
    full = lambda shape: pl.BlockSpec(shape, lambda n, j: (0,) * len(shape))
    shp = jax.ShapeDtypeStruct((2, L, HYENA_W), F32)
    hs, hd, knyq = pl.pallas_call(
        functools.partial(_hyfilter_kernel, L=L),
        grid=(2, ncb),
        in_specs=[full((L, FILTER_HID)), full((L, 1)),
                  pl.BlockSpec((1, cw), lambda n, j: (0, j)),
                  full((FILTER_HID, FILTER_HID)), full((1, FILTER_HID)), full((1, FILTER_HID)),
                  full((FILTER_HID, FILTER_HID)), full((1, FILTER_HID)), full((1, FILTER_HID)),
                  pl.BlockSpec((FILTER_HID, cw), lambda n, j: (0, n * 2 * ncb + j)),
                  pl.BlockSpec((FILTER_HID, cw), lambda n, j: (0, n * 2 * ncb + ncb + j))],
        out_specs=[pl.BlockSpec((None, L, cw), lambda n, j: (n, 0, j)),
                   pl.BlockSpec((None, L, cw), lambda n, j: (n, 0, j)),
                   pl.BlockSpec((None, 1, cw), lambda n, j: (n, 0, j))],
        out_shape=[shp, shp, jax.ShapeDtypeStruct((2, 1, HYENA_W), F32)],
        compiler_params=pltpu.CompilerParams(dimension_semantics=("arbitrary", "arbitrary"),
                                             vmem_limit_bytes=VMEM_LIMIT),
        name="hyena_filter_L%d" % L,
    )(jnp.asarray(zp), jnp.asarray(t), jnp.asarray(dl), w1, r2(p['hy_f_b1']), r2(p['hy_f_freq1']),
      p['hy_f_w2'], r2(p['hy_f_b2']), r2(p['hy_f_freq2']), p['hy_f_w3'], p['hy_f_w3'])
    tf = min(256, L)
    sw = 256
    kre, kim = pl.pallas_call(
        functools.partial(_hyspec_kernel, L=L, tf=tf),
        grid=(2, HYENA_W // sw, L // tf),
        in_specs=[pl.BlockSpec((tf, L), lambda n, j, i: (i, 0)),
                  pl.BlockSpec((tf, L), lambda n, j, i: (i, 0)),
                  pl.BlockSpec((None, L, sw), lambda n, j, i: (n, 0, j)),
                  pl.BlockSpec((None, L, sw), lambda n, j, i: (n, 0, j))],
        out_specs=[pl.BlockSpec((None, tf, sw), lambda n, j, i: (n, i, j)),
                   pl.BlockSpec((None, tf, sw), lambda n, j, i: (n, i, j))],
        out_shape=[shp, shp],
        compiler_params=pltpu.CompilerParams(dimension_semantics=("arbitrary",) * 3,
                                             vmem_limit_bytes=VMEM_LIMIT),
        name="hyena_spectrum_L%d" % L,
    )(cm, sm, hs, hd)
    return kre, kim, knyq


def _hyena_kernel(v_ref, x1_ref, x2_ref, sw_ref, bias_ref, cm_ref, sm_ref, kre_ref, kim_ref, knyq_ref,
                  o_ref, zb_scr, zf_scr, xs_scr, yre_scr, yim_scr, *, L, rb):
    nrb = L // rb
    t = lax.broadcasted_iota(jnp.int32, (L, 1), 0)
    first = t == 0
    last = t == L - 1
    alt_all = jnp.where(t % 2 == 0, 1.0, -1.0).astype(F32)
    tb = lax.broadcasted_iota(jnp.int32, (rb, 1), 0)
    alt_blk = jnp.where(tb % 2 == 0, 1.0, -1.0).astype(F32)
    sw = sw_ref[...]

    def short(ref, s):
        u = ref[...]
        prev = jnp.where(first, 0.0, pltpu.roll(u, 1, 0))
        nxt = jnp.where(last, 0.0, pltpu.roll(u, L - 1, 0))
        return prev * sw[0, s][None, :] + u * sw[1, s][None, :] + nxt * sw[2, s][None, :]

    z0 = short(v_ref, 0)
    zf_scr[...] = z0
    zb_scr[...] = z0.astype(BF16)
    unyq = jnp.sum(z0 * alt_all, axis=0, keepdims=True)

    for n in range(2):
        xs_scr[...] = short(x1_ref if n == 0 else x2_ref, n + 1)

        def fwd(fb, carry):
            rows = pl.ds(pl.multiple_of(fb * rb, rb), rb)
            zb = zb_scr[...]
            ure = jnp.dot(cm_ref[rows, :], zb, preferred_element_type=F32)
            uim = jnp.dot(sm_ref[rows, :], zb, preferred_element_type=F32)
            kre = kre_ref[n, rows, :]
            kim = kim_ref[n, rows, :]
            yre_scr[rows, :] = (ure * kre - uim * kim).astype(BF16)
            yim_scr[rows, :] = (ure * kim + uim * kre).astype(BF16)
            return carry

        lax.fori_loop(0, nrb, fwd, 0)
        ynyq = unyq * knyq_ref[n]
        bias = bias_ref[n]

        def inv(ib, acc):
            rows = pl.ds(pl.multiple_of(ib * rb, rb), rb)
            y = (jnp.dot(cm_ref[rows, :], yre_scr[...], preferred_element_type=F32)
                 + jnp.dot(sm_ref[rows, :], yim_scr[...], preferred_element_type=F32))
            y = y + alt_blk * ynyq + zf_scr[rows, :] * bias
            znew = xs_scr[rows, :] * y
            if n == 0:
                zf_scr[rows, :] = znew
                zb_scr[rows, :] = znew.astype(BF16)
                acc = acc + jnp.sum(znew * alt_blk, axis=0, keepdims=True)
            else:
                o_ref[rows, :] = znew
            return acc

        unyq = lax.fori_loop(0, nrb, inv, jnp.zeros_like(unyq))


def _hyena(proj, p, filt, cm, sm, *, L, n_seq, row_blk0, cw=256):
    kre, kim, knyq = filt
    ncb = HYENA_W // cw
    cb = lambda s: (lambda j, b: (b + row_blk0, (C_HY + s * HYENA_W) // cw + j))
    once = pl.Buffered(1)
    sw = p['hy_short_w'].reshape(3, 3, HYENA_W)
    rb = min(256, L)
    return pl.pallas_call(
        functools.partial(_hyena_kernel, L=L, rb=rb),
        grid=(ncb, n_seq),
        in_specs=[pl.BlockSpec((L, cw), cb(0)), pl.BlockSpec((L, cw), cb(1)), pl.BlockSpec((L, cw), cb(2)),
                  pl.BlockSpec((3, 3, cw), lambda j, b: (0, 0, j)),
                  pl.BlockSpec((2, 1, cw), lambda j, b: (0, 0, j)),
                  pl.BlockSpec((L, L), lambda j, b: (0, 0), pipeline_mode=once),
                  pl.BlockSpec((L, L), lambda j, b: (0, 0), pipeline_mode=once),
                  pl.BlockSpec((2, L, cw), lambda j, b: (0, 0, j), pipeline_mode=once),
                  pl.BlockSpec((2, L, cw), lambda j, b: (0, 0, j), pipeline_mode=once),
                  pl.BlockSpec((2, 1, cw), lambda j, b: (0, 0, j))],
        out_specs=pl.BlockSpec((L, cw), lambda j, b: (b, j)),
        out_shape=jax.ShapeDtypeStruct((n_seq * L, HYENA_W), F32),
        scratch_shapes=[pltpu.VMEM((L, cw), BF16), pltpu.VMEM((L, cw), F32), pltpu.VMEM((L, cw), F32),
                        pltpu.VMEM((L, cw), BF16), pltpu.VMEM((L, cw), BF16)],
        compiler_params=pltpu.CompilerParams(dimension_semantics=("arbitrary", "arbitrary"),
                                             vmem_limit_bytes=VMEM_LIMIT),
        name="hyena_L%d" % L,
    )(proj, proj, proj, sw, p['hy_bias'].reshape(2, 1, HYENA_W), cm.astype(BF16), sm.astype(BF16),
      kre, kim, knyq)


def _softmax_av(s, v, snk):
    m = jnp.maximum(jnp.max(s, axis=-1, keepdims=True), snk)
    pe = jnp.exp(s - m)
    den = jnp.sum(pe, axis=-1, keepdims=True) + jnp.exp(snk - m)
    return _dot(pe, v) / den


def _ctx_attn_kernel(q_ref, k_ref, v_ref, sink_ref, o_ref, *, S):
    scale = HEAD_DIM ** -0.5
    outs = []
    for h in range(ATTN_KV_HEADS):
        kh = k_ref[:, h * HEAD_DIM:(h + 1) * HEAD_DIM]
        vh = v_ref[:, h * HEAD_DIM:(h + 1) * HEAD_DIM]
        for g in range(GQA_GROUP):
            a = h * GQA_GROUP + g
            qa = q_ref[:, a * HEAD_DIM:(a + 1) * HEAD_DIM]
            s = _dot_nt(qa, kh) * scale
            snk = jnp.broadcast_to(sink_ref[0:1, a:a + 1], (S, 1))
            outs.append(_softmax_av(s, vh, snk))
    o_ref[...] = jnp.concatenate(outs, axis=1)


def _ctx_attention(proj, sink, *, S, n_seq):
    return pl.pallas_call(
        functools.partial(_ctx_attn_kernel, S=S),
        grid=(n_seq,),
        in_specs=[pl.BlockSpec((S, ATTN_W), lambda b: (b, C_Q // ATTN_W)),
                  pl.BlockSpec((S, KV_W), lambda b: (b, C_AK // KV_W)),
                  pl.BlockSpec((S, KV_W), lambda b: (b, C_AV // KV_W)),
                  pl.BlockSpec((1, LANES), lambda b: (0, 0))],
        out_specs=pl.BlockSpec((S, ATTN_W), lambda b: (b, 0)),
        out_shape=jax.ShapeDtypeStruct((n_seq * S, ATTN_W), F32),
        compiler_params=pltpu.CompilerParams(dimension_semantics=("arbitrary",),
                                             vmem_limit_bytes=VMEM_LIMIT),
        name="ctx_attention",
    )(proj, proj, proj, sink)


def _rope(x, cos, sin_signed, first_half):
    n = x.shape[-1]
    partner = jnp.where(first_half, pltpu.roll(x, n - HEAD_DIM // 2, 1), pltpu.roll(x, HEAD_DIM // 2, 1))
    return x * cos + partner * sin_signed


def _lat_attn_kernel(q_ref, kp_ref, kc_ref, kn_ref, vp_ref, vc_ref, vn_ref, ck_ref, cv_ref,
                     cq_ref, sq_ref, cp_ref, sp_ref, cc_ref, sc_ref, cn_ref, sn_ref, sink_ref, o_ref, *, T):
    i = pl.program_id(1)
    scale = HEAD_DIM ** -0.5
    fh_q = (lax.broadcasted_iota(jnp.int32, (1, ATTN_W), 1) % HEAD_DIM) < HEAD_DIM // 2
    fh_k = (lax.broadcasted_iota(jnp.int32, (1, KV_W), 1) % HEAD_DIM) < HEAD_DIM // 2
    q = _rope(q_ref[...], cq_ref[...], sq_ref[...], fh_q)
    kl = jnp.concatenate([_rope(kp_ref[...], cp_ref[...], sp_ref[...], fh_k),
                          _rope(kc_ref[...], cc_ref[...], sc_ref[...], fh_k),
                          _rope(kn_ref[...], cn_ref[...], sn_ref[...], fh_k)], axis=0)
    vl = jnp.concatenate([vp_ref[...], vc_ref[...], vn_ref[...]], axis=0)
    qpos = i * BLOCK + lax.broadcasted_iota(jnp.int32, (BLOCK, 3 * BLOCK), 0)
    kpos = (i - 1) * BLOCK + lax.broadcasted_iota(jnp.int32, (BLOCK, 3 * BLOCK), 1)
    ok = (jnp.abs(qpos - kpos) <= WINDOW) & (kpos >= 0) & (kpos < T)
    outs = []
    for h in range(ATTN_KV_HEADS):
        hs = slice(h * HEAD_DIM, (h + 1) * HEAD_DIM)
        klh, vlh = kl[:, hs], vl[:, hs]
        kch, vch = ck_ref[:, hs], cv_ref[:, hs]
        for g in range(GQA_GROUP):
            a = h * GQA_GROUP + g
            qa = q[:, a * HEAD_DIM:(a + 1) * HEAD_DIM]
            s_loc = jnp.where(ok, _dot_nt(qa, klh) * scale, -jnp.inf)
            s_ctx = _dot_nt(qa, kch) * scale
            snk = jnp.broadcast_to(sink_ref[0:1, a:a + 1], (BLOCK, 1))
            m = jnp.maximum(jnp.maximum(jnp.max(s_loc, axis=-1, keepdims=True),
                                        jnp.max(s_ctx, axis=-1, keepdims=True)), snk)
            p_loc = jnp.exp(s_loc - m)
            p_ctx = jnp.exp(s_ctx - m)
            den = (jnp.sum(p_loc, axis=-1, keepdims=True) + jnp.sum(p_ctx, axis=-1, keepdims=True)
                   + jnp.exp(snk - m))
            outs.append((_dot(p_loc, vlh) + _dot(p_ctx, vch)) / den)
    o_ref[...] = jnp.concatenate(outs, axis=1)


def _rope_tables(T):
    rows = T // GRID_W
    row = np.repeat(np.arange(rows), GRID_W).astype(np.float32)
    col = np.tile(np.arange(GRID_W), rows).astype(np.float32)
    nf = HEAD_DIM // 4
    inv = (ROPE_BASE ** (-np.arange(nf, dtype=np.float32) / nf)).astype(np.float32)
    ang = np.concatenate([row[:, None] * inv, col[:, None] * inv], -1).astype(np.float32)
    cos = np.cos(ang).astype(np.float32)
    sin = np.sin(ang).astype(np.float32)
    cos_h = np.concatenate([cos, cos], -1)
    sin_h = np.concatenate([-sin, sin], -1)
    return cos_h, sin_h


def _lat_attention(proj, cache_k, cache_v, sink, *, T, n_seq, row_blk0):
    nb = T // BLOCK
    P = cache_k.shape[1]
    cos_h, sin_h = _rope_tables(T)
    cq = jnp.asarray(np.tile(cos_h, (1, ATTN_HEADS)))
    sq = jnp.asarray(np.tile(sin_h, (1, ATTN_HEADS)))
    ck = jnp.asarray(np.tile(cos_h, (1, ATTN_KV_HEADS)))
    sk = jnp.asarray(np.tile(sin_h, (1, ATTN_KV_HEADS)))
    r0 = row_blk0 * nb
    prv = lambda i: jnp.maximum(i - 1, 0)
    nxt = lambda i: jnp.minimum(i + 1, nb - 1)
    kcol, vcol = C_AK // KV_W, C_AV // KV_W
    kv = lambda col, f: pl.BlockSpec((BLOCK, KV_W), lambda b, i: (r0 + b * nb + f(i), col))
    tab = lambda w, f: pl.BlockSpec((BLOCK, w), lambda b, i: (f(i), 0))
    cur = lambda i: i
    return pl.pallas_call(
        functools.partial(_lat_attn_kernel, T=T),
        grid=(n_seq, nb),
        in_specs=[pl.BlockSpec((BLOCK, ATTN_W), lambda b, i: (r0 + b * nb + i, C_Q // ATTN_W)),
                  kv(kcol, prv), kv(kcol, cur), kv(kcol, nxt),
                  kv(vcol, prv), kv(vcol, cur), kv(vcol, nxt),
                  pl.BlockSpec((None, P, KV_W), lambda b, i: (b, 0, 0)),
                  pl.BlockSpec((None, P, KV_W), lambda b, i: (b, 0, 0)),
                  tab(ATTN_W, cur), tab(ATTN_W, cur),
                  tab(KV_W, prv), tab(KV_W, prv), tab(KV_W, cur), tab(KV_W, cur), tab(KV_W, nxt), tab(KV_W, nxt),
                  pl.BlockSpec((1, LANES), lambda b, i: (0, 0))],
        out_specs=pl.BlockSpec((BLOCK, ATTN_W), lambda b, i: (b * nb + i, 0)),
        out_shape=jax.ShapeDtypeStruct((n_seq * T, ATTN_W), F32),
        compiler_params=pltpu.CompilerParams(dimension_semantics=("arbitrary", "arbitrary"),
                                             vmem_limit_bytes=VMEM_LIMIT),
        name="latent_attention",
    )(proj, proj, proj, proj, proj, proj, proj, cache_k, cache_v, cq, sq, ck, sk, ck, sk, ck, sk, sink)


def _layer_norm(y, g, b):
    mu = jnp.mean(y, axis=-1, keepdims=True)
    d = y - mu
    var = jnp.mean(d * d, axis=-1, keepdims=True)
    return d * lax.rsqrt(var + LN_EPS) * g + b


def _outproj_kernel(ac_ref, bc_ref, cc_ref, al_ref, bl_ref, cl_ref, x_ref, mod_ref, wa_ref, wb_ref, wc_ref,
                    g_ref, be_ref, x1_ref, h2_ref, m_scr, *, nsub, n_ctx_tiles):
    def mix(a_ref, b_ref, c_ref):
        m_scr[...] = (jnp.dot(a_ref[...].astype(BF16), wa_ref[...], preferred_element_type=F32)
                      + jnp.dot(b_ref[...].astype(BF16), wb_ref[...], preferred_element_type=F32)
                      + jnp.dot(c_ref[...].astype(BF16), wc_ref[...], preferred_element_type=F32))

    pl.when(pl.program_id(0) < n_ctx_tiles)(lambda: mix(ac_ref, bc_ref, cc_ref))
    pl.when(pl.program_id(0) >= n_ctx_tiles)(lambda: mix(al_ref, bl_ref, cl_ref))
    m = m_scr[...]
    for s in range(nsub):
        rows = slice(s * MOD_BLK, (s + 1) * MOD_BLK)
        gate1 = mod_ref[s, 2:3, :]
        shift2 = mod_ref[s, 3:4, :]
        scale2 = mod_ref[s, 4:5, :]
        x1 = _layer_norm(DN_ALPHA * x_ref[rows, :] + gate1 * m[rows, :], g_ref[...], be_ref[...])
        x1_ref[rows, :] = x1
        h2_ref[rows, :] = (x1 * (1.0 + scale2) + shift2).astype(BF16)


def _out_projection(mix_ctx, mix_lat, x, mod, w_out, g, b, tm=256):
    N = x.shape[0]
    nsub = tm // MOD_BLK
    nct = mix_ctx[0].shape[0] // tm
    nlt = mix_lat[0].shape[0] // tm
    row = lambda w: pl.BlockSpec((tm, w), lambda i: (i, 0))
    ctx = lambda w: pl.BlockSpec((tm, w), lambda i: (jnp.minimum(i, nct - 1), 0))
    lat = lambda w: pl.BlockSpec((tm, w), lambda i: (jnp.clip(i - nct, 0, nlt - 1), 0))
    full = lambda r: pl.BlockSpec((r, D_MODEL), lambda i: (0, 0))
    wa, wb, wc = w_out[:RWKV_W], w_out[RWKV_W:RWKV_W + HYENA_W], w_out[RWKV_W + HYENA_W:]
    return pl.pallas_call(
        functools.partial(_outproj_kernel, nsub=nsub, n_ctx_tiles=nct),
        grid=(N // tm,),
        in_specs=[ctx(RWKV_W), ctx(HYENA_W), ctx(ATTN_W), lat(RWKV_W), lat(HYENA_W), lat(ATTN_W), row(D_MODEL),
                  pl.BlockSpec((nsub, 6, D_MODEL), lambda i: (i, 0, 0)),
                  full(RWKV_W), full(HYENA_W), full(ATTN_W), full(1), full(1)],
        out_specs=[row(D_MODEL), row(D_MODEL)],
        out_shape=[jax.ShapeDtypeStruct((N, D_MODEL), F32), jax.ShapeDtypeStruct((N, D_MODEL), BF16)],
        scratch_shapes=[pltpu.VMEM((tm, D_MODEL), F32)],
        compiler_params=pltpu.CompilerParams(dimension_semantics=("arbitrary",),
                                             vmem_limit_bytes=VMEM_LIMIT),
        name="out_projection",
    )(*mix_ctx, *mix_lat, x, mod, wa, wb, wc, g.reshape(1, D_MODEL), b.reshape(1, D_MODEL))


HALO = 16


def _ffn_kernel(hp_ref, h_ref, hn_ref, x1_ref, mod_ref, wa_ref, wb_ref, cwa_ref, cwb_ref, wd_ref, g_ref, be_ref,
                o_ref, hx_scr, ua_scr, ub_scr, acc_scr, *, tm, tf, nsub, n_ctx, s_ctx, s_lat):
    i = pl.program_id(0)
    j = pl.program_id(1)

    @pl.when(j == 0)
    def _():
        hx_scr[0:HALO, :] = hp_ref[...]
        hx_scr[HALO:HALO + tm, :] = h_ref[...]
        hx_scr[HALO + tm:, :] = hn_ref[...]
        acc_scr[...] = jnp.zeros_like(acc_scr)

    hx = hx_scr[...]
    ua_scr[...] = jnp.dot(hx, wa_ref[...], preferred_element_type=F32)
    ub_scr[...] = jnp.dot(hx, wb_ref[...], preferred_element_type=F32)
    grow = i * tm + lax.broadcasted_iota(jnp.int32, (tm, 1), 0)
    pos = jnp.where(grow < n_ctx, grow % s_ctx, (grow - n_ctx) % s_lat)
    slen = jnp.where(grow < n_ctx, s_ctx, s_lat)
    mprev = jnp.where(pos == 0, 0.0, 1.0).astype(F32)
    mnext = jnp.where(pos == slen - 1, 0.0, 1.0).astype(F32)

    def conv(u_scr, cw_ref):
        up = u_scr[HALO - 1:HALO - 1 + tm, :] * mprev
        uc = u_scr[HALO:HALO + tm, :]
        un = u_scr[HALO + 1:HALO + 1 + tm, :] * mnext
        return up * cw_ref[0:1, :] + uc * cw_ref[1:2, :] + un * cw_ref[2:3, :]

    a = conv(ua_scr, cwa_ref)
    b = conv(ub_scr, cwb_ref)
    act = (a * _sigmoid(a) * b).astype(BF16)
    acc_scr[...] += jnp.dot(act, wd_ref[...], preferred_element_type=F32)

    @pl.when(j == pl.num_programs(1) - 1)
    def _():
        for s in range(nsub):
            rows = slice(s * MOD_BLK, (s + 1) * MOD_BLK)
            gate2 = mod_ref[s, 5:6, :]
            o_ref[rows, :] = _layer_norm(DN_ALPHA * x1_ref[rows, :] + gate2 * acc_scr[rows, :],
                                         g_ref[...], be_ref[...])


def _ffn(h2, x1, mod, wu, cw, wd, g, b, *, n_ctx, s_ctx, s_lat, tm=512, tf=256):
    N = x1.shape[0]
    nsub = tm // MOD_BLK
    nh = tm // HALO
    last_h = N // HALO - 1
    nf = D_FF_PAD // tf
    return pl.pallas_call(
        functools.partial(_ffn_kernel, tm=tm, tf=tf, nsub=nsub, n_ctx=n_ctx, s_ctx=s_ctx, s_lat=s_lat),
        grid=(N // tm, D_FF_PAD // tf),
        in_specs=[pl.BlockSpec((HALO, D_MODEL), lambda i, j: (jnp.maximum(i * nh - 1, 0), 0)),
                  pl.BlockSpec((tm, D_MODEL), lambda i, j: (i, 0)),
                  pl.BlockSpec((HALO, D_MODEL), lambda i, j: (jnp.minimum((i + 1) * nh, last_h), 0)),
                  pl.BlockSpec((tm, D_MODEL), lambda i, j: (i, 0)),
                  pl.BlockSpec((nsub, 6, D_MODEL), lambda i, j: (i, 0, 0)),
                  pl.BlockSpec((D_MODEL, tf), lambda i, j: (0, j)),
                  pl.BlockSpec((D_MODEL, tf), lambda i, j: (0, nf + j)),
                  pl.BlockSpec((3, tf), lambda i, j: (0, j)),
                  pl.BlockSpec((3, tf), lambda i, j: (0, nf + j)),
                  pl.BlockSpec((tf, D_MODEL), lambda i, j: (j, 0)),
                  pl.BlockSpec((1, D_MODEL), lambda i, j: (0, 0)),
                  pl.BlockSpec((1, D_MODEL), lambda i, j: (0, 0))],
        out_specs=pl.BlockSpec((tm, D_MODEL), lambda i, j: (i, 0)),
        out_shape=jax.ShapeDtypeStruct((N, D_MODEL), F32),
        scratch_shapes=[pltpu.VMEM((tm + 2 * HALO, D_MODEL), BF16),
                        pltpu.VMEM((tm + 2 * HALO, tf), F32),
                        pltpu.VMEM((tm + 2 * HALO, tf), F32),
                        pltpu.VMEM((tm, D_MODEL), F32)],
        compiler_params=pltpu.CompilerParams(dimension_semantics=("arbitrary", "arbitrary"),
                                             vmem_limit_bytes=VMEM_LIMIT),
        name="conv_ffn",
    )(h2, h2, h2, x1, mod, wu, wu, cw, cw, wd, g.reshape(1, D_MODEL), b.reshape(1, D_MODEL))


def _prep_ffn_weights(w_up, conv_w, w_down):
    pad = D_FF_PAD - D_FF

    def halves(w):
        R = w.shape[0]
        return jnp.pad(w.reshape(R, 2, D_FF), ((0, 0), (0, 0), (0, pad))).reshape(R, 2 * D_FF_PAD)

    wu = halves(w_up.astype(BF16))
    cw = halves(conv_w)
    wd = jnp.pad(w_down.astype(BF16), ((0, pad), (0, 0)))
    return wu, cw, wd


def kernel(x_prompt, x_sample, c, state_rwkv, cache_k, cache_v, c_ctx, w_mod, b_mod, w_in, rwkv_w0, rwkv_w1, rwkv_w2, rwkv_a0, rwkv_a1, rwkv_a2, rwkv_k_k, rwkv_k_a, rwkv_r_k, rwkv_lnx_g, rwkv_lnx_b, hy_short_w, hy_f_w1, hy_f_b1, hy_f_freq1, hy_f_w2, hy_f_b2, hy_f_freq2, hy_f_w3, hy_bias, attn_sink, w_out, ln1_g, ln1_b, ffn_w_up, ffn_conv_w, ffn_w_down, ln2_g, ln2_b):
    params = dict(rwkv_w0=rwkv_w0, rwkv_w2=rwkv_w2, rwkv_a0=rwkv_a0, rwkv_a2=rwkv_a2,
                  rwkv_k_k=rwkv_k_k, rwkv_k_a=rwkv_k_a, rwkv_r_k=rwkv_r_k,
                  rwkv_lnx_g=rwkv_lnx_g, rwkv_lnx_b=rwkv_lnx_b,
                  hy_short_w=hy_short_w, hy_f_w1=hy_f_w1, hy_f_b1=hy_f_b1, hy_f_freq1=hy_f_freq1,
                  hy_f_w2=hy_f_w2, hy_f_b2=hy_f_b2, hy_f_freq2=hy_f_freq2, hy_f_w3=hy_f_w3,
                  hy_bias=hy_bias)
    L = w_in.shape[0]
    Bc, Sc, D = x_prompt.shape
    Bl, Tl, _ = x_sample.shape
    P = cache_k.shape[2]
    n_ctx = Bc * Sc
    n_lat = Bl * Tl
    N = n_ctx + n_lat
    assert Sc % MOD_BLK == 0 and Tl % MOD_BLK == 0 and n_ctx % Tl == 0

    n_cond = 1 + Bl
    cond = jnp.zeros((16, D), F32).at[0].set(c_ctx).at[1:n_cond].set(c)
    mod_all = _modulation(cond, w_mod, b_mod)
    mod_ctx = jnp.broadcast_to(mod_all[:, 0:1, :], (L, n_ctx // MOD_BLK, 6 * D))
    mod_lat = jnp.broadcast_to(mod_all[:, 1:n_cond, None, :], (L, Bl, Tl // MOD_BLK, 6 * D))
    mod_blk = jnp.concatenate([mod_ctx, mod_lat.reshape(L, n_lat // MOD_BLK, 6 * D)], axis=1)
    mod_blk = mod_blk.reshape(L, N // MOD_BLK, 6, D)

    cm_c, sm_c = (jnp.asarray(m, F32) for m in _dft_mats(Sc))
    cm_l, sm_l = (jnp.asarray(m, F32) for m in _dft_mats(Tl))
    sink_pad = jnp.zeros((L, 1, LANES), F32).at[:, 0, :ATTN_HEADS].set(attn_sink)

    x = jnp.concatenate([x_prompt.reshape(n_ctx, D), x_sample.reshape(n_lat, D)], axis=0)
    states, keys, vals = [], [], []
    for l in range(L):
        p = {n: a[l] for n, a in params.items()}
        w_ext = jnp.concatenate([w_in[l], rwkv_w1[l, 0], rwkv_w1[l, 1], rwkv_a1[l, 0], rwkv_a1[l, 1]],
                                axis=1).astype(BF16)
        proj = _in_projection(x, mod_blk[l], w_ext)
        keys.append(proj[:n_ctx, C_AK:C_AK + KV_W].reshape(Bc, Sc, ATTN_KV_HEADS, HEAD_DIM))
        vals.append(proj[:n_ctx, C_AV:C_AV + KV_W].reshape(Bc, Sc, ATTN_KV_HEADS, HEAD_DIM))

        ya_c, s_fin = _rwkv(proj, p, None, T=Sc, n_seq=Bc, row_blk0=0)
        ya_l, _ = _rwkv(proj, p, state_rwkv[:, l], T=Tl, n_seq=Bl, row_blk0=n_ctx // Tl)
        states.append(s_fin)

        yb_c = _hyena(proj, p, _hyena_filter(p, Sc, cm_c, sm_c), cm_c, sm_c, L=Sc, n_seq=Bc, row_blk0=0)
        yb_l = _hyena(proj, p, _hyena_filter(p, Tl, cm_l, sm_l), cm_l, sm_l, L=Tl, n_seq=Bl,
                      row_blk0=n_ctx // Tl)

        yc_c = _ctx_attention(proj, sink_pad[l], S=Sc, n_seq=Bc)
        yc_l = _lat_attention(proj, cache_k[:, l].reshape(Bl, P, KV_W), cache_v[:, l].reshape(Bl, P, KV_W),
                              sink_pad[l], T=Tl, n_seq=Bl, row_blk0=n_ctx // Tl)

        x1, h2 = _out_projection((ya_c, yb_c, yc_c), (ya_l, yb_l, yc_l), x, mod_blk[l],
                                 w_out[l].astype(BF16), ln1_g[l], ln1_b[l])
        wu, cw, wd = _prep_ffn_weights(ffn_w_up[l], ffn_conv_w[l], ffn_w_down[l])
        x = _ffn(h2, x1, mod_blk[l], wu, cw, wd, ln2_g[l], ln2_b[l], n_ctx=n_ctx, s_ctx=Sc, s_lat=Tl)

    y_prompt = x[:n_ctx].reshape(Bc, Sc, D)
    y_sample = x[n_ctx:].reshape(Bl, Tl, D)
    new_state = jnp.stack(states, 1)
    new_k = jnp.stack(keys, 1)
    new_v = jnp.stack(vals, 1)
    return (y_prompt, y_sample, new_state, new_k, new_v)
```

```python
import functools
import math

import numpy as np
import jax
import jax.numpy as jnp
from jax import lax
from jax.experimental import pallas as pl
from jax.experimental.pallas import tpu as pltpu

F32 = jnp.float32
BF16 = jnp.bfloat16

D_MODEL = 2048
DEPTH = 4
GRID_W = 64
HEAD_DIM = 64
RWKV_W = 768
RWKV_HEADS = 12
HYENA_W = 512
ATTN_W = 768
ATTN_HEADS = 12
ATTN_KV_HEADS = 4
GQA_GROUP = 3
KV_W = 256
LORA = 64
GN_EPS = 64e-5
FILTER_EMB = 33
FILTER_HID = 64
HY_MIN_DECAY = math.log(1e-2) / 1.5
HY_MAX_DECAY = math.log(1e-2) / 0.3
WINDOW = 128
BLOCK = 128
ROPE_BASE = 10000.0
D_FF = 5504
LN_EPS = 1e-5
DN_ALPHA = (2 * DEPTH) ** 0.25

C_R, C_K, C_V, C_G = 0, 768, 1536, 2304
C_HY = 3072
C_Q, C_AK, C_AV = 4608, 5376, 5632
C_LORA = 5888
IN_W_EXT = 6144

MOD_BLK = 256
CHUNK = 64
RWKV_CB = 4
LANES = 128
D_FF_PAD = 5632
VMEM_LIMIT = 56 * 1024 * 1024


def _dot(a, b):
    return jnp.dot(a.astype(BF16), b.astype(BF16), preferred_element_type=F32)


def _dot_nt(a, b):
    return lax.dot_general(a.astype(BF16), b.astype(BF16), (((1,), (1,)), ((), ())),
                           preferred_element_type=F32)


def _split(x):
    hi = x.astype(BF16)
    lo = (x - hi.astype(F32)).astype(BF16)
    return hi, lo


def _dot3_dims(a, b, dims):
    ah, al = _split(a)
    bh, bl = _split(b)
    dg = functools.partial(lax.dot_general, dimension_numbers=dims, preferred_element_type=F32)
    return dg(ah, bh) + (dg(ah, bl) + dg(al, bh))


_NN = (((1,), (0,)), ((), ()))
_NT = (((1,), (1,)), ((), ()))
_TN = (((0,), (0,)), ((), ()))


def _dot3(a, b):
    return _dot3_dims(a, b, _NN)


def _sigmoid(x):
    return 1.0 / (1.0 + jnp.exp(-x))


def _softplus(x):
    return jnp.maximum(x, 0.0) + jnp.log(1.0 + jnp.exp(-jnp.abs(x)))


def _mod_kernel(c_ref, w_ref, b_ref, o_ref):
    c = c_ref[...]
    s = c * _sigmoid(c)
    o_ref[...] = _dot(s, w_ref[...]) + b_ref[...]


def _modulation(cond, w_mod, b_mod):
    L = w_mod.shape[0]
    R = cond.shape[0]
    tn = 1024
    return pl.pallas_call(
        _mod_kernel,
        grid=(L, 6 * D_MODEL // tn),
        in_specs=[pl.BlockSpec((R, D_MODEL), lambda l, j: (0, 0)),
                  pl.BlockSpec((None, D_MODEL, tn), lambda l, j: (l, 0, j)),
                  pl.BlockSpec((None, 1, tn), lambda l, j: (l, 0, j))],
        out_specs=pl.BlockSpec((None, R, tn), lambda l, j: (l, 0, j)),
        out_shape=jax.ShapeDtypeStruct((L, R, 6 * D_MODEL), F32),
        compiler_params=pltpu.CompilerParams(dimension_semantics=("arbitrary", "arbitrary"),
                                             vmem_limit_bytes=VMEM_LIMIT),
        name="modulation",
    )(cond, w_mod, b_mod.reshape(L, 1, 6 * D_MODEL))


def _inproj_kernel(x_ref, mod_ref, w_ref, o_ref, *, tn):
    shift = mod_ref[0, 0:1, :]
    scale = mod_ref[0, 1:2, :]
    h = (x_ref[...] * (1.0 + scale) + shift).astype(BF16)
    for j in range(IN_W_EXT // tn):
        cols = slice(j * tn, (j + 1) * tn)
        o_ref[:, cols] = jnp.dot(h, w_ref[:, cols], preferred_element_type=F32)


def _in_projection(x, mod, w_ext, l, tn=512):
    N = x.shape[0]
    tm = MOD_BLK
    return pl.pallas_call(
        functools.partial(_inproj_kernel, tn=tn),
        grid=(N // tm,),
        in_specs=[pl.BlockSpec((tm, D_MODEL), lambda i: (i, 0)),
                  pl.BlockSpec((1, 6, D_MODEL), lambda i: (i, 0, 0)),
                  pl.BlockSpec((None, D_MODEL, IN_W_EXT), lambda i: (l, 0, 0), pipeline_mode=pl.Buffered(1))],
        out_specs=pl.BlockSpec((tm, IN_W_EXT), lambda i: (i, 0)),
        out_shape=jax.ShapeDtypeStruct((N, IN_W_EXT), F32),
        compiler_params=pltpu.CompilerParams(dimension_semantics=("arbitrary",),
                                             vmem_limit_bytes=VMEM_LIMIT),
        name="in_projection",
    )(x, mod, w_ext)


PREC = dict(seg=2, cum=2, a=1, neu=1, post=1, mn=1, b_o=1, b_s=3, gn=2)


def _dotp(a, b, dims, passes):
    dg = functools.partial(lax.dot_general, dimension_numbers=dims, preferred_element_type=F32)
    if passes == 1:
        return dg(a.astype(BF16), b.astype(BF16))
    ah, al = _split(a)
    if passes == 2:
        bh = b.astype(BF16)
        return dg(ah, bh) + dg(al, bh)
    bh, bl = _split(b)
    return dg(ah, bh) + (dg(ah, bl) + dg(al, bh))


def _dotp_rhs(a, b, passes):
    dg = functools.partial(lax.dot_general, dimension_numbers=_NN, preferred_element_type=F32)
    ab = a.astype(BF16)
    if passes == 1:
        return dg(ab, b.astype(BF16))
    bh, bl = _split(b)
    return dg(ab, bh) + dg(ab, bl)


def _rwkv_kernel(*refs, T, has_init, cb):
    (r_ref, k_ref, v_ref, g_ref, lo_ref, w0_ref, w2_ref, a0_ref, a2_ref,
     kk_ref, ka_ref, rk_ref, lng_ref, lnb_ref) = refs[:14]
    pos = 14
    init_ref = None
    if has_init:
        init_ref = refs[pos]
        pos += 1
    y_ref, sfin_ref = refs[pos:pos + 2]
    rp_scr, ov_scr, m_scr, n_scr, o_scr, bonus_scr, s_scr = refs[pos + 2:]

    nC = T // CHUNK
    C = CHUNK
    H = HEAD_DIM
    P = PREC
    C2 = 2 * C
    row = lax.broadcasted_iota(jnp.int32, (C, C), 0)
    col = lax.broadcasted_iota(jnp.int32, (C, C), 1)
    tri = [jnp.where(col <= row, 1.0, 0.0).astype(F32), jnp.where(col >= row, 1.0, 0.0).astype(F32)]
    lrow = lax.broadcasted_iota(jnp.int32, (LANES, LANES), 0)
    lcol = lax.broadcasted_iota(jnp.int32, (LANES, LANES), 1)
    same_head = (lrow // H) == (lcol // H)
    head_ones = jnp.where(same_head, 1.0, 0.0).astype(F32)
    eye = lrow == lcol
    rt_, ct_ = lrow % C, lcol % C
    strict_m = [same_head & (ct_ < rt_), same_head & (ct_ > rt_)]
    incl_m = [same_head & (ct_ <= rt_), same_head & (ct_ >= rt_)]
    lane = lax.broadcasted_iota(jnp.int32, (1, LANES), 1)
    hmask = [lane < H, lane >= H]

    def stack_heads(x):
        return jnp.concatenate([jnp.where(hmask[0], x, 0.0), jnp.where(hmask[1], x, 0.0)], axis=0)

    kkw = kk_ref[...]
    kaw = ka_ref[...]
    rkw = rk_ref[...]

    def phase_a(ci, carry):
        units = []
        for cc in range(cb):
            c = ci * cb + cc
            rows = pl.ds(pl.multiple_of(c * C, C), C)
            r = r_ref[rows, :]
            k = k_ref[rows, :]
            v = v_ref[rows, :]
            lo = lo_ref[rows, :]
            kkraw = k * kkw
            ss = _dotp(kkraw * kkraw, head_ones, _NN, P['seg'])
            kkn = kkraw / jnp.maximum(jnp.sqrt(ss), 1e-12)
            bonus = jnp.zeros((C, LANES), F32)
            vm = stack_heads(v)
            for d in range(2):
                hw = lo[:, d * LORA:(d + 1) * LORA]
                ha = lo[:, 2 * LORA + d * LORA:2 * LORA + (d + 1) * LORA]
                wl = w0_ref[d] + _dot(jnp.tanh(hw), w2_ref[d])
                ld = -jnp.exp(-_softplus(-wl) - 0.5)
                alpha = _sigmoid(a0_ref[d] + _dot(ha, a2_ref[d]))
                kd = k * (1.0 + (alpha - 1.0) * kaw)
                bb = kkn * alpha
                bonus = bonus + _dotp(r * kd * rkw, head_ones, _NN, P['seg']) * v
                g_incl = _dotp_rhs(tri[d], ld, P['cum'])
                g_excl = g_incl - ld
                g_end = g_incl[C - 1:C, :] if d == 0 else g_incl[0:1, :]
                ginv = jnp.exp(-g_incl)
                dec_end = jnp.exp(g_end - g_incl)
                units.append(dict(
                    d=d, c=c, vm=vm,
                    at=stack_heads(-kkn * jnp.exp(g_excl)), rt=stack_heads(r * jnp.exp(g_incl)),
                    bt=stack_heads(bb * ginv), kt=stack_heads(kd * ginv),
                    bend=stack_heads(bb * dec_end), kend=stack_heads(kd * dec_end), gend=jnp.exp(g_end)))
            bonus_scr[rows, :] = bonus
            o_scr[rows, :] = jnp.zeros((C, LANES), F32)

        for u in units:
            ar = jnp.concatenate([u['at'], u['rt']], axis=0)
            bk = jnp.concatenate([u['bt'], u['kt']], axis=0)
            u['a'] = _dotp(ar, bk, _NT, P['a'])
        for u in units:
            a = u['a']
            st, ic = strict_m[u['d']], incl_m[u['d']]
            u['p'] = jnp.where(st, a[:C2, :C2], 0.0)
            u['a_ak'] = jnp.where(st, a[:C2, C2:], 0.0)
            u['a_rb'] = jnp.where(ic, a[C2:, :C2], 0.0)
            u['a_rk'] = jnp.where(ic, a[C2:, C2:], 0.0)
        for u in units:
            u['rhs'] = jnp.concatenate([u['at'], _dotp(u['a_ak'], u['vm'], _NN, P['post'])], axis=1)
            u['ov'] = _dotp(u['a_rk'], u['vm'], _NN, P['post'])
        for it in range(6):
            upd = [_dotp(u['p'], u['rhs'], _NN, P['neu']) for u in units]
            if it < 5:
                sq = [_dotp(u['p'], u['p'], _NN, P['neu']) for u in units]
            for i, u in enumerate(units):
                u['rhs'] = u['rhs'] + upd[i]
                if it < 5:
                    u['p'] = sq[i]
        xs = [_dotp(u['a_rb'], u['rhs'], _NN, P['post']) for u in units]
        ms = [_dotp(u['rhs'][:, :LANES], u['bend'], _TN, P['mn']) for u in units]
        ns = [_dotp(jnp.concatenate([u['rhs'][:, LANES:], u['vm']], axis=0),
                    jnp.concatenate([u['bend'], u['kend']], axis=0), _TN, P['mn']) for u in units]
        for i, u in enumerate(units):
            d, c = u['d'], u['c']
            rp = u['rt'] + xs[i][:, :LANES]
            ov = xs[i][:, LANES:] + u['ov']
            rp_scr[d, c] = rp[:C] + rp[C:]
            ov_scr[d, c] = ov[:C] + ov[C:]
            m_scr[d, c] = jnp.where(eye, jnp.broadcast_to(u['gend'], (LANES, LANES)), 0.0) + ms[i]
            n_scr[d, c] = ns[i]
        return carry

    lax.fori_loop(0, nC // cb, phase_a, 0)

    zero_blk = jnp.zeros((H, H), F32)
    for d in range(2):
        if has_init:
            s_scr[d] = jnp.concatenate([jnp.concatenate([init_ref[d, 0], zero_blk], axis=1),
                                        jnp.concatenate([zero_blk, init_ref[d, 1]], axis=1)], axis=0)
        else:
            s_scr[d] = jnp.zeros((LANES, LANES), F32)

    def phase_b(i, carry):
        cs = [i, nC - 1 - i]
        ss = [s_scr[d] for d in range(2)]
        os_ = [_dotp(rp_scr[d, cs[d]], ss[d], _NT, P['b_o']) for d in range(2)]
        sn = [_dotp(ss[d], m_scr[d, cs[d]], _NN, P['b_s']) for d in range(2)]
        for d in range(2):
            s_scr[d] = sn[d] + n_scr[d, cs[d]]
            rows = pl.ds(pl.multiple_of(cs[d] * C, C), C)
            o_scr[rows, :] = o_scr[rows, :] + (os_[d] + ov_scr[d, cs[d]])
        return carry

    lax.fori_loop(0, nC, phase_b, 0)

    for d in range(2):
        s = s_scr[d]
        sfin_ref[d, 0] = s[:H, :H]
        sfin_ref[d, 1] = s[H:, H:]

    lng = lng_ref[...]
    lnb = lnb_ref[...]

    rb = min(T, 256)

    def phase_c(c, carry):
        rows = pl.ds(pl.multiple_of(c * rb, rb), rb)
        o = o_scr[rows, :]
        st = _dotp(jnp.concatenate([o, o * o], axis=0), head_ones, _NN, P['gn']) * (1.0 / H)
        mu = st[:rb]
        var = st[rb:] - mu * mu
        on = (o - mu) * lax.rsqrt(var + GN_EPS) * lng + lnb
        y_ref[rows, :] = (on + bonus_scr[rows, :]) * _sigmoid(g_ref[rows, :])
        return carry

    lax.fori_loop(0, T // rb, phase_c, 0)


def _rwkv(proj, p, init, *, T, n_seq, row_blk0):
    nC = T // CHUNK
    npair = RWKV_HEADS // 2
    has_init = init is not None
    cb = lambda off: (lambda b, j: (b + row_blk0, off // LANES + j))
    vec = lambda b, j: (0, j)
    in_specs = [pl.BlockSpec((T, LANES), cb(C_R)),
                pl.BlockSpec((T, LANES), cb(C_K)),
                pl.BlockSpec((T, LANES), cb(C_V)),
                pl.BlockSpec((T, LANES), cb(C_G)),
                pl.BlockSpec((T, 4 * LORA), lambda b, j: (b + row_blk0, C_LORA // (4 * LORA))),
                pl.BlockSpec((2, 1, LANES), lambda b, j: (0, 0, j)),
                pl.BlockSpec((2, LORA, LANES), lambda b, j: (0, 0, j)),
                pl.BlockSpec((2, 1, LANES), lambda b, j: (0, 0, j)),
                pl.BlockSpec((2, LORA, LANES), lambda b, j: (0, 0, j)),
                pl.BlockSpec((1, LANES), vec), pl.BlockSpec((1, LANES), vec),
                pl.BlockSpec((1, LANES), vec), pl.BlockSpec((1, LANES), vec),
                pl.BlockSpec((1, LANES), vec)]
    args = [proj, proj, proj, proj, proj,
            p['rwkv_w0'].reshape(2, 1, RWKV_W), p['rwkv_w2'], p['rwkv_a0'].reshape(2, 1, RWKV_W), p['rwkv_a2'],
            p['rwkv_k_k'].reshape(1, RWKV_W), p['rwkv_k_a'].reshape(1, RWKV_W),
            p['rwkv_r_k'].reshape(1, RWKV_W), p['rwkv_lnx_g'].reshape(1, RWKV_W),
            p['rwkv_lnx_b'].reshape(1, RWKV_W)]
    st_spec = pl.BlockSpec((None, 2, 2, HEAD_DIM, HEAD_DIM), lambda b, j: (b, 0, j, 0, 0))
    if has_init:
        in_specs.append(st_spec)
        args.append(init)
    sc_o = lambda: pltpu.VMEM((2, nC, CHUNK, LANES), F32)
    sc_s = lambda: pltpu.VMEM((2, nC, LANES, LANES), F32)
    return pl.pallas_call(
        functools.partial(_rwkv_kernel, T=T, has_init=has_init, cb=min(RWKV_CB, nC)),
        grid=(n_seq, npair),
        in_specs=in_specs,
        out_specs=[pl.BlockSpec((T, LANES), lambda b, j: (b, j)), st_spec],
        out_shape=[jax.ShapeDtypeStruct((n_seq * T, RWKV_W), F32),
                   jax.ShapeDtypeStruct((n_seq, 2, RWKV_HEADS, HEAD_DIM, HEAD_DIM), F32)],
        scratch_shapes=[sc_o(), sc_o(), sc_s(), sc_s(),
                        pltpu.VMEM((T, LANES), F32), pltpu.VMEM((T, LANES), F32),
                        pltpu.VMEM((2, LANES, LANES), F32)],
        compiler_params=pltpu.CompilerParams(dimension_semantics=("arbitrary", "arbitrary"),
                                             vmem_limit_bytes=VMEM_LIMIT),
        name="rwkv_T%d" % T,
    )(*args)


def _dft_mats(L):
    f = np.arange(L, dtype=np.int64)
    ph = (np.outer(f, f) % (2 * L)).astype(np.float64) * (math.pi / L)
    return np.cos(ph), -np.sin(ph)


def _filter_features(L):
    t = np.linspace(0.0, 1.0, L, dtype=np.float32)[:, None]
    bands = (FILTER_EMB - 1) // 2
    t_res = np.arange(L, dtype=np.float32)[:, None]
    f = np.linspace(1e-4, bands - 1, bands, dtype=np.float32)[None, :]
    ang = (2.0 * math.pi * t_res * f / L).astype(np.float32)
    z = np.concatenate([t, np.cos(ang), np.sin(ang)], -1).astype(np.float32)
    zp = np.zeros((L, FILTER_HID), np.float32)
    zp[:, :FILTER_EMB] = z
    deltas = np.abs(np.linspace(HY_MIN_DECAY, HY_MAX_DECAY, HYENA_W, dtype=np.float32))
    return zp, t.astype(np.float32), deltas[None, :].astype(np.float32)


def _hyfilter_kernel(z_ref, t_ref, dl_ref, w1_ref, b1_ref, q1_ref, w2_ref, b2_ref, q2_ref, wc_ref, wa_ref,
                     hs_ref, hd_ref, knyq_ref, *, L):
    hm = jnp.sin(q1_ref[...] * (_dot3(z_ref[...], w1_ref[...]) + b1_ref[...]))
    hm = jnp.sin(q2_ref[...] * (_dot3(hm, w2_ref[...]) + b2_ref[...]))
    win = jnp.exp(-t_ref[...] * dl_ref[...])
    lag = lax.broadcasted_iota(jnp.int32, (L, 1), 0)
    alt = jnp.where(lag % 2 == 0, 1.0, -1.0).astype(F32)
    hc = _dot3(hm, wc_ref[...]) * win
    ha = jnp.where(lag == 0, 0.0, _dot3(hm, wa_ref[...]) * win)
    hs_ref[...] = hc + ha
    hd_ref[...] = hc - ha
    knyq_ref[...] = jnp.sum((hc + ha) * alt, axis=0, keepdims=True) * (1.0 / (2 * L))


def _hyspec_kernel(cm_ref, sm_ref, hs_ref, hd_ref, kre_ref, kim_ref, *, L, tf):
    f = pl.program_id(2) * tf + lax.broadcasted_iota(jnp.int32, (tf, 1), 0)
    wf = jnp.where(f == 0, 1.0, 2.0).astype(F32) * (1.0 / (2 * L))
    kre_ref[...] = _dot3(cm_ref[...], hs_ref[...]) * wf
    kim_ref[...] = _dot3(sm_ref[...], hd_ref[...]) * wf


def _hyena_filter(p, L, cm, sm, cw=128):
    zp, t, dl = _filter_features(L)
    w1 = jnp.zeros((FILTER_HID, FILTER_HID), F32).at[:FILTER_EMB].set(p['hy_f_w1'])
    r2 = lambda a: a.reshape(1, -1)
    ncb = HYENA_W // cw
    full = lambda shape: pl.BlockSpec(shape, lambda n, j: (0,) * len(shape))
    shp = jax.ShapeDtypeStruct((2, L, HYENA_W), F32)
    hs, hd, knyq = pl.pallas_call(
        functools.partial(_hyfilter_kernel, L=L),
        grid=(2, ncb),
        in_specs=[full((L, FILTER_HID)), full((L, 1)),
                  pl.BlockSpec((1, cw), lambda n, j: (0, j)),
                  full((FILTER_HID, FILTER_HID)), full((1, FILTER_HID)), full((1, FILTER_HID)),
                  full((FILTER_HID, FILTER_HID)), full((1, FILTER_HID)), full((1, FILTER_HID)),
                  pl.BlockSpec((FILTER_HID, cw), lambda n, j: (0, n * 2 * ncb + j)),
                  pl.BlockSpec((FILTER_HID, cw), lambda n, j: (0, n * 2 * ncb + ncb + j))],
        out_specs=[pl.BlockSpec((None, L, cw), lambda n, j: (n, 0, j)),
                   pl.BlockSpec((None, L, cw), lambda n, j: (n, 0, j)),
                   pl.BlockSpec((None, 1, cw), lambda n, j: (n, 0, j))],
        out_shape=[shp, shp, jax.ShapeDtypeStruct((2, 1, HYENA_W), F32)],
        compiler_params=pltpu.CompilerParams(dimension_semantics=("arbitrary", "arbitrary"),
                                             vmem_limit_bytes=VMEM_LIMIT),
        name="hyena_filter_L%d" % L,
    )(jnp.asarray(zp), jnp.asarray(t), jnp.asarray(dl), w1, r2(p['hy_f_b1']), r2(p['hy_f_freq1']),
      p['hy_f_w2'], r2(p['hy_f_b2']), r2(p['hy_f_freq2']), p['hy_f_w3'], p['hy_f_w3'])
    tf = min(256, L)
    sw = 256
    kre, kim = pl.pallas_call(
        functools.partial(_hyspec_kernel, L=L, tf=tf),
        grid=(2, HYENA_W // sw, L // tf),
        in_specs=[pl.BlockSpec((tf, L), lambda n, j, i: (i, 0)),
                  pl.BlockSpec((tf, L), lambda n, j, i: (i, 0)),
                  pl.BlockSpec((None, L, sw), lambda n, j, i: (n, 0, j)),
                  pl.BlockSpec((None, L, sw), lambda n, j, i: (n, 0, j))],
        out_specs=[pl.BlockSpec((None, tf, sw), lambda n, j, i: (n, i, j)),
                   pl.BlockSpec((None, tf, sw), lambda n, j, i: (n, i, j))],
        out_shape=[shp, shp],
        compiler_params=pltpu.CompilerParams(dimension_semantics=("arbitrary",) * 3,
                                             vmem_limit_bytes=VMEM_LIMIT),
        name="hyena_spectrum_L%d" % L,
    )(cm, sm, hs, hd)
    return kre, kim, knyq


def _hyena_kernel(v_ref, x1_ref, x2_ref, sw_ref, bias_ref, cm_ref, sm_ref, kre_ref, kim_ref, knyq_ref,
                  o_ref, zb_scr, zf_scr, xs_scr, yre_scr, yim_scr, *, L, rb):
    nrb = L // rb
    t = lax.broadcasted_iota(jnp.int32, (L, 1), 0)
    first = t == 0
    last = t == L - 1
    alt_all = jnp.where(t % 2 == 0, 1.0, -1.0).astype(F32)
    tb = lax.broadcasted_iota(jnp.int32, (rb, 1), 0)
    alt_blk = jnp.where(tb % 2 == 0, 1.0, -1.0).astype(F32)
    sw = sw_ref[...]

    def short(ref, s):
        u = ref[...]
        prev = jnp.where(first, 0.0, pltpu.roll(u, 1, 0))
        nxt = jnp.where(last, 0.0, pltpu.roll(u, L - 1, 0))
        return prev * sw[0, s][None, :] + u * sw[1, s][None, :] + nxt * sw[2, s][None, :]

    z0 = short(v_ref, 0)
    zf_scr[...] = z0
    zb_scr[...] = z0.astype(BF16)
    unyq = jnp.sum(z0 * alt_all, axis=0, keepdims=True)

    for n in range(2):
        xs_scr[...] = short(x1_ref if n == 0 else x2_ref, n + 1)

        def fwd(fb, carry):
            rows = pl.ds(pl.multiple_of(fb * rb, rb), rb)
            zb = zb_scr[...]
            ure = jnp.dot(cm_ref[rows, :], zb, preferred_element_type=F32)
            uim = jnp.dot(sm_ref[rows, :], zb, preferred_element_type=F32)
            kre = kre_ref[n, rows, :]
            kim = kim_ref[n, rows, :]
            yre_scr[rows, :] = (ure * kre - uim * kim).astype(BF16)
            yim_scr[rows, :] = (ure * kim + uim * kre).astype(BF16)
            return carry

        lax.fori_loop(0, nrb, fwd, 0)
        ynyq = unyq * knyq_ref[n]
        bias = bias_ref[n]

        def inv(ib, acc):
            rows = pl.ds(pl.multiple_of(ib * rb, rb), rb)
            y = (jnp.dot(cm_ref[rows, :], yre_scr[...], preferred_element_type=F32)
                 + jnp.dot(sm_ref[rows, :], yim_scr[...], preferred_element_type=F32))
            y = y + alt_blk * ynyq + zf_scr[rows, :] * bias
            znew = xs_scr[rows, :] * y
            if n == 0:
                zf_scr[rows, :] = znew
                zb_scr[rows, :] = znew.astype(BF16)
                acc = acc + jnp.sum(znew * alt_blk, axis=0, keepdims=True)
            else:
                o_ref[rows, :] = znew
            return acc

        unyq = lax.fori_loop(0, nrb, inv, jnp.zeros_like(unyq))


def _hyena(proj, p, filt, cm, sm, *, L, n_seq, row_blk0, cw=256):
    kre, kim, knyq = filt
    ncb = HYENA_W // cw
    cb = lambda s: (lambda j, b: (b + row_blk0, (C_HY + s * HYENA_W) // cw + j))
    once = pl.Buffered(1)
    sw = p['hy_short_w'].reshape(3, 3, HYENA_W)
    rb = min(256, L)
    return pl.pallas_call(
        functools.partial(_hyena_kernel, L=L, rb=rb),
        grid=(ncb, n_seq),
        in_specs=[pl.BlockSpec((L, cw), cb(0)), pl.BlockSpec((L, cw), cb(1)), pl.BlockSpec((L, cw), cb(2)),
                  pl.BlockSpec((3, 3, cw), lambda j, b: (0, 0, j)),
                  pl.BlockSpec((2, 1, cw), lambda j, b: (0, 0, j)),
                  pl.BlockSpec((L, L), lambda j, b: (0, 0), pipeline_mode=once),
                  pl.BlockSpec((L, L), lambda j, b: (0, 0), pipeline_mode=once),
                  pl.BlockSpec((2, L, cw), lambda j, b: (0, 0, j), pipeline_mode=once),
                  pl.BlockSpec((2, L, cw), lambda j, b: (0, 0, j), pipeline_mode=once),
                  pl.BlockSpec((2, 1, cw), lambda j, b: (0, 0, j))],
        out_specs=pl.BlockSpec((L, cw), lambda j, b: (b, j)),
        out_shape=jax.ShapeDtypeStruct((n_seq * L, HYENA_W), F32),
        scratch_shapes=[pltpu.VMEM((L, cw), BF16), pltpu.VMEM((L, cw), F32), pltpu.VMEM((L, cw), F32),
                        pltpu.VMEM((L, cw), BF16), pltpu.VMEM((L, cw), BF16)],
        compiler_params=pltpu.CompilerParams(dimension_semantics=("arbitrary", "arbitrary"),
                                             vmem_limit_bytes=VMEM_LIMIT),
        name="hyena_L%d" % L,
    )(proj, proj, proj, sw, p['hy_bias'].reshape(2, 1, HYENA_W), cm.astype(BF16), sm.astype(BF16),
      kre, kim, knyq)


def _softmax_av(s, v, snk):
    m = jnp.maximum(jnp.max(s, axis=-1, keepdims=True), snk)
    pe = jnp.exp(s - m)
    den = jnp.sum(pe, axis=-1, keepdims=True) + jnp.exp(snk - m)
    return _dot(pe, v) / den


def _ctx_attn_kernel(q_ref, k_ref, v_ref, sink_ref, o_ref, *, S):
    scale = HEAD_DIM ** -0.5
    outs = []
    for h in range(ATTN_KV_HEADS):
        kh = k_ref[:, h * HEAD_DIM:(h + 1) * HEAD_DIM]
        vh = v_ref[:, h * HEAD_DIM:(h + 1) * HEAD_DIM]
        for g in range(GQA_GROUP):
            a = h * GQA_GROUP + g
            qa = q_ref[:, a * HEAD_DIM:(a + 1) * HEAD_DIM]
            s = _dot_nt(qa, kh) * scale
            snk = jnp.broadcast_to(sink_ref[0:1, a:a + 1], (S, 1))
            outs.append(_softmax_av(s, vh, snk))
    o_ref[...] = jnp.concatenate(outs, axis=1)


def _ctx_attention(proj, sink, *, S, n_seq):
    return pl.pallas_call(
        functools.partial(_ctx_attn_kernel, S=S),
        grid=(n_seq,),
        in_specs=[pl.BlockSpec((S, ATTN_W), lambda b: (b, C_Q // ATTN_W)),
                  pl.BlockSpec((S, KV_W), lambda b: (b, C_AK // KV_W)),
                  pl.BlockSpec((S, KV_W), lambda b: (b, C_AV // KV_W)),
                  pl.BlockSpec((1, LANES), lambda b: (0, 0))],
        out_specs=pl.BlockSpec((S, ATTN_W), lambda b: (b, 0)),
        out_shape=jax.ShapeDtypeStruct((n_seq * S, ATTN_W), F32),
        compiler_params=pltpu.CompilerParams(dimension_semantics=("arbitrary",),
                                             vmem_limit_bytes=VMEM_LIMIT),
        name="ctx_attention",
    )(proj, proj, proj, sink)


def _rope(x, cos, sin_signed, first_half):
    n = x.shape[-1]
    partner = jnp.where(first_half, pltpu.roll(x, n - HEAD_DIM // 2, 1), pltpu.roll(x, HEAD_DIM // 2, 1))
    return x * cos + partner * sin_signed


def _lat_attn_kernel(q_ref, kp_ref, kc_ref, kn_ref, vp_ref, vc_ref, vn_ref, ck_ref, cv_ref,
                     cq_ref, sq_ref, cp_ref, sp_ref, cc_ref, sc_ref, cn_ref, sn_ref, sink_ref, o_ref, *, T):
    i = pl.program_id(1)
    scale = HEAD_DIM ** -0.5
    fh_q = (lax.broadcasted_iota(jnp.int32, (1, ATTN_W), 1) % HEAD_DIM) < HEAD_DIM // 2
    fh_k = (lax.broadcasted_iota(jnp.int32, (1, KV_W), 1) % HEAD_DIM) < HEAD_DIM // 2
    q = _rope(q_ref[...], cq_ref[...], sq_ref[...], fh_q)
    kl = jnp.concatenate([_rope(kp_ref[...], cp_ref[...], sp_ref[...], fh_k),
                          _rope(kc_ref[...], cc_ref[...], sc_ref[...], fh_k),
                          _rope(kn_ref[...], cn_ref[...], sn_ref[...], fh_k)], axis=0)
    vl = jnp.concatenate([vp_ref[...], vc_ref[...], vn_ref[...]], axis=0)
    qpos = i * BLOCK + lax.broadcasted_iota(jnp.int32, (BLOCK, 3 * BLOCK), 0)
    kpos = (i - 1) * BLOCK + lax.broadcasted_iota(jnp.int32, (BLOCK, 3 * BLOCK), 1)
    ok = (jnp.abs(qpos - kpos) <= WINDOW) & (kpos >= 0) & (kpos < T)
    outs = []
    for h in range(ATTN_KV_HEADS):
        hs = slice(h * HEAD_DIM, (h + 1) * HEAD_DIM)
        klh, vlh = kl[:, hs], vl[:, hs]
        kch, vch = ck_ref[:, hs], cv_ref[:, hs]
        for g in range(GQA_GROUP):
            a = h * GQA_GROUP + g
            qa = q[:, a * HEAD_DIM:(a + 1) * HEAD_DIM]
            s_loc = jnp.where(ok, _dot_nt(qa, klh) * scale, -jnp.inf)
            s_ctx = _dot_nt(qa, kch) * scale
            snk = jnp.broadcast_to(sink_ref[0:1, a:a + 1], (BLOCK, 1))
            m = jnp.maximum(jnp.maximum(jnp.max(s_loc, axis=-1, keepdims=True),
                                        jnp.max(s_ctx, axis=-1, keepdims=True)), snk)
            p_loc = jnp.exp(s_loc - m)
            p_ctx = jnp.exp(s_ctx - m)
            den = (jnp.sum(p_loc, axis=-1, keepdims=True) + jnp.sum(p_ctx, axis=-1, keepdims=True)
                   + jnp.exp(snk - m))
            outs.append((_dot(p_loc, vlh) + _dot(p_ctx, vch)) / den)
    o_ref[...] = jnp.concatenate(outs, axis=1)


def _rope_tables(T):
    rows = T // GRID_W
    row = np.repeat(np.arange(rows), GRID_W).astype(np.float32)
    col = np.tile(np.arange(GRID_W), rows).astype(np.float32)
    nf = HEAD_DIM // 4
    inv = (ROPE_BASE ** (-np.arange(nf, dtype=np.float32) / nf)).astype(np.float32)
    ang = np.concatenate([row[:, None] * inv, col[:, None] * inv], -1).astype(np.float32)
    cos = np.cos(ang).astype(np.float32)
    sin = np.sin(ang).astype(np.float32)
    cos_h = np.concatenate([cos, cos], -1)
    sin_h = np.concatenate([-sin, sin], -1)
    return cos_h, sin_h


def _lat_attention(proj, cache_k, cache_v, sink, *, T, n_seq, row_blk0):
    nb = T // BLOCK
    P = cache_k.shape[1]
    cos_h, sin_h = _rope_tables(T)
    cq = jnp.asarray(np.tile(cos_h, (1, ATTN_HEADS)))
    sq = jnp.asarray(np.tile(sin_h, (1, ATTN_HEADS)))
    ck = jnp.asarray(np.tile(cos_h, (1, ATTN_KV_HEADS)))
    sk = jnp.asarray(np.tile(sin_h, (1, ATTN_KV_HEADS)))
    r0 = row_blk0 * nb
    prv = lambda i: jnp.maximum(i - 1, 0)
    nxt = lambda i: jnp.minimum(i + 1, nb - 1)
    kcol, vcol = C_AK // KV_W, C_AV // KV_W
    kv = lambda col, f: pl.BlockSpec((BLOCK, KV_W), lambda b, i: (r0 + b * nb + f(i), col))
    tab = lambda w, f: pl.BlockSpec((BLOCK, w), lambda b, i: (f(i), 0))
    cur = lambda i: i
    return pl.pallas_call(
        functools.partial(_lat_attn_kernel, T=T),
        grid=(n_seq, nb),
        in_specs=[pl.BlockSpec((BLOCK, ATTN_W), lambda b, i: (r0 + b * nb + i, C_Q // ATTN_W)),
                  kv(kcol, prv), kv(kcol, cur), kv(kcol, nxt),
                  kv(vcol, prv), kv(vcol, cur), kv(vcol, nxt),
                  pl.BlockSpec((None, P, KV_W), lambda b, i: (b, 0, 0)),
                  pl.BlockSpec((None, P, KV_W), lambda b, i: (b, 0, 0)),
                  tab(ATTN_W, cur), tab(ATTN_W, cur),
                  tab(KV_W, prv), tab(KV_W, prv), tab(KV_W, cur), tab(KV_W, cur), tab(KV_W, nxt), tab(KV_W, nxt),
                  pl.BlockSpec((1, LANES), lambda b, i: (0, 0))],
        out_specs=pl.BlockSpec((BLOCK, ATTN_W), lambda b, i: (b * nb + i, 0)),
        out_shape=jax.ShapeDtypeStruct((n_seq * T, ATTN_W), F32),
        compiler_params=pltpu.CompilerParams(dimension_semantics=("arbitrary", "arbitrary"),
                                             vmem_limit_bytes=VMEM_LIMIT),
        name="latent_attention",
    )(proj, proj, proj, proj, proj, proj, proj, cache_k, cache_v, cq, sq, ck, sk, ck, sk, ck, sk, sink)


def _layer_norm(y, g, b):
    mu = jnp.mean(y, axis=-1, keepdims=True)
    d = y - mu
    var = jnp.mean(d * d, axis=-1, keepdims=True)
    return d * lax.rsqrt(var + LN_EPS) * g + b


def _outproj_kernel(ac_ref, bc_ref, cc_ref, al_ref, bl_ref, cl_ref, x_ref, mod_ref, w_ref,
                    g_ref, be_ref, x1_ref, h2_ref, m_scr, *, nsub, n_ctx_tiles):
    r1, r2 = RWKV_W, RWKV_W + HYENA_W

    def mix(a_ref, b_ref, c_ref):
        m_scr[...] = (jnp.dot(a_ref[...].astype(BF16), w_ref[0:r1, :], preferred_element_type=F32)
                      + jnp.dot(b_ref[...].astype(BF16), w_ref[r1:r2, :], preferred_element_type=F32)
                      + jnp.dot(c_ref[...].astype(BF16), w_ref[r2:, :], preferred_element_type=F32))

    pl.when(pl.program_id(0) < n_ctx_tiles)(lambda: mix(ac_ref, bc_ref, cc_ref))
    pl.when(pl.program_id(0) >= n_ctx_tiles)(lambda: mix(al_ref, bl_ref, cl_ref))
    m = m_scr[...]
    for s in range(nsub):
        rows = slice(s * MOD_BLK, (s + 1) * MOD_BLK)
        gate1 = mod_ref[s, 2:3, :]
        shift2 = mod_ref[s, 3:4, :]
        scale2 = mod_ref[s, 4:5, :]
        x1 = _layer_norm(DN_ALPHA * x_ref[rows, :] + gate1 * m[rows, :], g_ref[...], be_ref[...])
        x1_ref[rows, :] = x1
        h2_ref[rows, :] = (x1 * (1.0 + scale2) + shift2).astype(BF16)


def _out_projection(mix_ctx, mix_lat, x, mod, w_out, l, g, b, tm=256):
    N = x.shape[0]
    nsub = tm // MOD_BLK
    nct = mix_ctx[0].shape[0] // tm
    nlt = mix_lat[0].shape[0] // tm
    row = lambda w: pl.BlockSpec((tm, w), lambda i: (i, 0))
    ctx = lambda w: pl.BlockSpec((tm, w), lambda i: (jnp.minimum(i, nct - 1), 0))
    lat = lambda w: pl.BlockSpec((tm, w), lambda i: (jnp.clip(i - nct, 0, nlt - 1), 0))
    full = lambda r: pl.BlockSpec((r, D_MODEL), lambda i: (0, 0))
    return pl.pallas_call(
        functools.partial(_outproj_kernel, nsub=nsub, n_ctx_tiles=nct),
        grid=(N // tm,),
        in_specs=[ctx(RWKV_W), ctx(HYENA_W), ctx(ATTN_W), lat(RWKV_W), lat(HYENA_W), lat(ATTN_W), row(D_MODEL),
                  pl.BlockSpec((nsub, 6, D_MODEL), lambda i: (i, 0, 0)),
                  pl.BlockSpec((None, D_MODEL, D_MODEL), lambda i: (l, 0, 0)), full(1), full(1)],
        out_specs=[row(D_MODEL), row(D_MODEL)],
        out_shape=[jax.ShapeDtypeStruct((N, D_MODEL), F32), jax.ShapeDtypeStruct((N, D_MODEL), BF16)],
        scratch_shapes=[pltpu.VMEM((tm, D_MODEL), F32)],
        compiler_params=pltpu.CompilerParams(dimension_semantics=("arbitrary",),
                                             vmem_limit_bytes=VMEM_LIMIT),
        name="out_projection",
    )(*mix_ctx, *mix_lat, x, mod, w_out, g.reshape(1, D_MODEL), b.reshape(1, D_MODEL))


HALO = 16


def _ffn_kernel(hp_ref, h_ref, hn_ref, x1_ref, mod_ref, wa_ref, wb_ref, cwa_ref, cwb_ref, wd_ref, g_ref, be_ref,
                o_ref, hx_scr, ua_scr, ub_scr, acc_scr, *, tm, tf, nsub, n_ctx, s_ctx, s_lat):
    i = pl.program_id(0)
    j = pl.program_id(1)

    @pl.when(j == 0)
    def _():
        hx_scr[0:HALO, :] = hp_ref[...]
        hx_scr[HALO:HALO + tm, :] = h_ref[...]
        hx_scr[HALO + tm:, :] = hn_ref[...]
        acc_scr[...] = jnp.zeros_like(acc_scr)

    hx = hx_scr[...]
    ua_scr[...] = jnp.dot(hx, wa_ref[...], preferred_element_type=F32)
    ub_scr[...] = jnp.dot(hx, wb_ref[...], preferred_element_type=F32)
    grow = i * tm + lax.broadcasted_iota(jnp.int32, (tm, 1), 0)
    pos = jnp.where(grow < n_ctx, grow % s_ctx, (grow - n_ctx) % s_lat)
    slen = jnp.where(grow < n_ctx, s_ctx, s_lat)
    mprev = jnp.where(pos == 0, 0.0, 1.0).astype(F32)
    mnext = jnp.where(pos == slen - 1, 0.0, 1.0).astype(F32)

    def conv(u_scr, cw_ref):
        up = u_scr[HALO - 1:HALO - 1 + tm, :] * mprev
        uc = u_scr[HALO:HALO + tm, :]
        un = u_scr[HALO + 1:HALO + 1 + tm, :] * mnext
        return up * cw_ref[0:1, :] + uc * cw_ref[1:2, :] + un * cw_ref[2:3, :]

    a = conv(ua_scr, cwa_ref)
    b = conv(ub_scr, cwb_ref)
    act = (a * _sigmoid(a) * b).astype(BF16)
    acc_scr[...] += jnp.dot(act, wd_ref[...], preferred_element_type=F32)

    @pl.when(j == pl.num_programs(1) - 1)
    def _():
        for s in range(nsub):
            rows = slice(s * MOD_BLK, (s + 1) * MOD_BLK)
            gate2 = mod_ref[s, 5:6, :]
            o_ref[rows, :] = _layer_norm(DN_ALPHA * x1_ref[rows, :] + gate2 * acc_scr[rows, :],
                                         g_ref[...], be_ref[...])


def _ffn(h2, x1, mod, wa, wb, cw, wd, l, g, b, *, n_ctx, s_ctx, s_lat, tm=512, tf=512):
    N = x1.shape[0]
    nsub = tm // MOD_BLK
    nh = tm // HALO
    last_h = N // HALO - 1
    pad = ((0, 0), (0, D_FF_PAD - D_FF))
    cwa, cwb = jnp.pad(cw[:, :D_FF], pad), jnp.pad(cw[:, D_FF:], pad)
    return pl.pallas_call(
        functools.partial(_ffn_kernel, tm=tm, tf=tf, nsub=nsub, n_ctx=n_ctx, s_ctx=s_ctx, s_lat=s_lat),
        grid=(N // tm, D_FF_PAD // tf),
        in_specs=[pl.BlockSpec((HALO, D_MODEL), lambda i, j: (jnp.maximum(i * nh - 1, 0), 0)),
                  pl.BlockSpec((tm, D_MODEL), lambda i, j: (i, 0)),
                  pl.BlockSpec((HALO, D_MODEL), lambda i, j: (jnp.minimum((i + 1) * nh, last_h), 0)),
                  pl.BlockSpec((tm, D_MODEL), lambda i, j: (i, 0)),
                  pl.BlockSpec((nsub, 6, D_MODEL), lambda i, j: (i, 0, 0)),
                  pl.BlockSpec((None, D_MODEL, tf), lambda i, j: (l, 0, j)),
                  pl.BlockSpec((None, D_MODEL, tf), lambda i, j: (l, 0, j)),
                  pl.BlockSpec((3, tf), lambda i, j: (0, j)),
                  pl.BlockSpec((3, tf), lambda i, j: (0, j)),
                  pl.BlockSpec((None, tf, D_MODEL), lambda i, j: (l, j, 0)),
                  pl.BlockSpec((1, D_MODEL), lambda i, j: (0, 0)),
                  pl.BlockSpec((1, D_MODEL), lambda i, j: (0, 0))],
        out_specs=pl.BlockSpec((tm, D_MODEL), lambda i, j: (i, 0)),
        out_shape=jax.ShapeDtypeStruct((N, D_MODEL), F32),
        scratch_shapes=[pltpu.VMEM((tm + 2 * HALO, D_MODEL), BF16),
                        pltpu.VMEM((tm + 2 * HALO, tf), F32),
                        pltpu.VMEM((tm + 2 * HALO, tf), F32),
                        pltpu.VMEM((tm, D_MODEL), F32)],
        compiler_params=pltpu.CompilerParams(dimension_semantics=("arbitrary", "arbitrary"),
                                             vmem_limit_bytes=VMEM_LIMIT),
        name="conv_ffn",
    )(h2, h2, h2, x1, mod, wa, wb, cwa, cwb, wd, g.reshape(1, D_MODEL), b.reshape(1, D_MODEL))


def _prep_up_kernel(a_ref, b_ref, wa_ref, wb_ref):
    zeros = jnp.zeros((a_ref.shape[0], D_FF_PAD - D_FF), BF16)
    wa_ref[:, :D_FF] = a_ref[...].astype(BF16)
    wa_ref[:, D_FF:] = zeros
    wb_ref[:, :D_FF] = b_ref[...].astype(BF16)
    wb_ref[:, D_FF:] = zeros


def _prep_down_kernel(w_ref, o_ref, *, n_real):
    keep = pl.program_id(1) < n_real
    o_ref[...] = jnp.where(keep, w_ref[...], 0.0).astype(BF16)


def _prep_in_kernel(w_ref, w1_ref, a1_ref, o_ref):
    n = w_ref.shape[-1]
    o_ref[:, :n] = w_ref[...].astype(BF16)
    o_ref[:, n:] = jnp.concatenate([w1_ref[0], w1_ref[1], a1_ref[0], a1_ref[1]], axis=1).astype(BF16)


def _cast_kernel(w_ref, o_ref):
    o_ref[...] = w_ref[...].astype(BF16)


def _prep_weights(w_in, rwkv_w1, rwkv_a1, w_out, ffn_w_up, ffn_w_down):
    L, D = w_in.shape[0], D_MODEL
    cp = lambda n: pltpu.CompilerParams(dimension_semantics=("arbitrary",) * n, vmem_limit_bytes=VMEM_LIMIT)
    tr = 128
    half = jax.ShapeDtypeStruct((L, D, D_FF_PAD), BF16)
    wa, wb = pl.pallas_call(
        _prep_up_kernel,
        grid=(L, D // tr),
        in_specs=[pl.BlockSpec((None, tr, D_FF), lambda l, i: (l, i, 0)),
                  pl.BlockSpec((None, tr, D_FF), lambda l, i: (l, i, 1))],
        out_specs=[pl.BlockSpec((None, tr, D_FF_PAD), lambda l, i: (l, i, 0))] * 2,
        out_shape=[half, half], compiler_params=cp(2), name="prep_ffn_up",
    )(ffn_w_up, ffn_w_up)
    nr = D_FF // LANES
    wd = pl.pallas_call(
        functools.partial(_prep_down_kernel, n_real=nr),
        grid=(L, D_FF_PAD // LANES),
        in_specs=[pl.BlockSpec((None, LANES, D), lambda l, j: (l, jnp.minimum(j, nr - 1), 0))],
        out_specs=pl.BlockSpec((None, LANES, D), lambda l, j: (l, j, 0)),
        out_shape=jax.ShapeDtypeStruct((L, D_FF_PAD, D), BF16), compiler_params=cp(2), name="prep_ffn_down",
    )(ffn_w_down)
    tr2 = 256
    n_in = w_in.shape[-1]
    w_ext = pl.pallas_call(
        _prep_in_kernel,
        grid=(L, D // tr2),
        in_specs=[pl.BlockSpec((None, tr2, n_in), lambda l, i: (l, i, 0)),
                  pl.BlockSpec((None, 2, tr2, LORA), lambda l, i: (l, 0, i, 0)),
                  pl.BlockSpec((None, 2, tr2, LORA), lambda l, i: (l, 0, i, 0))],
        out_specs=pl.BlockSpec((None, tr2, IN_W_EXT), lambda l, i: (l, i, 0)),
        out_shape=jax.ShapeDtypeStruct((L, D, IN_W_EXT), BF16), compiler_params=cp(2), name="prep_in",
    )(w_in, rwkv_w1, rwkv_a1)
    wo = pl.pallas_call(
        _cast_kernel,
        grid=(L, D // tr2),
        in_specs=[pl.BlockSpec((None, tr2, D), lambda l, i: (l, i, 0))],
        out_specs=pl.BlockSpec((None, tr2, D), lambda l, i: (l, i, 0)),
        out_shape=jax.ShapeDtypeStruct((L, D, D), BF16), compiler_params=cp(2), name="prep_out",
    )(w_out)
    return w_ext, wo, wa, wb, wd


def kernel(x_prompt, x_sample, c, state_rwkv, cache_k, cache_v, c_ctx, w_mod, b_mod, w_in, rwkv_w0, rwkv_w1, rwkv_w2, rwkv_a0, rwkv_a1, rwkv_a2, rwkv_k_k, rwkv_k_a, rwkv_r_k, rwkv_lnx_g, rwkv_lnx_b, hy_short_w, hy_f_w1, hy_f_b1, hy_f_freq1, hy_f_w2, hy_f_b2, hy_f_freq2, hy_f_w3, hy_bias, attn_sink, w_out, ln1_g, ln1_b, ffn_w_up, ffn_conv_w, ffn_w_down, ln2_g, ln2_b):
    params = dict(rwkv_w0=rwkv_w0, rwkv_w2=rwkv_w2, rwkv_a0=rwkv_a0, rwkv_a2=rwkv_a2,
                  rwkv_k_k=rwkv_k_k, rwkv_k_a=rwkv_k_a, rwkv_r_k=rwkv_r_k,
                  rwkv_lnx_g=rwkv_lnx_g, rwkv_lnx_b=rwkv_lnx_b,
                  hy_short_w=hy_short_w, hy_f_w1=hy_f_w1, hy_f_b1=hy_f_b1, hy_f_freq1=hy_f_freq1,
                  hy_f_w2=hy_f_w2, hy_f_b2=hy_f_b2, hy_f_freq2=hy_f_freq2, hy_f_w3=hy_f_w3,
                  hy_bias=hy_bias)
    L = w_in.shape[0]
    Bc, Sc, D = x_prompt.shape
    Bl, Tl, _ = x_sample.shape
    P = cache_k.shape[2]
    n_ctx = Bc * Sc
    n_lat = Bl * Tl
    N = n_ctx + n_lat
    assert Sc % MOD_BLK == 0 and Tl % MOD_BLK == 0 and n_ctx % Tl == 0

    n_cond = 1 + Bl
    cond = jnp.zeros((16, D), F32).at[0].set(c_ctx).at[1:n_cond].set(c)
    mod_all = _modulation(cond, w_mod, b_mod)
    mod_ctx = jnp.broadcast_to(mod_all[:, 0:1, :], (L, n_ctx // MOD_BLK, 6 * D))
    mod_lat = jnp.broadcast_to(mod_all[:, 1:n_cond, None, :], (L, Bl, Tl // MOD_BLK, 6 * D))
    mod_blk = jnp.concatenate([mod_ctx, mod_lat.reshape(L, n_lat // MOD_BLK, 6 * D)], axis=1)
    mod_blk = mod_blk.reshape(L, N // MOD_BLK, 6, D)

    cm_c, sm_c = (jnp.asarray(m, F32) for m in _dft_mats(Sc))
    cm_l, sm_l = (jnp.asarray(m, F32) for m in _dft_mats(Tl))
    sink_pad = jnp.zeros((L, 1, LANES), F32).at[:, 0, :ATTN_HEADS].set(attn_sink)

    w_ext, wo, wa, wb, wd = _prep_weights(w_in, rwkv_w1, rwkv_a1, w_out, ffn_w_up, ffn_w_down)
    x = jnp.concatenate([x_prompt.reshape(n_ctx, D), x_sample.reshape(n_lat, D)], axis=0)
    states, keys, vals = [], [], []
    for l in range(L):
        p = {n: a[l] for n, a in params.items()}
        proj = _in_projection(x, mod_blk[l], w_ext, l)
        keys.append(proj[:n_ctx, C_AK:C_AK + KV_W].reshape(Bc, Sc, ATTN_KV_HEADS, HEAD_DIM))
        vals.append(proj[:n_ctx, C_AV:C_AV + KV_W].reshape(Bc, Sc, ATTN_KV_HEADS, HEAD_DIM))

        ya_c, s_fin = _rwkv(proj, p, None, T=Sc, n_seq=Bc, row_blk0=0)
        ya_l, _ = _rwkv(proj, p, state_rwkv[:, l], T=Tl, n_seq=Bl, row_blk0=n_ctx // Tl)
        states.append(s_fin)

        yb_c = _hyena(proj, p, _hyena_filter(p, Sc, cm_c, sm_c), cm_c, sm_c, L=Sc, n_seq=Bc, row_blk0=0)
        yb_l = _hyena(proj, p, _hyena_filter(p, Tl, cm_l, sm_l), cm_l, sm_l, L=Tl, n_seq=Bl,
                      row_blk0=n_ctx // Tl)

        yc_c = _ctx_attention(proj, sink_pad[l], S=Sc, n_seq=Bc)
        yc_l = _lat_attention(proj, cache_k[:, l].reshape(Bl, P, KV_W), cache_v[:, l].reshape(Bl, P, KV_W),
                              sink_pad[l], T=Tl, n_seq=Bl, row_blk0=n_ctx // Tl)

        x1, h2 = _out_projection((ya_c, yb_c, yc_c), (ya_l, yb_l, yc_l), x, mod_blk[l],
                                 wo, l, ln1_g[l], ln1_b[l])
        x = _ffn(h2, x1, mod_blk[l], wa, wb, ffn_conv_w[l], wd, l, ln2_g[l], ln2_b[l],
                 n_ctx=n_ctx, s_ctx=Sc, s_lat=Tl)

    y_prompt = x[:n_ctx].reshape(Bc, Sc, D)
    y_sample = x[n_ctx:].reshape(Bl, Tl, D)
    new_state = jnp.stack(states, 1)
    new_k = jnp.stack(keys, 1)
    new_v = jnp.stack(vals, 1)
    return (y_prompt, y_sample, new_state, new_k, new_v)
```

```python
import functools
import math

import numpy as np
import jax
import jax.numpy as jnp
from jax import lax
from jax.experimental import pallas as pl
from jax.experimental.pallas import tpu as pltpu

F32 = jnp.float32
BF16 = jnp.bfloat16

D_MODEL = 2048
DEPTH = 4
GRID_W = 64
HEAD_DIM = 64
RWKV_W = 768
RWKV_HEADS = 12
HYENA_W = 512
ATTN_W = 768
ATTN_HEADS = 12
ATTN_KV_HEADS = 4
GQA_GROUP = 3
KV_W = 256
LORA = 64
GN_EPS = 64e-5
FILTER_EMB = 33
FILTER_HID = 64
HY_MIN_DECAY = math.log(1e-2) / 1.5
HY_MAX_DECAY = math.log(1e-2) / 0.3
WINDOW = 128
BLOCK = 128
ROPE_BASE = 10000.0
D_FF = 5504
LN_EPS = 1e-5
DN_ALPHA = (2 * DEPTH) ** 0.25

C_R, C_K, C_V, C_G = 0, 768, 1536, 2304
C_HY = 3072
C_Q, C_AK, C_AV = 4608, 5376, 5632
C_LORA = 5888
IN_W_EXT = 6144

MOD_BLK = 256
CHUNK = 64
RWKV_CB = 4
LANES = 128
D_FF_PAD = 5632
VMEM_LIMIT = 56 * 1024 * 1024


def _dot(a, b):
    return jnp.dot(a.astype(BF16), b.astype(BF16), preferred_element_type=F32)


def _dot_nt(a, b):
    return lax.dot_general(a.astype(BF16), b.astype(BF16), (((1,), (1,)), ((), ())),
                           preferred_element_type=F32)


def _split(x):
    hi = x.astype(BF16)
    lo = (x - hi.astype(F32)).astype(BF16)
    return hi, lo


def _dot3_dims(a, b, dims):
    ah, al = _split(a)
    bh, bl = _split(b)
    dg = functools.partial(lax.dot_general, dimension_numbers=dims, preferred_element_type=F32)
    return dg(ah, bh) + (dg(ah, bl) + dg(al, bh))


_NN = (((1,), (0,)), ((), ()))
_NT = (((1,), (1,)), ((), ()))
_TN = (((0,), (0,)), ((), ()))


def _dot3(a, b):
    return _dot3_dims(a, b, _NN)


def _sigmoid(x):
    return 1.0 / (1.0 + jnp.exp(-x))


def _softplus(x):
    return jnp.maximum(x, 0.0) + jnp.log(1.0 + jnp.exp(-jnp.abs(x)))


def _mod_kernel(c_ref, w_ref, b_ref, o_ref):
    c = c_ref[...]
    s = c * _sigmoid(c)
    o_ref[...] = _dot(s, w_ref[...]) + b_ref[...]


def _modulation(cond, w_mod, b_mod):
    L = w_mod.shape[0]
    R = cond.shape[0]
    tn = 1024
    return pl.pallas_call(
        _mod_kernel,
        grid=(L, 6 * D_MODEL // tn),
        in_specs=[pl.BlockSpec((R, D_MODEL), lambda l, j: (0, 0)),
                  pl.BlockSpec((None, D_MODEL, tn), lambda l, j: (l, 0, j)),
                  pl.BlockSpec((None, 1, tn), lambda l, j: (l, 0, j))],
        out_specs=pl.BlockSpec((None, R, tn), lambda l, j: (l, 0, j)),
        out_shape=jax.ShapeDtypeStruct((L, R, 6 * D_MODEL), F32),
        compiler_params=pltpu.CompilerParams(dimension_semantics=("arbitrary", "arbitrary"),
                                             vmem_limit_bytes=VMEM_LIMIT),
        name="modulation",
    )(cond, w_mod, b_mod.reshape(L, 1, 6 * D_MODEL))


def _inproj_kernel(x_ref, mod_ref, w_ref, o_ref, *, tn):
    shift = mod_ref[0, 0:1, :]
    scale = mod_ref[0, 1:2, :]
    h = (x_ref[...] * (1.0 + scale) + shift).astype(BF16)
    for j in range(IN_W_EXT // tn):
        cols = slice(j * tn, (j + 1) * tn)
        o_ref[:, cols] = jnp.dot(h, w_ref[:, cols], preferred_element_type=F32)


def _in_projection(x, mod, w_ext, l, tn=512):
    N = x.shape[0]
    tm = MOD_BLK
    return pl.pallas_call(
        functools.partial(_inproj_kernel, tn=tn),
        grid=(N // tm,),
        in_specs=[pl.BlockSpec((tm, D_MODEL), lambda i: (i, 0)),
                  pl.BlockSpec((1, 6, D_MODEL), lambda i: (i, 0, 0)),
                  pl.BlockSpec((None, D_MODEL, IN_W_EXT), lambda i: (l, 0, 0), pipeline_mode=pl.Buffered(1))],
        out_specs=pl.BlockSpec((tm, IN_W_EXT), lambda i: (i, 0)),
        out_shape=jax.ShapeDtypeStruct((N, IN_W_EXT), F32),
        compiler_params=pltpu.CompilerParams(dimension_semantics=("arbitrary",),
                                             vmem_limit_bytes=VMEM_LIMIT),
        name="in_projection",
    )(x, mod, w_ext)


PREC = dict(seg=2, cum=2, a=1, neu=1, post=1, mn=1, b_o=1, b_s=3, gn=2)


def _dotp(a, b, dims, passes):
    dg = functools.partial(lax.dot_general, dimension_numbers=dims, preferred_element_type=F32)
    if passes == 1:
        return dg(a.astype(BF16), b.astype(BF16))
    ah, al = _split(a)
    if passes == 2:
        bh = b.astype(BF16)
        return dg(ah, bh) + dg(al, bh)
    bh, bl = _split(b)
    return dg(ah, bh) + (dg(ah, bl) + dg(al, bh))


def _dotp_rhs(a, b, passes):
    dg = functools.partial(lax.dot_general, dimension_numbers=_NN, preferred_element_type=F32)
    ab = a.astype(BF16)
    if passes == 1:
        return dg(ab, b.astype(BF16))
    bh, bl = _split(b)
    return dg(ab, bh) + dg(ab, bl)


def _rwkv_kernel(*refs, T, has_init, cb):
    (r_ref, k_ref, v_ref, g_ref, lo_ref, w0_ref, w2_ref, a0_ref, a2_ref,
     kk_ref, ka_ref, rk_ref, lng_ref, lnb_ref) = refs[:14]
    pos = 14
    init_ref = None
    if has_init:
        init_ref = refs[pos]
        pos += 1
    y_ref, sfin_ref = refs[pos:pos + 2]
    rp_scr, ov_scr, m_scr, n_scr, o_scr, bonus_scr, s_scr = refs[pos + 2:]

    nC = T // CHUNK
    C = CHUNK
    H = HEAD_DIM
    P = PREC
    C2 = 2 * C
    row = lax.broadcasted_iota(jnp.int32, (C, C), 0)
    col = lax.broadcasted_iota(jnp.int32, (C, C), 1)
    tri = [jnp.where(col <= row, 1.0, 0.0).astype(F32), jnp.where(col >= row, 1.0, 0.0).astype(F32)]
    lrow = lax.broadcasted_iota(jnp.int32, (LANES, LANES), 0)
    lcol = lax.broadcasted_iota(jnp.int32, (LANES, LANES), 1)
    same_head = (lrow // H) == (lcol // H)
    head_ones = jnp.where(same_head, 1.0, 0.0).astype(F32)
    eye = lrow == lcol
    rt_, ct_ = lrow % C, lcol % C
    strict_m = [same_head & (ct_ < rt_), same_head & (ct_ > rt_)]
    incl_m = [same_head & (ct_ <= rt_), same_head & (ct_ >= rt_)]
    lane = lax.broadcasted_iota(jnp.int32, (1, LANES), 1)
    hmask = [lane < H, lane >= H]

    def stack_heads(x):
        return jnp.concatenate([jnp.where(hmask[0], x, 0.0), jnp.where(hmask[1], x, 0.0)], axis=0)

    kkw = kk_ref[...]
    kaw = ka_ref[...]
    rkw = rk_ref[...]

    def phase_a(ci, carry):
        units = []
        for cc in range(cb):
            c = ci * cb + cc
            rows = pl.ds(pl.multiple_of(c * C, C), C)
            r = r_ref[rows, :]
            k = k_ref[rows, :]
            v = v_ref[rows, :]
            lo = lo_ref[rows, :]
            kkraw = k * kkw
            ss = _dotp(kkraw * kkraw, head_ones, _NN, P['seg'])
            kkn = kkraw / jnp.maximum(jnp.sqrt(ss), 1e-12)
            bonus = jnp.zeros((C, LANES), F32)
            vm = stack_heads(v)
            for d in range(2):
                hw = lo[:, d * LORA:(d + 1) * LORA]
                ha = lo[:, 2 * LORA + d * LORA:2 * LORA + (d + 1) * LORA]
                wl = w0_ref[d] + _dot(jnp.tanh(hw), w2_ref[d])
                ld = -jnp.exp(-_softplus(-wl) - 0.5)
                alpha = _sigmoid(a0_ref[d] + _dot(ha, a2_ref[d]))
                kd = k * (1.0 + (alpha - 1.0) * kaw)
                bb = kkn * alpha
                bonus = bonus + _dotp(r * kd * rkw, head_ones, _NN, P['seg']) * v
                g_incl = _dotp_rhs(tri[d], ld, P['cum'])
                g_excl = g_incl - ld
                g_end = g_incl[C - 1:C, :] if d == 0 else g_incl[0:1, :]
                ginv = jnp.exp(-g_incl)
                dec_end = jnp.exp(g_end - g_incl)
                units.append(dict(
                    d=d, c=c, vm=vm,
                    at=stack_heads(-kkn * jnp.exp(g_excl)), rt=stack_heads(r * jnp.exp(g_incl)),
                    bt=stack_heads(bb * ginv), kt=stack_heads(kd * ginv),
                    bend=stack_heads(bb * dec_end), kend=stack_heads(kd * dec_end), gend=jnp.exp(g_end)))
            bonus_scr[rows, :] = bonus
            o_scr[rows, :] = jnp.zeros((C, LANES), F32)

        for u in units:
            ar = jnp.concatenate([u['at'], u['rt']], axis=0)
            bk = jnp.concatenate([u['bt'], u['kt']], axis=0)
            u['a'] = _dotp(ar, bk, _NT, P['a'])
        for u in units:
            a = u['a']
            st, ic = strict_m[u['d']], incl_m[u['d']]
            u['p'] = jnp.where(st, a[:C2, :C2], 0.0)
            u['a_ak'] = jnp.where(st, a[:C2, C2:], 0.0)
            u['a_rb'] = jnp.where(ic, a[C2:, :C2], 0.0)
            u['a_rk'] = jnp.where(ic, a[C2:, C2:], 0.0)
        for u in units:
            u['rhs'] = jnp.concatenate([u['at'], _dotp(u['a_ak'], u['vm'], _NN, P['post'])], axis=1)
            u['ov'] = _dotp(u['a_rk'], u['vm'], _NN, P['post'])
        for it in range(6):
            upd = [_dotp(u['p'], u['rhs'], _NN, P['neu']) for u in units]
            if it < 5:
                sq = [_dotp(u['p'], u['p'], _NN, P['neu']) for u in units]
            for i, u in enumerate(units):
                u['rhs'] = u['rhs'] + upd[i]
                if it < 5:
                    u['p'] = sq[i]
        xs = [_dotp(u['a_rb'], u['rhs'], _NN, P['post']) for u in units]
        ms = [_dotp(u['rhs'][:, :LANES], u['bend'], _TN, P['mn']) for u in units]
        ns = [_dotp(jnp.concatenate([u['rhs'][:, LANES:], u['vm']], axis=0),
                    jnp.concatenate([u['bend'], u['kend']], axis=0), _TN, P['mn']) for u in units]
        for i, u in enumerate(units):
            d, c = u['d'], u['c']
            rp = u['rt'] + xs[i][:, :LANES]
            ov = xs[i][:, LANES:] + u['ov']
            rp_scr[d, c] = rp[:C] + rp[C:]
            ov_scr[d, c] = ov[:C] + ov[C:]
            m_scr[d, c] = jnp.where(eye, jnp.broadcast_to(u['gend'], (LANES, LANES)), 0.0) + ms[i]
            n_scr[d, c] = ns[i]
        return carry

    lax.fori_loop(0, nC // cb, phase_a, 0)

    zero_blk = jnp.zeros((H, H), F32)
    for d in range(2):
        if has_init:
            s_scr[d] = jnp.concatenate([jnp.concatenate([init_ref[d, 0], zero_blk], axis=1),
                                        jnp.concatenate([zero_blk, init_ref[d, 1]], axis=1)], axis=0)
        else:
            s_scr[d] = jnp.zeros((LANES, LANES), F32)

    def phase_b(i, carry):
        cs = [i, nC - 1 - i]
        ss = [s_scr[d] for d in range(2)]
        os_ = [_dotp(rp_scr[d, cs[d]], ss[d], _NT, P['b_o']) for d in range(2)]
        sn = [_dotp(ss[d], m_scr[d, cs[d]], _NN, P['b_s']) for d in range(2)]
        for d in range(2):
            s_scr[d] = sn[d] + n_scr[d, cs[d]]
            rows = pl.ds(pl.multiple_of(cs[d] * C, C), C)
            o_scr[rows, :] = o_scr[rows, :] + (os_[d] + ov_scr[d, cs[d]])
        return carry

    lax.fori_loop(0, nC, phase_b, 0)

    for d in range(2):
        s = s_scr[d]
        sfin_ref[d, 0] = s[:H, :H]
        sfin_ref[d, 1] = s[H:, H:]

    lng = lng_ref[...]
    lnb = lnb_ref[...]

    rb = min(T, 256)

    def phase_c(c, carry):
        rows = pl.ds(pl.multiple_of(c * rb, rb), rb)
        o = o_scr[rows, :]
        st = _dotp(jnp.concatenate([o, o * o], axis=0), head_ones, _NN, P['gn']) * (1.0 / H)
        mu = st[:rb]
        var = st[rb:] - mu * mu
        on = (o - mu) * lax.rsqrt(var + GN_EPS) * lng + lnb
        y_ref[rows, :] = (on + bonus_scr[rows, :]) * _sigmoid(g_ref[rows, :])
        return carry

    lax.fori_loop(0, T // rb, phase_c, 0)


def _rwkv(proj, p, init, *, T, n_seq, row_blk0):
    nC = T // CHUNK
    npair = RWKV_HEADS // 2
    has_init = init is not None
    cb = lambda off: (lambda b, j: (b + row_blk0, off // LANES + j))
    vec = lambda b, j: (0, j)
    in_specs = [pl.BlockSpec((T, LANES), cb(C_R)),
                pl.BlockSpec((T, LANES), cb(C_K)),
                pl.BlockSpec((T, LANES), cb(C_V)),
                pl.BlockSpec((T, LANES), cb(C_G)),
                pl.BlockSpec((T, 4 * LORA), lambda b, j: (b + row_blk0, C_LORA // (4 * LORA))),
                pl.BlockSpec((2, 1, LANES), lambda b, j: (0, 0, j)),
                pl.BlockSpec((2, LORA, LANES), lambda b, j: (0, 0, j)),
                pl.BlockSpec((2, 1, LANES), lambda b, j: (0, 0, j)),
                pl.BlockSpec((2, LORA, LANES), lambda b, j: (0, 0, j)),
                pl.BlockSpec((1, LANES), vec), pl.BlockSpec((1, LANES), vec),
                pl.BlockSpec((1, LANES), vec), pl.BlockSpec((1, LANES), vec),
                pl.BlockSpec((1, LANES), vec)]
    args = [proj, proj, proj, proj, proj,
            p['rwkv_w0'].reshape(2, 1, RWKV_W), p['rwkv_w2'], p['rwkv_a0'].reshape(2, 1, RWKV_W), p['rwkv_a2'],
            p['rwkv_k_k'].reshape(1, RWKV_W), p['rwkv_k_a'].reshape(1, RWKV_W),
            p['rwkv_r_k'].reshape(1, RWKV_W), p['rwkv_lnx_g'].reshape(1, RWKV_W),
            p['rwkv_lnx_b'].reshape(1, RWKV_W)]
    st_spec = pl.BlockSpec((None, 2, 2, HEAD_DIM, HEAD_DIM), lambda b, j: (b, 0, j, 0, 0))
    if has_init:
        in_specs.append(st_spec)
        args.append(init)
    sc_o = lambda: pltpu.VMEM((2, nC, CHUNK, LANES), F32)
    sc_s = lambda: pltpu.VMEM((2, nC, LANES, LANES), F32)
    return pl.pallas_call(
        functools.partial(_rwkv_kernel, T=T, has_init=has_init, cb=min(RWKV_CB, nC)),
        grid=(n_seq, npair),
        in_specs=in_specs,
        out_specs=[pl.BlockSpec((T, LANES), lambda b, j: (b, j)), st_spec],
        out_shape=[jax.ShapeDtypeStruct((n_seq * T, RWKV_W), F32),
                   jax.ShapeDtypeStruct((n_seq, 2, RWKV_HEADS, HEAD_DIM, HEAD_DIM), F32)],
        scratch_shapes=[sc_o(), sc_o(), sc_s(), sc_s(),
                        pltpu.VMEM((T, LANES), F32), pltpu.VMEM((T, LANES), F32),
                        pltpu.VMEM((2, LANES, LANES), F32)],
        compiler_params=pltpu.CompilerParams(dimension_semantics=("arbitrary", "arbitrary"),
                                             vmem_limit_bytes=VMEM_LIMIT),
        name="rwkv_T%d" % T,
    )(*args)


def _dft_mats(L):
    f = np.arange(L, dtype=np.int64)
    ph = (np.outer(f, f) % (2 * L)).astype(np.float64) * (math.pi / L)
    return np.cos(ph), -np.sin(ph)


def _filter_features(L):
    t = np.linspace(0.0, 1.0, L, dtype=np.float32)[:, None]
    bands = (FILTER_EMB - 1) // 2
    t_res = np.arange(L, dtype=np.float32)[:, None]
    f = np.linspace(1e-4, bands - 1, bands, dtype=np.float32)[None, :]
    ang = (2.0 * math.pi * t_res * f / L).astype(np.float32)
    z = np.concatenate([t, np.cos(ang), np.sin(ang)], -1).astype(np.float32)
    zp = np.zeros((L, FILTER_HID), np.float32)
    zp[:, :FILTER_EMB] = z
    deltas = np.abs(np.linspace(HY_MIN_DECAY, HY_MAX_DECAY, HYENA_W, dtype=np.float32))
    return zp, t.astype(np.float32), deltas[None, :].astype(np.float32)


def _hyfilter_kernel(z_ref, t_ref, dl_ref, w1_ref, b1_ref, q1_ref, w2_ref, b2_ref, q2_ref, wc_ref, wa_ref,
                     hs_ref, hd_ref, knyq_ref, *, L):
    hm = jnp.sin(q1_ref[...] * (_dot3(z_ref[...], w1_ref[...]) + b1_ref[...]))
    hm = jnp.sin(q2_ref[...] * (_dot3(hm, w2_ref[...]) + b2_ref[...]))
    win = jnp.exp(-t_ref[...] * dl_ref[...])
    lag = lax.broadcasted_iota(jnp.int32, (L, 1), 0)
    alt = jnp.where(lag % 2 == 0, 1.0, -1.0).astype(F32)
    hc = _dot3(hm, wc_ref[...]) * win
    ha = jnp.where(lag == 0, 0.0, _dot3(hm, wa_ref[...]) * win)
    hs_ref[...] = hc + ha
    hd_ref[...] = hc - ha
    knyq_ref[...] = jnp.sum((hc + ha) * alt, axis=0, keepdims=True) * (1.0 / (2 * L))


def _hyspec_kernel(cm_ref, sm_ref, hs_ref, hd_ref, kre_ref, kim_ref, *, L, tf):
    f = pl.program_id(2) * tf + lax.broadcasted_iota(jnp.int32, (tf, 1), 0)
    wf = jnp.where(f == 0, 1.0, 2.0).astype(F32) * (1.0 / (2 * L))
    kre_ref[...] = _dot3(cm_ref[...], hs_ref[...]) * wf
    kim_ref[...] = _dot3(sm_ref[...], hd_ref[...]) * wf


def _hyena_filter(p, L, cm, sm, cw=128):
    zp, t, dl = _filter_features(L)
    w1 = jnp.zeros((FILTER_HID, FILTER_HID), F32).at[:FILTER_EMB].set(p['hy_f_w1'])
    r2 = lambda a: a.reshape(1, -1)
    ncb = HYENA_W // cw
    full = lambda shape: pl.BlockSpec(shape, lambda n, j: (0,) * len(shape))
    shp = jax.ShapeDtypeStruct((2, L, HYENA_W), F32)
    hs, hd, knyq = pl.pallas_call(
        functools.partial(_hyfilter_kernel, L=L),
        grid=(2, ncb),
        in_specs=[full((L, FILTER_HID)), full((L, 1)),
                  pl.BlockSpec((1, cw), lambda n, j: (0, j)),
                  full((FILTER_HID, FILTER_HID)), full((1, FILTER_HID)), full((1, FILTER_HID)),
                  full((FILTER_HID, FILTER_HID)), full((1, FILTER_HID)), full((1, FILTER_HID)),
                  pl.BlockSpec((FILTER_HID, cw), lambda n, j: (0, n * 2 * ncb + j)),
                  pl.BlockSpec((FILTER_HID, cw), lambda n, j: (0, n * 2 * ncb + ncb + j))],
        out_specs=[pl.BlockSpec((None, L, cw), lambda n, j: (n, 0, j)),
                   pl.BlockSpec((None, L, cw), lambda n, j: (n, 0, j)),
                   pl.BlockSpec((None, 1, cw), lambda n, j: (n, 0, j))],
        out_shape=[shp, shp, jax.ShapeDtypeStruct((2, 1, HYENA_W), F32)],
        compiler_params=pltpu.CompilerParams(dimension_semantics=("arbitrary", "arbitrary"),
                                             vmem_limit_bytes=VMEM_LIMIT),
        name="hyena_filter_L%d" % L,
    )(jnp.asarray(zp), jnp.asarray(t), jnp.asarray(dl), w1, r2(p['hy_f_b1']), r2(p['hy_f_freq1']),
      p['hy_f_w2'], r2(p['hy_f_b2']), r2(p['hy_f_freq2']), p['hy_f_w3'], p['hy_f_w3'])
    tf = min(256, L)
    sw = 256
    kre, kim = pl.pallas_call(
        functools.partial(_hyspec_kernel, L=L, tf=tf),
        grid=(2, HYENA_W // sw, L // tf),
        in_specs=[pl.BlockSpec((tf, L), lambda n, j, i: (i, 0)),
                  pl.BlockSpec((tf, L), lambda n, j, i: (i, 0)),
                  pl.BlockSpec((None, L, sw), lambda n, j, i: (n, 0, j)),
                  pl.BlockSpec((None, L, sw), lambda n, j, i: (n, 0, j))],
        out_specs=[pl.BlockSpec((None, tf, sw), lambda n, j, i: (n, i, j)),
                   pl.BlockSpec((None, tf, sw), lambda n, j, i: (n, i, j))],
        out_shape=[shp, shp],
        compiler_params=pltpu.CompilerParams(dimension_semantics=("arbitrary",) * 3,
                                             vmem_limit_bytes=VMEM_LIMIT),
        name="hyena_spectrum_L%d" % L,
    )(cm, sm, hs, hd)
    return kre, kim, knyq


def _hyena_kernel(v_ref, x1_ref, x2_ref, sw_ref, bias_ref, cm_ref, sm_ref, kre_ref, kim_ref, knyq_ref,
                  o_ref, zb_scr, zf_scr, xs_scr, yre_scr, yim_scr, *, L, rb):
    nrb = L // rb
    t = lax.broadcasted_iota(jnp.int32, (L, 1), 0)
    first = t == 0
    last = t == L - 1
    alt_all = jnp.where(t % 2 == 0, 1.0, -1.0).astype(F32)
    tb = lax.broadcasted_iota(jnp.int32, (rb, 1), 0)
    alt_blk = jnp.where(tb % 2 == 0, 1.0, -1.0).astype(F32)
    sw = sw_ref[...]

    def short(ref, s):
        u = ref[...]
        prev = jnp.where(first, 0.0, pltpu.roll(u, 1, 0))
        nxt = jnp.where(last, 0.0, pltpu.roll(u, L - 1, 0))
        return prev * sw[0, s][None, :] + u * sw[1, s][None, :] + nxt * sw[2, s][None, :]

    z0 = short(v_ref, 0)
    zf_scr[...] = z0
    zb_scr[...] = z0.astype(BF16)
    unyq = jnp.sum(z0 * alt_all, axis=0, keepdims=True)

    for n in range(2):
        xs_scr[...] = short(x1_ref if n == 0 else x2_ref, n + 1)

        def fwd(fb, carry):
            rows = pl.ds(pl.multiple_of(fb * rb, rb), rb)
            zb = zb_scr[...]
            ure = jnp.dot(cm_ref[rows, :], zb, preferred_element_type=F32)
            uim = jnp.dot(sm_ref[rows, :], zb, preferred_element_type=F32)
            kre = kre_ref[n, rows, :]
            kim = kim_ref[n, rows, :]
            yre_scr[rows, :] = (ure * kre - uim * kim).astype(BF16)
            yim_scr[rows, :] = (ure * kim + uim * kre).astype(BF16)
            return carry

        lax.fori_loop(0, nrb, fwd, 0)
        ynyq = unyq * knyq_ref[n]
        bias = bias_ref[n]

        def inv(ib, acc):
            rows = pl.ds(pl.multiple_of(ib * rb, rb), rb)
            y = (jnp.dot(cm_ref[rows, :], yre_scr[...], preferred_element_type=F32)
                 + jnp.dot(sm_ref[rows, :], yim_scr[...], preferred_element_type=F32))
            y = y + alt_blk * ynyq + zf_scr[rows, :] * bias
            znew = xs_scr[rows, :] * y
            if n == 0:
                zf_scr[rows, :] = znew
                zb_scr[rows, :] = znew.astype(BF16)
                acc = acc + jnp.sum(znew * alt_blk, axis=0, keepdims=True)
            else:
                o_ref[rows, :] = znew
            return acc

        unyq = lax.fori_loop(0, nrb, inv, jnp.zeros_like(unyq))


def _hyena(proj, p, filt, cm, sm, *, L, n_seq, row_blk0, cw=256):
    kre, kim, knyq = filt
    ncb = HYENA_W // cw
    cb = lambda s: (lambda j, b: (b + row_blk0, (C_HY + s * HYENA_W) // cw + j))
    once = pl.Buffered(1)
    sw = p['hy_short_w'].reshape(3, 3, HYENA_W)
    rb = min(256, L)
    return pl.pallas_call(
        functools.partial(_hyena_kernel, L=L, rb=rb),
        grid=(ncb, n_seq),
        in_specs=[pl.BlockSpec((L, cw), cb(0)), pl.BlockSpec((L, cw), cb(1)), pl.BlockSpec((L, cw), cb(2)),
                  pl.BlockSpec((3, 3, cw), lambda j, b: (0, 0, j)),
                  pl.BlockSpec((2, 1, cw), lambda j, b: (0, 0, j)),
                  pl.BlockSpec((L, L), lambda j, b: (0, 0), pipeline_mode=once),
                  pl.BlockSpec((L, L), lambda j, b: (0, 0), pipeline_mode=once),
                  pl.BlockSpec((2, L, cw), lambda j, b: (0, 0, j), pipeline_mode=once),
                  pl.BlockSpec((2, L, cw), lambda j, b: (0, 0, j), pipeline_mode=once),
                  pl.BlockSpec((2, 1, cw), lambda j, b: (0, 0, j))],
        out_specs=pl.BlockSpec((L, cw), lambda j, b: (b, j)),
        out_shape=jax.ShapeDtypeStruct((n_seq * L, HYENA_W), F32),
        scratch_shapes=[pltpu.VMEM((L, cw), BF16), pltpu.VMEM((L, cw), F32), pltpu.VMEM((L, cw), F32),
                        pltpu.VMEM((L, cw), BF16), pltpu.VMEM((L, cw), BF16)],
        compiler_params=pltpu.CompilerParams(dimension_semantics=("arbitrary", "arbitrary"),
                                             vmem_limit_bytes=VMEM_LIMIT),
        name="hyena_L%d" % L,
    )(proj, proj, proj, sw, p['hy_bias'].reshape(2, 1, HYENA_W), cm.astype(BF16), sm.astype(BF16),
      kre, kim, knyq)


def _row_max(x):
    t = x[:, :LANES]
    for c in range(1, x.shape[1] // LANES):
        t = jnp.maximum(t, x[:, c * LANES:(c + 1) * LANES])
    return jnp.max(t, axis=-1, keepdims=True)


def _row_sum(x):
    t = x[:, :LANES]
    for c in range(1, x.shape[1] // LANES):
        t = t + x[:, c * LANES:(c + 1) * LANES]
    return jnp.sum(t, axis=-1, keepdims=True)


def _head_masks():
    lane = lax.broadcasted_iota(jnp.int32, (1, KV_W), 1) // HEAD_DIM
    return [lane == h for h in range(ATTN_KV_HEADS)]


def _attend_group(qg, keys, vals, masks, sinks, hm):
    nh = len(hm)
    qm = [jnp.where(hm[h], qg, 0.0).astype(BF16) for h in range(nh)]
    s = [[_dot_nt(qm[h], k) for k in keys] for h in range(nh)]
    s = [[sc if mk is None else jnp.where(mk, sc, -jnp.inf) for sc, mk in zip(s[h], masks)] for h in range(nh)]
    m = [jnp.maximum(_row_max(jnp.concatenate(s[h], axis=1)), sinks[h]) for h in range(nh)]
    p = [[jnp.exp(sc - m[h]) for sc in s[h]] for h in range(nh)]
    den = [_row_sum(jnp.concatenate(p[h], axis=1)) + jnp.exp(sinks[h] - m[h]) for h in range(nh)]
    pv = [[_dot(pc, v) for pc, v in zip(p[h], vals)] for h in range(nh)]
    acc = jnp.zeros(qg.shape, F32)
    for h in range(nh):
        tot = pv[h][0]
        for t in pv[h][1:]:
            tot = tot + t
        acc = jnp.where(hm[h], tot / den[h], acc)
    return acc


def _ctx_attn_kernel(q_ref, k_ref, v_ref, sink_ref, o_ref, *, S):
    scale = HEAD_DIM ** -0.5
    hm = _head_masks()
    k = k_ref[...].astype(BF16)
    v = v_ref[...].astype(BF16)
    for g in range(GQA_GROUP):
        qg = q_ref[:, g * KV_W:(g + 1) * KV_W] * scale
        sinks = [jnp.broadcast_to(sink_ref[0:1, h * GQA_GROUP + g:h * GQA_GROUP + g + 1], (S, 1))
                 for h in range(ATTN_KV_HEADS)]
        o_ref[:, g * KV_W:(g + 1) * KV_W] = _attend_group(qg, [k], [v], [None], sinks, hm)


def _ctx_attention(proj, sink, *, S, n_seq):
    return pl.pallas_call(
        functools.partial(_ctx_attn_kernel, S=S),
        grid=(n_seq,),
        in_specs=[pl.BlockSpec((S, ATTN_W), lambda b: (b, C_Q // ATTN_W)),
                  pl.BlockSpec((S, KV_W), lambda b: (b, C_AK // KV_W)),
                  pl.BlockSpec((S, KV_W), lambda b: (b, C_AV // KV_W)),
                  pl.BlockSpec((1, LANES), lambda b: (0, 0))],
        out_specs=pl.BlockSpec((S, ATTN_W), lambda b: (b, 0)),
        out_shape=jax.ShapeDtypeStruct((n_seq * S, ATTN_W), F32),
        compiler_params=pltpu.CompilerParams(dimension_semantics=("arbitrary",),
                                             vmem_limit_bytes=VMEM_LIMIT),
        name="ctx_attention",
    )(proj, proj, proj, sink)


def _rope(x, cos, sin_signed, first_half):
    n = x.shape[-1]
    partner = jnp.where(first_half, pltpu.roll(x, n - HEAD_DIM // 2, 1), pltpu.roll(x, HEAD_DIM // 2, 1))
    return x * cos + partner * sin_signed


def _lat_attn_kernel(q_ref, kp_ref, kc_ref, kn_ref, vp_ref, vc_ref, vn_ref, ck_ref, cv_ref,
                     cq_ref, sq_ref, cp_ref, sp_ref, cc_ref, sc_ref, cn_ref, sn_ref, sink_ref, o_ref, *, T):
    i = pl.program_id(1)
    scale = HEAD_DIM ** -0.5
    hm = _head_masks()
    fh_q = (lax.broadcasted_iota(jnp.int32, (1, ATTN_W), 1) % HEAD_DIM) < HEAD_DIM // 2
    fh_k = (lax.broadcasted_iota(jnp.int32, (1, KV_W), 1) % HEAD_DIM) < HEAD_DIM // 2
    q = _rope(q_ref[...], cq_ref[...], sq_ref[...], fh_q) * scale
    kl = jnp.concatenate([_rope(kp_ref[...], cp_ref[...], sp_ref[...], fh_k),
                          _rope(kc_ref[...], cc_ref[...], sc_ref[...], fh_k),
                          _rope(kn_ref[...], cn_ref[...], sn_ref[...], fh_k)], axis=0).astype(BF16)
    vl = jnp.concatenate([vp_ref[...], vc_ref[...], vn_ref[...]], axis=0).astype(BF16)
    kc = ck_ref[...].astype(BF16)
    vc = cv_ref[...].astype(BF16)
    qpos = i * BLOCK + lax.broadcasted_iota(jnp.int32, (BLOCK, 3 * BLOCK), 0)
    kpos = (i - 1) * BLOCK + lax.broadcasted_iota(jnp.int32, (BLOCK, 3 * BLOCK), 1)
    ok = (jnp.abs(qpos - kpos) <= WINDOW) & (kpos >= 0) & (kpos < T)
    for g in range(GQA_GROUP):
        qg = q[:, g * KV_W:(g + 1) * KV_W]
        sinks = [jnp.broadcast_to(sink_ref[0:1, h * GQA_GROUP + g:h * GQA_GROUP + g + 1], (BLOCK, 1))
                 for h in range(ATTN_KV_HEADS)]
        o_ref[:, g * KV_W:(g + 1) * KV_W] = _attend_group(qg, [kl, kc], [vl, vc], [ok, None], sinks, hm)


def _rope_tables(T):
    rows = T // GRID_W
    row = np.repeat(np.arange(rows), GRID_W).astype(np.float32)
    col = np.tile(np.arange(GRID_W), rows).astype(np.float32)
    nf = HEAD_DIM // 4
    inv = (ROPE_BASE ** (-np.arange(nf, dtype=np.float32) / nf)).astype(np.float32)
    ang = np.concatenate([row[:, None] * inv, col[:, None] * inv], -1).astype(np.float32)
    cos = np.cos(ang).astype(np.float32)
    sin = np.sin(ang).astype(np.float32)
    cos_h = np.concatenate([cos, cos], -1)
    sin_h = np.concatenate([-sin, sin], -1)
    return cos_h, sin_h


def _lat_attention(proj, cache_k, cache_v, sink, *, T, n_seq, row_blk0):
    nb = T // BLOCK
    P = cache_k.shape[1]
    cos_h, sin_h = _rope_tables(T)
    cq = jnp.asarray(np.tile(cos_h, (1, ATTN_HEADS)))
    sq = jnp.asarray(np.tile(sin_h, (1, ATTN_HEADS)))
    ck = jnp.asarray(np.tile(cos_h, (1, ATTN_KV_HEADS)))
    sk = jnp.asarray(np.tile(sin_h, (1, ATTN_KV_HEADS)))
    r0 = row_blk0 * nb
    prv = lambda i: jnp.maximum(i - 1, 0)
    nxt = lambda i: jnp.minimum(i + 1, nb - 1)
    kcol, vcol = C_AK // KV_W, C_AV // KV_W
    kv = lambda col, f: pl.BlockSpec((BLOCK, KV_W), lambda b, i: (r0 + b * nb + f(i), col))
    tab = lambda w, f: pl.BlockSpec((BLOCK, w), lambda b, i: (f(i), 0))
    cur = lambda i: i
    return pl.pallas_call(
        functools.partial(_lat_attn_kernel, T=T),
        grid=(n_seq, nb),
        in_specs=[pl.BlockSpec((BLOCK, ATTN_W), lambda b, i: (r0 + b * nb + i, C_Q // ATTN_W)),
                  kv(kcol, prv), kv(kcol, cur), kv(kcol, nxt),
                  kv(vcol, prv), kv(vcol, cur), kv(vcol, nxt),
                  pl.BlockSpec((None, P, KV_W), lambda b, i: (b, 0, 0)),
                  pl.BlockSpec((None, P, KV_W), lambda b, i: (b, 0, 0)),
                  tab(ATTN_W, cur), tab(ATTN_W, cur),
                  tab(KV_W, prv), tab(KV_W, prv), tab(KV_W, cur), tab(KV_W, cur), tab(KV_W, nxt), tab(KV_W, nxt),
                  pl.BlockSpec((1, LANES), lambda b, i: (0, 0))],
        out_specs=pl.BlockSpec((BLOCK, ATTN_W), lambda b, i: (b * nb + i, 0)),
        out_shape=jax.ShapeDtypeStruct((n_seq * T, ATTN_W), F32),
        compiler_params=pltpu.CompilerParams(dimension_semantics=("arbitrary", "arbitrary"),
                                             vmem_limit_bytes=VMEM_LIMIT),
        name="latent_attention",
    )(proj, proj, proj, proj, proj, proj, proj, cache_k, cache_v, cq, sq, ck, sk, ck, sk, ck, sk, sink)


def _layer_norm(y, g, b):
    mu = jnp.mean(y, axis=-1, keepdims=True)
    d = y - mu
    var = jnp.mean(d * d, axis=-1, keepdims=True)
    return d * lax.rsqrt(var + LN_EPS) * g + b


def _outproj_kernel(ac_ref, bc_ref, cc_ref, al_ref, bl_ref, cl_ref, x_ref, mod_ref, w_ref,
                    g_ref, be_ref, x1_ref, h2_ref, m_scr, *, nsub, n_ctx_tiles):
    r1, r2 = RWKV_W, RWKV_W + HYENA_W

    def mix(a_ref, b_ref, c_ref):
        m_scr[...] = (jnp.dot(a_ref[...].astype(BF16), w_ref[0:r1, :], preferred_element_type=F32)
                      + jnp.dot(b_ref[...].astype(BF16), w_ref[r1:r2, :], preferred_element_type=F32)
                      + jnp.dot(c_ref[...].astype(BF16), w_ref[r2:, :], preferred_element_type=F32))

    pl.when(pl.program_id(0) < n_ctx_tiles)(lambda: mix(ac_ref, bc_ref, cc_ref))
    pl.when(pl.program_id(0) >= n_ctx_tiles)(lambda: mix(al_ref, bl_ref, cl_ref))
    m = m_scr[...]
    for s in range(nsub):
        rows = slice(s * MOD_BLK, (s + 1) * MOD_BLK)
        gate1 = mod_ref[s, 2:3, :]
        shift2 = mod_ref[s, 3:4, :]
        scale2 = mod_ref[s, 4:5, :]
        x1 = _layer_norm(DN_ALPHA * x_ref[rows, :] + gate1 * m[rows, :], g_ref[...], be_ref[...])
        x1_ref[rows, :] = x1
        h2_ref[rows, :] = (x1 * (1.0 + scale2) + shift2).astype(BF16)


def _out_projection(mix_ctx, mix_lat, x, mod, w_out, l, g, b, tm=256):
    N = x.shape[0]
    nsub = tm // MOD_BLK
    nct = mix_ctx[0].shape[0] // tm
    nlt = mix_lat[0].shape[0] // tm
    row = lambda w: pl.BlockSpec((tm, w), lambda i: (i, 0))
    ctx = lambda w: pl.BlockSpec((tm, w), lambda i: (jnp.minimum(i, nct - 1), 0))
    lat = lambda w: pl.BlockSpec((tm, w), lambda i: (jnp.clip(i - nct, 0, nlt - 1), 0))
    full = lambda r: pl.BlockSpec((r, D_MODEL), lambda i: (0, 0))
    return pl.pallas_call(
        functools.partial(_outproj_kernel, nsub=nsub, n_ctx_tiles=nct),
        grid=(N // tm,),
        in_specs=[ctx(RWKV_W), ctx(HYENA_W), ctx(ATTN_W), lat(RWKV_W), lat(HYENA_W), lat(ATTN_W), row(D_MODEL),
                  pl.BlockSpec((nsub, 6, D_MODEL), lambda i: (i, 0, 0)),
                  pl.BlockSpec((None, D_MODEL, D_MODEL), lambda i: (l, 0, 0)), full(1), full(1)],
        out_specs=[row(D_MODEL), row(D_MODEL)],
        out_shape=[jax.ShapeDtypeStruct((N, D_MODEL), F32), jax.ShapeDtypeStruct((N, D_MODEL), BF16)],
        scratch_shapes=[pltpu.VMEM((tm, D_MODEL), F32)],
        compiler_params=pltpu.CompilerParams(dimension_semantics=("arbitrary",),
                                             vmem_limit_bytes=VMEM_LIMIT),
        name="out_projection",
    )(*mix_ctx, *mix_lat, x, mod, w_out, g.reshape(1, D_MODEL), b.reshape(1, D_MODEL))


HALO = 16


def _ffn_kernel(hp_ref, h_ref, hn_ref, x1_ref, mod_ref, wa_ref, wb_ref, cwa_ref, cwb_ref, wd_ref, g_ref, be_ref,
                o_ref, hx_scr, ua_scr, ub_scr, acc_scr, *, tm, tf, nsub, n_ctx, s_ctx, s_lat):
    i = pl.program_id(0)
    j = pl.program_id(1)

    @pl.when(j == 0)
    def _():
        hx_scr[0:HALO, :] = hp_ref[...]
        hx_scr[HALO:HALO + tm, :] = h_ref[...]
        hx_scr[HALO + tm:, :] = hn_ref[...]
        acc_scr[...] = jnp.zeros_like(acc_scr)

    hx = hx_scr[...]
    ua_scr[...] = jnp.dot(hx, wa_ref[...], preferred_element_type=F32)
    ub_scr[...] = jnp.dot(hx, wb_ref[...], preferred_element_type=F32)
    grow = i * tm + lax.broadcasted_iota(jnp.int32, (tm, 1), 0)
    pos = jnp.where(grow < n_ctx, grow % s_ctx, (grow - n_ctx) % s_lat)
    slen = jnp.where(grow < n_ctx, s_ctx, s_lat)
    mprev = jnp.where(pos == 0, 0.0, 1.0).astype(F32)
    mnext = jnp.where(pos == slen - 1, 0.0, 1.0).astype(F32)

    def conv(u_scr, cw_ref):
        up = u_scr[HALO - 1:HALO - 1 + tm, :] * mprev
        uc = u_scr[HALO:HALO + tm, :]
        un = u_scr[HALO + 1:HALO + 1 + tm, :] * mnext
        return up * cw_ref[0:1, :] + uc * cw_ref[1:2, :] + un * cw_ref[2:3, :]

    a = conv(ua_scr, cwa_ref)
    b = conv(ub_scr, cwb_ref)
    act = (a * _sigmoid(a) * b).astype(BF16)
    acc_scr[...] += jnp.dot(act, wd_ref[...], preferred_element_type=F32)

    @pl.when(j == pl.num_programs(1) - 1)
    def _():
        for s in range(nsub):
            rows = slice(s * MOD_BLK, (s + 1) * MOD_BLK)
            gate2 = mod_ref[s, 5:6, :]
            o_ref[rows, :] = _layer_norm(DN_ALPHA * x1_ref[rows, :] + gate2 * acc_scr[rows, :],
                                         g_ref[...], be_ref[...])


def _ffn(h2, x1, mod, wa, wb, cw, wd, l, g, b, *, n_ctx, s_ctx, s_lat, tm=512, tf=512):
    N = x1.shape[0]
    nsub = tm // MOD_BLK
    nh = tm // HALO
    last_h = N // HALO - 1
    pad = ((0, 0), (0, D_FF_PAD - D_FF))
    cwa, cwb = jnp.pad(cw[:, :D_FF], pad), jnp.pad(cw[:, D_FF:], pad)
    return pl.pallas_call(
        functools.partial(_ffn_kernel, tm=tm, tf=tf, nsub=nsub, n_ctx=n_ctx, s_ctx=s_ctx, s_lat=s_lat),
        grid=(N // tm, D_FF_PAD // tf),
        in_specs=[pl.BlockSpec((HALO, D_MODEL), lambda i, j: (jnp.maximum(i * nh - 1, 0), 0)),
                  pl.BlockSpec((tm, D_MODEL), lambda i, j: (i, 0)),
                  pl.BlockSpec((HALO, D_MODEL), lambda i, j: (jnp.minimum((i + 1) * nh, last_h), 0)),
                  pl.BlockSpec((tm, D_MODEL), lambda i, j: (i, 0)),
                  pl.BlockSpec((nsub, 6, D_MODEL), lambda i, j: (i, 0, 0)),
                  pl.BlockSpec((None, D_MODEL, tf), lambda i, j: (l, 0, j)),
                  pl.BlockSpec((None, D_MODEL, tf), lambda i, j: (l, 0, j)),
                  pl.BlockSpec((3, tf), lambda i, j: (0, j)),
                  pl.BlockSpec((3, tf), lambda i, j: (0, j)),
                  pl.BlockSpec((None, tf, D_MODEL), lambda i, j: (l, j, 0)),
                  pl.BlockSpec((1, D_MODEL), lambda i, j: (0, 0)),
                  pl.BlockSpec((1, D_MODEL), lambda i, j: (0, 0))],
        out_specs=pl.BlockSpec((tm, D_MODEL), lambda i, j: (i, 0)),
        out_shape=jax.ShapeDtypeStruct((N, D_MODEL), F32),
        scratch_shapes=[pltpu.VMEM((tm + 2 * HALO, D_MODEL), BF16),
                        pltpu.VMEM((tm + 2 * HALO, tf), F32),
                        pltpu.VMEM((tm + 2 * HALO, tf), F32),
                        pltpu.VMEM((tm, D_MODEL), F32)],
        compiler_params=pltpu.CompilerParams(dimension_semantics=("arbitrary", "arbitrary"),
                                             vmem_limit_bytes=VMEM_LIMIT),
        name="conv_ffn",
    )(h2, h2, h2, x1, mod, wa, wb, cwa, cwb, wd, g.reshape(1, D_MODEL), b.reshape(1, D_MODEL))


def _prep_up_kernel(a_ref, b_ref, wa_ref, wb_ref):
    zeros = jnp.zeros((a_ref.shape[0], D_FF_PAD - D_FF), BF16)
    wa_ref[:, :D_FF] = a_ref[...].astype(BF16)
    wa_ref[:, D_FF:] = zeros
    wb_ref[:, :D_FF] = b_ref[...].astype(BF16)
    wb_ref[:, D_FF:] = zeros


def _prep_down_kernel(w_ref, o_ref, *, n_real):
    keep = pl.program_id(1) < n_real
    o_ref[...] = jnp.where(keep, w_ref[...], 0.0).astype(BF16)


def _prep_in_kernel(w_ref, w1_ref, a1_ref, o_ref):
    n = w_ref.shape[-1]
    o_ref[:, :C_Q] = w_ref[:, :C_Q].astype(BF16)
    for g in range(GQA_GROUP):
        for h in range(ATTN_KV_HEADS):
            src = C_Q + (h * GQA_GROUP + g) * HEAD_DIM
            dst = C_Q + (g * ATTN_KV_HEADS + h) * HEAD_DIM
            o_ref[:, dst:dst + HEAD_DIM] = w_ref[:, src:src + HEAD_DIM].astype(BF16)
    o_ref[:, C_AK:n] = w_ref[:, C_AK:n].astype(BF16)
    o_ref[:, n:] = jnp.concatenate([w1_ref[0], w1_ref[1], a1_ref[0], a1_ref[1]], axis=1).astype(BF16)


def _cast_kernel(w_ref, o_ref):
    o_ref[...] = w_ref[...].astype(BF16)


def _prep_weights(w_in, rwkv_w1, rwkv_a1, w_out, ffn_w_up, ffn_w_down):
    L, D = w_in.shape[0], D_MODEL
    cp = lambda n: pltpu.CompilerParams(dimension_semantics=("arbitrary",) * n, vmem_limit_bytes=VMEM_LIMIT)
    tr = 128
    half = jax.ShapeDtypeStruct((L, D, D_FF_PAD), BF16)
    wa, wb = pl.pallas_call(
        _prep_up_kernel,
        grid=(L, D // tr),
        in_specs=[pl.BlockSpec((None, tr, D_FF), lambda l, i: (l, i, 0)),
                  pl.BlockSpec((None, tr, D_FF), lambda l, i: (l, i, 1))],
        out_specs=[pl.BlockSpec((None, tr, D_FF_PAD), lambda l, i: (l, i, 0))] * 2,
        out_shape=[half, half], compiler_params=cp(2), name="prep_ffn_up",
    )(ffn_w_up, ffn_w_up)
    nr = D_FF // LANES
    wd = pl.pallas_call(
        functools.partial(_prep_down_kernel, n_real=nr),
        grid=(L, D_FF_PAD // LANES),
        in_specs=[pl.BlockSpec((None, LANES, D), lambda l, j: (l, jnp.minimum(j, nr - 1), 0))],
        out_specs=pl.BlockSpec((None, LANES, D), lambda l, j: (l, j, 0)),
        out_shape=jax.ShapeDtypeStruct((L, D_FF_PAD, D), BF16), compiler_params=cp(2), name="prep_ffn_down",
    )(ffn_w_down)
    tr2 = 256
    n_in = w_in.shape[-1]
    w_ext = pl.pallas_call(
        _prep_in_kernel,
        grid=(L, D // tr2),
        in_specs=[pl.BlockSpec((None, tr2, n_in), lambda l, i: (l, i, 0)),
                  pl.BlockSpec((None, 2, tr2, LORA), lambda l, i: (l, 0, i, 0)),
                  pl.BlockSpec((None, 2, tr2, LORA), lambda l, i: (l, 0, i, 0))],
        out_specs=pl.BlockSpec((None, tr2, IN_W_EXT), lambda l, i: (l, i, 0)),
        out_shape=jax.ShapeDtypeStruct((L, D, IN_W_EXT), BF16), compiler_params=cp(2), name="prep_in",
    )(w_in, rwkv_w1, rwkv_a1)
    hb0 = (RWKV_W + HYENA_W) // HEAD_DIM

    def src_rows(i):
        a = i - hb0
        perm = hb0 + (a % ATTN_KV_HEADS) * GQA_GROUP + a // ATTN_KV_HEADS
        return jnp.where((a >= 0) & (a < ATTN_HEADS), perm, i)

    wo = pl.pallas_call(
        _cast_kernel,
        grid=(L, D // HEAD_DIM),
        in_specs=[pl.BlockSpec((None, HEAD_DIM, D), lambda l, i: (l, src_rows(i), 0))],
        out_specs=pl.BlockSpec((None, HEAD_DIM, D), lambda l, i: (l, i, 0)),
        out_shape=jax.ShapeDtypeStruct((L, D, D), BF16), compiler_params=cp(2), name="prep_out",
    )(w_out)
    return w_ext, wo, wa, wb, wd


def kernel(x_prompt, x_sample, c, state_rwkv, cache_k, cache_v, c_ctx, w_mod, b_mod, w_in, rwkv_w0, rwkv_w1, rwkv_w2, rwkv_a0, rwkv_a1, rwkv_a2, rwkv_k_k, rwkv_k_a, rwkv_r_k, rwkv_lnx_g, rwkv_lnx_b, hy_short_w, hy_f_w1, hy_f_b1, hy_f_freq1, hy_f_w2, hy_f_b2, hy_f_freq2, hy_f_w3, hy_bias, attn_sink, w_out, ln1_g, ln1_b, ffn_w_up, ffn_conv_w, ffn_w_down, ln2_g, ln2_b):
    params = dict(rwkv_w0=rwkv_w0, rwkv_w2=rwkv_w2, rwkv_a0=rwkv_a0, rwkv_a2=rwkv_a2,
                  rwkv_k_k=rwkv_k_k, rwkv_k_a=rwkv_k_a, rwkv_r_k=rwkv_r_k,
                  rwkv_lnx_g=rwkv_lnx_g, rwkv_lnx_b=rwkv_lnx_b,
                  hy_short_w=hy_short_w, hy_f_w1=hy_f_w1, hy_f_b1=hy_f_b1, hy_f_freq1=hy_f_freq1,
                  hy_f_w2=hy_f_w2, hy_f_b2=hy_f_b2, hy_f_freq2=hy_f_freq2, hy_f_w3=hy_f_w3,
                  hy_bias=hy_bias)
    L = w_in.shape[0]
    Bc, Sc, D = x_prompt.shape
    Bl, Tl, _ = x_sample.shape
    P = cache_k.shape[2]
    n_ctx = Bc * Sc
    n_lat = Bl * Tl
    N = n_ctx + n_lat
    assert Sc % MOD_BLK == 0 and Tl % MOD_BLK == 0 and n_ctx % Tl == 0

    n_cond = 1 + Bl
    cond = jnp.zeros((16, D), F32).at[0].set(c_ctx).at[1:n_cond].set(c)
    mod_all = _modulation(cond, w_mod, b_mod)
    mod_ctx = jnp.broadcast_to(mod_all[:, 0:1, :], (L, n_ctx // MOD_BLK, 6 * D))
    mod_lat = jnp.broadcast_to(mod_all[:, 1:n_cond, None, :], (L, Bl, Tl // MOD_BLK, 6 * D))
    mod_blk = jnp.concatenate([mod_ctx, mod_lat.reshape(L, n_lat // MOD_BLK, 6 * D)], axis=1)
    mod_blk = mod_blk.reshape(L, N // MOD_BLK, 6, D)

    cm_c, sm_c = (jnp.asarray(m, F32) for m in _dft_mats(Sc))
    cm_l, sm_l = (jnp.asarray(m, F32) for m in _dft_mats(Tl))
    sink_pad = jnp.zeros((L, 1, LANES), F32).at[:, 0, :ATTN_HEADS].set(attn_sink)

    w_ext, wo, wa, wb, wd = _prep_weights(w_in, rwkv_w1, rwkv_a1, w_out, ffn_w_up, ffn_w_down)
    x = jnp.concatenate([x_prompt.reshape(n_ctx, D), x_sample.reshape(n_lat, D)], axis=0)
    states, keys, vals = [], [], []
    for l in range(L):
        p = {n: a[l] for n, a in params.items()}
        proj = _in_projection(x, mod_blk[l], w_ext, l)
        keys.append(proj[:n_ctx, C_AK:C_AK + KV_W].reshape(Bc, Sc, ATTN_KV_HEADS, HEAD_DIM))
        vals.append(proj[:n_ctx, C_AV:C_AV + KV_W].reshape(Bc, Sc, ATTN_KV_HEADS, HEAD_DIM))

        ya_c, s_fin = _rwkv(proj, p, None, T=Sc, n_seq=Bc, row_blk0=0)
        ya_l, _ = _rwkv(proj, p, state_rwkv[:, l], T=Tl, n_seq=Bl, row_blk0=n_ctx // Tl)
        states.append(s_fin)

        yb_c = _hyena(proj, p, _hyena_filter(p, Sc, cm_c, sm_c), cm_c, sm_c, L=Sc, n_seq=Bc, row_blk0=0)
        yb_l = _hyena(proj, p, _hyena_filter(p, Tl, cm_l, sm_l), cm_l, sm_l, L=Tl, n_seq=Bl,
                      row_blk0=n_ctx // Tl)

        yc_c = _ctx_attention(proj, sink_pad[l], S=Sc, n_seq=Bc)
        yc_l = _lat_attention(proj, cache_k[:, l].reshape(Bl, P, KV_W), cache_v[:, l].reshape(Bl, P, KV_W),
                              sink_pad[l], T=Tl, n_seq=Bl, row_blk0=n_ctx // Tl)

        x1, h2 = _out_projection((ya_c, yb_c, yc_c), (ya_l, yb_l, yc_l), x, mod_blk[l],
                                 wo, l, ln1_g[l], ln1_b[l])
        x = _ffn(h2, x1, mod_blk[l], wa, wb, ffn_conv_w[l], wd, l, ln2_g[l], ln2_b[l],
                 n_ctx=n_ctx, s_ctx=Sc, s_lat=Tl)

    y_prompt = x[:n_ctx].reshape(Bc, Sc, D)
    y_sample = x[n_ctx:].reshape(Bl, Tl, D)
    new_state = jnp.stack(states, 1)
    new_k = jnp.stack(keys, 1)
    new_v = jnp.stack(vals, 1)
    return (y_prompt, y_sample, new_state, new_k, new_v)
```

```python
import functools
import math

import numpy as np
import jax
import jax.numpy as jnp
from jax import lax
from jax.experimental import pallas as pl
from jax.experimental.pallas import tpu as pltpu

F32 = jnp.float32
BF16 = jnp.bfloat16

D_MODEL = 2048
DEPTH = 4
GRID_W = 64
HEAD_DIM = 64
RWKV_W = 768
RWKV_HEADS = 12
HYENA_W = 512
ATTN_W = 768
ATTN_HEADS = 12
ATTN_KV_HEADS = 4
GQA_GROUP = 3
KV_W = 256
LORA = 64
GN_EPS = 64e-5
FILTER_EMB = 33
FILTER_HID = 64
HY_MIN_DECAY = math.log(1e-2) / 1.5
HY_MAX_DECAY = math.log(1e-2) / 0.3
WINDOW = 128
BLOCK = 128
ROPE_BASE = 10000.0
D_FF = 5504
LN_EPS = 1e-5
DN_ALPHA = (2 * DEPTH) ** 0.25

C_R, C_K, C_V, C_G = 0, 768, 1536, 2304
C_HY = 3072
C_Q, C_AK, C_AV = 4608, 5376, 5632
C_LORA = 5888
IN_W_EXT = 6144

MOD_BLK = 256
CHUNK = 64
RWKV_CB = 4
LANES = 128
D_FF_PAD = 5632
VMEM_LIMIT = 56 * 1024 * 1024


def _dot(a, b):
    return jnp.dot(a.astype(BF16), b.astype(BF16), preferred_element_type=F32)


def _dot_nt(a, b):
    return lax.dot_general(a.astype(BF16), b.astype(BF16), (((1,), (1,)), ((), ())),
                           preferred_element_type=F32)


def _split(x):
    hi = x.astype(BF16)
    lo = (x - hi.astype(F32)).astype(BF16)
    return hi, lo


def _dot3_dims(a, b, dims):
    ah, al = _split(a)
    bh, bl = _split(b)
    dg = functools.partial(lax.dot_general, dimension_numbers=dims, preferred_element_type=F32)
    return dg(ah, bh) + (dg(ah, bl) + dg(al, bh))


_NN = (((1,), (0,)), ((), ()))
_NT = (((1,), (1,)), ((), ()))
_TN = (((0,), (0,)), ((), ()))


def _dot3(a, b):
    return _dot3_dims(a, b, _NN)


def _sigmoid(x):
    return 1.0 / (1.0 + jnp.exp(-x))


def _softplus(x):
    return jnp.maximum(x, 0.0) + jnp.log(1.0 + jnp.exp(-jnp.abs(x)))


def _mod_kernel(c_ref, w_ref, b_ref, o_ref):
    c = c_ref[...]
    s = c * _sigmoid(c)
    o_ref[...] = _dot(s, w_ref[...]) + b_ref[...]


def _modulation(cond, w_mod, b_mod):
    L = w_mod.shape[0]
    R = cond.shape[0]
    tn = 1024
    return pl.pallas_call(
        _mod_kernel,
        grid=(L, 6 * D_MODEL // tn),
        in_specs=[pl.BlockSpec((R, D_MODEL), lambda l, j: (0, 0)),
                  pl.BlockSpec((None, D_MODEL, tn), lambda l, j: (l, 0, j)),
                  pl.BlockSpec((None, 1, tn), lambda l, j: (l, 0, j))],
        out_specs=pl.BlockSpec((None, R, tn), lambda l, j: (l, 0, j)),
        out_shape=jax.ShapeDtypeStruct((L, R, 6 * D_MODEL), F32),
        compiler_params=pltpu.CompilerParams(dimension_semantics=("arbitrary", "arbitrary"),
                                             vmem_limit_bytes=VMEM_LIMIT),
        name="modulation",
    )(cond, w_mod, b_mod.reshape(L, 1, 6 * D_MODEL))


def _inproj_kernel(x_ref, mod_ref, w_ref, o_ref, *, tn):
    shift = mod_ref[0, 0:1, :]
    scale = mod_ref[0, 1:2, :]
    h = (x_ref[...] * (1.0 + scale) + shift).astype(BF16)
    for j in range(IN_W_EXT // tn):
        cols = slice(j * tn, (j + 1) * tn)
        o_ref[:, cols] = jnp.dot(h, w_ref[:, cols], preferred_element_type=F32)


def _in_projection(x, mod, w_ext, l, tn=512):
    N = x.shape[0]
    tm = MOD_BLK
    return pl.pallas_call(
        functools.partial(_inproj_kernel, tn=tn),
        grid=(N // tm,),
        in_specs=[pl.BlockSpec((tm, D_MODEL), lambda i: (i, 0)),
                  pl.BlockSpec((1, 6, D_MODEL), lambda i: (i, 0, 0)),
                  pl.BlockSpec((None, D_MODEL, IN_W_EXT), lambda i: (l, 0, 0), pipeline_mode=pl.Buffered(1))],
        out_specs=pl.BlockSpec((tm, IN_W_EXT), lambda i: (i, 0)),
        out_shape=jax.ShapeDtypeStruct((N, IN_W_EXT), F32),
        compiler_params=pltpu.CompilerParams(dimension_semantics=("arbitrary",),
                                             vmem_limit_bytes=VMEM_LIMIT),
        name="in_projection",
    )(x, mod, w_ext)


PREC = dict(seg=2, cum=2, a=1, neu=1, post=1, mn=1, b_o=1, b_s=3, gn=2)


def _dotp(a, b, dims, passes):
    dg = functools.partial(lax.dot_general, dimension_numbers=dims, preferred_element_type=F32)
    if passes == 1:
        return dg(a.astype(BF16), b.astype(BF16))
    ah, al = _split(a)
    if passes == 2:
        bh = b.astype(BF16)
        return dg(ah, bh) + dg(al, bh)
    bh, bl = _split(b)
    return dg(ah, bh) + (dg(ah, bl) + dg(al, bh))


def _dotp_rhs(a, b, passes):
    dg = functools.partial(lax.dot_general, dimension_numbers=_NN, preferred_element_type=F32)
    ab = a.astype(BF16)
    if passes == 1:
        return dg(ab, b.astype(BF16))
    bh, bl = _split(b)
    return dg(ab, bh) + dg(ab, bl)


def _rwkv_kernel(*refs, T, has_init, cb):
    (r_ref, k_ref, v_ref, g_ref, lo_ref, w0_ref, w2_ref, a0_ref, a2_ref,
     kk_ref, ka_ref, rk_ref, lng_ref, lnb_ref) = refs[:14]
    pos = 14
    init_ref = None
    if has_init:
        init_ref = refs[pos]
        pos += 1
    y_ref, sfin_ref = refs[pos:pos + 2]
    rp_scr, ov_scr, m_scr, n_scr, o_scr, bonus_scr, s_scr = refs[pos + 2:]

    nC = T // CHUNK
    C = CHUNK
    H = HEAD_DIM
    P = PREC
    C2 = 2 * C
    row = lax.broadcasted_iota(jnp.int32, (C, C), 0)
    col = lax.broadcasted_iota(jnp.int32, (C, C), 1)
    tri = [jnp.where(col <= row, 1.0, 0.0).astype(F32), jnp.where(col >= row, 1.0, 0.0).astype(F32)]
    lrow = lax.broadcasted_iota(jnp.int32, (LANES, LANES), 0)
    lcol = lax.broadcasted_iota(jnp.int32, (LANES, LANES), 1)
    same_head = (lrow // H) == (lcol // H)
    head_ones = jnp.where(same_head, 1.0, 0.0).astype(F32)
    eye = lrow == lcol
    rt_, ct_ = lrow % C, lcol % C
    strict_m = [same_head & (ct_ < rt_), same_head & (ct_ > rt_)]
    incl_m = [same_head & (ct_ <= rt_), same_head & (ct_ >= rt_)]
    lane = lax.broadcasted_iota(jnp.int32, (1, LANES), 1)
    hmask = [lane < H, lane >= H]

    def stack_heads(x):
        return jnp.concatenate([jnp.where(hmask[0], x, 0.0), jnp.where(hmask[1], x, 0.0)], axis=0)

    kkw = kk_ref[...]
    kaw = ka_ref[...]
    rkw = rk_ref[...]

    def phase_a(ci, carry):
        units = []
        for cc in range(cb):
            c = ci * cb + cc
            rows = pl.ds(pl.multiple_of(c * C, C), C)
            r = r_ref[rows, :]
            k = k_ref[rows, :]
            v = v_ref[rows, :]
            lo = lo_ref[rows, :]
            kkraw = k * kkw
            ss = _dotp(kkraw * kkraw, head_ones, _NN, P['seg'])
            kkn = kkraw / jnp.maximum(jnp.sqrt(ss), 1e-12)
            bonus = jnp.zeros((C, LANES), F32)
            vm = stack_heads(v)
            for d in range(2):
                hw = lo[:, d * LORA:(d + 1) * LORA]
                ha = lo[:, 2 * LORA + d * LORA:2 * LORA + (d + 1) * LORA]
                wl = w0_ref[d] + _dot(jnp.tanh(hw), w2_ref[d])
                ld = -jnp.exp(-_softplus(-wl) - 0.5)
                alpha = _sigmoid(a0_ref[d] + _dot(ha, a2_ref[d]))
                kd = k * (1.0 + (alpha - 1.0) * kaw)
                bb = kkn * alpha
                bonus = bonus + _dotp(r * kd * rkw, head_ones, _NN, P['seg']) * v
                g_incl = _dotp_rhs(tri[d], ld, P['cum'])
                g_excl = g_incl - ld
                g_end = g_incl[C - 1:C, :] if d == 0 else g_incl[0:1, :]
                ginv = jnp.exp(-g_incl)
                dec_end = jnp.exp(g_end - g_incl)
                units.append(dict(
                    d=d, c=c, vm=vm,
                    at=stack_heads(-kkn * jnp.exp(g_excl)), rt=stack_heads(r * jnp.exp(g_incl)),
                    bt=stack_heads(bb * ginv), kt=stack_heads(kd * ginv),
                    bend=stack_heads(bb * dec_end), kend=stack_heads(kd * dec_end), gend=jnp.exp(g_end)))
            bonus_scr[rows, :] = bonus
            o_scr[rows, :] = jnp.zeros((C, LANES), F32)

        for u in units:
            ar = jnp.concatenate([u['at'], u['rt']], axis=0)
            bk = jnp.concatenate([u['bt'], u['kt']], axis=0)
            u['a'] = _dotp(ar, bk, _NT, P['a'])
        for u in units:
            a = u['a']
            st, ic = strict_m[u['d']], incl_m[u['d']]
            u['p'] = jnp.where(st, a[:C2, :C2], 0.0)
            u['a_ak'] = jnp.where(st, a[:C2, C2:], 0.0)
            u['a_rb'] = jnp.where(ic, a[C2:, :C2], 0.0)
            u['a_rk'] = jnp.where(ic, a[C2:, C2:], 0.0)
        rvs = [_dotp(u['a_ak'], u['vm'], _NN, P['post']) for u in units]
        sq = [_dotp(u['p'], u['p'], _NN, P['neu']) for u in units]
        for i, u in enumerate(units):
            u['x'] = jnp.where(eye, 1.0, 0.0) + u['p']
            u['p'] = sq[i]
        for it in range(1, 6):
            res = [_dotp(u['p'], jnp.concatenate([u['x'], u['p']], axis=1) if it < 5 else u['x'], _NN, P['neu'])
                   for u in units]
            for i, u in enumerate(units):
                u['x'] = u['x'] + res[i][:, :C2]
                if it < 5:
                    u['p'] = res[i][:, C2:]
        rhs = [_dotp(u['x'], jnp.concatenate([u['at'], rvs[i]], axis=1), _NN, P['post'])
               for i, u in enumerate(units)]
        for i, u in enumerate(units):
            u['rhs'] = rhs[i]
        xs = [_dotp(jnp.concatenate([u['a_rb'], u['a_rk']], axis=1),
                    jnp.concatenate([u['rhs'], jnp.concatenate([jnp.zeros_like(u['vm']), u['vm']], axis=1)], axis=0),
                    _NN, P['post']) for u in units]
        ms = [_dotp(u['rhs'][:, :LANES], u['bend'], _TN, P['mn']) for u in units]
        ns = [_dotp(jnp.concatenate([u['rhs'][:, LANES:], u['vm']], axis=0),
                    jnp.concatenate([u['bend'], u['kend']], axis=0), _TN, P['mn']) for u in units]
        for i, u in enumerate(units):
            d, c = u['d'], u['c']
            rp = u['rt'] + xs[i][:, :LANES]
            ov = xs[i][:, LANES:]
            rp_scr[d, c] = rp[:C] + rp[C:]
            ov_scr[d, c] = ov[:C] + ov[C:]
            m_scr[d, c] = jnp.where(eye, jnp.broadcast_to(u['gend'], (LANES, LANES)), 0.0) + ms[i]
            n_scr[d, c] = ns[i]
        return carry

    lax.fori_loop(0, nC // cb, phase_a, 0)

    zero_blk = jnp.zeros((H, H), F32)
    for d in range(2):
        if has_init:
            s_scr[d] = jnp.concatenate([jnp.concatenate([init_ref[d, 0], zero_blk], axis=1),
                                        jnp.concatenate([zero_blk, init_ref[d, 1]], axis=1)], axis=0)
        else:
            s_scr[d] = jnp.zeros((LANES, LANES), F32)

    def phase_b(i, carry):
        cs = [i, nC - 1 - i]
        ss = [s_scr[d] for d in range(2)]
        os_ = [_dotp(rp_scr[d, cs[d]], ss[d], _NT, P['b_o']) for d in range(2)]
        sn = [_dotp(ss[d], m_scr[d, cs[d]], _NN, P['b_s']) for d in range(2)]
        for d in range(2):
            s_scr[d] = sn[d] + n_scr[d, cs[d]]
            rows = pl.ds(pl.multiple_of(cs[d] * C, C), C)
            o_scr[rows, :] = o_scr[rows, :] + (os_[d] + ov_scr[d, cs[d]])
        return carry

    lax.fori_loop(0, nC, phase_b, 0)

    for d in range(2):
        s = s_scr[d]
        sfin_ref[d, 0] = s[:H, :H]
        sfin_ref[d, 1] = s[H:, H:]

    lng = lng_ref[...]
    lnb = lnb_ref[...]

    rb = min(T, 256)

    def phase_c(c, carry):
        rows = pl.ds(pl.multiple_of(c * rb, rb), rb)
        o = o_scr[rows, :]
        st = _dotp(jnp.concatenate([o, o * o], axis=0), head_ones, _NN, P['gn']) * (1.0 / H)
        mu = st[:rb]
        var = st[rb:] - mu * mu
        on = (o - mu) * lax.rsqrt(var + GN_EPS) * lng + lnb
        y_ref[rows, :] = (on + bonus_scr[rows, :]) * _sigmoid(g_ref[rows, :])
        return carry

    lax.fori_loop(0, T // rb, phase_c, 0)


def _rwkv(proj, p, init, *, T, n_seq, row_blk0):
    nC = T // CHUNK
    npair = RWKV_HEADS // 2
    has_init = init is not None
    cb = lambda off: (lambda b, j: (b + row_blk0, off // LANES + j))
    vec = lambda b, j: (0, j)
    in_specs = [pl.BlockSpec((T, LANES), cb(C_R)),
                pl.BlockSpec((T, LANES), cb(C_K)),
                pl.BlockSpec((T, LANES), cb(C_V)),
                pl.BlockSpec((T, LANES), cb(C_G)),
                pl.BlockSpec((T, 4 * LORA), lambda b, j: (b + row_blk0, C_LORA // (4 * LORA))),
                pl.BlockSpec((2, 1, LANES), lambda b, j: (0, 0, j)),
                pl.BlockSpec((2, LORA, LANES), lambda b, j: (0, 0, j)),
                pl.BlockSpec((2, 1, LANES), lambda b, j: (0, 0, j)),
                pl.BlockSpec((2, LORA, LANES), lambda b, j: (0, 0, j)),
                pl.BlockSpec((1, LANES), vec), pl.BlockSpec((1, LANES), vec),
                pl.BlockSpec((1, LANES), vec), pl.BlockSpec((1, LANES), vec),
                pl.BlockSpec((1, LANES), vec)]
    args = [proj, proj, proj, proj, proj,
            p['rwkv_w0'].reshape(2, 1, RWKV_W), p['rwkv_w2'], p['rwkv_a0'].reshape(2, 1, RWKV_W), p['rwkv_a2'],
            p['rwkv_k_k'].reshape(1, RWKV_W), p['rwkv_k_a'].reshape(1, RWKV_W),
            p['rwkv_r_k'].reshape(1, RWKV_W), p['rwkv_lnx_g'].reshape(1, RWKV_W),
            p['rwkv_lnx_b'].reshape(1, RWKV_W)]
    st_spec = pl.BlockSpec((None, 2, 2, HEAD_DIM, HEAD_DIM), lambda b, j: (b, 0, j, 0, 0))
    if has_init:
        in_specs.append(st_spec)
        args.append(init)
    sc_o = lambda: pltpu.VMEM((2, nC, CHUNK, LANES), F32)
    sc_s = lambda: pltpu.VMEM((2, nC, LANES, LANES), F32)
    return pl.pallas_call(
        functools.partial(_rwkv_kernel, T=T, has_init=has_init, cb=min(RWKV_CB, nC)),
        grid=(n_seq, npair),
        in_specs=in_specs,
        out_specs=[pl.BlockSpec((T, LANES), lambda b, j: (b, j)), st_spec],
        out_shape=[jax.ShapeDtypeStruct((n_seq * T, RWKV_W), F32),
                   jax.ShapeDtypeStruct((n_seq, 2, RWKV_HEADS, HEAD_DIM, HEAD_DIM), F32)],
        scratch_shapes=[sc_o(), sc_o(), sc_s(), sc_s(),
                        pltpu.VMEM((T, LANES), F32), pltpu.VMEM((T, LANES), F32),
                        pltpu.VMEM((2, LANES, LANES), F32)],
        compiler_params=pltpu.CompilerParams(dimension_semantics=("arbitrary", "arbitrary"),
                                             vmem_limit_bytes=VMEM_LIMIT),
        name="rwkv_T%d" % T,
    )(*args)


def _dft_mats(L):
    f = np.arange(L, dtype=np.int64)
    ph = (np.outer(f, f) % (2 * L)).astype(np.float64) * (math.pi / L)
    return np.cos(ph), -np.sin(ph)


def _filter_features(L):
    t = np.linspace(0.0, 1.0, L, dtype=np.float32)[:, None]
    bands = (FILTER_EMB - 1) // 2
    t_res = np.arange(L, dtype=np.float32)[:, None]
    f = np.linspace(1e-4, bands - 1, bands, dtype=np.float32)[None, :]
    ang = (2.0 * math.pi * t_res * f / L).astype(np.float32)
    z = np.concatenate([t, np.cos(ang), np.sin(ang)], -1).astype(np.float32)
    zp = np.zeros((L, FILTER_HID), np.float32)
    zp[:, :FILTER_EMB] = z
    deltas = np.abs(np.linspace(HY_MIN_DECAY, HY_MAX_DECAY, HYENA_W, dtype=np.float32))
    return zp, t.astype(np.float32), deltas[None, :].astype(np.float32)


def _hyfilter_kernel(z_ref, t_ref, dl_ref, w1_ref, b1_ref, q1_ref, w2_ref, b2_ref, q2_ref, wc_ref, wa_ref,
                     hs_ref, hd_ref, knyq_ref, *, L):
    hm = jnp.sin(q1_ref[...] * (_dot3(z_ref[...], w1_ref[...]) + b1_ref[...]))
    hm = jnp.sin(q2_ref[...] * (_dot3(hm, w2_ref[...]) + b2_ref[...]))
    win = jnp.exp(-t_ref[...] * dl_ref[...])
    lag = lax.broadcasted_iota(jnp.int32, (L, 1), 0)
    alt = jnp.where(lag % 2 == 0, 1.0, -1.0).astype(F32)
    hc = _dot3(hm, wc_ref[...]) * win
    ha = jnp.where(lag == 0, 0.0, _dot3(hm, wa_ref[...]) * win)
    hs_ref[...] = hc + ha
    hd_ref[...] = hc - ha
    knyq_ref[...] = jnp.sum((hc + ha) * alt, axis=0, keepdims=True) * (1.0 / (2 * L))


def _hyspec_kernel(cm_ref, sm_ref, hs_ref, hd_ref, kre_ref, kim_ref, *, L, tf):
    f = pl.program_id(2) * tf + lax.broadcasted_iota(jnp.int32, (tf, 1), 0)
    wf = jnp.where(f == 0, 1.0, 2.0).astype(F32) * (1.0 / (2 * L))
    kre_ref[...] = _dot3(cm_ref[...], hs_ref[...]) * wf
    kim_ref[...] = _dot3(sm_ref[...], hd_ref[...]) * wf


def _hyena_filter(p, L, cm, sm, cw=128):
    zp, t, dl = _filter_features(L)
    w1 = jnp.zeros((FILTER_HID, FILTER_HID), F32).at[:FILTER_EMB].set(p['hy_f_w1'])
    r2 = lambda a: a.reshape(1, -1)
    ncb = HYENA_W // cw
    full = lambda shape: pl.BlockSpec(shape, lambda n, j: (0,) * len(shape))
    shp = jax.ShapeDtypeStruct((2, L, HYENA_W), F32)
    hs, hd, knyq = pl.pallas_call(
        functools.partial(_hyfilter_kernel, L=L),
        grid=(2, ncb),
        in_specs=[full((L, FILTER_HID)), full((L, 1)),
                  pl.BlockSpec((1, cw), lambda n, j: (0, j)),
                  full((FILTER_HID, FILTER_HID)), full((1, FILTER_HID)), full((1, FILTER_HID)),
                  full((FILTER_HID, FILTER_HID)), full((1, FILTER_HID)), full((1, FILTER_HID)),
                  pl.BlockSpec((FILTER_HID, cw), lambda n, j: (0, n * 2 * ncb + j)),
                  pl.BlockSpec((FILTER_HID, cw), lambda n, j: (0, n * 2 * ncb + ncb + j))],
        out_specs=[pl.BlockSpec((None, L, cw), lambda n, j: (n, 0, j)),
                   pl.BlockSpec((None, L, cw), lambda n, j: (n, 0, j)),
                   pl.BlockSpec((None, 1, cw), lambda n, j: (n, 0, j))],
        out_shape=[shp, shp, jax.ShapeDtypeStruct((2, 1, HYENA_W), F32)],
        compiler_params=pltpu.CompilerParams(dimension_semantics=("arbitrary", "arbitrary"),
                                             vmem_limit_bytes=VMEM_LIMIT),
        name="hyena_filter_L%d" % L,
    )(jnp.asarray(zp), jnp.asarray(t), jnp.asarray(dl), w1, r2(p['hy_f_b1']), r2(p['hy_f_freq1']),
      p['hy_f_w2'], r2(p['hy_f_b2']), r2(p['hy_f_freq2']), p['hy_f_w3'], p['hy_f_w3'])
    tf = min(256, L)
    sw = 256
    kre, kim = pl.pallas_call(
        functools.partial(_hyspec_kernel, L=L, tf=tf),
        grid=(2, HYENA_W // sw, L // tf),
        in_specs=[pl.BlockSpec((tf, L), lambda n, j, i: (i, 0)),
                  pl.BlockSpec((tf, L), lambda n, j, i: (i, 0)),
                  pl.BlockSpec((None, L, sw), lambda n, j, i: (n, 0, j)),
                  pl.BlockSpec((None, L, sw), lambda n, j, i: (n, 0, j))],
        out_specs=[pl.BlockSpec((None, tf, sw), lambda n, j, i: (n, i, j)),
                   pl.BlockSpec((None, tf, sw), lambda n, j, i: (n, i, j))],
        out_shape=[shp, shp],
        compiler_params=pltpu.CompilerParams(dimension_semantics=("arbitrary",) * 3,
                                             vmem_limit_bytes=VMEM_LIMIT),
        name="hyena_spectrum_L%d" % L,
    )(cm, sm, hs, hd)
    return kre, kim, knyq


def _hyena_kernel(v_ref, x1_ref, x2_ref, sw_ref, bias_ref, cm_ref, sm_ref, kre_ref, kim_ref, knyq_ref,
                  o_ref, zb_scr, zf_scr, xs_scr, yre_scr, yim_scr, *, L, rb):
    nrb = L // rb
    t = lax.broadcasted_iota(jnp.int32, (L, 1), 0)
    first = t == 0
    last = t == L - 1
    alt_all = jnp.where(t % 2 == 0, 1.0, -1.0).astype(F32)
    tb = lax.broadcasted_iota(jnp.int32, (rb, 1), 0)
    alt_blk = jnp.where(tb % 2 == 0, 1.0, -1.0).astype(F32)
    sw = sw_ref[...]

    def short(ref, s):
        u = ref[...]
        prev = jnp.where(first, 0.0, pltpu.roll(u, 1, 0))
        nxt = jnp.where(last, 0.0, pltpu.roll(u, L - 1, 0))
        return prev * sw[0, s][None, :] + u * sw[1, s][None, :] + nxt * sw[2, s][None, :]

    z0 = short(v_ref, 0)
    zf_scr[...] = z0
    zb_scr[...] = z0.astype(BF16)
    unyq = jnp.sum(z0 * alt_all, axis=0, keepdims=True)

    for n in range(2):
        xs_scr[...] = short(x1_ref if n == 0 else x2_ref, n + 1)

        def fwd(fb, carry):
            rows = pl.ds(pl.multiple_of(fb * rb, rb), rb)
            zb = zb_scr[...]
            ure = jnp.dot(cm_ref[rows, :], zb, preferred_element_type=F32)
            uim = jnp.dot(sm_ref[rows, :], zb, preferred_element_type=F32)
            kre = kre_ref[n, rows, :]
            kim = kim_ref[n, rows, :]
            yre_scr[rows, :] = (ure * kre - uim * kim).astype(BF16)
            yim_scr[rows, :] = (ure * kim + uim * kre).astype(BF16)
            return carry

        lax.fori_loop(0, nrb, fwd, 0)
        ynyq = unyq * knyq_ref[n]
        bias = bias_ref[n]

        def inv(ib, acc):
            rows = pl.ds(pl.multiple_of(ib * rb, rb), rb)
            y = (jnp.dot(cm_ref[rows, :], yre_scr[...], preferred_element_type=F32)
                 + jnp.dot(sm_ref[rows, :], yim_scr[...], preferred_element_type=F32))
            y = y + alt_blk * ynyq + zf_scr[rows, :] * bias
            znew = xs_scr[rows, :] * y
            if n == 0:
                zf_scr[rows, :] = znew
                zb_scr[rows, :] = znew.astype(BF16)
                acc = acc + jnp.sum(znew * alt_blk, axis=0, keepdims=True)
            else:
                o_ref[rows, :] = znew
            return acc

        unyq = lax.fori_loop(0, nrb, inv, jnp.zeros_like(unyq))


def _hyena(proj, p, filt, cm, sm, *, L, n_seq, row_blk0, cw=256):
    kre, kim, knyq = filt
    ncb = HYENA_W // cw
    cb = lambda s: (lambda j, b: (b + row_blk0, (C_HY + s * HYENA_W) // cw + j))
    once = pl.Buffered(1)
    sw = p['hy_short_w'].reshape(3, 3, HYENA_W)
    rb = min(256, L)
    return pl.pallas_call(
        functools.partial(_hyena_kernel, L=L, rb=rb),
        grid=(ncb, n_seq),
        in_specs=[pl.BlockSpec((L, cw), cb(0)), pl.BlockSpec((L, cw), cb(1)), pl.BlockSpec((L, cw), cb(2)),
                  pl.BlockSpec((3, 3, cw), lambda j, b: (0, 0, j)),
                  pl.BlockSpec((2, 1, cw), lambda j, b: (0, 0, j)),
                  pl.BlockSpec((L, L), lambda j, b: (0, 0), pipeline_mode=once),
                  pl.BlockSpec((L, L), lambda j, b: (0, 0), pipeline_mode=once),
                  pl.BlockSpec((2, L, cw), lambda j, b: (0, 0, j), pipeline_mode=once),
                  pl.BlockSpec((2, L, cw), lambda j, b: (0, 0, j), pipeline_mode=once),
                  pl.BlockSpec((2, 1, cw), lambda j, b: (0, 0, j))],
        out_specs=pl.BlockSpec((L, cw), lambda j, b: (b, j)),
        out_shape=jax.ShapeDtypeStruct((n_seq * L, HYENA_W), F32),
        scratch_shapes=[pltpu.VMEM((L, cw), BF16), pltpu.VMEM((L, cw), F32), pltpu.VMEM((L, cw), F32),
                        pltpu.VMEM((L, cw), BF16), pltpu.VMEM((L, cw), BF16)],
        compiler_params=pltpu.CompilerParams(dimension_semantics=("arbitrary", "arbitrary"),
                                             vmem_limit_bytes=VMEM_LIMIT),
        name="hyena_L%d" % L,
    )(proj, proj, proj, sw, p['hy_bias'].reshape(2, 1, HYENA_W), cm.astype(BF16), sm.astype(BF16),
      kre, kim, knyq)


def _row_max(x):
    t = x[:, :LANES]
    for c in range(1, x.shape[1] // LANES):
        t = jnp.maximum(t, x[:, c * LANES:(c + 1) * LANES])
    return jnp.max(t, axis=-1, keepdims=True)


def _row_sum(x):
    t = x[:, :LANES]
    for c in range(1, x.shape[1] // LANES):
        t = t + x[:, c * LANES:(c + 1) * LANES]
    return jnp.sum(t, axis=-1, keepdims=True)


def _head_masks():
    lane = lax.broadcasted_iota(jnp.int32, (1, KV_W), 1) // HEAD_DIM
    return [lane == h for h in range(ATTN_KV_HEADS)]


def _attend_group(qg, keys, vals, masks, sinks, hm):
    nh = len(hm)
    qm = [jnp.where(hm[h], qg, 0.0).astype(BF16) for h in range(nh)]
    s = [[_dot_nt(qm[h], k) for k in keys] for h in range(nh)]
    s = [[sc if mk is None else jnp.where(mk, sc, -jnp.inf) for sc, mk in zip(s[h], masks)] for h in range(nh)]
    m = [jnp.maximum(_row_max(jnp.concatenate(s[h], axis=1)), sinks[h]) for h in range(nh)]
    p = [[jnp.exp(sc - m[h]) for sc in s[h]] for h in range(nh)]
    den = [_row_sum(jnp.concatenate(p[h], axis=1)) + jnp.exp(sinks[h] - m[h]) for h in range(nh)]
    pv = [[_dot(pc, v) for pc, v in zip(p[h], vals)] for h in range(nh)]
    acc = jnp.zeros(qg.shape, F32)
    for h in range(nh):
        tot = pv[h][0]
        for t in pv[h][1:]:
            tot = tot + t
        acc = jnp.where(hm[h], tot / den[h], acc)
    return acc


def _ctx_attn_kernel(q_ref, k_ref, v_ref, sink_ref, o_ref, *, S):
    scale = HEAD_DIM ** -0.5
    hm = _head_masks()
    k = k_ref[...].astype(BF16)
    v = v_ref[...].astype(BF16)
    for g in range(GQA_GROUP):
        qg = q_ref[:, g * KV_W:(g + 1) * KV_W] * scale
        sinks = [jnp.broadcast_to(sink_ref[0:1, h * GQA_GROUP + g:h * GQA_GROUP + g + 1], (S, 1))
                 for h in range(ATTN_KV_HEADS)]
        o_ref[:, g * KV_W:(g + 1) * KV_W] = _attend_group(qg, [k], [v], [None], sinks, hm)


def _ctx_attention(proj, sink, *, S, n_seq):
    return pl.pallas_call(
        functools.partial(_ctx_attn_kernel, S=S),
        grid=(n_seq,),
        in_specs=[pl.BlockSpec((S, ATTN_W), lambda b: (b, C_Q // ATTN_W)),
                  pl.BlockSpec((S, KV_W), lambda b: (b, C_AK // KV_W)),
                  pl.BlockSpec((S, KV_W), lambda b: (b, C_AV // KV_W)),
                  pl.BlockSpec((1, LANES), lambda b: (0, 0))],
        out_specs=pl.BlockSpec((S, ATTN_W), lambda b: (b, 0)),
        out_shape=jax.ShapeDtypeStruct((n_seq * S, ATTN_W), F32),
        compiler_params=pltpu.CompilerParams(dimension_semantics=("arbitrary",),
                                             vmem_limit_bytes=VMEM_LIMIT),
        name="ctx_attention",
    )(proj, proj, proj, sink)


def _rope(x, cos, sin_signed, first_half):
    n = x.shape[-1]
    partner = jnp.where(first_half, pltpu.roll(x, n - HEAD_DIM // 2, 1), pltpu.roll(x, HEAD_DIM // 2, 1))
    return x * cos + partner * sin_signed


def _lat_attn_kernel(q_ref, kp_ref, kc_ref, kn_ref, vp_ref, vc_ref, vn_ref, ck_ref, cv_ref,
                     cq_ref, sq_ref, cp_ref, sp_ref, cc_ref, sc_ref, cn_ref, sn_ref, sink_ref, o_ref, *, T):
    i = pl.program_id(1)
    scale = HEAD_DIM ** -0.5
    hm = _head_masks()
    fh_q = (lax.broadcasted_iota(jnp.int32, (1, ATTN_W), 1) % HEAD_DIM) < HEAD_DIM // 2
    fh_k = (lax.broadcasted_iota(jnp.int32, (1, KV_W), 1) % HEAD_DIM) < HEAD_DIM // 2
    q = _rope(q_ref[...], cq_ref[...], sq_ref[...], fh_q) * scale
    kl = jnp.concatenate([_rope(kp_ref[...], cp_ref[...], sp_ref[...], fh_k),
                          _rope(kc_ref[...], cc_ref[...], sc_ref[...], fh_k),
                          _rope(kn_ref[...], cn_ref[...], sn_ref[...], fh_k)], axis=0).astype(BF16)
    vl = jnp.concatenate([vp_ref[...], vc_ref[...], vn_ref[...]], axis=0).astype(BF16)
    kc = ck_ref[...].astype(BF16)
    vc = cv_ref[...].astype(BF16)
    qpos = i * BLOCK + lax.broadcasted_iota(jnp.int32, (BLOCK, 3 * BLOCK), 0)
    kpos = (i - 1) * BLOCK + lax.broadcasted_iota(jnp.int32, (BLOCK, 3 * BLOCK), 1)
    ok = (jnp.abs(qpos - kpos) <= WINDOW) & (kpos >= 0) & (kpos < T)
    for g in range(GQA_GROUP):
        qg = q[:, g * KV_W:(g + 1) * KV_W]
        sinks = [jnp.broadcast_to(sink_ref[0:1, h * GQA_GROUP + g:h * GQA_GROUP + g + 1], (BLOCK, 1))
                 for h in range(ATTN_KV_HEADS)]
        o_ref[:, g * KV_W:(g + 1) * KV_W] = _attend_group(qg, [kl, kc], [vl, vc], [ok, None], sinks, hm)


def _rope_tables(T):
    rows = T // GRID_W
    row = np.repeat(np.arange(rows), GRID_W).astype(np.float32)
    col = np.tile(np.arange(GRID_W), rows).astype(np.float32)
    nf = HEAD_DIM // 4
    inv = (ROPE_BASE ** (-np.arange(nf, dtype=np.float32) / nf)).astype(np.float32)
    ang = np.concatenate([row[:, None] * inv, col[:, None] * inv], -1).astype(np.float32)
    cos = np.cos(ang).astype(np.float32)
    sin = np.sin(ang).astype(np.float32)
    cos_h = np.concatenate([cos, cos], -1)
    sin_h = np.concatenate([-sin, sin], -1)
    return cos_h, sin_h


def _lat_attention(proj, cache_k, cache_v, sink, *, T, n_seq, row_blk0):
    nb = T // BLOCK
    P = cache_k.shape[1]
    cos_h, sin_h = _rope_tables(T)
    cq = jnp.asarray(np.tile(cos_h, (1, ATTN_HEADS)))
    sq = jnp.asarray(np.tile(sin_h, (1, ATTN_HEADS)))
    ck = jnp.asarray(np.tile(cos_h, (1, ATTN_KV_HEADS)))
    sk = jnp.asarray(np.tile(sin_h, (1, ATTN_KV_HEADS)))
    r0 = row_blk0 * nb
    prv = lambda i: jnp.maximum(i - 1, 0)
    nxt = lambda i: jnp.minimum(i + 1, nb - 1)
    kcol, vcol = C_AK // KV_W, C_AV // KV_W
    kv = lambda col, f: pl.BlockSpec((BLOCK, KV_W), lambda b, i: (r0 + b * nb + f(i), col))
    tab = lambda w, f: pl.BlockSpec((BLOCK, w), lambda b, i: (f(i), 0))
    cur = lambda i: i
    return pl.pallas_call(
        functools.partial(_lat_attn_kernel, T=T),
        grid=(n_seq, nb),
        in_specs=[pl.BlockSpec((BLOCK, ATTN_W), lambda b, i: (r0 + b * nb + i, C_Q // ATTN_W)),
                  kv(kcol, prv), kv(kcol, cur), kv(kcol, nxt),
                  kv(vcol, prv), kv(vcol, cur), kv(vcol, nxt),
                  pl.BlockSpec((None, P, KV_W), lambda b, i: (b, 0, 0)),
                  pl.BlockSpec((None, P, KV_W), lambda b, i: (b, 0, 0)),
                  tab(ATTN_W, cur), tab(ATTN_W, cur),
                  tab(KV_W, prv), tab(KV_W, prv), tab(KV_W, cur), tab(KV_W, cur), tab(KV_W, nxt), tab(KV_W, nxt),
                  pl.BlockSpec((1, LANES), lambda b, i: (0, 0))],
        out_specs=pl.BlockSpec((BLOCK, ATTN_W), lambda b, i: (b * nb + i, 0)),
        out_shape=jax.ShapeDtypeStruct((n_seq * T, ATTN_W), F32),
        compiler_params=pltpu.CompilerParams(dimension_semantics=("arbitrary", "arbitrary"),
                                             vmem_limit_bytes=VMEM_LIMIT),
        name="latent_attention",
    )(proj, proj, proj, proj, proj, proj, proj, cache_k, cache_v, cq, sq, ck, sk, ck, sk, ck, sk, sink)


def _layer_norm(y, g, b):
    mu = jnp.mean(y, axis=-1, keepdims=True)
    d = y - mu
    var = jnp.mean(d * d, axis=-1, keepdims=True)
    return d * lax.rsqrt(var + LN_EPS) * g + b


def _outproj_kernel(ac_ref, bc_ref, cc_ref, al_ref, bl_ref, cl_ref, x_ref, mod_ref, w_ref,
                    g_ref, be_ref, x1_ref, h2_ref, *, tm, n_ctx_tiles):
    r1, r2 = RWKV_W, RWKV_W + HYENA_W
    sub = 128

    def tile(a_ref, b_ref, c_ref):
        ms = []
        for s in range(tm // sub):
            rows = slice(s * sub, (s + 1) * sub)
            ms.append(jnp.dot(a_ref[rows, :].astype(BF16), w_ref[0:r1, :], preferred_element_type=F32)
                      + jnp.dot(b_ref[rows, :].astype(BF16), w_ref[r1:r2, :], preferred_element_type=F32)
                      + jnp.dot(c_ref[rows, :].astype(BF16), w_ref[r2:, :], preferred_element_type=F32))
        for s in range(tm // sub):
            rows = slice(s * sub, (s + 1) * sub)
            blk = (s * sub) // MOD_BLK
            gate1 = mod_ref[blk, 2:3, :]
            shift2 = mod_ref[blk, 3:4, :]
            scale2 = mod_ref[blk, 4:5, :]
            x1 = _layer_norm(DN_ALPHA * x_ref[rows, :] + gate1 * ms[s], g_ref[...], be_ref[...])
            x1_ref[rows, :] = x1
            h2_ref[rows, :] = (x1 * (1.0 + scale2) + shift2).astype(BF16)

    pl.when(pl.program_id(0) < n_ctx_tiles)(lambda: tile(ac_ref, bc_ref, cc_ref))
    pl.when(pl.program_id(0) >= n_ctx_tiles)(lambda: tile(al_ref, bl_ref, cl_ref))


def _out_projection(mix_ctx, mix_lat, x, mod, w_out, l, g, b, tm=256):
    N = x.shape[0]
    nsub = tm // MOD_BLK
    nct = mix_ctx[0].shape[0] // tm
    nlt = mix_lat[0].shape[0] // tm
    row = lambda w: pl.BlockSpec((tm, w), lambda i: (i, 0))
    ctx = lambda w: pl.BlockSpec((tm, w), lambda i: (jnp.minimum(i, nct - 1), 0))
    lat = lambda w: pl.BlockSpec((tm, w), lambda i: (jnp.clip(i - nct, 0, nlt - 1), 0))
    full = lambda r: pl.BlockSpec((r, D_MODEL), lambda i: (0, 0))
    return pl.pallas_call(
        functools.partial(_outproj_kernel, tm=tm, n_ctx_tiles=nct),
        grid=(N // tm,),
        in_specs=[ctx(RWKV_W), ctx(HYENA_W), ctx(ATTN_W), lat(RWKV_W), lat(HYENA_W), lat(ATTN_W), row(D_MODEL),
                  pl.BlockSpec((nsub, 6, D_MODEL), lambda i: (i, 0, 0)),
                  pl.BlockSpec((None, D_MODEL, D_MODEL), lambda i: (l, 0, 0)), full(1), full(1)],
        out_specs=[row(D_MODEL), row(D_MODEL)],
        out_shape=[jax.ShapeDtypeStruct((N, D_MODEL), F32), jax.ShapeDtypeStruct((N, D_MODEL), BF16)],
        compiler_params=pltpu.CompilerParams(dimension_semantics=("arbitrary",),
                                             vmem_limit_bytes=VMEM_LIMIT),
        name="out_projection",
    )(*mix_ctx, *mix_lat, x, mod, w_out, g.reshape(1, D_MODEL), b.reshape(1, D_MODEL))


HALO = 16


def _ffn_kernel(hp_ref, h_ref, hn_ref, x1_ref, mod_ref, wa_ref, wb_ref, cwa_ref, cwb_ref, wd_ref, g_ref, be_ref,
                o_ref, hx_scr, ua_scr, ub_scr, acc_scr, *, tm, tf, nsub, n_ctx, s_ctx, s_lat):
    i = pl.program_id(0)
    j = pl.program_id(1)

    @pl.when(j == 0)
    def _():
        hx_scr[0:HALO, :] = hp_ref[...]
        hx_scr[HALO:HALO + tm, :] = h_ref[...]
        hx_scr[HALO + tm:, :] = hn_ref[...]
        acc_scr[...] = jnp.zeros_like(acc_scr)

    hx = hx_scr[...]
    ua_scr[...] = jnp.dot(hx, wa_ref[...], preferred_element_type=F32)
    ub_scr[...] = jnp.dot(hx, wb_ref[...], preferred_element_type=F32)
    grow = i * tm + lax.broadcasted_iota(jnp.int32, (tm, 1), 0)
    pos = jnp.where(grow < n_ctx, grow % s_ctx, (grow - n_ctx) % s_lat)
    slen = jnp.where(grow < n_ctx, s_ctx, s_lat)
    mprev = jnp.where(pos == 0, 0.0, 1.0).astype(F32)
    mnext = jnp.where(pos == slen - 1, 0.0, 1.0).astype(F32)

    def conv(u_scr, cw_ref):
        up = u_scr[HALO - 1:HALO - 1 + tm, :] * mprev
        uc = u_scr[HALO:HALO + tm, :]
        un = u_scr[HALO + 1:HALO + 1 + tm, :] * mnext
        return up * cw_ref[0:1, :] + uc * cw_ref[1:2, :] + un * cw_ref[2:3, :]

    a = conv(ua_scr, cwa_ref)
    b = conv(ub_scr, cwb_ref)
    act = (a * _sigmoid(a) * b).astype(BF16)
    acc_scr[...] += jnp.dot(act, wd_ref[...], preferred_element_type=F32)

    @pl.when(j == pl.num_programs(1) - 1)
    def _():
        for s in range(nsub):
            rows = slice(s * MOD_BLK, (s + 1) * MOD_BLK)
            gate2 = mod_ref[s, 5:6, :]
            o_ref[rows, :] = _layer_norm(DN_ALPHA * x1_ref[rows, :] + gate2 * acc_scr[rows, :],
                                         g_ref[...], be_ref[...])


def _ffn(h2, x1, mod, wa, wb, cw, wd, l, g, b, *, n_ctx, s_ctx, s_lat, tm=512, tf=512):
    N = x1.shape[0]
    nsub = tm // MOD_BLK
    nh = tm // HALO
    last_h = N // HALO - 1
    pad = ((0, 0), (0, D_FF_PAD - D_FF))
    cwa, cwb = jnp.pad(cw[:, :D_FF], pad), jnp.pad(cw[:, D_FF:], pad)
    return pl.pallas_call(
        functools.partial(_ffn_kernel, tm=tm, tf=tf, nsub=nsub, n_ctx=n_ctx, s_ctx=s_ctx, s_lat=s_lat),
        grid=(N // tm, D_FF_PAD // tf),
        in_specs=[pl.BlockSpec((HALO, D_MODEL), lambda i, j: (jnp.maximum(i * nh - 1, 0), 0)),
                  pl.BlockSpec((tm, D_MODEL), lambda i, j: (i, 0)),
                  pl.BlockSpec((HALO, D_MODEL), lambda i, j: (jnp.minimum((i + 1) * nh, last_h), 0)),
                  pl.BlockSpec((tm, D_MODEL), lambda i, j: (i, 0)),
                  pl.BlockSpec((nsub, 6, D_MODEL), lambda i, j: (i, 0, 0)),
                  pl.BlockSpec((None, D_MODEL, tf), lambda i, j: (l, 0, j)),
                  pl.BlockSpec((None, D_MODEL, tf), lambda i, j: (l, 0, j)),
                  pl.BlockSpec((3, tf), lambda i, j: (0, j)),
                  pl.BlockSpec((3, tf), lambda i, j: (0, j)),
                  pl.BlockSpec((None, tf, D_MODEL), lambda i, j: (l, j, 0)),
                  pl.BlockSpec((1, D_MODEL), lambda i, j: (0, 0)),
                  pl.BlockSpec((1, D_MODEL), lambda i, j: (0, 0))],
        out_specs=pl.BlockSpec((tm, D_MODEL), lambda i, j: (i, 0)),
        out_shape=jax.ShapeDtypeStruct((N, D_MODEL), F32),
        scratch_shapes=[pltpu.VMEM((tm + 2 * HALO, D_MODEL), BF16),
                        pltpu.VMEM((tm + 2 * HALO, tf), F32),
                        pltpu.VMEM((tm + 2 * HALO, tf), F32),
                        pltpu.VMEM((tm, D_MODEL), F32)],
        compiler_params=pltpu.CompilerParams(dimension_semantics=("arbitrary", "arbitrary"),
                                             vmem_limit_bytes=VMEM_LIMIT),
        name="conv_ffn",
    )(h2, h2, h2, x1, mod, wa, wb, cwa, cwb, wd, g.reshape(1, D_MODEL), b.reshape(1, D_MODEL))


def _prep_up_kernel(a_ref, b_ref, wa_ref, wb_ref):
    zeros = jnp.zeros((a_ref.shape[0], D_FF_PAD - D_FF), BF16)
    wa_ref[:, :D_FF] = a_ref[...].astype(BF16)
    wa_ref[:, D_FF:] = zeros
    wb_ref[:, :D_FF] = b_ref[...].astype(BF16)
    wb_ref[:, D_FF:] = zeros


def _prep_down_kernel(w_ref, o_ref, *, n_real):
    keep = pl.program_id(1) < n_real
    o_ref[...] = jnp.where(keep, w_ref[...], 0.0).astype(BF16)


def _prep_in_kernel(w_ref, w1_ref, a1_ref, o_ref):
    n = w_ref.shape[-1]
    o_ref[:, :C_Q] = w_ref[:, :C_Q].astype(BF16)
    for g in range(GQA_GROUP):
        for h in range(ATTN_KV_HEADS):
            src = C_Q + (h * GQA_GROUP + g) * HEAD_DIM
            dst = C_Q + (g * ATTN_KV_HEADS + h) * HEAD_DIM
            o_ref[:, dst:dst + HEAD_DIM] = w_ref[:, src:src + HEAD_DIM].astype(BF16)
    o_ref[:, C_AK:n] = w_ref[:, C_AK:n].astype(BF16)
    o_ref[:, n:] = jnp.concatenate([w1_ref[0], w1_ref[1], a1_ref[0], a1_ref[1]], axis=1).astype(BF16)


def _cast_kernel(w_ref, o_ref):
    o_ref[...] = w_ref[...].astype(BF16)


def _prep_weights(w_in, rwkv_w1, rwkv_a1, w_out, ffn_w_up, ffn_w_down):
    L, D = w_in.shape[0], D_MODEL
    cp = lambda n: pltpu.CompilerParams(dimension_semantics=("arbitrary",) * n, vmem_limit_bytes=VMEM_LIMIT)
    tr = 128
    half = jax.ShapeDtypeStruct((L, D, D_FF_PAD), BF16)
    wa, wb = pl.pallas_call(
        _prep_up_kernel,
        grid=(L, D // tr),
        in_specs=[pl.BlockSpec((None, tr, D_FF), lambda l, i: (l, i, 0)),
                  pl.BlockSpec((None, tr, D_FF), lambda l, i: (l, i, 1))],
        out_specs=[pl.BlockSpec((None, tr, D_FF_PAD), lambda l, i: (l, i, 0))] * 2,
        out_shape=[half, half], compiler_params=cp(2), name="prep_ffn_up",
    )(ffn_w_up, ffn_w_up)
    nr = D_FF // LANES
    wd = pl.pallas_call(
        functools.partial(_prep_down_kernel, n_real=nr),
        grid=(L, D_FF_PAD // LANES),
        in_specs=[pl.BlockSpec((None, LANES, D), lambda l, j: (l, jnp.minimum(j, nr - 1), 0))],
        out_specs=pl.BlockSpec((None, LANES, D), lambda l, j: (l, j, 0)),
        out_shape=jax.ShapeDtypeStruct((L, D_FF_PAD, D), BF16), compiler_params=cp(2), name="prep_ffn_down",
    )(ffn_w_down)
    tr2 = 256
    n_in = w_in.shape[-1]
    w_ext = pl.pallas_call(
        _prep_in_kernel,
        grid=(L, D // tr2),
        in_specs=[pl.BlockSpec((None, tr2, n_in), lambda l, i: (l, i, 0)),
                  pl.BlockSpec((None, 2, tr2, LORA), lambda l, i: (l, 0, i, 0)),
                  pl.BlockSpec((None, 2, tr2, LORA), lambda l, i: (l, 0, i, 0))],
        out_specs=pl.BlockSpec((None, tr2, IN_W_EXT), lambda l, i: (l, i, 0)),
        out_shape=jax.ShapeDtypeStruct((L, D, IN_W_EXT), BF16), compiler_params=cp(2), name="prep_in",
    )(w_in, rwkv_w1, rwkv_a1)
    hb0 = (RWKV_W + HYENA_W) // HEAD_DIM

    def src_rows(i):
        a = i - hb0
        perm = hb0 + (a % ATTN_KV_HEADS) * GQA_GROUP + a // ATTN_KV_HEADS
        return jnp.where((a >= 0) & (a < ATTN_HEADS), perm, i)

    wo = pl.pallas_call(
        _cast_kernel,
        grid=(L, D // HEAD_DIM),
        in_specs=[pl.BlockSpec((None, HEAD_DIM, D), lambda l, i: (l, src_rows(i), 0))],
        out_specs=pl.BlockSpec((None, HEAD_DIM, D), lambda l, i: (l, i, 0)),
        out_shape=jax.ShapeDtypeStruct((L, D, D), BF16), compiler_params=cp(2), name="prep_out",
    )(w_out)
    return w_ext, wo, wa, wb, wd


def kernel(x_prompt, x_sample, c, state_rwkv, cache_k, cache_v, c_ctx, w_mod, b_mod, w_in, rwkv_w0, rwkv_w1, rwkv_w2, rwkv_a0, rwkv_a1, rwkv_a2, rwkv_k_k, rwkv_k_a, rwkv_r_k, rwkv_lnx_g, rwkv_lnx_b, hy_short_w, hy_f_w1, hy_f_b1, hy_f_freq1, hy_f_w2, hy_f_b2, hy_f_freq2, hy_f_w3, hy_bias, attn_sink, w_out, ln1_g, ln1_b, ffn_w_up, ffn_conv_w, ffn_w_down, ln2_g, ln2_b):
    params = dict(rwkv_w0=rwkv_w0, rwkv_w2=rwkv_w2, rwkv_a0=rwkv_a0, rwkv_a2=rwkv_a2,
                  rwkv_k_k=rwkv_k_k, rwkv_k_a=rwkv_k_a, rwkv_r_k=rwkv_r_k,
                  rwkv_lnx_g=rwkv_lnx_g, rwkv_lnx_b=rwkv_lnx_b,
                  hy_short_w=hy_short_w, hy_f_w1=hy_f_w1, hy_f_b1=hy_f_b1, hy_f_freq1=hy_f_freq1,
                  hy_f_w2=hy_f_w2, hy_f_b2=hy_f_b2, hy_f_freq2=hy_f_freq2, hy_f_w3=hy_f_w3,
                  hy_bias=hy_bias)
    L = w_in.shape[0]
    Bc, Sc, D = x_prompt.shape
    Bl, Tl, _ = x_sample.shape
    P = cache_k.shape[2]
    n_ctx = Bc * Sc
    n_lat = Bl * Tl
    N = n_ctx + n_lat
    assert Sc % MOD_BLK == 0 and Tl % MOD_BLK == 0 and n_ctx % Tl == 0

    n_cond = 1 + Bl
    cond = jnp.zeros((16, D), F32).at[0].set(c_ctx).at[1:n_cond].set(c)
    mod_all = _modulation(cond, w_mod, b_mod)
    mod_ctx = jnp.broadcast_to(mod_all[:, 0:1, :], (L, n_ctx // MOD_BLK, 6 * D))
    mod_lat = jnp.broadcast_to(mod_all[:, 1:n_cond, None, :], (L, Bl, Tl // MOD_BLK, 6 * D))
    mod_blk = jnp.concatenate([mod_ctx, mod_lat.reshape(L, n_lat // MOD_BLK, 6 * D)], axis=1)
    mod_blk = mod_blk.reshape(L, N // MOD_BLK, 6, D)

    cm_c, sm_c = (jnp.asarray(m, F32) for m in _dft_mats(Sc))
    cm_l, sm_l = (jnp.asarray(m, F32) for m in _dft_mats(Tl))
    sink_pad = jnp.zeros((L, 1, LANES), F32).at[:, 0, :ATTN_HEADS].set(attn_sink)

    w_ext, wo, wa, wb, wd = _prep_weights(w_in, rwkv_w1, rwkv_a1, w_out, ffn_w_up, ffn_w_down)
    x = jnp.concatenate([x_prompt.reshape(n_ctx, D), x_sample.reshape(n_lat, D)], axis=0)
    states, keys, vals = [], [], []
    for l in range(L):
        p = {n: a[l] for n, a in params.items()}
        proj = _in_projection(x, mod_blk[l], w_ext, l)
        keys.append(proj[:n_ctx, C_AK:C_AK + KV_W].reshape(Bc, Sc, ATTN_KV_HEADS, HEAD_DIM))
        vals.append(proj[:n_ctx, C_AV:C_AV + KV_W].reshape(Bc, Sc, ATTN_KV_HEADS, HEAD_DIM))

        ya_c, s_fin = _rwkv(proj, p, None, T=Sc, n_seq=Bc, row_blk0=0)
        ya_l, _ = _rwkv(proj, p, state_rwkv[:, l], T=Tl, n_seq=Bl, row_blk0=n_ctx // Tl)
        states.append(s_fin)

        yb_c = _hyena(proj, p, _hyena_filter(p, Sc, cm_c, sm_c), cm_c, sm_c, L=Sc, n_seq=Bc, row_blk0=0)
        yb_l = _hyena(proj, p, _hyena_filter(p, Tl, cm_l, sm_l), cm_l, sm_l, L=Tl, n_seq=Bl,
                      row_blk0=n_ctx // Tl)

        yc_c = _ctx_attention(proj, sink_pad[l], S=Sc, n_seq=Bc)
        yc_l = _lat_attention(proj, cache_k[:, l].reshape(Bl, P, KV_W), cache_v[:, l].reshape(Bl, P, KV_W),
                              sink_pad[l], T=Tl, n_seq=Bl, row_blk0=n_ctx // Tl)

        x1, h2 = _out_projection((ya_c, yb_c, yc_c), (ya_l, yb_l, yc_l), x, mod_blk[l],
                                 wo, l, ln1_g[l], ln1_b[l])
        x = _ffn(h2, x1, mod_blk[l], wa, wb, ffn_conv_w[l], wd, l, ln2_g[l], ln2_b[l],
                 n_ctx=n_ctx, s_ctx=Sc, s_lat=Tl)

    y_prompt = x[:n_ctx].reshape(Bc, Sc, D)
    y_sample = x[n_ctx:].reshape(Bl, Tl, D)
    new_state = jnp.stack(states, 1)
    new_k = jnp.stack(keys, 1)
    new_v = jnp.stack(vals, 1)
    return (y_prompt, y_sample, new_state, new_k, new_v)
```

```python
import functools
import math

import numpy as np
import jax
import jax.numpy as jnp
from jax import lax
from jax.experimental import pallas as pl
from jax.experimental.pallas import tpu as pltpu

F32 = jnp.float32
BF16 = jnp.bfloat16

D_MODEL = 2048
DEPTH = 4
GRID_W = 64
HEAD_DIM = 64
RWKV_W = 768
RWKV_HEADS = 12
HYENA_W = 512
ATTN_W = 768
ATTN_HEADS = 12
ATTN_KV_HEADS = 4
GQA_GROUP = 3
KV_W = 256
LORA = 64
GN_EPS = 64e-5
FILTER_EMB = 33
FILTER_HID = 64
HY_MIN_DECAY = math.log(1e-2) / 1.5
HY_MAX_DECAY = math.log(1e-2) / 0.3
WINDOW = 128
BLOCK = 128
ROPE_BASE = 10000.0
D_FF = 5504
LN_EPS = 1e-5
DN_ALPHA = (2 * DEPTH) ** 0.25

C_R, C_K, C_V, C_G = 0, 768, 1536, 2304
C_HY = 3072
C_Q, C_AK, C_AV = 4608, 5376, 5632
C_LORA = 5888
IN_W_EXT = 6144

MOD_BLK = 256
CHUNK = 64
RWKV_CB = 4
LANES = 128
D_FF_PAD = 5632
VMEM_LIMIT = 56 * 1024 * 1024


def _dot(a, b):
    return jnp.dot(a.astype(BF16), b.astype(BF16), preferred_element_type=F32)


def _dot_nt(a, b):
    return lax.dot_general(a.astype(BF16), b.astype(BF16), (((1,), (1,)), ((), ())),
                           preferred_element_type=F32)


def _split(x):
    hi = x.astype(BF16)
    lo = (x - hi.astype(F32)).astype(BF16)
    return hi, lo


def _dot3_dims(a, b, dims):
    ah, al = _split(a)
    bh, bl = _split(b)
    dg = functools.partial(lax.dot_general, dimension_numbers=dims, preferred_element_type=F32)
    return dg(ah, bh) + (dg(ah, bl) + dg(al, bh))


_NN = (((1,), (0,)), ((), ()))
_NT = (((1,), (1,)), ((), ()))
_TN = (((0,), (0,)), ((), ()))


def _dot3(a, b):
    return _dot3_dims(a, b, _NN)


def _sigmoid(x):
    return 1.0 / (1.0 + jnp.exp(-x))


def _softplus(x):
    return jnp.maximum(x, 0.0) + jnp.log(1.0 + jnp.exp(-jnp.abs(x)))


def _mod_kernel(c_ref, w_ref, b_ref, o_ref):
    c = c_ref[...]
    s = c * _sigmoid(c)
    o_ref[...] = _dot(s, w_ref[...]) + b_ref[...]


def _modulation(cond, w_mod, b_mod):
    L = w_mod.shape[0]
    R = cond.shape[0]
    tn = 1024
    return pl.pallas_call(
        _mod_kernel,
        grid=(L, 6 * D_MODEL // tn),
        in_specs=[pl.BlockSpec((R, D_MODEL), lambda l, j: (0, 0)),
                  pl.BlockSpec((None, D_MODEL, tn), lambda l, j: (l, 0, j)),
                  pl.BlockSpec((None, 1, tn), lambda l, j: (l, 0, j))],
        out_specs=pl.BlockSpec((None, R, tn), lambda l, j: (l, 0, j)),
        out_shape=jax.ShapeDtypeStruct((L, R, 6 * D_MODEL), F32),
        compiler_params=pltpu.CompilerParams(dimension_semantics=("arbitrary", "arbitrary"),
                                             vmem_limit_bytes=VMEM_LIMIT),
        name="modulation",
    )(cond, w_mod, b_mod.reshape(L, 1, 6 * D_MODEL))


def _inproj_kernel(x_ref, mod_ref, w_ref, o_ref, *, tn):
    shift = mod_ref[0, 0:1, :]
    scale = mod_ref[0, 1:2, :]
    h = (x_ref[...] * (1.0 + scale) + shift).astype(BF16)
    for j in range(IN_W_EXT // tn):
        cols = slice(j * tn, (j + 1) * tn)
        o_ref[:, cols] = jnp.dot(h, w_ref[:, cols], preferred_element_type=F32)


def _in_projection(x, mod, w_ext, l, tn=512):
    N = x.shape[0]
    tm = MOD_BLK
    return pl.pallas_call(
        functools.partial(_inproj_kernel, tn=tn),
        grid=(N // tm,),
        in_specs=[pl.BlockSpec((tm, D_MODEL), lambda i: (i, 0)),
                  pl.BlockSpec((1, 6, D_MODEL), lambda i: (i, 0, 0)),
                  pl.BlockSpec((None, D_MODEL, IN_W_EXT), lambda i: (l, 0, 0), pipeline_mode=pl.Buffered(1))],
        out_specs=pl.BlockSpec((tm, IN_W_EXT), lambda i: (i, 0)),
        out_shape=jax.ShapeDtypeStruct((N, IN_W_EXT), F32),
        compiler_params=pltpu.CompilerParams(dimension_semantics=("arbitrary",),
                                             vmem_limit_bytes=VMEM_LIMIT),
        name="in_projection",
    )(x, mod, w_ext)


PREC = dict(seg=1, cum=2, a=1, neu=1, post=1, mn=1, b_o=1, b_s=1, gn=2)


def _dotp(a, b, dims, passes):
    dg = functools.partial(lax.dot_general, dimension_numbers=dims, preferred_element_type=F32)
    if passes == 1:
        return dg(a.astype(BF16), b.astype(BF16))
    ah, al = _split(a)
    if passes == 2:
        bh = b.astype(BF16)
        return dg(ah, bh) + dg(al, bh)
    bh, bl = _split(b)
    return dg(ah, bh) + (dg(ah, bl) + dg(al, bh))


def _dotp_rhs(a, b, passes):
    dg = functools.partial(lax.dot_general, dimension_numbers=_NN, preferred_element_type=F32)
    ab = a.astype(BF16)
    if passes == 1:
        return dg(ab, b.astype(BF16))
    bh, bl = _split(b)
    return dg(ab, bh) + dg(ab, bl)


def _rwkv_kernel(*refs, T, has_init, cb):
    (r_ref, k_ref, v_ref, g_ref, lo_ref, w0_ref, w2_ref, a0_ref, a2_ref,
     kk_ref, ka_ref, rk_ref, lng_ref, lnb_ref) = refs[:14]
    pos = 14
    init_ref = None
    if has_init:
        init_ref = refs[pos]
        pos += 1
    y_ref, sfin_ref = refs[pos:pos + 2]
    rp_scr, ov_scr, m_scr, n_scr, o_scr, bonus_scr, s_scr = refs[pos + 2:]

    nC = T // CHUNK
    C = CHUNK
    H = HEAD_DIM
    P = PREC
    C2 = 2 * C
    row = lax.broadcasted_iota(jnp.int32, (C, C), 0)
    col = lax.broadcasted_iota(jnp.int32, (C, C), 1)
    tri_f = jnp.where(col <= row, 1.0, 0.0).astype(F32)
    lrow = lax.broadcasted_iota(jnp.int32, (LANES, LANES), 0)
    lcol = lax.broadcasted_iota(jnp.int32, (LANES, LANES), 1)
    same_head = (lrow // H) == (lcol // H)
    head_ones = jnp.where(same_head, 1.0, 0.0).astype(F32)
    eye = lrow == lcol
    rt_, ct_ = lrow % C, lcol % C
    strict_m = [same_head & (ct_ < rt_), same_head & (ct_ > rt_)]
    incl_m = [same_head & (ct_ <= rt_), same_head & (ct_ >= rt_)]
    lane = lax.broadcasted_iota(jnp.int32, (1, LANES), 1)
    hmask = [lane < H, lane >= H]

    def stack_heads(x):
        return jnp.concatenate([jnp.where(hmask[0], x, 0.0), jnp.where(hmask[1], x, 0.0)], axis=0)

    kkw = kk_ref[...]
    kaw = ka_ref[...]
    rkw = rk_ref[...]
    zl = jnp.zeros((LORA, LANES), F32)
    lora_w = jnp.concatenate(
        [jnp.concatenate([w if j == i else zl for j in range(4)], axis=1)
         for i, w in enumerate([w2_ref[0], w2_ref[1], a2_ref[0], a2_ref[1]])], axis=0).astype(BF16)

    def phase_a(ci, carry):
        units = []
        for cc in range(cb):
            c = ci * cb + cc
            rows = pl.ds(pl.multiple_of(c * C, C), C)
            r = r_ref[rows, :]
            k = k_ref[rows, :]
            v = v_ref[rows, :]
            lo = lo_ref[rows, :]
            kkraw = k * kkw
            ss = _dotp(kkraw * kkraw, head_ones, _NN, P['seg'])
            kkn = kkraw / jnp.maximum(jnp.sqrt(ss), 1e-12)
            bonus = jnp.zeros((C, LANES), F32)
            vm = stack_heads(v)
            lo_in = jnp.concatenate([jnp.tanh(lo[:, :2 * LORA]), lo[:, 2 * LORA:]], axis=1)
            lo_out = _dot(lo_in, lora_w)
            lds, alphas = [], []
            for d in range(2):
                wl = w0_ref[d] + lo_out[:, d * LANES:(d + 1) * LANES]
                lds.append(-jnp.exp(-_softplus(-wl) - 0.5))
                alphas.append(_sigmoid(a0_ref[d] + lo_out[:, (2 + d) * LANES:(3 + d) * LANES]))
            pref = _dotp_rhs(tri_f, jnp.concatenate(lds, axis=1), P['cum'])
            for d in range(2):
                ld, alpha = lds[d], alphas[d]
                kd = k * (1.0 + (alpha - 1.0) * kaw)
                bb = kkn * alpha
                bonus = bonus + _dotp(r * kd * rkw, head_ones, _NN, P['seg']) * v
                pf = pref[:, d * LANES:(d + 1) * LANES]
                g_incl = pf if d == 0 else pf[C - 1:C, :] - pf + ld
                g_excl = g_incl - ld
                g_end = g_incl[C - 1:C, :] if d == 0 else g_incl[0:1, :]
                ginv = jnp.exp(-g_incl)
                dec_end = jnp.exp(g_end - g_incl)
                units.append(dict(
                    d=d, c=c, vm=vm,
                    at=stack_heads(-kkn * jnp.exp(g_excl)), rt=stack_heads(r * jnp.exp(g_incl)),
                    bt=stack_heads(bb * ginv), kt=stack_heads(kd * ginv),
                    bend=stack_heads(bb * dec_end), kend=stack_heads(kd * dec_end), gend=jnp.exp(g_end)))
            bonus_scr[rows, :] = bonus
            o_scr[rows, :] = jnp.zeros((C, LANES), F32)

        for u in units:
            ar = jnp.concatenate([u['at'], u['rt']], axis=0)
            bk = jnp.concatenate([u['bt'], u['kt']], axis=0)
            u['a'] = _dotp(ar, bk, _NT, P['a'])
        for u in units:
            a = u['a']
            st, ic = strict_m[u['d']], incl_m[u['d']]
            u['p'] = jnp.where(st, a[:C2, :C2], 0.0)
            u['a_ak'] = jnp.where(st, a[:C2, C2:], 0.0)
            u['a_rb'] = jnp.where(ic, a[C2:, :C2], 0.0)
            u['a_rk'] = jnp.where(ic, a[C2:, C2:], 0.0)
        rvs = [_dotp(u['a_ak'], u['vm'], _NN, P['post']) for u in units]
        sq = [_dotp(u['p'], u['p'], _NN, P['neu']) for u in units]
        for i, u in enumerate(units):
            u['x'] = jnp.where(eye, 1.0, 0.0) + u['p']
            u['p'] = sq[i]
        for it in range(1, 6):
            res = [_dotp(u['p'], jnp.concatenate([u['x'], u['p']], axis=1) if it < 5 else u['x'], _NN, P['neu'])
                   for u in units]
            for i, u in enumerate(units):
                u['x'] = u['x'] + res[i][:, :C2]
                if it < 5:
                    u['p'] = res[i][:, C2:]
        rhs = [_dotp(u['x'], jnp.concatenate([u['at'], rvs[i]], axis=1), _NN, P['post'])
               for i, u in enumerate(units)]
        for i, u in enumerate(units):
            u['rhs'] = rhs[i]
        xs = [_dotp(jnp.concatenate([u['a_rb'], u['a_rk']], axis=1),
                    jnp.concatenate([u['rhs'], jnp.concatenate([jnp.zeros_like(u['vm']), u['vm']], axis=1)], axis=0),
                    _NN, P['post']) for u in units]
        ms = [_dotp(u['rhs'][:, :LANES], u['bend'], _TN, P['mn']) for u in units]
        ns = [_dotp(jnp.concatenate([u['rhs'][:, LANES:], u['vm']], axis=0),
                    jnp.concatenate([u['bend'], u['kend']], axis=0), _TN, P['mn']) for u in units]
        for i, u in enumerate(units):
            d, c = u['d'], u['c']
            rp = u['rt'] + xs[i][:, :LANES]
            ov = xs[i][:, LANES:]
            rp_scr[d, c] = rp[:C] + rp[C:]
            ov_scr[d, c] = ov[:C] + ov[C:]
            m_scr[d, c] = jnp.where(eye, jnp.broadcast_to(u['gend'], (LANES, LANES)), 0.0) + ms[i]
            n_scr[d, c] = ns[i]
        return carry

    lax.fori_loop(0, nC // cb, phase_a, 0)

    zero_blk = jnp.zeros((H, H), F32)
    for d in range(2):
        if has_init:
            s_scr[d] = jnp.concatenate([jnp.concatenate([init_ref[d, 0], zero_blk], axis=1),
                                        jnp.concatenate([zero_blk, init_ref[d, 1]], axis=1)], axis=0)
        else:
            s_scr[d] = jnp.zeros((LANES, LANES), F32)

    def phase_b(i, carry):
        cs = [i, nC - 1 - i]
        ss = [s_scr[d] for d in range(2)]
        os_ = [_dotp(rp_scr[d, cs[d]], ss[d], _NT, P['b_o']) for d in range(2)]
        sn = [_dotp(ss[d], m_scr[d, cs[d]], _NN, P['b_s']) for d in range(2)]
        for d in range(2):
            s_scr[d] = sn[d] + n_scr[d, cs[d]]
            rows = pl.ds(pl.multiple_of(cs[d] * C, C), C)
            o_scr[rows, :] = o_scr[rows, :] + (os_[d] + ov_scr[d, cs[d]])
        return carry

    lax.fori_loop(0, nC, phase_b, 0, unroll=4)

    for d in range(2):
        s = s_scr[d]
        sfin_ref[d, 0] = s[:H, :H]
        sfin_ref[d, 1] = s[H:, H:]

    lng = lng_ref[...]
    lnb = lnb_ref[...]

    rb = min(T, 256)

    def phase_c(c, carry):
        rows = pl.ds(pl.multiple_of(c * rb, rb), rb)
        o = o_scr[rows, :]
        st = _dotp(jnp.concatenate([o, o * o], axis=0), head_ones, _NN, P['gn']) * (1.0 / H)
        mu = st[:rb]
        var = st[rb:] - mu * mu
        on = (o - mu) * lax.rsqrt(var + GN_EPS) * lng + lnb
        y_ref[rows, :] = (on + bonus_scr[rows, :]) * _sigmoid(g_ref[rows, :])
        return carry

    lax.fori_loop(0, T // rb, phase_c, 0)


def _rwkv(proj, p, init, *, T, n_seq, row_blk0):
    nC = T // CHUNK
    npair = RWKV_HEADS // 2
    has_init = init is not None
    cb = lambda off: (lambda b, j: (b + row_blk0, off // LANES + j))
    vec = lambda b, j: (0, j)
    in_specs = [pl.BlockSpec((T, LANES), cb(C_R)),
                pl.BlockSpec((T, LANES), cb(C_K)),
                pl.BlockSpec((T, LANES), cb(C_V)),
                pl.BlockSpec((T, LANES), cb(C_G)),
                pl.BlockSpec((T, 4 * LORA), lambda b, j: (b + row_blk0, C_LORA // (4 * LORA))),
                pl.BlockSpec((2, 1, LANES), lambda b, j: (0, 0, j)),
                pl.BlockSpec((2, LORA, LANES), lambda b, j: (0, 0, j)),
                pl.BlockSpec((2, 1, LANES), lambda b, j: (0, 0, j)),
                pl.BlockSpec((2, LORA, LANES), lambda b, j: (0, 0, j)),
                pl.BlockSpec((1, LANES), vec), pl.BlockSpec((1, LANES), vec),
                pl.BlockSpec((1, LANES), vec), pl.BlockSpec((1, LANES), vec),
                pl.BlockSpec((1, LANES), vec)]
    args = [proj, proj, proj, proj, proj,
            p['rwkv_w0'].reshape(2, 1, RWKV_W), p['rwkv_w2'], p['rwkv_a0'].reshape(2, 1, RWKV_W), p['rwkv_a2'],
            p['rwkv_k_k'].reshape(1, RWKV_W), p['rwkv_k_a'].reshape(1, RWKV_W),
            p['rwkv_r_k'].reshape(1, RWKV_W), p['rwkv_lnx_g'].reshape(1, RWKV_W),
            p['rwkv_lnx_b'].reshape(1, RWKV_W)]
    st_spec = pl.BlockSpec((None, 2, 2, HEAD_DIM, HEAD_DIM), lambda b, j: (b, 0, j, 0, 0))
    if has_init:
        in_specs.append(st_spec)
        args.append(init)
    sc_o = lambda: pltpu.VMEM((2, nC, CHUNK, LANES), F32)
    sc_s = lambda: pltpu.VMEM((2, nC, LANES, LANES), F32)
    return pl.pallas_call(
        functools.partial(_rwkv_kernel, T=T, has_init=has_init, cb=min(RWKV_CB, nC)),
        grid=(n_seq, npair),
        in_specs=in_specs,
        out_specs=[pl.BlockSpec((T, LANES), lambda b, j: (b, j)), st_spec],
        out_shape=[jax.ShapeDtypeStruct((n_seq * T, RWKV_W), F32),
                   jax.ShapeDtypeStruct((n_seq, 2, RWKV_HEADS, HEAD_DIM, HEAD_DIM), F32)],
        scratch_shapes=[sc_o(), sc_o(), sc_s(), sc_s(),
                        pltpu.VMEM((T, LANES), F32), pltpu.VMEM((T, LANES), F32),
                        pltpu.VMEM((2, LANES, LANES), F32)],
        compiler_params=pltpu.CompilerParams(dimension_semantics=("arbitrary", "arbitrary"),
                                             vmem_limit_bytes=VMEM_LIMIT),
        name="rwkv_T%d" % T,
    )(*args)


def _dft_mats(L):
    f = np.arange(L, dtype=np.int64)
    ph = (np.outer(f, f) % (2 * L)).astype(np.float64) * (math.pi / L)
    return np.cos(ph), -np.sin(ph)


def _filter_features(L):
    t = np.linspace(0.0, 1.0, L, dtype=np.float32)[:, None]
    bands = (FILTER_EMB - 1) // 2
    t_res = np.arange(L, dtype=np.float32)[:, None]
    f = np.linspace(1e-4, bands - 1, bands, dtype=np.float32)[None, :]
    ang = (2.0 * math.pi * t_res * f / L).astype(np.float32)
    z = np.concatenate([t, np.cos(ang), np.sin(ang)], -1).astype(np.float32)
    zp = np.zeros((L, FILTER_HID), np.float32)
    zp[:, :FILTER_EMB] = z
    deltas = np.abs(np.linspace(HY_MIN_DECAY, HY_MAX_DECAY, HYENA_W, dtype=np.float32))
    return zp, t.astype(np.float32), deltas[None, :].astype(np.float32)


def _hyfilter_kernel(z_ref, t_ref, dl_ref, w1_ref, b1_ref, q1_ref, w2_ref, b2_ref, q2_ref, wc_ref, wa_ref,
                     hs_ref, hd_ref, knyq_ref, hm_scr, *, L):
    @pl.when((pl.program_id(0) == 0) & (pl.program_id(1) == 0))
    def _():
        h1 = jnp.sin(q1_ref[...] * (_dot3(z_ref[...], w1_ref[...]) + b1_ref[...]))
        hm_scr[...] = jnp.sin(q2_ref[...] * (_dot3(h1, w2_ref[...]) + b2_ref[...]))

    hm = hm_scr[...]
    win = jnp.exp(-t_ref[...] * dl_ref[...])
    lag = lax.broadcasted_iota(jnp.int32, (L, 1), 0)
    alt = jnp.where(lag % 2 == 0, 1.0, -1.0).astype(F32)
    hc = _dot3(hm, wc_ref[...]) * win
    ha = jnp.where(lag == 0, 0.0, _dot3(hm, wa_ref[...]) * win)
    hs_ref[...] = hc + ha
    hd_ref[...] = hc - ha
    knyq_ref[...] = jnp.sum((hc + ha) * alt, axis=0, keepdims=True) * (1.0 / (2 * L))


def _hyspec_kernel(cm_ref, sm_ref, hs_ref, hd_ref, kre_ref, kim_ref, *, L, tf):
    f = pl.program_id(2) * tf + lax.broadcasted_iota(jnp.int32, (tf, 1), 0)
    wf = jnp.where(f == 0, 1.0, 2.0).astype(F32) * (1.0 / (2 * L))
    kre_ref[...] = _dot3(cm_ref[...], hs_ref[...]) * wf
    kim_ref[...] = _dot3(sm_ref[...], hd_ref[...]) * wf


def _hyena_filter(p, L, cm, sm, cw=128):
    zp, t, dl = _filter_features(L)
    w1 = jnp.zeros((FILTER_HID, FILTER_HID), F32).at[:FILTER_EMB].set(p['hy_f_w1'])
    r2 = lambda a: a.reshape(1, -1)
    ncb = HYENA_W // cw
    full = lambda shape: pl.BlockSpec(shape, lambda n, j: (0,) * len(shape))
    shp = jax.ShapeDtypeStruct((2, L, HYENA_W), F32)
    hs, hd, knyq = pl.pallas_call(
        functools.partial(_hyfilter_kernel, L=L),
        grid=(2, ncb),
        in_specs=[full((L, FILTER_HID)), full((L, 1)),
                  pl.BlockSpec((1, cw), lambda n, j: (0, j)),
                  full((FILTER_HID, FILTER_HID)), full((1, FILTER_HID)), full((1, FILTER_HID)),
                  full((FILTER_HID, FILTER_HID)), full((1, FILTER_HID)), full((1, FILTER_HID)),
                  pl.BlockSpec((FILTER_HID, cw), lambda n, j: (0, n * 2 * ncb + j)),
                  pl.BlockSpec((FILTER_HID, cw), lambda n, j: (0, n * 2 * ncb + ncb + j))],
        out_specs=[pl.BlockSpec((None, L, cw), lambda n, j: (n, 0, j)),
                   pl.BlockSpec((None, L, cw), lambda n, j: (n, 0, j)),
                   pl.BlockSpec((None, 1, cw), lambda n, j: (n, 0, j))],
        out_shape=[shp, shp, jax.ShapeDtypeStruct((2, 1, HYENA_W), F32)],
        scratch_shapes=[pltpu.VMEM((L, FILTER_HID), F32)],
        compiler_params=pltpu.CompilerParams(dimension_semantics=("arbitrary", "arbitrary"),
                                             vmem_limit_bytes=VMEM_LIMIT),
        name="hyena_filter_L%d" % L,
    )(jnp.asarray(zp), jnp.asarray(t), jnp.asarray(dl), w1, r2(p['hy_f_b1']), r2(p['hy_f_freq1']),
      p['hy_f_w2'], r2(p['hy_f_b2']), r2(p['hy_f_freq2']), p['hy_f_w3'], p['hy_f_w3'])
    tf = min(256, L)
    sw = 256
    kre, kim = pl.pallas_call(
        functools.partial(_hyspec_kernel, L=L, tf=tf),
        grid=(2, HYENA_W // sw, L // tf),
        in_specs=[pl.BlockSpec((tf, L), lambda n, j, i: (i, 0)),
                  pl.BlockSpec((tf, L), lambda n, j, i: (i, 0)),
                  pl.BlockSpec((None, L, sw), lambda n, j, i: (n, 0, j)),
                  pl.BlockSpec((None, L, sw), lambda n, j, i: (n, 0, j))],
        out_specs=[pl.BlockSpec((None, tf, sw), lambda n, j, i: (n, i, j)),
                   pl.BlockSpec((None, tf, sw), lambda n, j, i: (n, i, j))],
        out_shape=[shp, shp],
        compiler_params=pltpu.CompilerParams(dimension_semantics=("arbitrary",) * 3,
                                             vmem_limit_bytes=VMEM_LIMIT),
        name="hyena_spectrum_L%d" % L,
    )(cm, sm, hs, hd)
    return kre, kim, knyq


def _hyena_kernel(v_ref, x1_ref, x2_ref, sw_ref, bias_ref, cm_ref, sm_ref, kre_ref, kim_ref, knyq_ref,
                  o_ref, zb_scr, zf_scr, xs_scr, yre_scr, yim_scr, *, L, rb):
    nrb = L // rb
    t = lax.broadcasted_iota(jnp.int32, (L, 1), 0)
    first = t == 0
    last = t == L - 1
    alt_all = jnp.where(t % 2 == 0, 1.0, -1.0).astype(F32)
    tb = lax.broadcasted_iota(jnp.int32, (rb, 1), 0)
    alt_blk = jnp.where(tb % 2 == 0, 1.0, -1.0).astype(F32)
    sw = sw_ref[...]

    def short(ref, s):
        u = ref[...]
        prev = jnp.where(first, 0.0, pltpu.roll(u, 1, 0))
        nxt = jnp.where(last, 0.0, pltpu.roll(u, L - 1, 0))
        return prev * sw[0, s][None, :] + u * sw[1, s][None, :] + nxt * sw[2, s][None, :]

    z0 = short(v_ref, 0)
    zf_scr[...] = z0
    zb_scr[...] = z0.astype(BF16)
    unyq = jnp.sum(z0 * alt_all, axis=0, keepdims=True)

    for n in range(2):
        xs_scr[...] = short(x1_ref if n == 0 else x2_ref, n + 1)

        def fwd(fb, carry):
            rows = pl.ds(pl.multiple_of(fb * rb, rb), rb)
            zb = zb_scr[...]
            ure = jnp.dot(cm_ref[rows, :], zb, preferred_element_type=F32)
            uim = jnp.dot(sm_ref[rows, :], zb, preferred_element_type=F32)
            kre = kre_ref[n, rows, :]
            kim = kim_ref[n, rows, :]
            yre_scr[rows, :] = (ure * kre - uim * kim).astype(BF16)
            yim_scr[rows, :] = (ure * kim + uim * kre).astype(BF16)
            return carry

        lax.fori_loop(0, nrb, fwd, 0)
        ynyq = unyq * knyq_ref[n]
        bias = bias_ref[n]

        def inv(ib, acc):
            rows = pl.ds(pl.multiple_of(ib * rb, rb), rb)
            y = (jnp.dot(cm_ref[rows, :], yre_scr[...], preferred_element_type=F32)
                 + jnp.dot(sm_ref[rows, :], yim_scr[...], preferred_element_type=F32))
            y = y + alt_blk * ynyq + zf_scr[rows, :] * bias
            znew = xs_scr[rows, :] * y
            if n == 0:
                zf_scr[rows, :] = znew
                zb_scr[rows, :] = znew.astype(BF16)
                acc = acc + jnp.sum(znew * alt_blk, axis=0, keepdims=True)
            else:
                o_ref[rows, :] = znew
            return acc

        unyq = lax.fori_loop(0, nrb, inv, jnp.zeros_like(unyq))


def _hyena(proj, p, filt, cm, sm, *, L, n_seq, row_blk0, cw=256):
    kre, kim, knyq = filt
    ncb = HYENA_W // cw
    cb = lambda s: (lambda j, b: (b + row_blk0, (C_HY + s * HYENA_W) // cw + j))
    once = pl.Buffered(1)
    sw = p['hy_short_w'].reshape(3, 3, HYENA_W)
    rb = min(256, L)
    return pl.pallas_call(
        functools.partial(_hyena_kernel, L=L, rb=rb),
        grid=(ncb, n_seq),
        in_specs=[pl.BlockSpec((L, cw), cb(0)), pl.BlockSpec((L, cw), cb(1)), pl.BlockSpec((L, cw), cb(2)),
                  pl.BlockSpec((3, 3, cw), lambda j, b: (0, 0, j)),
                  pl.BlockSpec((2, 1, cw), lambda j, b: (0, 0, j)),
                  pl.BlockSpec((L, L), lambda j, b: (0, 0), pipeline_mode=once),
                  pl.BlockSpec((L, L), lambda j, b: (0, 0), pipeline_mode=once),
                  pl.BlockSpec((2, L, cw), lambda j, b: (0, 0, j), pipeline_mode=once),
                  pl.BlockSpec((2, L, cw), lambda j, b: (0, 0, j), pipeline_mode=once),
                  pl.BlockSpec((2, 1, cw), lambda j, b: (0, 0, j))],
        out_specs=pl.BlockSpec((L, cw), lambda j, b: (b, j)),
        out_shape=jax.ShapeDtypeStruct((n_seq * L, HYENA_W), F32),
        scratch_shapes=[pltpu.VMEM((L, cw), BF16), pltpu.VMEM((L, cw), F32), pltpu.VMEM((L, cw), F32),
                        pltpu.VMEM((L, cw), BF16), pltpu.VMEM((L, cw), BF16)],
        compiler_params=pltpu.CompilerParams(dimension_semantics=("arbitrary", "arbitrary"),
                                             vmem_limit_bytes=VMEM_LIMIT),
        name="hyena_L%d" % L,
    )(proj, proj, proj, sw, p['hy_bias'].reshape(2, 1, HYENA_W), cm.astype(BF16), sm.astype(BF16),
      kre, kim, knyq)


def _row_max(x):
    t = x[:, :LANES]
    for c in range(1, x.shape[1] // LANES):
        t = jnp.maximum(t, x[:, c * LANES:(c + 1) * LANES])
    return jnp.max(t, axis=-1, keepdims=True)


def _row_sum(x):
    t = x[:, :LANES]
    for c in range(1, x.shape[1] // LANES):
        t = t + x[:, c * LANES:(c + 1) * LANES]
    return jnp.sum(t, axis=-1, keepdims=True)


def _head_masks():
    lane = lax.broadcasted_iota(jnp.int32, (1, KV_W), 1) // HEAD_DIM
    return [lane == h for h in range(ATTN_KV_HEADS)]


def _attend_group(qg, keys, vals, masks, sinks, hm):
    nh = len(hm)
    qm = [jnp.where(hm[h], qg, 0.0).astype(BF16) for h in range(nh)]
    s = [[_dot_nt(qm[h], k) for k in keys] for h in range(nh)]
    s = [[sc if mk is None else jnp.where(mk, sc, -jnp.inf) for sc, mk in zip(s[h], masks)] for h in range(nh)]
    m = [jnp.maximum(_row_max(jnp.concatenate(s[h], axis=1)), sinks[h]) for h in range(nh)]
    p = [[jnp.exp(sc - m[h]) for sc in s[h]] for h in range(nh)]
    den = [_row_sum(jnp.concatenate(p[h], axis=1)) + jnp.exp(sinks[h] - m[h]) for h in range(nh)]
    pv = [[_dot(pc, v) for pc, v in zip(p[h], vals)] for h in range(nh)]
    acc = jnp.zeros(qg.shape, F32)
    for h in range(nh):
        tot = pv[h][0]
        for t in pv[h][1:]:
            tot = tot + t
        acc = jnp.where(hm[h], tot / den[h], acc)
    return acc


def _ctx_attn_kernel(q_ref, k_ref, v_ref, sink_ref, o_ref, *, S):
    scale = HEAD_DIM ** -0.5
    hm = _head_masks()
    k = k_ref[...].astype(BF16)
    v = v_ref[...].astype(BF16)
    for g in range(GQA_GROUP):
        qg = q_ref[:, g * KV_W:(g + 1) * KV_W] * scale
        sinks = [jnp.broadcast_to(sink_ref[0:1, h * GQA_GROUP + g:h * GQA_GROUP + g + 1], (S, 1))
                 for h in range(ATTN_KV_HEADS)]
        o_ref[:, g * KV_W:(g + 1) * KV_W] = _attend_group(qg, [k], [v], [None], sinks, hm)


def _ctx_attention(proj, sink, *, S, n_seq):
    return pl.pallas_call(
        functools.partial(_ctx_attn_kernel, S=S),
        grid=(n_seq,),
        in_specs=[pl.BlockSpec((S, ATTN_W), lambda b: (b, C_Q // ATTN_W)),
                  pl.BlockSpec((S, KV_W), lambda b: (b, C_AK // KV_W)),
                  pl.BlockSpec((S, KV_W), lambda b: (b, C_AV // KV_W)),
                  pl.BlockSpec((1, LANES), lambda b: (0, 0))],
        out_specs=pl.BlockSpec((S, ATTN_W), lambda b: (b, 0)),
        out_shape=jax.ShapeDtypeStruct((n_seq * S, ATTN_W), F32),
        compiler_params=pltpu.CompilerParams(dimension_semantics=("arbitrary",),
                                             vmem_limit_bytes=VMEM_LIMIT),
        name="ctx_attention",
    )(proj, proj, proj, sink)


def _rope(x, cos, sin_signed, first_half):
    n = x.shape[-1]
    partner = jnp.where(first_half, pltpu.roll(x, n - HEAD_DIM // 2, 1), pltpu.roll(x, HEAD_DIM // 2, 1))
    return x * cos + partner * sin_signed


def _lat_attn_kernel(q_ref, kp_ref, kc_ref, kn_ref, vp_ref, vc_ref, vn_ref, ck_ref, cv_ref,
                     cq_ref, sq_ref, cp_ref, sp_ref, cc_ref, sc_ref, cn_ref, sn_ref, sink_ref, o_ref, *, T):
    i = pl.program_id(1)
    scale = HEAD_DIM ** -0.5
    hm = _head_masks()
    fh_q = (lax.broadcasted_iota(jnp.int32, (1, ATTN_W), 1) % HEAD_DIM) < HEAD_DIM // 2
    fh_k = (lax.broadcasted_iota(jnp.int32, (1, KV_W), 1) % HEAD_DIM) < HEAD_DIM // 2
    q = _rope(q_ref[...], cq_ref[...], sq_ref[...], fh_q) * scale
    kl = jnp.concatenate([_rope(kp_ref[...], cp_ref[...], sp_ref[...], fh_k),
                          _rope(kc_ref[...], cc_ref[...], sc_ref[...], fh_k),
                          _rope(kn_ref[...], cn_ref[...], sn_ref[...], fh_k)], axis=0).astype(BF16)
    vl = jnp.concatenate([vp_ref[...], vc_ref[...], vn_ref[...]], axis=0).astype(BF16)
    kc = ck_ref[...].astype(BF16)
    vc = cv_ref[...].astype(BF16)
    qpos = i * BLOCK + lax.broadcasted_iota(jnp.int32, (BLOCK, 3 * BLOCK), 0)
    kpos = (i - 1) * BLOCK + lax.broadcasted_iota(jnp.int32, (BLOCK, 3 * BLOCK), 1)
    ok = (jnp.abs(qpos - kpos) <= WINDOW) & (kpos >= 0) & (kpos < T)
    for g in range(GQA_GROUP):
        qg = q[:, g * KV_W:(g + 1) * KV_W]
        sinks = [jnp.broadcast_to(sink_ref[0:1, h * GQA_GROUP + g:h * GQA_GROUP + g + 1], (BLOCK, 1))
                 for h in range(ATTN_KV_HEADS)]
        o_ref[:, g * KV_W:(g + 1) * KV_W] = _attend_group(qg, [kl, kc], [vl, vc], [ok, None], sinks, hm)


def _rope_tables(T):
    rows = T // GRID_W
    row = np.repeat(np.arange(rows), GRID_W).astype(np.float32)
    col = np.tile(np.arange(GRID_W), rows).astype(np.float32)
    nf = HEAD_DIM // 4
    inv = (ROPE_BASE ** (-np.arange(nf, dtype=np.float32) / nf)).astype(np.float32)
    ang = np.concatenate([row[:, None] * inv, col[:, None] * inv], -1).astype(np.float32)
    cos = np.cos(ang).astype(np.float32)
    sin = np.sin(ang).astype(np.float32)
    cos_h = np.concatenate([cos, cos], -1)
    sin_h = np.concatenate([-sin, sin], -1)
    return cos_h, sin_h


def _lat_attention(proj, cache_k, cache_v, sink, *, T, n_seq, row_blk0):
    nb = T // BLOCK
    P = cache_k.shape[1]
    cos_h, sin_h = _rope_tables(T)
    cq = jnp.asarray(np.tile(cos_h, (1, ATTN_HEADS)))
    sq = jnp.asarray(np.tile(sin_h, (1, ATTN_HEADS)))
    ck = jnp.asarray(np.tile(cos_h, (1, ATTN_KV_HEADS)))
    sk = jnp.asarray(np.tile(sin_h, (1, ATTN_KV_HEADS)))
    r0 = row_blk0 * nb
    prv = lambda i: jnp.maximum(i - 1, 0)
    nxt = lambda i: jnp.minimum(i + 1, nb - 1)
    kcol, vcol = C_AK // KV_W, C_AV // KV_W
    kv = lambda col, f: pl.BlockSpec((BLOCK, KV_W), lambda b, i: (r0 + b * nb + f(i), col))
    tab = lambda w, f: pl.BlockSpec((BLOCK, w), lambda b, i: (f(i), 0))
    cur = lambda i: i
    return pl.pallas_call(
        functools.partial(_lat_attn_kernel, T=T),
        grid=(n_seq, nb),
        in_specs=[pl.BlockSpec((BLOCK, ATTN_W), lambda b, i: (r0 + b * nb + i, C_Q // ATTN_W)),
                  kv(kcol, prv), kv(kcol, cur), kv(kcol, nxt),
                  kv(vcol, prv), kv(vcol, cur), kv(vcol, nxt),
                  pl.BlockSpec((None, P, KV_W), lambda b, i: (b, 0, 0)),
                  pl.BlockSpec((None, P, KV_W), lambda b, i: (b, 0, 0)),
                  tab(ATTN_W, cur), tab(ATTN_W, cur),
                  tab(KV_W, prv), tab(KV_W, prv), tab(KV_W, cur), tab(KV_W, cur), tab(KV_W, nxt), tab(KV_W, nxt),
                  pl.BlockSpec((1, LANES), lambda b, i: (0, 0))],
        out_specs=pl.BlockSpec((BLOCK, ATTN_W), lambda b, i: (b * nb + i, 0)),
        out_shape=jax.ShapeDtypeStruct((n_seq * T, ATTN_W), F32),
        compiler_params=pltpu.CompilerParams(dimension_semantics=("arbitrary", "arbitrary"),
                                             vmem_limit_bytes=VMEM_LIMIT),
        name="latent_attention",
    )(proj, proj, proj, proj, proj, proj, proj, cache_k, cache_v, cq, sq, ck, sk, ck, sk, ck, sk, sink)


def _layer_norm(y, g, b):
    mu = jnp.mean(y, axis=-1, keepdims=True)
    d = y - mu
    var = jnp.mean(d * d, axis=-1, keepdims=True)
    return d * lax.rsqrt(var + LN_EPS) * g + b


def _outproj_kernel(ac_ref, bc_ref, cc_ref, al_ref, bl_ref, cl_ref, x_ref, mod_ref, w_ref,
                    g_ref, be_ref, x1_ref, h2_ref, *, tm, n_ctx_tiles):
    r1, r2 = RWKV_W, RWKV_W + HYENA_W
    sub = 128

    def tile(a_ref, b_ref, c_ref):
        ms = []
        for s in range(tm // sub):
            rows = slice(s * sub, (s + 1) * sub)
            ms.append(jnp.dot(a_ref[rows, :].astype(BF16), w_ref[0:r1, :], preferred_element_type=F32)
                      + jnp.dot(b_ref[rows, :].astype(BF16), w_ref[r1:r2, :], preferred_element_type=F32)
                      + jnp.dot(c_ref[rows, :].astype(BF16), w_ref[r2:, :], preferred_element_type=F32))
        for s in range(tm // sub):
            rows = slice(s * sub, (s + 1) * sub)
            blk = (s * sub) // MOD_BLK
            gate1 = mod_ref[blk, 2:3, :]
            shift2 = mod_ref[blk, 3:4, :]
            scale2 = mod_ref[blk, 4:5, :]
            x1 = _layer_norm(DN_ALPHA * x_ref[rows, :] + gate1 * ms[s], g_ref[...], be_ref[...])
            x1_ref[rows, :] = x1
            h2_ref[rows, :] = (x1 * (1.0 + scale2) + shift2).astype(BF16)

    pl.when(pl.program_id(0) < n_ctx_tiles)(lambda: tile(ac_ref, bc_ref, cc_ref))
    pl.when(pl.program_id(0) >= n_ctx_tiles)(lambda: tile(al_ref, bl_ref, cl_ref))


def _out_projection(mix_ctx, mix_lat, x, mod, w_out, l, g, b, tm=256):
    N = x.shape[0]
    nsub = tm // MOD_BLK
    nct = mix_ctx[0].shape[0] // tm
    nlt = mix_lat[0].shape[0] // tm
    row = lambda w: pl.BlockSpec((tm, w), lambda i: (i, 0))
    ctx = lambda w: pl.BlockSpec((tm, w), lambda i: (jnp.minimum(i, nct - 1), 0))
    lat = lambda w: pl.BlockSpec((tm, w), lambda i: (jnp.clip(i - nct, 0, nlt - 1), 0))
    full = lambda r: pl.BlockSpec((r, D_MODEL), lambda i: (0, 0))
    return pl.pallas_call(
        functools.partial(_outproj_kernel, tm=tm, n_ctx_tiles=nct),
        grid=(N // tm,),
        in_specs=[ctx(RWKV_W), ctx(HYENA_W), ctx(ATTN_W), lat(RWKV_W), lat(HYENA_W), lat(ATTN_W), row(D_MODEL),
                  pl.BlockSpec((nsub, 6, D_MODEL), lambda i: (i, 0, 0)),
                  pl.BlockSpec((None, D_MODEL, D_MODEL), lambda i: (l, 0, 0)), full(1), full(1)],
        out_specs=[row(D_MODEL), row(D_MODEL)],
        out_shape=[jax.ShapeDtypeStruct((N, D_MODEL), F32), jax.ShapeDtypeStruct((N, D_MODEL), BF16)],
        compiler_params=pltpu.CompilerParams(dimension_semantics=("arbitrary",),
                                             vmem_limit_bytes=VMEM_LIMIT),
        name="out_projection",
    )(*mix_ctx, *mix_lat, x, mod, w_out, g.reshape(1, D_MODEL), b.reshape(1, D_MODEL))


HALO = 16


def _ffn_kernel(hp_ref, h_ref, hn_ref, x1_ref, mod_ref, wa_ref, wb_ref, cwa_ref, cwb_ref, wd_ref, g_ref, be_ref,
                o_ref, hx_scr, ua_scr, ub_scr, acc_scr, *, tm, tf, nsub, n_ctx, s_ctx, s_lat):
    i = pl.program_id(0)
    j = pl.program_id(1)

    @pl.when(j == 0)
    def _():
        hx_scr[0:HALO, :] = hp_ref[...]
        hx_scr[HALO:HALO + tm, :] = h_ref[...]
        hx_scr[HALO + tm:, :] = hn_ref[...]
        acc_scr[...] = jnp.zeros_like(acc_scr)

    hx = hx_scr[...]
    ua_scr[...] = jnp.dot(hx, wa_ref[...], preferred_element_type=F32)
    ub_scr[...] = jnp.dot(hx, wb_ref[...], preferred_element_type=F32)
    grow = i * tm + lax.broadcasted_iota(jnp.int32, (tm, 1), 0)
    pos = jnp.where(grow < n_ctx, grow % s_ctx, (grow - n_ctx) % s_lat)
    slen = jnp.where(grow < n_ctx, s_ctx, s_lat)
    mprev = jnp.where(pos == 0, 0.0, 1.0).astype(F32)
    mnext = jnp.where(pos == slen - 1, 0.0, 1.0).astype(F32)

    def conv(u_scr, cw_ref):
        up = u_scr[HALO - 1:HALO - 1 + tm, :] * mprev
        uc = u_scr[HALO:HALO + tm, :]
        un = u_scr[HALO + 1:HALO + 1 + tm, :] * mnext
        return up * cw_ref[0:1, :] + uc * cw_ref[1:2, :] + un * cw_ref[2:3, :]

    a = conv(ua_scr, cwa_ref)
    b = conv(ub_scr, cwb_ref)
    act = (a * _sigmoid(a) * b).astype(BF16)
    acc_scr[...] += jnp.dot(act, wd_ref[...], preferred_element_type=F32)

    @pl.when(j == pl.num_programs(1) - 1)
    def _():
        for s in range(nsub):
            rows = slice(s * MOD_BLK, (s + 1) * MOD_BLK)
            gate2 = mod_ref[s, 5:6, :]
            o_ref[rows, :] = _layer_norm(DN_ALPHA * x1_ref[rows, :] + gate2 * acc_scr[rows, :],
                                         g_ref[...], be_ref[...])


def _ffn(h2, x1, mod, wa, wb, cw, wd, l, g, b, *, n_ctx, s_ctx, s_lat, tm=512, tf=512):
    N = x1.shape[0]
    nsub = tm // MOD_BLK
    nh = tm // HALO
    last_h = N // HALO - 1
    pad = ((0, 0), (0, D_FF_PAD - D_FF))
    cwa, cwb = jnp.pad(cw[:, :D_FF], pad), jnp.pad(cw[:, D_FF:], pad)
    return pl.pallas_call(
        functools.partial(_ffn_kernel, tm=tm, tf=tf, nsub=nsub, n_ctx=n_ctx, s_ctx=s_ctx, s_lat=s_lat),
        grid=(N // tm, D_FF_PAD // tf),
        in_specs=[pl.BlockSpec((HALO, D_MODEL), lambda i, j: (jnp.maximum(i * nh - 1, 0), 0)),
                  pl.BlockSpec((tm, D_MODEL), lambda i, j: (i, 0)),
                  pl.BlockSpec((HALO, D_MODEL), lambda i, j: (jnp.minimum((i + 1) * nh, last_h), 0)),
                  pl.BlockSpec((tm, D_MODEL), lambda i, j: (i, 0)),
                  pl.BlockSpec((nsub, 6, D_MODEL), lambda i, j: (i, 0, 0)),
                  pl.BlockSpec((None, D_MODEL, tf), lambda i, j: (l, 0, j)),
                  pl.BlockSpec((None, D_MODEL, tf), lambda i, j: (l, 0, j)),
                  pl.BlockSpec((3, tf), lambda i, j: (0, j)),
                  pl.BlockSpec((3, tf), lambda i, j: (0, j)),
                  pl.BlockSpec((None, tf, D_MODEL), lambda i, j: (l, j, 0)),
                  pl.BlockSpec((1, D_MODEL), lambda i, j: (0, 0)),
                  pl.BlockSpec((1, D_MODEL), lambda i, j: (0, 0))],
        out_specs=pl.BlockSpec((tm, D_MODEL), lambda i, j: (i, 0)),
        out_shape=jax.ShapeDtypeStruct((N, D_MODEL), F32),
        scratch_shapes=[pltpu.VMEM((tm + 2 * HALO, D_MODEL), BF16),
                        pltpu.VMEM((tm + 2 * HALO, tf), F32),
                        pltpu.VMEM((tm + 2 * HALO, tf), F32),
                        pltpu.VMEM((tm, D_MODEL), F32)],
        compiler_params=pltpu.CompilerParams(dimension_semantics=("arbitrary", "arbitrary"),
                                             vmem_limit_bytes=VMEM_LIMIT),
        name="conv_ffn",
    )(h2, h2, h2, x1, mod, wa, wb, cwa, cwb, wd, g.reshape(1, D_MODEL), b.reshape(1, D_MODEL))


def _prep_up_kernel(a_ref, b_ref, wa_ref, wb_ref):
    zeros = jnp.zeros((a_ref.shape[0], D_FF_PAD - D_FF), BF16)
    wa_ref[:, :D_FF] = a_ref[...].astype(BF16)
    wa_ref[:, D_FF:] = zeros
    wb_ref[:, :D_FF] = b_ref[...].astype(BF16)
    wb_ref[:, D_FF:] = zeros


def _prep_down_kernel(w_ref, o_ref, *, n_real):
    keep = pl.program_id(1) < n_real
    o_ref[...] = jnp.where(keep, w_ref[...], 0.0).astype(BF16)


def _prep_in_kernel(w_ref, w1_ref, a1_ref, o_ref):
    n = w_ref.shape[-1]
    o_ref[:, :C_Q] = w_ref[:, :C_Q].astype(BF16)
    for g in range(GQA_GROUP):
        for h in range(ATTN_KV_HEADS):
            src = C_Q + (h * GQA_GROUP + g) * HEAD_DIM
            dst = C_Q + (g * ATTN_KV_HEADS + h) * HEAD_DIM
            o_ref[:, dst:dst + HEAD_DIM] = w_ref[:, src:src + HEAD_DIM].astype(BF16)
    o_ref[:, C_AK:n] = w_ref[:, C_AK:n].astype(BF16)
    o_ref[:, n:] = jnp.concatenate([w1_ref[0], w1_ref[1], a1_ref[0], a1_ref[1]], axis=1).astype(BF16)


def _cast_kernel(w_ref, o_ref):
    o_ref[...] = w_ref[...].astype(BF16)


def _prep_weights(w_in, rwkv_w1, rwkv_a1, w_out, ffn_w_up, ffn_w_down):
    L, D = w_in.shape[0], D_MODEL
    cp = lambda n: pltpu.CompilerParams(dimension_semantics=("arbitrary",) * n, vmem_limit_bytes=VMEM_LIMIT)
    tr = 128
    half = jax.ShapeDtypeStruct((L, D, D_FF_PAD), BF16)
    wa, wb = pl.pallas_call(
        _prep_up_kernel,
        grid=(L, D // tr),
        in_specs=[pl.BlockSpec((None, tr, D_FF), lambda l, i: (l, i, 0)),
                  pl.BlockSpec((None, tr, D_FF), lambda l, i: (l, i, 1))],
        out_specs=[pl.BlockSpec((None, tr, D_FF_PAD), lambda l, i: (l, i, 0))] * 2,
        out_shape=[half, half], compiler_params=cp(2), name="prep_ffn_up",
    )(ffn_w_up, ffn_w_up)
    nr = D_FF // LANES
    wd = pl.pallas_call(
        functools.partial(_prep_down_kernel, n_real=nr),
        grid=(L, D_FF_PAD // LANES),
        in_specs=[pl.BlockSpec((None, LANES, D), lambda l, j: (l, jnp.minimum(j, nr - 1), 0))],
        out_specs=pl.BlockSpec((None, LANES, D), lambda l, j: (l, j, 0)),
        out_shape=jax.ShapeDtypeStruct((L, D_FF_PAD, D), BF16), compiler_params=cp(2), name="prep_ffn_down",
    )(ffn_w_down)
    tr2 = 256
    n_in = w_in.shape[-1]
    w_ext = pl.pallas_call(
        _prep_in_kernel,
        grid=(L, D // tr2),
        in_specs=[pl.BlockSpec((None, tr2, n_in), lambda l, i: (l, i, 0)),
                  pl.BlockSpec((None, 2, tr2, LORA), lambda l, i: (l, 0, i, 0)),
                  pl.BlockSpec((None, 2, tr2, LORA), lambda l, i: (l, 0, i, 0))],
        out_specs=pl.BlockSpec((None, tr2, IN_W_EXT), lambda l, i: (l, i, 0)),
        out_shape=jax.ShapeDtypeStruct((L, D, IN_W_EXT), BF16), compiler_params=cp(2), name="prep_in",
    )(w_in, rwkv_w1, rwkv_a1)
    hb0 = (RWKV_W + HYENA_W) // HEAD_DIM

    def src_rows(i):
        a = i - hb0
        perm = hb0 + (a % ATTN_KV_HEADS) * GQA_GROUP + a // ATTN_KV_HEADS
        return jnp.where((a >= 0) & (a < ATTN_HEADS), perm, i)

    wo = pl.pallas_call(
        _cast_kernel,
        grid=(L, D // HEAD_DIM),
        in_specs=[pl.BlockSpec((None, HEAD_DIM, D), lambda l, i: (l, src_rows(i), 0))],
        out_specs=pl.BlockSpec((None, HEAD_DIM, D), lambda l, i: (l, i, 0)),
        out_shape=jax.ShapeDtypeStruct((L, D, D), BF16), compiler_params=cp(2), name="prep_out",
    )(w_out)
    return w_ext, wo, wa, wb, wd


def kernel(x_prompt, x_sample, c, state_rwkv, cache_k, cache_v, c_ctx, w_mod, b_mod, w_in, rwkv_w0, rwkv_w1, rwkv_w2, rwkv_a0, rwkv_a1, rwkv_a2, rwkv_k_k, rwkv_k_a, rwkv_r_k, rwkv_lnx_g, rwkv_lnx_b, hy_short_w, hy_f_w1, hy_f_b1, hy_f_freq1, hy_f_w2, hy_f_b2, hy_f_freq2, hy_f_w3, hy_bias, attn_sink, w_out, ln1_g, ln1_b, ffn_w_up, ffn_conv_w, ffn_w_down, ln2_g, ln2_b):
    params = dict(rwkv_w0=rwkv_w0, rwkv_w2=rwkv_w2, rwkv_a0=rwkv_a0, rwkv_a2=rwkv_a2,
                  rwkv_k_k=rwkv_k_k, rwkv_k_a=rwkv_k_a, rwkv_r_k=rwkv_r_k,
                  rwkv_lnx_g=rwkv_lnx_g, rwkv_lnx_b=rwkv_lnx_b,
                  hy_short_w=hy_short_w, hy_f_w1=hy_f_w1, hy_f_b1=hy_f_b1, hy_f_freq1=hy_f_freq1,
                  hy_f_w2=hy_f_w2, hy_f_b2=hy_f_b2, hy_f_freq2=hy_f_freq2, hy_f_w3=hy_f_w3,
                  hy_bias=hy_bias)
    L = w_in.shape[0]
    Bc, Sc, D = x_prompt.shape
    Bl, Tl, _ = x_sample.shape
    P = cache_k.shape[2]
    n_ctx = Bc * Sc
    n_lat = Bl * Tl
    N = n_ctx + n_lat
    assert Sc % MOD_BLK == 0 and Tl % MOD_BLK == 0 and n_ctx % Tl == 0

    n_cond = 1 + Bl
    cond = jnp.zeros((16, D), F32).at[0].set(c_ctx).at[1:n_cond].set(c)
    mod_all = _modulation(cond, w_mod, b_mod)
    mod_ctx = jnp.broadcast_to(mod_all[:, 0:1, :], (L, n_ctx // MOD_BLK, 6 * D))
    mod_lat = jnp.broadcast_to(mod_all[:, 1:n_cond, None, :], (L, Bl, Tl // MOD_BLK, 6 * D))
    mod_blk = jnp.concatenate([mod_ctx, mod_lat.reshape(L, n_lat // MOD_BLK, 6 * D)], axis=1)
    mod_blk = mod_blk.reshape(L, N // MOD_BLK, 6, D)

    cm_c, sm_c = (jnp.asarray(m, F32) for m in _dft_mats(Sc))
    cm_l, sm_l = (jnp.asarray(m, F32) for m in _dft_mats(Tl))
    sink_pad = jnp.zeros((L, 1, LANES), F32).at[:, 0, :ATTN_HEADS].set(attn_sink)

    w_ext, wo, wa, wb, wd = _prep_weights(w_in, rwkv_w1, rwkv_a1, w_out, ffn_w_up, ffn_w_down)
    x = jnp.concatenate([x_prompt.reshape(n_ctx, D), x_sample.reshape(n_lat, D)], axis=0)
    states, keys, vals = [], [], []
    for l in range(L):
        p = {n: a[l] for n, a in params.items()}
        proj = _in_projection(x, mod_blk[l], w_ext, l)
        keys.append(proj[:n_ctx, C_AK:C_AK + KV_W].reshape(Bc, Sc, ATTN_KV_HEADS, HEAD_DIM))
        vals.append(proj[:n_ctx, C_AV:C_AV + KV_W].reshape(Bc, Sc, ATTN_KV_HEADS, HEAD_DIM))

        ya_c, s_fin = _rwkv(proj, p, None, T=Sc, n_seq=Bc, row_blk0=0)
        ya_l, _ = _rwkv(proj, p, state_rwkv[:, l], T=Tl, n_seq=Bl, row_blk0=n_ctx // Tl)
        states.append(s_fin)

        yb_c = _hyena(proj, p, _hyena_filter(p, Sc, cm_c, sm_c), cm_c, sm_c, L=Sc, n_seq=Bc, row_blk0=0)
        yb_l = _hyena(proj, p, _hyena_filter(p, Tl, cm_l, sm_l), cm_l, sm_l, L=Tl, n_seq=Bl,
                      row_blk0=n_ctx // Tl)

        yc_c = _ctx_attention(proj, sink_pad[l], S=Sc, n_seq=Bc)
        yc_l = _lat_attention(proj, cache_k[:, l].reshape(Bl, P, KV_W), cache_v[:, l].reshape(Bl, P, KV_W),
                              sink_pad[l], T=Tl, n_seq=Bl, row_blk0=n_ctx // Tl)

        x1, h2 = _out_projection((ya_c, yb_c, yc_c), (ya_l, yb_l, yc_l), x, mod_blk[l],
                                 wo, l, ln1_g[l], ln1_b[l])
        x = _ffn(h2, x1, mod_blk[l], wa, wb, ffn_conv_w[l], wd, l, ln2_g[l], ln2_b[l],
                 n_ctx=n_ctx, s_ctx=Sc, s_lat=Tl)

    y_prompt = x[:n_ctx].reshape(Bc, Sc, D)
    y_sample = x[n_ctx:].reshape(Bl, Tl, D)
    new_state = jnp.stack(states, 1)
    new_k = jnp.stack(keys, 1)
    new_v = jnp.stack(vals, 1)
    return (y_prompt, y_sample, new_state, new_k, new_v)
```

```python
import functools
import math

import numpy as np
import jax
import jax.numpy as jnp
from jax import lax
from jax.experimental import pallas as pl
from jax.experimental.pallas import tpu as pltpu

F32 = jnp.float32
BF16 = jnp.bfloat16

D_MODEL = 2048
DEPTH = 4
GRID_W = 64
HEAD_DIM = 64
RWKV_W = 768
RWKV_HEADS = 12
HYENA_W = 512
ATTN_W = 768
ATTN_HEADS = 12
ATTN_KV_HEADS = 4
GQA_GROUP = 3
KV_W = 256
LORA = 64
GN_EPS = 64e-5
FILTER_EMB = 33
FILTER_HID = 64
HY_MIN_DECAY = math.log(1e-2) / 1.5
HY_MAX_DECAY = math.log(1e-2) / 0.3
WINDOW = 128
BLOCK = 128
ROPE_BASE = 10000.0
D_FF = 5504
LN_EPS = 1e-5
DN_ALPHA = (2 * DEPTH) ** 0.25

C_R, C_K, C_V, C_G = 0, 768, 1536, 2304
C_HY = 3072
C_Q, C_AK, C_AV = 4608, 5376, 5632
C_LORA = 5888
IN_W_EXT = 6144

MOD_BLK = 256
CHUNK = 64
RWKV_CB = 8
RWKV_CTX_SEQS = 4
LANES = 128
D_FF_PAD = 5632
VMEM_LIMIT = 56 * 1024 * 1024


def _dot(a, b):
    return jnp.dot(a.astype(BF16), b.astype(BF16), preferred_element_type=F32)


def _dot_nt(a, b):
    return lax.dot_general(a.astype(BF16), b.astype(BF16), (((1,), (1,)), ((), ())),
                           preferred_element_type=F32)


def _split(x):
    hi = x.astype(BF16)
    lo = (x - hi.astype(F32)).astype(BF16)
    return hi, lo


def _dot3_dims(a, b, dims):
    ah, al = _split(a)
    bh, bl = _split(b)
    dg = functools.partial(lax.dot_general, dimension_numbers=dims, preferred_element_type=F32)
    return dg(ah, bh) + (dg(ah, bl) + dg(al, bh))


_NN = (((1,), (0,)), ((), ()))
_NT = (((1,), (1,)), ((), ()))
_TN = (((0,), (0,)), ((), ()))


def _dot3(a, b):
    return _dot3_dims(a, b, _NN)


def _sigmoid(x):
    return 1.0 / (1.0 + jnp.exp(-x))


def _softplus(x):
    return jnp.maximum(x, 0.0) + jnp.log(1.0 + jnp.exp(-jnp.abs(x)))


def _mod_kernel(c_ref, w_ref, b_ref, o_ref):
    c = c_ref[...]
    s = c * _sigmoid(c)
    o_ref[...] = _dot(s, w_ref[...]) + b_ref[...]


def _modulation(cond, w_mod, b_mod):
    L = w_mod.shape[0]
    R = cond.shape[0]
    tn = 1024
    return pl.pallas_call(
        _mod_kernel,
        grid=(L, 6 * D_MODEL // tn),
        in_specs=[pl.BlockSpec((R, D_MODEL), lambda l, j: (0, 0)),
                  pl.BlockSpec((None, D_MODEL, tn), lambda l, j: (l, 0, j)),
                  pl.BlockSpec((None, 1, tn), lambda l, j: (l, 0, j))],
        out_specs=pl.BlockSpec((None, R, tn), lambda l, j: (l, 0, j)),
        out_shape=jax.ShapeDtypeStruct((L, R, 6 * D_MODEL), F32),
        compiler_params=pltpu.CompilerParams(dimension_semantics=("arbitrary", "arbitrary"),
                                             vmem_limit_bytes=VMEM_LIMIT),
        name="modulation",
    )(cond, w_mod, b_mod.reshape(L, 1, 6 * D_MODEL))


def _inproj_kernel(x_ref, mod_ref, w_ref, o_ref, *, tn):
    shift = mod_ref[0, 0:1, :]
    scale = mod_ref[0, 1:2, :]
    h = (x_ref[...] * (1.0 + scale) + shift).astype(BF16)
    for j in range(IN_W_EXT // tn):
        cols = slice(j * tn, (j + 1) * tn)
        o_ref[:, cols] = jnp.dot(h, w_ref[:, cols], preferred_element_type=F32)


def _in_projection(x, mod, w_ext, l, tn=512):
    N = x.shape[0]
    tm = MOD_BLK
    return pl.pallas_call(
        functools.partial(_inproj_kernel, tn=tn),
        grid=(N // tm,),
        in_specs=[pl.BlockSpec((tm, D_MODEL), lambda i: (i, 0)),
                  pl.BlockSpec((1, 6, D_MODEL), lambda i: (i, 0, 0)),
                  pl.BlockSpec((None, D_MODEL, IN_W_EXT), lambda i: (l, 0, 0), pipeline_mode=pl.Buffered(1))],
        out_specs=pl.BlockSpec((tm, IN_W_EXT), lambda i: (i, 0)),
        out_shape=jax.ShapeDtypeStruct((N, IN_W_EXT), F32),
        compiler_params=pltpu.CompilerParams(dimension_semantics=("arbitrary",),
                                             vmem_limit_bytes=VMEM_LIMIT),
        name="in_projection",
    )(x, mod, w_ext)


PREC = dict(seg=1, cum=2, a=1, neu=1, post=1, mn=1, b_o=1, b_s=1, gn=2)


def _dotp(a, b, dims, passes):
    dg = functools.partial(lax.dot_general, dimension_numbers=dims, preferred_element_type=F32)
    if passes == 1:
        return dg(a.astype(BF16), b.astype(BF16))
    ah, al = _split(a)
    if passes == 2:
        bh = b.astype(BF16)
        return dg(ah, bh) + dg(al, bh)
    bh, bl = _split(b)
    return dg(ah, bh) + (dg(ah, bl) + dg(al, bh))


def _dotp_rhs(a, b, passes):
    dg = functools.partial(lax.dot_general, dimension_numbers=_NN, preferred_element_type=F32)
    ab = a.astype(BF16)
    if passes == 1:
        return dg(ab, b.astype(BF16))
    bh, bl = _split(b)
    return dg(ab, bh) + dg(ab, bl)


def _rwkv_kernel(*refs, T, nsq, has_init, cb):
    (r_ref, k_ref, v_ref, g_ref, lo_ref, w0_ref, w2_ref, a0_ref, a2_ref,
     kk_ref, ka_ref, rk_ref, lng_ref, lnb_ref) = refs[:14]
    pos = 14
    init_ref = None
    if has_init:
        init_ref = refs[pos]
        pos += 1
    y_ref, sfin_ref = refs[pos:pos + 2]
    rp_scr, ov_scr, m_scr, n_scr, o_scr, bonus_scr, s_scr = refs[pos + 2:]

    nC = T // CHUNK
    nCs = nC // nsq
    C = CHUNK
    H = HEAD_DIM
    P = PREC
    C2 = 2 * C
    row = lax.broadcasted_iota(jnp.int32, (C, C), 0)
    col = lax.broadcasted_iota(jnp.int32, (C, C), 1)
    tri_f = jnp.where(col <= row, 1.0, 0.0).astype(F32)
    lrow = lax.broadcasted_iota(jnp.int32, (LANES, LANES), 0)
    lcol = lax.broadcasted_iota(jnp.int32, (LANES, LANES), 1)
    same_head = (lrow // H) == (lcol // H)
    head_ones = jnp.where(same_head, 1.0, 0.0).astype(F32)
    eye = lrow == lcol
    rt_, ct_ = lrow % C, lcol % C
    strict_m = [same_head & (ct_ < rt_), same_head & (ct_ > rt_)]
    incl_m = [same_head & (ct_ <= rt_), same_head & (ct_ >= rt_)]
    lane = lax.broadcasted_iota(jnp.int32, (1, LANES), 1)
    hmask = [lane < H, lane >= H]

    def stack_heads(x):
        return jnp.concatenate([jnp.where(hmask[0], x, 0.0), jnp.where(hmask[1], x, 0.0)], axis=0)

    kkw = kk_ref[...]
    kaw = ka_ref[...]
    rkw = rk_ref[...]
    zl = jnp.zeros((LORA, LANES), F32)
    lora_w = jnp.concatenate(
        [jnp.concatenate([w if j == i else zl for j in range(4)], axis=1)
         for i, w in enumerate([w2_ref[0], w2_ref[1], a2_ref[0], a2_ref[1]])], axis=0).astype(BF16)

    def phase_a(ci, carry):
        units = []
        for cc in range(cb):
            c = ci * cb + cc
            rows = pl.ds(pl.multiple_of(c * C, C), C)
            r = r_ref[rows, :]
            k = k_ref[rows, :]
            v = v_ref[rows, :]
            lo = lo_ref[rows, :]
            kkraw = k * kkw
            ss = _dotp(kkraw * kkraw, head_ones, _NN, P['seg'])
            kkn = kkraw / jnp.maximum(jnp.sqrt(ss), 1e-12)
            bonus = jnp.zeros((C, LANES), F32)
            vm = stack_heads(v)
            lo_in = jnp.concatenate([jnp.tanh(lo[:, :2 * LORA]), lo[:, 2 * LORA:]], axis=1)
            lo_out = _dot(lo_in, lora_w)
            lds, alphas = [], []
            for d in range(2):
                wl = w0_ref[d] + lo_out[:, d * LANES:(d + 1) * LANES]
                lds.append(-jnp.exp(-_softplus(-wl) - 0.5))
                alphas.append(_sigmoid(a0_ref[d] + lo_out[:, (2 + d) * LANES:(3 + d) * LANES]))
            pref = _dotp_rhs(tri_f, jnp.concatenate(lds, axis=1), P['cum'])
            for d in range(2):
                ld, alpha = lds[d], alphas[d]
                kd = k * (1.0 + (alpha - 1.0) * kaw)
                bb = kkn * alpha
                bonus = bonus + _dotp(r * kd * rkw, head_ones, _NN, P['seg']) * v
                pf = pref[:, d * LANES:(d + 1) * LANES]
                g_incl = pf if d == 0 else pf[C - 1:C, :] - pf + ld
                g_excl = g_incl - ld
                g_end = g_incl[C - 1:C, :] if d == 0 else g_incl[0:1, :]
                ginv = jnp.exp(-g_incl)
                dec_end = jnp.exp(g_end - g_incl)
                units.append(dict(
                    d=d, c=c, vm=vm,
                    at=stack_heads(-kkn * jnp.exp(g_excl)), rt=stack_heads(r * jnp.exp(g_incl)),
                    bt=stack_heads(bb * ginv), kt=stack_heads(kd * ginv),
                    bend=stack_heads(bb * dec_end), kend=stack_heads(kd * dec_end), gend=jnp.exp(g_end)))
            bonus_scr[rows, :] = bonus
            o_scr[rows, :] = jnp.zeros((C, LANES), F32)

        for u in units:
            ar = jnp.concatenate([u['at'], u['rt']], axis=0)
            bk = jnp.concatenate([u['bt'], u['kt']], axis=0)
            u['a'] = _dotp(ar, bk, _NT, P['a'])
        for u in units:
            a = u['a']
            st, ic = strict_m[u['d']], incl_m[u['d']]
            u['p'] = jnp.where(st, a[:C2, :C2], 0.0)
            u['a_ak'] = jnp.where(st, a[:C2, C2:], 0.0)
            u['a_rb'] = jnp.where(ic, a[C2:, :C2], 0.0)
            u['a_rk'] = jnp.where(ic, a[C2:, C2:], 0.0)
        rvs = [_dotp(u['a_ak'], u['vm'], _NN, P['post']) for u in units]
        sq = [_dotp(u['p'], u['p'], _NN, P['neu']) for u in units]
        for i, u in enumerate(units):
            u['x'] = jnp.where(eye, 1.0, 0.0) + u['p']
            u['p'] = sq[i]
        for it in range(1, 6):
            res = [_dotp(u['p'], jnp.concatenate([u['x'], u['p']], axis=1) if it < 5 else u['x'], _NN, P['neu'])
                   for u in units]
            for i, u in enumerate(units):
                u['x'] = u['x'] + res[i][:, :C2]
                if it < 5:
                    u['p'] = res[i][:, C2:]
        rhs = [_dotp(u['x'], jnp.concatenate([u['at'], rvs[i]], axis=1), _NN, P['post'])
               for i, u in enumerate(units)]
        for i, u in enumerate(units):
            u['rhs'] = rhs[i]
        xs = [_dotp(jnp.concatenate([u['a_rb'], u['a_rk']], axis=1),
                    jnp.concatenate([u['rhs'], jnp.concatenate([jnp.zeros_like(u['vm']), u['vm']], axis=1)], axis=0),
                    _NN, P['post']) for u in units]
        ms = [_dotp(u['rhs'][:, :LANES], u['bend'], _TN, P['mn']) for u in units]
        ns = [_dotp(jnp.concatenate([u['rhs'][:, LANES:], u['vm']], axis=0),
                    jnp.concatenate([u['bend'], u['kend']], axis=0), _TN, P['mn']) for u in units]
        for i, u in enumerate(units):
            d, c = u['d'], u['c']
            rp = u['rt'] + xs[i][:, :LANES]
            ov = xs[i][:, LANES:]
            rp_scr[d, c] = rp[:C] + rp[C:]
            ov_scr[d, c] = ov[:C] + ov[C:]
            m_scr[d, c] = jnp.where(eye, jnp.broadcast_to(u['gend'], (LANES, LANES)), 0.0) + ms[i]
            n_scr[d, c] = ns[i]
        return carry

    lax.fori_loop(0, nC // cb, phase_a, 0)

    zero_blk = jnp.zeros((H, H), F32)
    for q in range(nsq):
        for d in range(2):
            if has_init:
                s_scr[q, d] = jnp.concatenate([jnp.concatenate([init_ref[q, d, 0], zero_blk], axis=1),
                                               jnp.concatenate([zero_blk, init_ref[q, d, 1]], axis=1)], axis=0)
            else:
                s_scr[q, d] = jnp.zeros((LANES, LANES), F32)

    def phase_b(i, carry):
        idx = [(q, d, q * nCs + (i if d == 0 else nCs - 1 - i)) for q in range(nsq) for d in range(2)]
        ss = [s_scr[q, d] for q, d, c in idx]
        os_ = [_dotp(rp_scr[d, c], ss[j], _NT, P['b_o']) for j, (q, d, c) in enumerate(idx)]
        sn = [_dotp(ss[j], m_scr[d, c], _NN, P['b_s']) for j, (q, d, c) in enumerate(idx)]
        for j, (q, d, c) in enumerate(idx):
            s_scr[q, d] = sn[j] + n_scr[d, c]
            rows = pl.ds(c * C if isinstance(c, int) else pl.multiple_of(c * C, C), C)
            o_scr[rows, :] = o_scr[rows, :] + (os_[j] + ov_scr[d, c])
        return carry

    if nCs <= 4:
        for i in range(nCs):
            phase_b(i, 0)
    else:
        lax.fori_loop(0, nCs, phase_b, 0, unroll=4)

    for q in range(nsq):
        for d in range(2):
            s = s_scr[q, d]
            sfin_ref[q, d, 0] = s[:H, :H]
            sfin_ref[q, d, 1] = s[H:, H:]

    lng = lng_ref[...]
    lnb = lnb_ref[...]

    rb = min(T, 256)

    def phase_c(c, carry):
        rows = pl.ds(pl.multiple_of(c * rb, rb), rb)
        o = o_scr[rows, :]
        st = _dotp(jnp.concatenate([o, o * o], axis=0), head_ones, _NN, P['gn']) * (1.0 / H)
        mu = st[:rb]
        var = st[rb:] - mu * mu
        on = (o - mu) * lax.rsqrt(var + GN_EPS) * lng + lnb
        y_ref[rows, :] = (on + bonus_scr[rows, :]) * _sigmoid(g_ref[rows, :])
        return carry

    lax.fori_loop(0, T // rb, phase_c, 0)


def _rwkv(proj, p, init, *, T, n_seq, row_blk0, nsq=1):
    seq_len, T = T, nsq * T
    nC = T // CHUNK
    npair = RWKV_HEADS // 2
    has_init = init is not None
    cb = lambda off: (lambda b, j: (b + row_blk0, off // LANES + j))
    vec = lambda b, j: (0, j)
    in_specs = [pl.BlockSpec((T, LANES), cb(C_R)),
                pl.BlockSpec((T, LANES), cb(C_K)),
                pl.BlockSpec((T, LANES), cb(C_V)),
                pl.BlockSpec((T, LANES), cb(C_G)),
                pl.BlockSpec((T, 4 * LORA), lambda b, j: (b + row_blk0, C_LORA // (4 * LORA))),
                pl.BlockSpec((2, 1, LANES), lambda b, j: (0, 0, j)),
                pl.BlockSpec((2, LORA, LANES), lambda b, j: (0, 0, j)),
                pl.BlockSpec((2, 1, LANES), lambda b, j: (0, 0, j)),
                pl.BlockSpec((2, LORA, LANES), lambda b, j: (0, 0, j)),
                pl.BlockSpec((1, LANES), vec), pl.BlockSpec((1, LANES), vec),
                pl.BlockSpec((1, LANES), vec), pl.BlockSpec((1, LANES), vec),
                pl.BlockSpec((1, LANES), vec)]
    args = [proj, proj, proj, proj, proj,
            p['rwkv_w0'].reshape(2, 1, RWKV_W), p['rwkv_w2'], p['rwkv_a0'].reshape(2, 1, RWKV_W), p['rwkv_a2'],
            p['rwkv_k_k'].reshape(1, RWKV_W), p['rwkv_k_a'].reshape(1, RWKV_W),
            p['rwkv_r_k'].reshape(1, RWKV_W), p['rwkv_lnx_g'].reshape(1, RWKV_W),
            p['rwkv_lnx_b'].reshape(1, RWKV_W)]
    st_spec = pl.BlockSpec((nsq, 2, 2, HEAD_DIM, HEAD_DIM), lambda b, j: (b, 0, j, 0, 0))
    if has_init:
        in_specs.append(st_spec)
        args.append(init)
    sc_o = lambda: pltpu.VMEM((2, nC, CHUNK, LANES), F32)
    sc_s = lambda: pltpu.VMEM((2, nC, LANES, LANES), F32)
    return pl.pallas_call(
        functools.partial(_rwkv_kernel, T=T, nsq=nsq, has_init=has_init, cb=min(RWKV_CB, nC)),
        grid=(n_seq // nsq, npair),
        in_specs=in_specs,
        out_specs=[pl.BlockSpec((T, LANES), lambda b, j: (b, j)), st_spec],
        out_shape=[jax.ShapeDtypeStruct((n_seq * seq_len, RWKV_W), F32),
                   jax.ShapeDtypeStruct((n_seq, 2, RWKV_HEADS, HEAD_DIM, HEAD_DIM), F32)],
        scratch_shapes=[sc_o(), sc_o(), sc_s(), sc_s(),
                        pltpu.VMEM((T, LANES), F32), pltpu.VMEM((T, LANES), F32),
                        pltpu.VMEM((nsq, 2, LANES, LANES), F32)],
        compiler_params=pltpu.CompilerParams(dimension_semantics=("arbitrary", "arbitrary"),
                                             vmem_limit_bytes=VMEM_LIMIT),
        name="rwkv_T%d" % T,
    )(*args)


def _dft_mats(L):
    f = np.arange(L, dtype=np.int64)
    ph = (np.outer(f, f) % (2 * L)).astype(np.float64) * (math.pi / L)
    return np.cos(ph), -np.sin(ph)


def _filter_features(L):
    t = np.linspace(0.0, 1.0, L, dtype=np.float32)[:, None]
    bands = (FILTER_EMB - 1) // 2
    t_res = np.arange(L, dtype=np.float32)[:, None]
    f = np.linspace(1e-4, bands - 1, bands, dtype=np.float32)[None, :]
    ang = (2.0 * math.pi * t_res * f / L).astype(np.float32)
    z = np.concatenate([t, np.cos(ang), np.sin(ang)], -1).astype(np.float32)
    zp = np.zeros((L, FILTER_HID), np.float32)
    zp[:, :FILTER_EMB] = z
    deltas = np.abs(np.linspace(HY_MIN_DECAY, HY_MAX_DECAY, HYENA_W, dtype=np.float32))
    return zp, t.astype(np.float32), deltas[None, :].astype(np.float32)


def _hyfilter_kernel(z_ref, t_ref, dl_ref, w1_ref, b1_ref, q1_ref, w2_ref, b2_ref, q2_ref, wc_ref, wa_ref,
                     hs_ref, hd_ref, knyq_ref, hm_scr, *, L):
    @pl.when((pl.program_id(0) == 0) & (pl.program_id(1) == 0))
    def _():
        h1 = jnp.sin(q1_ref[...] * (_dot3(z_ref[...], w1_ref[...]) + b1_ref[...]))
        hm_scr[...] = jnp.sin(q2_ref[...] * (_dot3(h1, w2_ref[...]) + b2_ref[...]))

    hm = hm_scr[...]
    win = jnp.exp(-t_ref[...] * dl_ref[...])
    lag = lax.broadcasted_iota(jnp.int32, (L, 1), 0)
    alt = jnp.where(lag % 2 == 0, 1.0, -1.0).astype(F32)
    hc = _dot3(hm, wc_ref[...]) * win
    ha = jnp.where(lag == 0, 0.0, _dot3(hm, wa_ref[...]) * win)
    hs_ref[...] = hc + ha
    hd_ref[...] = hc - ha
    knyq_ref[...] = jnp.sum((hc + ha) * alt, axis=0, keepdims=True) * (1.0 / (2 * L))


def _hyspec_kernel(cm_ref, sm_ref, hs_ref, hd_ref, kre_ref, kim_ref, *, L, tf):
    f = pl.program_id(2) * tf + lax.broadcasted_iota(jnp.int32, (tf, 1), 0)
    wf = jnp.where(f == 0, 1.0, 2.0).astype(F32) * (1.0 / (2 * L))
    kre_ref[...] = _dot3(cm_ref[...], hs_ref[...]) * wf
    kim_ref[...] = _dot3(sm_ref[...], hd_ref[...]) * wf


def _hyena_filter(p, L, cm, sm, cw=128):
    zp, t, dl = _filter_features(L)
    w1 = jnp.zeros((FILTER_HID, FILTER_HID), F32).at[:FILTER_EMB].set(p['hy_f_w1'])
    r2 = lambda a: a.reshape(1, -1)
    ncb = HYENA_W // cw
    full = lambda shape: pl.BlockSpec(shape, lambda n, j: (0,) * len(shape))
    shp = jax.ShapeDtypeStruct((2, L, HYENA_W), F32)
    hs, hd, knyq = pl.pallas_call(
        functools.partial(_hyfilter_kernel, L=L),
        grid=(2, ncb),
        in_specs=[full((L, FILTER_HID)), full((L, 1)),
                  pl.BlockSpec((1, cw), lambda n, j: (0, j)),
                  full((FILTER_HID, FILTER_HID)), full((1, FILTER_HID)), full((1, FILTER_HID)),
                  full((FILTER_HID, FILTER_HID)), full((1, FILTER_HID)), full((1, FILTER_HID)),
                  pl.BlockSpec((FILTER_HID, cw), lambda n, j: (0, n * 2 * ncb + j)),
                  pl.BlockSpec((FILTER_HID, cw), lambda n, j: (0, n * 2 * ncb + ncb + j))],
        out_specs=[pl.BlockSpec((None, L, cw), lambda n, j: (n, 0, j)),
                   pl.BlockSpec((None, L, cw), lambda n, j: (n, 0, j)),
                   pl.BlockSpec((None, 1, cw), lambda n, j: (n, 0, j))],
        out_shape=[shp, shp, jax.ShapeDtypeStruct((2, 1, HYENA_W), F32)],
        scratch_shapes=[pltpu.VMEM((L, FILTER_HID), F32)],
        compiler_params=pltpu.CompilerParams(dimension_semantics=("arbitrary", "arbitrary"),
                                             vmem_limit_bytes=VMEM_LIMIT),
        name="hyena_filter_L%d" % L,
    )(jnp.asarray(zp), jnp.asarray(t), jnp.asarray(dl), w1, r2(p['hy_f_b1']), r2(p['hy_f_freq1']),
      p['hy_f_w2'], r2(p['hy_f_b2']), r2(p['hy_f_freq2']), p['hy_f_w3'], p['hy_f_w3'])
    tf = min(256, L)
    sw = 256
    kre, kim = pl.pallas_call(
        functools.partial(_hyspec_kernel, L=L, tf=tf),
        grid=(2, HYENA_W // sw, L // tf),
        in_specs=[pl.BlockSpec((tf, L), lambda n, j, i: (i, 0)),
                  pl.BlockSpec((tf, L), lambda n, j, i: (i, 0)),
                  pl.BlockSpec((None, L, sw), lambda n, j, i: (n, 0, j)),
                  pl.BlockSpec((None, L, sw), lambda n, j, i: (n, 0, j))],
        out_specs=[pl.BlockSpec((None, tf, sw), lambda n, j, i: (n, i, j)),
                   pl.BlockSpec((None, tf, sw), lambda n, j, i: (n, i, j))],
        out_shape=[shp, shp],
        compiler_params=pltpu.CompilerParams(dimension_semantics=("arbitrary",) * 3,
                                             vmem_limit_bytes=VMEM_LIMIT),
        name="hyena_spectrum_L%d" % L,
    )(cm, sm, hs, hd)
    return kre, kim, knyq


def _hyena_kernel(v_ref, x1_ref, x2_ref, sw_ref, bias_ref, cm_ref, sm_ref, kre_ref, kim_ref, knyq_ref,
                  o_ref, zb_scr, zf_scr, xs_scr, yre_scr, yim_scr, *, L, rb):
    nrb = L // rb
    t = lax.broadcasted_iota(jnp.int32, (L, 1), 0)
    first = t == 0
    last = t == L - 1
    alt_all = jnp.where(t % 2 == 0, 1.0, -1.0).astype(F32)
    tb = lax.broadcasted_iota(jnp.int32, (rb, 1), 0)
    alt_blk = jnp.where(tb % 2 == 0, 1.0, -1.0).astype(F32)
    sw = sw_ref[...]

    def short(ref, s):
        u = ref[...]
        prev = jnp.where(first, 0.0, pltpu.roll(u, 1, 0))
        nxt = jnp.where(last, 0.0, pltpu.roll(u, L - 1, 0))
        return prev * sw[0, s][None, :] + u * sw[1, s][None, :] + nxt * sw[2, s][None, :]

    z0 = short(v_ref, 0)
    zf_scr[...] = z0
    zb_scr[...] = z0.astype(BF16)
    unyq = jnp.sum(z0 * alt_all, axis=0, keepdims=True)

    for n in range(2):
        xs_scr[...] = short(x1_ref if n == 0 else x2_ref, n + 1)

        def fwd(fb, carry):
            rows = pl.ds(pl.multiple_of(fb * rb, rb), rb)
            zb = zb_scr[...]
            ure = jnp.dot(cm_ref[rows, :], zb, preferred_element_type=F32)
            uim = jnp.dot(sm_ref[rows, :], zb, preferred_element_type=F32)
            kre = kre_ref[n, rows, :]
            kim = kim_ref[n, rows, :]
            yre_scr[rows, :] = (ure * kre - uim * kim).astype(BF16)
            yim_scr[rows, :] = (ure * kim + uim * kre).astype(BF16)
            return carry

        lax.fori_loop(0, nrb, fwd, 0)
        ynyq = unyq * knyq_ref[n]
        bias = bias_ref[n]

        def inv(ib, acc):
            rows = pl.ds(pl.multiple_of(ib * rb, rb), rb)
            y = (jnp.dot(cm_ref[rows, :], yre_scr[...], preferred_element_type=F32)
                 + jnp.dot(sm_ref[rows, :], yim_scr[...], preferred_element_type=F32))
            y = y + alt_blk * ynyq + zf_scr[rows, :] * bias
            znew = xs_scr[rows, :] * y
            if n == 0:
                zf_scr[rows, :] = znew
                zb_scr[rows, :] = znew.astype(BF16)
                acc = acc + jnp.sum(znew * alt_blk, axis=0, keepdims=True)
            else:
                o_ref[rows, :] = znew
            return acc

        unyq = lax.fori_loop(0, nrb, inv, jnp.zeros_like(unyq))


def _hyena(proj, p, filt, cm, sm, *, L, n_seq, row_blk0, cw=256):
    kre, kim, knyq = filt
    ncb = HYENA_W // cw
    cb = lambda s: (lambda j, b: (b + row_blk0, (C_HY + s * HYENA_W) // cw + j))
    once = pl.Buffered(1)
    sw = p['hy_short_w'].reshape(3, 3, HYENA_W)
    rb = min(256, L)
    return pl.pallas_call(
        functools.partial(_hyena_kernel, L=L, rb=rb),
        grid=(ncb, n_seq),
        in_specs=[pl.BlockSpec((L, cw), cb(0)), pl.BlockSpec((L, cw), cb(1)), pl.BlockSpec((L, cw), cb(2)),
                  pl.BlockSpec((3, 3, cw), lambda j, b: (0, 0, j)),
                  pl.BlockSpec((2, 1, cw), lambda j, b: (0, 0, j)),
                  pl.BlockSpec((L, L), lambda j, b: (0, 0), pipeline_mode=once),
                  pl.BlockSpec((L, L), lambda j, b: (0, 0), pipeline_mode=once),
                  pl.BlockSpec((2, L, cw), lambda j, b: (0, 0, j), pipeline_mode=once),
                  pl.BlockSpec((2, L, cw), lambda j, b: (0, 0, j), pipeline_mode=once),
                  pl.BlockSpec((2, 1, cw), lambda j, b: (0, 0, j))],
        out_specs=pl.BlockSpec((L, cw), lambda j, b: (b, j)),
        out_shape=jax.ShapeDtypeStruct((n_seq * L, HYENA_W), F32),
        scratch_shapes=[pltpu.VMEM((L, cw), BF16), pltpu.VMEM((L, cw), F32), pltpu.VMEM((L, cw), F32),
                        pltpu.VMEM((L, cw), BF16), pltpu.VMEM((L, cw), BF16)],
        compiler_params=pltpu.CompilerParams(dimension_semantics=("arbitrary", "arbitrary"),
                                             vmem_limit_bytes=VMEM_LIMIT),
        name="hyena_L%d" % L,
    )(proj, proj, proj, sw, p['hy_bias'].reshape(2, 1, HYENA_W), cm.astype(BF16), sm.astype(BF16),
      kre, kim, knyq)


def _row_max(x):
    t = x[:, :LANES]
    for c in range(1, x.shape[1] // LANES):
        t = jnp.maximum(t, x[:, c * LANES:(c + 1) * LANES])
    return jnp.max(t, axis=-1, keepdims=True)


def _row_sum(x):
    t = x[:, :LANES]
    for c in range(1, x.shape[1] // LANES):
        t = t + x[:, c * LANES:(c + 1) * LANES]
    return jnp.sum(t, axis=-1, keepdims=True)


def _head_masks():
    lane = lax.broadcasted_iota(jnp.int32, (1, KV_W), 1) // HEAD_DIM
    return [lane == h for h in range(ATTN_KV_HEADS)]


def _attend_group(qg, keys, vals, masks, sinks, hm):
    nh = len(hm)
    qm = [jnp.where(hm[h], qg, 0.0).astype(BF16) for h in range(nh)]
    s = [[_dot_nt(qm[h], k) for k in keys] for h in range(nh)]
    s = [[sc if mk is None else jnp.where(mk, sc, -jnp.inf) for sc, mk in zip(s[h], masks)] for h in range(nh)]
    m = [jnp.maximum(_row_max(jnp.concatenate(s[h], axis=1)), sinks[h]) for h in range(nh)]
    p = [[jnp.exp(sc - m[h]) for sc in s[h]] for h in range(nh)]
    den = [_row_sum(jnp.concatenate(p[h], axis=1)) + jnp.exp(sinks[h] - m[h]) for h in range(nh)]
    pv = [[_dot(pc, v) for pc, v in zip(p[h], vals)] for h in range(nh)]
    acc = jnp.zeros(qg.shape, F32)
    for h in range(nh):
        tot = pv[h][0]
        for t in pv[h][1:]:
            tot = tot + t
        acc = jnp.where(hm[h], tot / den[h], acc)
    return acc


def _ctx_attn_kernel(q_ref, k_ref, v_ref, sink_ref, o_ref, *, S):
    scale = HEAD_DIM ** -0.5
    hm = _head_masks()
    k = k_ref[...].astype(BF16)
    v = v_ref[...].astype(BF16)
    for g in range(GQA_GROUP):
        qg = q_ref[:, g * KV_W:(g + 1) * KV_W] * scale
        sinks = [jnp.broadcast_to(sink_ref[0:1, h * GQA_GROUP + g:h * GQA_GROUP + g + 1], (S, 1))
                 for h in range(ATTN_KV_HEADS)]
        o_ref[:, g * KV_W:(g + 1) * KV_W] = _attend_group(qg, [k], [v], [None], sinks, hm)


def _ctx_attention(proj, sink, *, S, n_seq):
    return pl.pallas_call(
        functools.partial(_ctx_attn_kernel, S=S),
        grid=(n_seq,),
        in_specs=[pl.BlockSpec((S, ATTN_W), lambda b: (b, C_Q // ATTN_W)),
                  pl.BlockSpec((S, KV_W), lambda b: (b, C_AK // KV_W)),
                  pl.BlockSpec((S, KV_W), lambda b: (b, C_AV // KV_W)),
                  pl.BlockSpec((1, LANES), lambda b: (0, 0))],
        out_specs=pl.BlockSpec((S, ATTN_W), lambda b: (b, 0)),
        out_shape=jax.ShapeDtypeStruct((n_seq * S, ATTN_W), F32),
        compiler_params=pltpu.CompilerParams(dimension_semantics=("arbitrary",),
                                             vmem_limit_bytes=VMEM_LIMIT),
        name="ctx_attention",
    )(proj, proj, proj, sink)


def _rope(x, cos, sin_signed, first_half):
    n = x.shape[-1]
    partner = jnp.where(first_half, pltpu.roll(x, n - HEAD_DIM // 2, 1), pltpu.roll(x, HEAD_DIM // 2, 1))
    return x * cos + partner * sin_signed


def _lat_attn_kernel(q_ref, kp_ref, kc_ref, kn_ref, vp_ref, vc_ref, vn_ref, ck_ref, cv_ref,
                     cq_ref, sq_ref, cp_ref, sp_ref, cc_ref, sc_ref, cn_ref, sn_ref, sink_ref, o_ref, *, T):
    i = pl.program_id(1)
    scale = HEAD_DIM ** -0.5
    hm = _head_masks()
    fh_q = (lax.broadcasted_iota(jnp.int32, (1, ATTN_W), 1) % HEAD_DIM) < HEAD_DIM // 2
    fh_k = (lax.broadcasted_iota(jnp.int32, (1, KV_W), 1) % HEAD_DIM) < HEAD_DIM // 2
    q = _rope(q_ref[...], cq_ref[...], sq_ref[...], fh_q) * scale
    kl = jnp.concatenate([_rope(kp_ref[...], cp_ref[...], sp_ref[...], fh_k),
                          _rope(kc_ref[...], cc_ref[...], sc_ref[...], fh_k),
                          _rope(kn_ref[...], cn_ref[...], sn_ref[...], fh_k)], axis=0).astype(BF16)
    vl = jnp.concatenate([vp_ref[...], vc_ref[...], vn_ref[...]], axis=0).astype(BF16)
    kc = ck_ref[...].astype(BF16)
    vc = cv_ref[...].astype(BF16)
    qpos = i * BLOCK + lax.broadcasted_iota(jnp.int32, (BLOCK, 3 * BLOCK), 0)
    kpos = (i - 1) * BLOCK + lax.broadcasted_iota(jnp.int32, (BLOCK, 3 * BLOCK), 1)
    ok = (jnp.abs(qpos - kpos) <= WINDOW) & (kpos >= 0) & (kpos < T)
    for g in range(GQA_GROUP):
        qg = q[:, g * KV_W:(g + 1) * KV_W]
        sinks = [jnp.broadcast_to(sink_ref[0:1, h * GQA_GROUP + g:h * GQA_GROUP + g + 1], (BLOCK, 1))
                 for h in range(ATTN_KV_HEADS)]
        o_ref[:, g * KV_W:(g + 1) * KV_W] = _attend_group(qg, [kl, kc], [vl, vc], [ok, None], sinks, hm)


def _rope_tables(T):
    rows = T // GRID_W
    row = np.repeat(np.arange(rows), GRID_W).astype(np.float32)
    col = np.tile(np.arange(GRID_W), rows).astype(np.float32)
    nf = HEAD_DIM // 4
    inv = (ROPE_BASE ** (-np.arange(nf, dtype=np.float32) / nf)).astype(np.float32)
    ang = np.concatenate([row[:, None] * inv, col[:, None] * inv], -1).astype(np.float32)
    cos = np.cos(ang).astype(np.float32)
    sin = np.sin(ang).astype(np.float32)
    cos_h = np.concatenate([cos, cos], -1)
    sin_h = np.concatenate([-sin, sin], -1)
    return cos_h, sin_h


def _lat_attention(proj, cache_k, cache_v, sink, *, T, n_seq, row_blk0):
    nb = T // BLOCK
    P = cache_k.shape[1]
    cos_h, sin_h = _rope_tables(T)
    cq = jnp.asarray(np.tile(cos_h, (1, ATTN_HEADS)))
    sq = jnp.asarray(np.tile(sin_h, (1, ATTN_HEADS)))
    ck = jnp.asarray(np.tile(cos_h, (1, ATTN_KV_HEADS)))
    sk = jnp.asarray(np.tile(sin_h, (1, ATTN_KV_HEADS)))
    r0 = row_blk0 * nb
    prv = lambda i: jnp.maximum(i - 1, 0)
    nxt = lambda i: jnp.minimum(i + 1, nb - 1)
    kcol, vcol = C_AK // KV_W, C_AV // KV_W
    kv = lambda col, f: pl.BlockSpec((BLOCK, KV_W), lambda b, i: (r0 + b * nb + f(i), col))
    tab = lambda w, f: pl.BlockSpec((BLOCK, w), lambda b, i: (f(i), 0))
    cur = lambda i: i
    return pl.pallas_call(
        functools.partial(_lat_attn_kernel, T=T),
        grid=(n_seq, nb),
        in_specs=[pl.BlockSpec((BLOCK, ATTN_W), lambda b, i: (r0 + b * nb + i, C_Q // ATTN_W)),
                  kv(kcol, prv), kv(kcol, cur), kv(kcol, nxt),
                  kv(vcol, prv), kv(vcol, cur), kv(vcol, nxt),
                  pl.BlockSpec((None, P, KV_W), lambda b, i: (b, 0, 0)),
                  pl.BlockSpec((None, P, KV_W), lambda b, i: (b, 0, 0)),
                  tab(ATTN_W, cur), tab(ATTN_W, cur),
                  tab(KV_W, prv), tab(KV_W, prv), tab(KV_W, cur), tab(KV_W, cur), tab(KV_W, nxt), tab(KV_W, nxt),
                  pl.BlockSpec((1, LANES), lambda b, i: (0, 0))],
        out_specs=pl.BlockSpec((BLOCK, ATTN_W), lambda b, i: (b * nb + i, 0)),
        out_shape=jax.ShapeDtypeStruct((n_seq * T, ATTN_W), F32),
        compiler_params=pltpu.CompilerParams(dimension_semantics=("arbitrary", "arbitrary"),
                                             vmem_limit_bytes=VMEM_LIMIT),
        name="latent_attention",
    )(proj, proj, proj, proj, proj, proj, proj, cache_k, cache_v, cq, sq, ck, sk, ck, sk, ck, sk, sink)


def _layer_norm(y, g, b):
    mu = jnp.mean(y, axis=-1, keepdims=True)
    d = y - mu
    var = jnp.mean(d * d, axis=-1, keepdims=True)
    return d * lax.rsqrt(var + LN_EPS) * g + b


def _outproj_kernel(ac_ref, bc_ref, cc_ref, al_ref, bl_ref, cl_ref, x_ref, mod_ref, w_ref,
                    g_ref, be_ref, x1_ref, h2_ref, *, tm, n_ctx_tiles):
    r1, r2 = RWKV_W, RWKV_W + HYENA_W
    sub = 128

    def tile(a_ref, b_ref, c_ref):
        ms = []
        for s in range(tm // sub):
            rows = slice(s * sub, (s + 1) * sub)
            ms.append(jnp.dot(a_ref[rows, :].astype(BF16), w_ref[0:r1, :], preferred_element_type=F32)
                      + jnp.dot(b_ref[rows, :].astype(BF16), w_ref[r1:r2, :], preferred_element_type=F32)
                      + jnp.dot(c_ref[rows, :].astype(BF16), w_ref[r2:, :], preferred_element_type=F32))
        for s in range(tm // sub):
            rows = slice(s * sub, (s + 1) * sub)
            blk = (s * sub) // MOD_BLK
            gate1 = mod_ref[blk, 2:3, :]
            shift2 = mod_ref[blk, 3:4, :]
            scale2 = mod_ref[blk, 4:5, :]
            x1 = _layer_norm(DN_ALPHA * x_ref[rows, :] + gate1 * ms[s], g_ref[...], be_ref[...])
            x1_ref[rows, :] = x1
            h2_ref[rows, :] = (x1 * (1.0 + scale2) + shift2).astype(BF16)

    pl.when(pl.program_id(0) < n_ctx_tiles)(lambda: tile(ac_ref, bc_ref, cc_ref))
    pl.when(pl.program_id(0) >= n_ctx_tiles)(lambda: tile(al_ref, bl_ref, cl_ref))


def _out_projection(mix_ctx, mix_lat, x, mod, w_out, l, g, b, tm=256):
    N = x.shape[0]
    nsub = tm // MOD_BLK
    nct = mix_ctx[0].shape[0] // tm
    nlt = mix_lat[0].shape[0] // tm
    row = lambda w: pl.BlockSpec((tm, w), lambda i: (i, 0))
    ctx = lambda w: pl.BlockSpec((tm, w), lambda i: (jnp.minimum(i, nct - 1), 0))
    lat = lambda w: pl.BlockSpec((tm, w), lambda i: (jnp.clip(i - nct, 0, nlt - 1), 0))
    full = lambda r: pl.BlockSpec((r, D_MODEL), lambda i: (0, 0))
    return pl.pallas_call(
        functools.partial(_outproj_kernel, tm=tm, n_ctx_tiles=nct),
        grid=(N // tm,),
        in_specs=[ctx(RWKV_W), ctx(HYENA_W), ctx(ATTN_W), lat(RWKV_W), lat(HYENA_W), lat(ATTN_W), row(D_MODEL),
                  pl.BlockSpec((nsub, 6, D_MODEL), lambda i: (i, 0, 0)),
                  pl.BlockSpec((None, D_MODEL, D_MODEL), lambda i: (l, 0, 0)), full(1), full(1)],
        out_specs=[row(D_MODEL), row(D_MODEL)],
        out_shape=[jax.ShapeDtypeStruct((N, D_MODEL), F32), jax.ShapeDtypeStruct((N, D_MODEL), BF16)],
        compiler_params=pltpu.CompilerParams(dimension_semantics=("arbitrary",),
                                             vmem_limit_bytes=VMEM_LIMIT),
        name="out_projection",
    )(*mix_ctx, *mix_lat, x, mod, w_out, g.reshape(1, D_MODEL), b.reshape(1, D_MODEL))


HALO = 16


def _ffn_kernel(hp_ref, h_ref, hn_ref, x1_ref, mod_ref, wa_ref, wb_ref, cwa_ref, cwb_ref, wd_ref, g_ref, be_ref,
                o_ref, hx_scr, ua_scr, ub_scr, acc_scr, *, tm, tf, nsub, n_ctx, s_ctx, s_lat):
    i = pl.program_id(0)
    j = pl.program_id(1)

    @pl.when(j == 0)
    def _():
        hx_scr[0:HALO, :] = hp_ref[...]
        hx_scr[HALO:HALO + tm, :] = h_ref[...]
        hx_scr[HALO + tm:, :] = hn_ref[...]
        acc_scr[...] = jnp.zeros_like(acc_scr)

    hx = hx_scr[...]
    ua_scr[...] = jnp.dot(hx, wa_ref[...], preferred_element_type=F32)
    ub_scr[...] = jnp.dot(hx, wb_ref[...], preferred_element_type=F32)
    grow = i * tm + lax.broadcasted_iota(jnp.int32, (tm, 1), 0)
    pos = jnp.where(grow < n_ctx, grow % s_ctx, (grow - n_ctx) % s_lat)
    slen = jnp.where(grow < n_ctx, s_ctx, s_lat)
    mprev = jnp.where(pos == 0, 0.0, 1.0).astype(F32)
    mnext = jnp.where(pos == slen - 1, 0.0, 1.0).astype(F32)

    def conv(u_scr, cw_ref):
        up = u_scr[HALO - 1:HALO - 1 + tm, :] * mprev
        uc = u_scr[HALO:HALO + tm, :]
        un = u_scr[HALO + 1:HALO + 1 + tm, :] * mnext
        return up * cw_ref[0:1, :] + uc * cw_ref[1:2, :] + un * cw_ref[2:3, :]

    a = conv(ua_scr, cwa_ref)
    b = conv(ub_scr, cwb_ref)
    act = (a * _sigmoid(a) * b).astype(BF16)
    acc_scr[...] += jnp.dot(act, wd_ref[...], preferred_element_type=F32)

    @pl.when(j == pl.num_programs(1) - 1)
    def _():
        for s in range(nsub):
            rows = slice(s * MOD_BLK, (s + 1) * MOD_BLK)
            gate2 = mod_ref[s, 5:6, :]
            o_ref[rows, :] = _layer_norm(DN_ALPHA * x1_ref[rows, :] + gate2 * acc_scr[rows, :],
                                         g_ref[...], be_ref[...])


def _ffn(h2, x1, mod, wa, wb, cw, wd, l, g, b, *, n_ctx, s_ctx, s_lat, tm=512, tf=512):
    N = x1.shape[0]
    nsub = tm // MOD_BLK
    nh = tm // HALO
    last_h = N // HALO - 1
    pad = ((0, 0), (0, D_FF_PAD - D_FF))
    cwa, cwb = jnp.pad(cw[:, :D_FF], pad), jnp.pad(cw[:, D_FF:], pad)
    return pl.pallas_call(
        functools.partial(_ffn_kernel, tm=tm, tf=tf, nsub=nsub, n_ctx=n_ctx, s_ctx=s_ctx, s_lat=s_lat),
        grid=(N // tm, D_FF_PAD // tf),
        in_specs=[pl.BlockSpec((HALO, D_MODEL), lambda i, j: (jnp.maximum(i * nh - 1, 0), 0)),
                  pl.BlockSpec((tm, D_MODEL), lambda i, j: (i, 0)),
                  pl.BlockSpec((HALO, D_MODEL), lambda i, j: (jnp.minimum((i + 1) * nh, last_h), 0)),
                  pl.BlockSpec((tm, D_MODEL), lambda i, j: (i, 0)),
                  pl.BlockSpec((nsub, 6, D_MODEL), lambda i, j: (i, 0, 0)),
                  pl.BlockSpec((None, D_MODEL, tf), lambda i, j: (l, 0, j)),
                  pl.BlockSpec((None, D_MODEL, tf), lambda i, j: (l, 0, j)),
                  pl.BlockSpec((3, tf), lambda i, j: (0, j)),
                  pl.BlockSpec((3, tf), lambda i, j: (0, j)),
                  pl.BlockSpec((None, tf, D_MODEL), lambda i, j: (l, j, 0)),
                  pl.BlockSpec((1, D_MODEL), lambda i, j: (0, 0)),
                  pl.BlockSpec((1, D_MODEL), lambda i, j: (0, 0))],
        out_specs=pl.BlockSpec((tm, D_MODEL), lambda i, j: (i, 0)),
        out_shape=jax.ShapeDtypeStruct((N, D_MODEL), F32),
        scratch_shapes=[pltpu.VMEM((tm + 2 * HALO, D_MODEL), BF16),
                        pltpu.VMEM((tm + 2 * HALO, tf), F32),
                        pltpu.VMEM((tm + 2 * HALO, tf), F32),
                        pltpu.VMEM((tm, D_MODEL), F32)],
        compiler_params=pltpu.CompilerParams(dimension_semantics=("arbitrary", "arbitrary"),
                                             vmem_limit_bytes=VMEM_LIMIT),
        name="conv_ffn",
    )(h2, h2, h2, x1, mod, wa, wb, cwa, cwb, wd, g.reshape(1, D_MODEL), b.reshape(1, D_MODEL))


def _prep_up_kernel(a_ref, b_ref, wa_ref, wb_ref):
    zeros = jnp.zeros((a_ref.shape[0], D_FF_PAD - D_FF), BF16)
    wa_ref[:, :D_FF] = a_ref[...].astype(BF16)
    wa_ref[:, D_FF:] = zeros
    wb_ref[:, :D_FF] = b_ref[...].astype(BF16)
    wb_ref[:, D_FF:] = zeros


def _prep_down_kernel(w_ref, o_ref, *, n_real):
    keep = pl.program_id(1) < n_real
    o_ref[...] = jnp.where(keep, w_ref[...], 0.0).astype(BF16)


def _prep_in_kernel(w_ref, w1_ref, a1_ref, o_ref):
    n = w_ref.shape[-1]
    o_ref[:, :C_Q] = w_ref[:, :C_Q].astype(BF16)
    for g in range(GQA_GROUP):
        for h in range(ATTN_KV_HEADS):
            src = C_Q + (h * GQA_GROUP + g) * HEAD_DIM
            dst = C_Q + (g * ATTN_KV_HEADS + h) * HEAD_DIM
            o_ref[:, dst:dst + HEAD_DIM] = w_ref[:, src:src + HEAD_DIM].astype(BF16)
    o_ref[:, C_AK:n] = w_ref[:, C_AK:n].astype(BF16)
    o_ref[:, n:] = jnp.concatenate([w1_ref[0], w1_ref[1], a1_ref[0], a1_ref[1]], axis=1).astype(BF16)


def _cast_kernel(w_ref, o_ref):
    o_ref[...] = w_ref[...].astype(BF16)


def _prep_weights(w_in, rwkv_w1, rwkv_a1, w_out, ffn_w_up, ffn_w_down):
    L, D = w_in.shape[0], D_MODEL
    cp = lambda n: pltpu.CompilerParams(dimension_semantics=("arbitrary",) * n, vmem_limit_bytes=VMEM_LIMIT)
    tr = 128
    half = jax.ShapeDtypeStruct((L, D, D_FF_PAD), BF16)
    wa, wb = pl.pallas_call(
        _prep_up_kernel,
        grid=(L, D // tr),
        in_specs=[pl.BlockSpec((None, tr, D_FF), lambda l, i: (l, i, 0)),
                  pl.BlockSpec((None, tr, D_FF), lambda l, i: (l, i, 1))],
        out_specs=[pl.BlockSpec((None, tr, D_FF_PAD), lambda l, i: (l, i, 0))] * 2,
        out_shape=[half, half], compiler_params=cp(2), name="prep_ffn_up",
    )(ffn_w_up, ffn_w_up)
    nr = D_FF // LANES
    wd = pl.pallas_call(
        functools.partial(_prep_down_kernel, n_real=nr),
        grid=(L, D_FF_PAD // LANES),
        in_specs=[pl.BlockSpec((None, LANES, D), lambda l, j: (l, jnp.minimum(j, nr - 1), 0))],
        out_specs=pl.BlockSpec((None, LANES, D), lambda l, j: (l, j, 0)),
        out_shape=jax.ShapeDtypeStruct((L, D_FF_PAD, D), BF16), compiler_params=cp(2), name="prep_ffn_down",
    )(ffn_w_down)
    tr2 = 256
    n_in = w_in.shape[-1]
    w_ext = pl.pallas_call(
        _prep_in_kernel,
        grid=(L, D // tr2),
        in_specs=[pl.BlockSpec((None, tr2, n_in), lambda l, i: (l, i, 0)),
                  pl.BlockSpec((None, 2, tr2, LORA), lambda l, i: (l, 0, i, 0)),
                  pl.BlockSpec((None, 2, tr2, LORA), lambda l, i: (l, 0, i, 0))],
        out_specs=pl.BlockSpec((None, tr2, IN_W_EXT), lambda l, i: (l, i, 0)),
        out_shape=jax.ShapeDtypeStruct((L, D, IN_W_EXT), BF16), compiler_params=cp(2), name="prep_in",
    )(w_in, rwkv_w1, rwkv_a1)
    hb0 = (RWKV_W + HYENA_W) // HEAD_DIM

    def src_rows(i):
        a = i - hb0
        perm = hb0 + (a % ATTN_KV_HEADS) * GQA_GROUP + a // ATTN_KV_HEADS
        return jnp.where((a >= 0) & (a < ATTN_HEADS), perm, i)

    wo = pl.pallas_call(
        _cast_kernel,
        grid=(L, D // HEAD_DIM),
        in_specs=[pl.BlockSpec((None, HEAD_DIM, D), lambda l, i: (l, src_rows(i), 0))],
        out_specs=pl.BlockSpec((None, HEAD_DIM, D), lambda l, i: (l, i, 0)),
        out_shape=jax.ShapeDtypeStruct((L, D, D), BF16), compiler_params=cp(2), name="prep_out",
    )(w_out)
    return w_ext, wo, wa, wb, wd


def kernel(x_prompt, x_sample, c, state_rwkv, cache_k, cache_v, c_ctx, w_mod, b_mod, w_in, rwkv_w0, rwkv_w1, rwkv_w2, rwkv_a0, rwkv_a1, rwkv_a2, rwkv_k_k, rwkv_k_a, rwkv_r_k, rwkv_lnx_g, rwkv_lnx_b, hy_short_w, hy_f_w1, hy_f_b1, hy_f_freq1, hy_f_w2, hy_f_b2, hy_f_freq2, hy_f_w3, hy_bias, attn_sink, w_out, ln1_g, ln1_b, ffn_w_up, ffn_conv_w, ffn_w_down, ln2_g, ln2_b):
    params = dict(rwkv_w0=rwkv_w0, rwkv_w2=rwkv_w2, rwkv_a0=rwkv_a0, rwkv_a2=rwkv_a2,
                  rwkv_k_k=rwkv_k_k, rwkv_k_a=rwkv_k_a, rwkv_r_k=rwkv_r_k,
                  rwkv_lnx_g=rwkv_lnx_g, rwkv_lnx_b=rwkv_lnx_b,
                  hy_short_w=hy_short_w, hy_f_w1=hy_f_w1, hy_f_b1=hy_f_b1, hy_f_freq1=hy_f_freq1,
                  hy_f_w2=hy_f_w2, hy_f_b2=hy_f_b2, hy_f_freq2=hy_f_freq2, hy_f_w3=hy_f_w3,
                  hy_bias=hy_bias)
    L = w_in.shape[0]
    Bc, Sc, D = x_prompt.shape
    Bl, Tl, _ = x_sample.shape
    P = cache_k.shape[2]
    n_ctx = Bc * Sc
    n_lat = Bl * Tl
    N = n_ctx + n_lat
    assert Sc % MOD_BLK == 0 and Tl % MOD_BLK == 0 and n_ctx % Tl == 0

    n_cond = 1 + Bl
    cond = jnp.zeros((16, D), F32).at[0].set(c_ctx).at[1:n_cond].set(c)
    mod_all = _modulation(cond, w_mod, b_mod)
    mod_ctx = jnp.broadcast_to(mod_all[:, 0:1, :], (L, n_ctx // MOD_BLK, 6 * D))
    mod_lat = jnp.broadcast_to(mod_all[:, 1:n_cond, None, :], (L, Bl, Tl // MOD_BLK, 6 * D))
    mod_blk = jnp.concatenate([mod_ctx, mod_lat.reshape(L, n_lat // MOD_BLK, 6 * D)], axis=1)
    mod_blk = mod_blk.reshape(L, N // MOD_BLK, 6, D)

    cm_c, sm_c = (jnp.asarray(m, F32) for m in _dft_mats(Sc))
    cm_l, sm_l = (jnp.asarray(m, F32) for m in _dft_mats(Tl))
    sink_pad = jnp.zeros((L, 1, LANES), F32).at[:, 0, :ATTN_HEADS].set(attn_sink)

    w_ext, wo, wa, wb, wd = _prep_weights(w_in, rwkv_w1, rwkv_a1, w_out, ffn_w_up, ffn_w_down)
    x = jnp.concatenate([x_prompt.reshape(n_ctx, D), x_sample.reshape(n_lat, D)], axis=0)
    states, keys, vals = [], [], []
    for l in range(L):
        p = {n: a[l] for n, a in params.items()}
        proj = _in_projection(x, mod_blk[l], w_ext, l)
        keys.append(proj[:n_ctx, C_AK:C_AK + KV_W].reshape(Bc, Sc, ATTN_KV_HEADS, HEAD_DIM))
        vals.append(proj[:n_ctx, C_AV:C_AV + KV_W].reshape(Bc, Sc, ATTN_KV_HEADS, HEAD_DIM))

        ya_c, s_fin = _rwkv(proj, p, None, T=Sc, n_seq=Bc, row_blk0=0, nsq=RWKV_CTX_SEQS)
        ya_l, _ = _rwkv(proj, p, state_rwkv[:, l], T=Tl, n_seq=Bl, row_blk0=n_ctx // Tl)
        states.append(s_fin)

        yb_c = _hyena(proj, p, _hyena_filter(p, Sc, cm_c, sm_c), cm_c, sm_c, L=Sc, n_seq=Bc, row_blk0=0)
        yb_l = _hyena(proj, p, _hyena_filter(p, Tl, cm_l, sm_l), cm_l, sm_l, L=Tl, n_seq=Bl,
                      row_blk0=n_ctx // Tl)

        yc_c = _ctx_attention(proj, sink_pad[l], S=Sc, n_seq=Bc)
        yc_l = _lat_attention(proj, cache_k[:, l].reshape(Bl, P, KV_W), cache_v[:, l].reshape(Bl, P, KV_W),
                              sink_pad[l], T=Tl, n_seq=Bl, row_blk0=n_ctx // Tl)

        x1, h2 = _out_projection((ya_c, yb_c, yc_c), (ya_l, yb_l, yc_l), x, mod_blk[l],
                                 wo, l, ln1_g[l], ln1_b[l])
        x = _ffn(h2, x1, mod_blk[l], wa, wb, ffn_conv_w[l], wd, l, ln2_g[l], ln2_b[l],
                 n_ctx=n_ctx, s_ctx=Sc, s_lat=Tl)

    y_prompt = x[:n_ctx].reshape(Bc, Sc, D)
    y_sample = x[n_ctx:].reshape(Bl, Tl, D)
    new_state = jnp.stack(states, 1)
    new_k = jnp.stack(keys, 1)
    new_v = jnp.stack(vals, 1)
    return (y_prompt, y_sample, new_state, new_k, new_v)
```

```python
import functools
import math

import numpy as np
import jax
import jax.numpy as jnp
from jax import lax
from jax.experimental import pallas as pl
from jax.experimental.pallas import tpu as pltpu

F32 = jnp.float32
BF16 = jnp.bfloat16

D_MODEL = 2048
DEPTH = 4
GRID_W = 64
HEAD_DIM = 64
RWKV_W = 768
RWKV_HEADS = 12
HYENA_W = 512
ATTN_W = 768
ATTN_HEADS = 12
ATTN_KV_HEADS = 4
GQA_GROUP = 3
KV_W = 256
LORA = 64
GN_EPS = 64e-5
FILTER_EMB = 33
FILTER_HID = 64
HY_MIN_DECAY = math.log(1e-2) / 1.5
HY_MAX_DECAY = math.log(1e-2) / 0.3
WINDOW = 128
BLOCK = 128
ROPE_BASE = 10000.0
D_FF = 5504
LN_EPS = 1e-5
DN_ALPHA = (2 * DEPTH) ** 0.25

C_R, C_K, C_V, C_G = 0, 768, 1536, 2304
C_HY = 3072
C_Q, C_AK, C_AV = 4608, 5376, 5632
C_LORA = 5888
IN_W_EXT = 6144

MOD_BLK = 256
CHUNK = 64
RWKV_CB = 8
RWKV_CTX_SEQS = 4
LANES = 128
D_FF_PAD = 5632
VMEM_LIMIT = 56 * 1024 * 1024


def _dot(a, b):
    return jnp.dot(a.astype(BF16), b.astype(BF16), preferred_element_type=F32)


def _dot_nt(a, b):
    return lax.dot_general(a.astype(BF16), b.astype(BF16), (((1,), (1,)), ((), ())),
                           preferred_element_type=F32)


def _split(x):
    hi = x.astype(BF16)
    lo = (x - hi.astype(F32)).astype(BF16)
    return hi, lo


def _dot3_dims(a, b, dims):
    ah, al = _split(a)
    bh, bl = _split(b)
    dg = functools.partial(lax.dot_general, dimension_numbers=dims, preferred_element_type=F32)
    return dg(ah, bh) + (dg(ah, bl) + dg(al, bh))


_NN = (((1,), (0,)), ((), ()))
_NT = (((1,), (1,)), ((), ()))
_TN = (((0,), (0,)), ((), ()))


def _dot3(a, b):
    return _dot3_dims(a, b, _NN)


def _sigmoid(x):
    return 1.0 / (1.0 + jnp.exp(-x))


def _softplus(x):
    return jnp.maximum(x, 0.0) + jnp.log(1.0 + jnp.exp(-jnp.abs(x)))


def _mod_kernel(c_ref, w_ref, b_ref, o_ref):
    c = c_ref[...]
    s = c * _sigmoid(c)
    o_ref[...] = _dot(s, w_ref[...]) + b_ref[...]


def _modulation(cond, w_mod, b_mod):
    L = w_mod.shape[0]
    R = cond.shape[0]
    tn = 1024
    return pl.pallas_call(
        _mod_kernel,
        grid=(L, 6 * D_MODEL // tn),
        in_specs=[pl.BlockSpec((R, D_MODEL), lambda l, j: (0, 0)),
                  pl.BlockSpec((None, D_MODEL, tn), lambda l, j: (l, 0, j)),
                  pl.BlockSpec((None, 1, tn), lambda l, j: (l, 0, j))],
        out_specs=pl.BlockSpec((None, R, tn), lambda l, j: (l, 0, j)),
        out_shape=jax.ShapeDtypeStruct((L, R, 6 * D_MODEL), F32),
        compiler_params=pltpu.CompilerParams(dimension_semantics=("arbitrary", "arbitrary"),
                                             vmem_limit_bytes=VMEM_LIMIT),
        name="modulation",
    )(cond, w_mod, b_mod.reshape(L, 1, 6 * D_MODEL))


def _two_source_specs(tm, width, nct, nlt, lat_off):
    ctx = pl.BlockSpec((tm, width), lambda i: (jnp.minimum(i, nct - 1), 0))
    lat = pl.BlockSpec((tm, width), lambda i: (jnp.clip(i - nct, 0, nlt - 1) + lat_off, 0))
    return ctx, lat


def _inproj_kernel(xc_ref, xl_ref, mod_ref, w_ref, o_ref, h_scr, *, tn, n_ctx_tiles):
    shift = mod_ref[0, 0:1, :]
    scale = mod_ref[0, 1:2, :]

    def modulate(x_ref):
        h_scr[...] = (x_ref[...] * (1.0 + scale) + shift).astype(BF16)

    pl.when(pl.program_id(0) < n_ctx_tiles)(lambda: modulate(xc_ref))
    pl.when(pl.program_id(0) >= n_ctx_tiles)(lambda: modulate(xl_ref))
    h = h_scr[...]
    for j in range(IN_W_EXT // tn):
        cols = slice(j * tn, (j + 1) * tn)
        o_ref[:, cols] = jnp.dot(h, w_ref[:, cols], preferred_element_type=F32)


def _in_projection(x_src, mod, w_ext, l, tn=512):
    xc, xl, n_ctx, n_lat, lat_off = x_src
    N = n_ctx + n_lat
    tm = MOD_BLK
    nct = n_ctx // tm
    ctx_spec, lat_spec = _two_source_specs(tm, D_MODEL, nct, n_lat // tm, lat_off // tm)
    return pl.pallas_call(
        functools.partial(_inproj_kernel, tn=tn, n_ctx_tiles=nct),
        grid=(N // tm,),
        in_specs=[ctx_spec, lat_spec,
                  pl.BlockSpec((1, 6, D_MODEL), lambda i: (i, 0, 0)),
                  pl.BlockSpec((None, D_MODEL, IN_W_EXT), lambda i: (l, 0, 0), pipeline_mode=pl.Buffered(1))],
        out_specs=pl.BlockSpec((tm, IN_W_EXT), lambda i: (i, 0)),
        out_shape=jax.ShapeDtypeStruct((N, IN_W_EXT), F32),
        scratch_shapes=[pltpu.VMEM((tm, D_MODEL), BF16)],
        compiler_params=pltpu.CompilerParams(dimension_semantics=("arbitrary",),
                                             vmem_limit_bytes=VMEM_LIMIT),
        name="in_projection",
    )(xc, xl, mod, w_ext)


PREC = dict(seg=1, cum=2, a=1, neu=1, post=1, mn=1, b_o=1, b_s=1, gn=2)


def _dotp(a, b, dims, passes):
    dg = functools.partial(lax.dot_general, dimension_numbers=dims, preferred_element_type=F32)
    if passes == 1:
        return dg(a.astype(BF16), b.astype(BF16))
    ah, al = _split(a)
    if passes == 2:
        bh = b.astype(BF16)
        return dg(ah, bh) + dg(al, bh)
    bh, bl = _split(b)
    return dg(ah, bh) + (dg(ah, bl) + dg(al, bh))


def _dotp_rhs(a, b, passes):
    dg = functools.partial(lax.dot_general, dimension_numbers=_NN, preferred_element_type=F32)
    ab = a.astype(BF16)
    if passes == 1:
        return dg(ab, b.astype(BF16))
    bh, bl = _split(b)
    return dg(ab, bh) + dg(ab, bl)


def _rwkv_kernel(*refs, T, nsq, has_init, cb):
    (r_ref, k_ref, v_ref, g_ref, lo_ref, w0_ref, w2_ref, a0_ref, a2_ref,
     kk_ref, ka_ref, rk_ref, lng_ref, lnb_ref) = refs[:14]
    pos = 14
    init_ref = None
    if has_init:
        init_ref = refs[pos]
        pos += 1
    y_ref, sfin_ref = refs[pos:pos + 2]
    rp_scr, ov_scr, m_scr, n_scr, o_scr, bonus_scr, s_scr = refs[pos + 2:]

    nC = T // CHUNK
    nCs = nC // nsq
    C = CHUNK
    H = HEAD_DIM
    P = PREC
    C2 = 2 * C
    row = lax.broadcasted_iota(jnp.int32, (C, C), 0)
    col = lax.broadcasted_iota(jnp.int32, (C, C), 1)
    tri_f = jnp.where(col <= row, 1.0, 0.0).astype(F32)
    lrow = lax.broadcasted_iota(jnp.int32, (LANES, LANES), 0)
    lcol = lax.broadcasted_iota(jnp.int32, (LANES, LANES), 1)
    same_head = (lrow // H) == (lcol // H)
    head_ones = jnp.where(same_head, 1.0, 0.0).astype(F32)
    eye = lrow == lcol
    rt_, ct_ = lrow % C, lcol % C
    strict_m = [same_head & (ct_ < rt_), same_head & (ct_ > rt_)]
    incl_m = [same_head & (ct_ <= rt_), same_head & (ct_ >= rt_)]
    lane = lax.broadcasted_iota(jnp.int32, (1, LANES), 1)
    hmask = [lane < H, lane >= H]

    def stack_heads(x):
        return jnp.concatenate([jnp.where(hmask[0], x, 0.0), jnp.where(hmask[1], x, 0.0)], axis=0)

    kkw = kk_ref[...]
    kaw = ka_ref[...]
    rkw = rk_ref[...]
    zl = jnp.zeros((LORA, LANES), F32)
    lora_w = jnp.concatenate(
        [jnp.concatenate([w if j == i else zl for j in range(4)], axis=1)
         for i, w in enumerate([w2_ref[0], w2_ref[1], a2_ref[0], a2_ref[1]])], axis=0).astype(BF16)

    def phase_a(ci, carry):
        units = []
        for cc in range(cb):
            c = ci * cb + cc
            rows = pl.ds(pl.multiple_of(c * C, C), C)
            r = r_ref[rows, :]
            k = k_ref[rows, :]
            v = v_ref[rows, :]
            lo = lo_ref[rows, :]
            kkraw = k * kkw
            ss = _dotp(kkraw * kkraw, head_ones, _NN, P['seg'])
            kkn = kkraw / jnp.maximum(jnp.sqrt(ss), 1e-12)
            bonus = jnp.zeros((C, LANES), F32)
            vm = stack_heads(v)
            lo_in = jnp.concatenate([jnp.tanh(lo[:, :2 * LORA]), lo[:, 2 * LORA:]], axis=1)
            lo_out = _dot(lo_in, lora_w)
            lds, alphas = [], []
            for d in range(2):
                wl = w0_ref[d] + lo_out[:, d * LANES:(d + 1) * LANES]
                lds.append(-jnp.exp(-_softplus(-wl) - 0.5))
                alphas.append(_sigmoid(a0_ref[d] + lo_out[:, (2 + d) * LANES:(3 + d) * LANES]))
            pref = _dotp_rhs(tri_f, jnp.concatenate(lds, axis=1), P['cum'])
            for d in range(2):
                ld, alpha = lds[d], alphas[d]
                kd = k * (1.0 + (alpha - 1.0) * kaw)
                bb = kkn * alpha
                bonus = bonus + _dotp(r * kd * rkw, head_ones, _NN, P['seg']) * v
                pf = pref[:, d * LANES:(d + 1) * LANES]
                g_incl = pf if d == 0 else pf[C - 1:C, :] - pf + ld
                g_excl = g_incl - ld
                g_end = g_incl[C - 1:C, :] if d == 0 else g_incl[0:1, :]
                ginv = jnp.exp(-g_incl)
                dec_end = jnp.exp(g_end - g_incl)
                units.append(dict(
                    d=d, c=c, vm=vm,
                    at=stack_heads(-kkn * jnp.exp(g_excl)), rt=stack_heads(r * jnp.exp(g_incl)),
                    bt=stack_heads(bb * ginv), kt=stack_heads(kd * ginv),
                    bend=stack_heads(bb * dec_end), kend=stack_heads(kd * dec_end), gend=jnp.exp(g_end)))
            bonus_scr[rows, :] = bonus
            o_scr[rows, :] = jnp.zeros((C, LANES), F32)

        for u in units:
            ar = jnp.concatenate([u['at'], u['rt']], axis=0)
            bk = jnp.concatenate([u['bt'], u['kt']], axis=0)
            u['a'] = _dotp(ar, bk, _NT, P['a'])
        for u in units:
            a = u['a']
            st, ic = strict_m[u['d']], incl_m[u['d']]
            u['p'] = jnp.where(st, a[:C2, :C2], 0.0)
            u['a_ak'] = jnp.where(st, a[:C2, C2:], 0.0)
            u['a_rb'] = jnp.where(ic, a[C2:, :C2], 0.0)
            u['a_rk'] = jnp.where(ic, a[C2:, C2:], 0.0)
        rvs = [_dotp(u['a_ak'], u['vm'], _NN, P['post']) for u in units]
        sq = [_dotp(u['p'], u['p'], _NN, P['neu']) for u in units]
        for i, u in enumerate(units):
            u['x'] = jnp.where(eye, 1.0, 0.0) + u['p']
            u['p'] = sq[i]
        for it in range(1, 6):
            res = [_dotp(u['p'], jnp.concatenate([u['x'], u['p']], axis=1) if it < 5 else u['x'], _NN, P['neu'])
                   for u in units]
            for i, u in enumerate(units):
                u['x'] = u['x'] + res[i][:, :C2]
                if it < 5:
                    u['p'] = res[i][:, C2:]
        rhs = [_dotp(u['x'], jnp.concatenate([u['at'], rvs[i]], axis=1), _NN, P['post'])
               for i, u in enumerate(units)]
        for i, u in enumerate(units):
            u['rhs'] = rhs[i]
        xs = [_dotp(jnp.concatenate([u['a_rb'], u['a_rk']], axis=1),
                    jnp.concatenate([u['rhs'], jnp.concatenate([jnp.zeros_like(u['vm']), u['vm']], axis=1)], axis=0),
                    _NN, P['post']) for u in units]
        ms = [_dotp(u['rhs'][:, :LANES], u['bend'], _TN, P['mn']) for u in units]
        ns = [_dotp(jnp.concatenate([u['rhs'][:, LANES:], u['vm']], axis=0),
                    jnp.concatenate([u['bend'], u['kend']], axis=0), _TN, P['mn']) for u in units]
        for i, u in enumerate(units):
            d, c = u['d'], u['c']
            rp = u['rt'] + xs[i][:, :LANES]
            ov = xs[i][:, LANES:]
            rp_scr[d, c] = rp[:C] + rp[C:]
            ov_scr[d, c] = ov[:C] + ov[C:]
            m_scr[d, c] = jnp.where(eye, jnp.broadcast_to(u['gend'], (LANES, LANES)), 0.0) + ms[i]
            n_scr[d, c] = ns[i]
        return carry

    lax.fori_loop(0, nC // cb, phase_a, 0)

    zero_blk = jnp.zeros((H, H), F32)
    for q in range(nsq):
        for d in range(2):
            if has_init:
                s_scr[q, d] = jnp.concatenate([jnp.concatenate([init_ref[q, d, 0], zero_blk], axis=1),
                                               jnp.concatenate([zero_blk, init_ref[q, d, 1]], axis=1)], axis=0)
            else:
                s_scr[q, d] = jnp.zeros((LANES, LANES), F32)

    def phase_b(i, carry):
        idx = [(q, d, q * nCs + (i if d == 0 else nCs - 1 - i)) for q in range(nsq) for d in range(2)]
        ss = [s_scr[q, d] for q, d, c in idx]
        os_ = [_dotp(rp_scr[d, c], ss[j], _NT, P['b_o']) for j, (q, d, c) in enumerate(idx)]
        sn = [_dotp(ss[j], m_scr[d, c], _NN, P['b_s']) for j, (q, d, c) in enumerate(idx)]
        for j, (q, d, c) in enumerate(idx):
            s_scr[q, d] = sn[j] + n_scr[d, c]
            rows = pl.ds(c * C if isinstance(c, int) else pl.multiple_of(c * C, C), C)
            o_scr[rows, :] = o_scr[rows, :] + (os_[j] + ov_scr[d, c])
        return carry

    if nCs <= 4:
        for i in range(nCs):
            phase_b(i, 0)
    else:
        lax.fori_loop(0, nCs, phase_b, 0, unroll=4)

    for q in range(nsq):
        for d in range(2):
            s = s_scr[q, d]
            sfin_ref[q, d, 0] = s[:H, :H]
            sfin_ref[q, d, 1] = s[H:, H:]

    lng = lng_ref[...]
    lnb = lnb_ref[...]

    rb = min(T, 256)

    def phase_c(c, carry):
        rows = pl.ds(pl.multiple_of(c * rb, rb), rb)
        o = o_scr[rows, :]
        st = _dotp(jnp.concatenate([o, o * o], axis=0), head_ones, _NN, P['gn']) * (1.0 / H)
        mu = st[:rb]
        var = st[rb:] - mu * mu
        on = (o - mu) * lax.rsqrt(var + GN_EPS) * lng + lnb
        y_ref[rows, :] = (on + bonus_scr[rows, :]) * _sigmoid(g_ref[rows, :])
        return carry

    lax.fori_loop(0, T // rb, phase_c, 0)


def _rwkv(proj, p, init, *, T, n_seq, row_blk0, nsq=1):
    seq_len, T = T, nsq * T
    nC = T // CHUNK
    npair = RWKV_HEADS // 2
    has_init = init is not None
    cb = lambda off: (lambda b, j: (b + row_blk0, off // LANES + j))
    vec = lambda b, j: (0, j)
    in_specs = [pl.BlockSpec((T, LANES), cb(C_R)),
                pl.BlockSpec((T, LANES), cb(C_K)),
                pl.BlockSpec((T, LANES), cb(C_V)),
                pl.BlockSpec((T, LANES), cb(C_G)),
                pl.BlockSpec((T, 4 * LORA), lambda b, j: (b + row_blk0, C_LORA // (4 * LORA))),
                pl.BlockSpec((2, 1, LANES), lambda b, j: (0, 0, j)),
                pl.BlockSpec((2, LORA, LANES), lambda b, j: (0, 0, j)),
                pl.BlockSpec((2, 1, LANES), lambda b, j: (0, 0, j)),
                pl.BlockSpec((2, LORA, LANES), lambda b, j: (0, 0, j)),
                pl.BlockSpec((1, LANES), vec), pl.BlockSpec((1, LANES), vec),
                pl.BlockSpec((1, LANES), vec), pl.BlockSpec((1, LANES), vec),
                pl.BlockSpec((1, LANES), vec)]
    args = [proj, proj, proj, proj, proj,
            p['rwkv_w0'].reshape(2, 1, RWKV_W), p['rwkv_w2'], p['rwkv_a0'].reshape(2, 1, RWKV_W), p['rwkv_a2'],
            p['rwkv_k_k'].reshape(1, RWKV_W), p['rwkv_k_a'].reshape(1, RWKV_W),
            p['rwkv_r_k'].reshape(1, RWKV_W), p['rwkv_lnx_g'].reshape(1, RWKV_W),
            p['rwkv_lnx_b'].reshape(1, RWKV_W)]
    st_spec = pl.BlockSpec((nsq, 2, 2, HEAD_DIM, HEAD_DIM), lambda b, j: (b, 0, j, 0, 0))
    if has_init:
        in_specs.append(st_spec)
        args.append(init)
    sc_o = lambda: pltpu.VMEM((2, nC, CHUNK, LANES), F32)
    sc_s = lambda: pltpu.VMEM((2, nC, LANES, LANES), F32)
    return pl.pallas_call(
        functools.partial(_rwkv_kernel, T=T, nsq=nsq, has_init=has_init, cb=min(RWKV_CB, nC)),
        grid=(n_seq // nsq, npair),
        in_specs=in_specs,
        out_specs=[pl.BlockSpec((T, LANES), lambda b, j: (b, j)), st_spec],
        out_shape=[jax.ShapeDtypeStruct((n_seq * seq_len, RWKV_W), F32),
                   jax.ShapeDtypeStruct((n_seq, 2, RWKV_HEADS, HEAD_DIM, HEAD_DIM), F32)],
        scratch_shapes=[sc_o(), sc_o(), sc_s(), sc_s(),
                        pltpu.VMEM((T, LANES), F32), pltpu.VMEM((T, LANES), F32),
                        pltpu.VMEM((nsq, 2, LANES, LANES), F32)],
        compiler_params=pltpu.CompilerParams(dimension_semantics=("arbitrary", "arbitrary"),
                                             vmem_limit_bytes=VMEM_LIMIT),
        name="rwkv_T%d" % T,
    )(*args)


def _dft_mats(L):
    f = np.arange(L, dtype=np.int64)
    ph = (np.outer(f, f) % (2 * L)).astype(np.float64) * (math.pi / L)
    return np.cos(ph), -np.sin(ph)


def _filter_features(L):
    t = np.linspace(0.0, 1.0, L, dtype=np.float32)[:, None]
    bands = (FILTER_EMB - 1) // 2
    t_res = np.arange(L, dtype=np.float32)[:, None]
    f = np.linspace(1e-4, bands - 1, bands, dtype=np.float32)[None, :]
    ang = (2.0 * math.pi * t_res * f / L).astype(np.float32)
    z = np.concatenate([t, np.cos(ang), np.sin(ang)], -1).astype(np.float32)
    zp = np.zeros((L, FILTER_HID), np.float32)
    zp[:, :FILTER_EMB] = z
    deltas = np.abs(np.linspace(HY_MIN_DECAY, HY_MAX_DECAY, HYENA_W, dtype=np.float32))
    return zp, t.astype(np.float32), deltas[None, :].astype(np.float32)


def _hyfilter_kernel(z_ref, t_ref, dl_ref, w1_ref, b1_ref, q1_ref, w2_ref, b2_ref, q2_ref, wc_ref, wa_ref,
                     hs_ref, hd_ref, knyq_ref, hm_scr, *, L):
    @pl.when((pl.program_id(0) == 0) & (pl.program_id(1) == 0))
    def _():
        h1 = jnp.sin(q1_ref[...] * (_dot3(z_ref[...], w1_ref[...]) + b1_ref[...]))
        hm_scr[...] = jnp.sin(q2_ref[...] * (_dot3(h1, w2_ref[...]) + b2_ref[...]))

    hm = hm_scr[...]
    win = jnp.exp(-t_ref[...] * dl_ref[...])
    lag = lax.broadcasted_iota(jnp.int32, (L, 1), 0)
    alt = jnp.where(lag % 2 == 0, 1.0, -1.0).astype(F32)
    hc = _dot3(hm, wc_ref[...]) * win
    ha = jnp.where(lag == 0, 0.0, _dot3(hm, wa_ref[...]) * win)
    hs_ref[...] = hc + ha
    hd_ref[...] = hc - ha
    knyq_ref[...] = jnp.sum((hc + ha) * alt, axis=0, keepdims=True) * (1.0 / (2 * L))


def _hyspec_kernel(cm_ref, sm_ref, hs_ref, hd_ref, kre_ref, kim_ref, *, L, tf):
    f = pl.program_id(2) * tf + lax.broadcasted_iota(jnp.int32, (tf, 1), 0)
    wf = jnp.where(f == 0, 1.0, 2.0).astype(F32) * (1.0 / (2 * L))
    kre_ref[...] = _dot3(cm_ref[...], hs_ref[...]) * wf
    kim_ref[...] = _dot3(sm_ref[...], hd_ref[...]) * wf


def _hyena_filter(p, L, cm, sm, cw=128):
    zp, t, dl = _filter_features(L)
    w1 = jnp.zeros((FILTER_HID, FILTER_HID), F32).at[:FILTER_EMB].set(p['hy_f_w1'])
    r2 = lambda a: a.reshape(1, -1)
    ncb = HYENA_W // cw
    full = lambda shape: pl.BlockSpec(shape, lambda n, j: (0,) * len(shape))
    shp = jax.ShapeDtypeStruct((2, L, HYENA_W), F32)
    hs, hd, knyq = pl.pallas_call(
        functools.partial(_hyfilter_kernel, L=L),
        grid=(2, ncb),
        in_specs=[full((L, FILTER_HID)), full((L, 1)),
                  pl.BlockSpec((1, cw), lambda n, j: (0, j)),
                  full((FILTER_HID, FILTER_HID)), full((1, FILTER_HID)), full((1, FILTER_HID)),
                  full((FILTER_HID, FILTER_HID)), full((1, FILTER_HID)), full((1, FILTER_HID)),
                  pl.BlockSpec((FILTER_HID, cw), lambda n, j: (0, n * 2 * ncb + j)),
                  pl.BlockSpec((FILTER_HID, cw), lambda n, j: (0, n * 2 * ncb + ncb + j))],
        out_specs=[pl.BlockSpec((None, L, cw), lambda n, j: (n, 0, j)),
                   pl.BlockSpec((None, L, cw), lambda n, j: (n, 0, j)),
                   pl.BlockSpec((None, 1, cw), lambda n, j: (n, 0, j))],
        out_shape=[shp, shp, jax.ShapeDtypeStruct((2, 1, HYENA_W), F32)],
        scratch_shapes=[pltpu.VMEM((L, FILTER_HID), F32)],
        compiler_params=pltpu.CompilerParams(dimension_semantics=("arbitrary", "arbitrary"),
                                             vmem_limit_bytes=VMEM_LIMIT),
        name="hyena_filter_L%d" % L,
    )(jnp.asarray(zp), jnp.asarray(t), jnp.asarray(dl), w1, r2(p['hy_f_b1']), r2(p['hy_f_freq1']),
      p['hy_f_w2'], r2(p['hy_f_b2']), r2(p['hy_f_freq2']), p['hy_f_w3'], p['hy_f_w3'])
    tf = min(256, L)
    sw = 256
    kre, kim = pl.pallas_call(
        functools.partial(_hyspec_kernel, L=L, tf=tf),
        grid=(2, HYENA_W // sw, L // tf),
        in_specs=[pl.BlockSpec((tf, L), lambda n, j, i: (i, 0)),
                  pl.BlockSpec((tf, L), lambda n, j, i: (i, 0)),
                  pl.BlockSpec((None, L, sw), lambda n, j, i: (n, 0, j)),
                  pl.BlockSpec((None, L, sw), lambda n, j, i: (n, 0, j))],
        out_specs=[pl.BlockSpec((None, tf, sw), lambda n, j, i: (n, i, j)),
                   pl.BlockSpec((None, tf, sw), lambda n, j, i: (n, i, j))],
        out_shape=[shp, shp],
        compiler_params=pltpu.CompilerParams(dimension_semantics=("arbitrary",) * 3,
                                             vmem_limit_bytes=VMEM_LIMIT),
        name="hyena_spectrum_L%d" % L,
    )(cm, sm, hs, hd)
    return kre, kim, knyq


def _hyena_kernel(v_ref, x1_ref, x2_ref, sw_ref, bias_ref, cm_ref, sm_ref, kre_ref, kim_ref, knyq_ref,
                  o_ref, zb_scr, zf_scr, xs_scr, yre_scr, yim_scr, *, L, rb):
    nrb = L // rb
    t = lax.broadcasted_iota(jnp.int32, (L, 1), 0)
    first = t == 0
    last = t == L - 1
    alt_all = jnp.where(t % 2 == 0, 1.0, -1.0).astype(F32)
    tb = lax.broadcasted_iota(jnp.int32, (rb, 1), 0)
    alt_blk = jnp.where(tb % 2 == 0, 1.0, -1.0).astype(F32)
    sw = sw_ref[...]

    def short(ref, s):
        u = ref[...]
        prev = jnp.where(first, 0.0, pltpu.roll(u, 1, 0))
        nxt = jnp.where(last, 0.0, pltpu.roll(u, L - 1, 0))
        return prev * sw[0, s][None, :] + u * sw[1, s][None, :] + nxt * sw[2, s][None, :]

    z0 = short(v_ref, 0)
    zf_scr[...] = z0
    zb_scr[...] = z0.astype(BF16)
    unyq = jnp.sum(z0 * alt_all, axis=0, keepdims=True)

    for n in range(2):
        xs_scr[...] = short(x1_ref if n == 0 else x2_ref, n + 1)

        def fwd(fb, carry):
            rows = pl.ds(pl.multiple_of(fb * rb, rb), rb)
            zb = zb_scr[...]
            ure = jnp.dot(cm_ref[rows, :], zb, preferred_element_type=F32)
            uim = jnp.dot(sm_ref[rows, :], zb, preferred_element_type=F32)
            kre = kre_ref[n, rows, :]
            kim = kim_ref[n, rows, :]
            yre_scr[rows, :] = (ure * kre - uim * kim).astype(BF16)
            yim_scr[rows, :] = (ure * kim + uim * kre).astype(BF16)
            return carry

        lax.fori_loop(0, nrb, fwd, 0)
        ynyq = unyq * knyq_ref[n]
        bias = bias_ref[n]

        def inv(ib, acc):
            rows = pl.ds(pl.multiple_of(ib * rb, rb), rb)
            y = (jnp.dot(cm_ref[rows, :], yre_scr[...], preferred_element_type=F32)
                 + jnp.dot(sm_ref[rows, :], yim_scr[...], preferred_element_type=F32))
            y = y + alt_blk * ynyq + zf_scr[rows, :] * bias
            znew = xs_scr[rows, :] * y
            if n == 0:
                zf_scr[rows, :] = znew
                zb_scr[rows, :] = znew.astype(BF16)
                acc = acc + jnp.sum(znew * alt_blk, axis=0, keepdims=True)
            else:
                o_ref[rows, :] = znew
            return acc

        unyq = lax.fori_loop(0, nrb, inv, jnp.zeros_like(unyq))


def _hyena(proj, p, filt, cm, sm, *, L, n_seq, row_blk0, cw=256):
    kre, kim, knyq = filt
    ncb = HYENA_W // cw
    cb = lambda s: (lambda j, b: (b + row_blk0, (C_HY + s * HYENA_W) // cw + j))
    once = pl.Buffered(1)
    sw = p['hy_short_w'].reshape(3, 3, HYENA_W)
    rb = min(256, L)
    return pl.pallas_call(
        functools.partial(_hyena_kernel, L=L, rb=rb),
        grid=(ncb, n_seq),
        in_specs=[pl.BlockSpec((L, cw), cb(0)), pl.BlockSpec((L, cw), cb(1)), pl.BlockSpec((L, cw), cb(2)),
                  pl.BlockSpec((3, 3, cw), lambda j, b: (0, 0, j)),
                  pl.BlockSpec((2, 1, cw), lambda j, b: (0, 0, j)),
                  pl.BlockSpec((L, L), lambda j, b: (0, 0), pipeline_mode=once),
                  pl.BlockSpec((L, L), lambda j, b: (0, 0), pipeline_mode=once),
                  pl.BlockSpec((2, L, cw), lambda j, b: (0, 0, j), pipeline_mode=once),
                  pl.BlockSpec((2, L, cw), lambda j, b: (0, 0, j), pipeline_mode=once),
                  pl.BlockSpec((2, 1, cw), lambda j, b: (0, 0, j))],
        out_specs=pl.BlockSpec((L, cw), lambda j, b: (b, j)),
        out_shape=jax.ShapeDtypeStruct((n_seq * L, HYENA_W), F32),
        scratch_shapes=[pltpu.VMEM((L, cw), BF16), pltpu.VMEM((L, cw), F32), pltpu.VMEM((L, cw), F32),
                        pltpu.VMEM((L, cw), BF16), pltpu.VMEM((L, cw), BF16)],
        compiler_params=pltpu.CompilerParams(dimension_semantics=("arbitrary", "arbitrary"),
                                             vmem_limit_bytes=VMEM_LIMIT),
        name="hyena_L%d" % L,
    )(proj, proj, proj, sw, p['hy_bias'].reshape(2, 1, HYENA_W), cm.astype(BF16), sm.astype(BF16),
      kre, kim, knyq)


def _row_max(x):
    t = x[:, :LANES]
    for c in range(1, x.shape[1] // LANES):
        t = jnp.maximum(t, x[:, c * LANES:(c + 1) * LANES])
    return jnp.max(t, axis=-1, keepdims=True)


def _row_sum(x):
    t = x[:, :LANES]
    for c in range(1, x.shape[1] // LANES):
        t = t + x[:, c * LANES:(c + 1) * LANES]
    return jnp.sum(t, axis=-1, keepdims=True)


def _head_masks():
    lane = lax.broadcasted_iota(jnp.int32, (1, KV_W), 1) // HEAD_DIM
    return [lane == h for h in range(ATTN_KV_HEADS)]


def _attend_group(qg, keys, vals, masks, sinks, hm):
    nh = len(hm)
    qm = [jnp.where(hm[h], qg, 0.0).astype(BF16) for h in range(nh)]
    s = [[_dot_nt(qm[h], k) for k in keys] for h in range(nh)]
    s = [[sc if mk is None else jnp.where(mk, sc, -jnp.inf) for sc, mk in zip(s[h], masks)] for h in range(nh)]
    m = [jnp.maximum(_row_max(jnp.concatenate(s[h], axis=1)), sinks[h]) for h in range(nh)]
    p = [[jnp.exp(sc - m[h]) for sc in s[h]] for h in range(nh)]
    den = [_row_sum(jnp.concatenate(p[h], axis=1)) + jnp.exp(sinks[h] - m[h]) for h in range(nh)]
    pv = [[_dot(pc, v) for pc, v in zip(p[h], vals)] for h in range(nh)]
    acc = jnp.zeros(qg.shape, F32)
    for h in range(nh):
        tot = pv[h][0]
        for t in pv[h][1:]:
            tot = tot + t
        acc = jnp.where(hm[h], tot / den[h], acc)
    return acc


def _ctx_attn_kernel(q_ref, k_ref, v_ref, sink_ref, o_ref, *, S):
    scale = HEAD_DIM ** -0.5
    hm = _head_masks()
    k = k_ref[...].astype(BF16)
    v = v_ref[...].astype(BF16)
    for g in range(GQA_GROUP):
        qg = q_ref[:, g * KV_W:(g + 1) * KV_W] * scale
        sinks = [jnp.broadcast_to(sink_ref[0:1, h * GQA_GROUP + g:h * GQA_GROUP + g + 1], (S, 1))
                 for h in range(ATTN_KV_HEADS)]
        o_ref[:, g * KV_W:(g + 1) * KV_W] = _attend_group(qg, [k], [v], [None], sinks, hm)


def _ctx_attention(proj, sink, *, S, n_seq):
    return pl.pallas_call(
        functools.partial(_ctx_attn_kernel, S=S),
        grid=(n_seq,),
        in_specs=[pl.BlockSpec((S, ATTN_W), lambda b: (b, C_Q // ATTN_W)),
                  pl.BlockSpec((S, KV_W), lambda b: (b, C_AK // KV_W)),
                  pl.BlockSpec((S, KV_W), lambda b: (b, C_AV // KV_W)),
                  pl.BlockSpec((1, LANES), lambda b: (0, 0))],
        out_specs=pl.BlockSpec((S, ATTN_W), lambda b: (b, 0)),
        out_shape=jax.ShapeDtypeStruct((n_seq * S, ATTN_W), F32),
        compiler_params=pltpu.CompilerParams(dimension_semantics=("arbitrary",),
                                             vmem_limit_bytes=VMEM_LIMIT),
        name="ctx_attention",
    )(proj, proj, proj, sink)


def _rope(x, cos, sin_signed, first_half):
    n = x.shape[-1]
    partner = jnp.where(first_half, pltpu.roll(x, n - HEAD_DIM // 2, 1), pltpu.roll(x, HEAD_DIM // 2, 1))
    return x * cos + partner * sin_signed


def _lat_attn_kernel(q_ref, kp_ref, kc_ref, kn_ref, vp_ref, vc_ref, vn_ref, ck_ref, cv_ref,
                     cq_ref, sq_ref, cp_ref, sp_ref, cc_ref, sc_ref, cn_ref, sn_ref, sink_ref, o_ref, *, T):
    i = pl.program_id(1)
    scale = HEAD_DIM ** -0.5
    hm = _head_masks()
    fh_q = (lax.broadcasted_iota(jnp.int32, (1, ATTN_W), 1) % HEAD_DIM) < HEAD_DIM // 2
    fh_k = (lax.broadcasted_iota(jnp.int32, (1, KV_W), 1) % HEAD_DIM) < HEAD_DIM // 2
    q = _rope(q_ref[...], cq_ref[...], sq_ref[...], fh_q) * scale
    kl = jnp.concatenate([_rope(kp_ref[...], cp_ref[...], sp_ref[...], fh_k),
                          _rope(kc_ref[...], cc_ref[...], sc_ref[...], fh_k),
                          _rope(kn_ref[...], cn_ref[...], sn_ref[...], fh_k)], axis=0).astype(BF16)
    vl = jnp.concatenate([vp_ref[...], vc_ref[...], vn_ref[...]], axis=0).astype(BF16)
    kc = ck_ref[...].astype(BF16)
    vc = cv_ref[...].astype(BF16)
    qpos = i * BLOCK + lax.broadcasted_iota(jnp.int32, (BLOCK, 3 * BLOCK), 0)
    kpos = (i - 1) * BLOCK + lax.broadcasted_iota(jnp.int32, (BLOCK, 3 * BLOCK), 1)
    ok = (jnp.abs(qpos - kpos) <= WINDOW) & (kpos >= 0) & (kpos < T)
    for g in range(GQA_GROUP):
        qg = q[:, g * KV_W:(g + 1) * KV_W]
        sinks = [jnp.broadcast_to(sink_ref[0:1, h * GQA_GROUP + g:h * GQA_GROUP + g + 1], (BLOCK, 1))
                 for h in range(ATTN_KV_HEADS)]
        o_ref[:, g * KV_W:(g + 1) * KV_W] = _attend_group(qg, [kl, kc], [vl, vc], [ok, None], sinks, hm)


def _rope_tables(T):
    rows = T // GRID_W
    row = np.repeat(np.arange(rows), GRID_W).astype(np.float32)
    col = np.tile(np.arange(GRID_W), rows).astype(np.float32)
    nf = HEAD_DIM // 4
    inv = (ROPE_BASE ** (-np.arange(nf, dtype=np.float32) / nf)).astype(np.float32)
    ang = np.concatenate([row[:, None] * inv, col[:, None] * inv], -1).astype(np.float32)
    cos = np.cos(ang).astype(np.float32)
    sin = np.sin(ang).astype(np.float32)
    cos_h = np.concatenate([cos, cos], -1)
    sin_h = np.concatenate([-sin, sin], -1)
    return cos_h, sin_h


def _lat_attention(proj, cache_k, cache_v, sink, *, T, n_seq, row_blk0):
    nb = T // BLOCK
    P = cache_k.shape[1]
    cos_h, sin_h = _rope_tables(T)
    cq = jnp.asarray(np.tile(cos_h, (1, ATTN_HEADS)))
    sq = jnp.asarray(np.tile(sin_h, (1, ATTN_HEADS)))
    ck = jnp.asarray(np.tile(cos_h, (1, ATTN_KV_HEADS)))
    sk = jnp.asarray(np.tile(sin_h, (1, ATTN_KV_HEADS)))
    r0 = row_blk0 * nb
    prv = lambda i: jnp.maximum(i - 1, 0)
    nxt = lambda i: jnp.minimum(i + 1, nb - 1)
    kcol, vcol = C_AK // KV_W, C_AV // KV_W
    kv = lambda col, f: pl.BlockSpec((BLOCK, KV_W), lambda b, i: (r0 + b * nb + f(i), col))
    tab = lambda w, f: pl.BlockSpec((BLOCK, w), lambda b, i: (f(i), 0))
    cur = lambda i: i
    return pl.pallas_call(
        functools.partial(_lat_attn_kernel, T=T),
        grid=(n_seq, nb),
        in_specs=[pl.BlockSpec((BLOCK, ATTN_W), lambda b, i: (r0 + b * nb + i, C_Q // ATTN_W)),
                  kv(kcol, prv), kv(kcol, cur), kv(kcol, nxt),
                  kv(vcol, prv), kv(vcol, cur), kv(vcol, nxt),
                  pl.BlockSpec((None, P, KV_W), lambda b, i: (b, 0, 0)),
                  pl.BlockSpec((None, P, KV_W), lambda b, i: (b, 0, 0)),
                  tab(ATTN_W, cur), tab(ATTN_W, cur),
                  tab(KV_W, prv), tab(KV_W, prv), tab(KV_W, cur), tab(KV_W, cur), tab(KV_W, nxt), tab(KV_W, nxt),
                  pl.BlockSpec((1, LANES), lambda b, i: (0, 0))],
        out_specs=pl.BlockSpec((BLOCK, ATTN_W), lambda b, i: (b * nb + i, 0)),
        out_shape=jax.ShapeDtypeStruct((n_seq * T, ATTN_W), F32),
        compiler_params=pltpu.CompilerParams(dimension_semantics=("arbitrary", "arbitrary"),
                                             vmem_limit_bytes=VMEM_LIMIT),
        name="latent_attention",
    )(proj, proj, proj, proj, proj, proj, proj, cache_k, cache_v, cq, sq, ck, sk, ck, sk, ck, sk, sink)


def _layer_norm(y, g, b):
    mu = jnp.mean(y, axis=-1, keepdims=True)
    d = y - mu
    var = jnp.mean(d * d, axis=-1, keepdims=True)
    return d * lax.rsqrt(var + LN_EPS) * g + b


def _outproj_kernel(ac_ref, bc_ref, cc_ref, al_ref, bl_ref, cl_ref, xc_ref, xl_ref, mod_ref, w_ref,
                    g_ref, be_ref, x1_ref, h2_ref, *, tm, n_ctx_tiles):
    r1, r2 = RWKV_W, RWKV_W + HYENA_W
    sub = 128

    def tile(a_ref, b_ref, c_ref, x_ref):
        ms = []
        for s in range(tm // sub):
            rows = slice(s * sub, (s + 1) * sub)
            ms.append(jnp.dot(a_ref[rows, :].astype(BF16), w_ref[0:r1, :], preferred_element_type=F32)
                      + jnp.dot(b_ref[rows, :].astype(BF16), w_ref[r1:r2, :], preferred_element_type=F32)
                      + jnp.dot(c_ref[rows, :].astype(BF16), w_ref[r2:, :], preferred_element_type=F32))
        for s in range(tm // sub):
            rows = slice(s * sub, (s + 1) * sub)
            blk = (s * sub) // MOD_BLK
            gate1 = mod_ref[blk, 2:3, :]
            shift2 = mod_ref[blk, 3:4, :]
            scale2 = mod_ref[blk, 4:5, :]
            x1 = _layer_norm(DN_ALPHA * x_ref[rows, :] + gate1 * ms[s], g_ref[...], be_ref[...])
            x1_ref[rows, :] = x1
            h2_ref[rows, :] = (x1 * (1.0 + scale2) + shift2).astype(BF16)

    pl.when(pl.program_id(0) < n_ctx_tiles)(lambda: tile(ac_ref, bc_ref, cc_ref, xc_ref))
    pl.when(pl.program_id(0) >= n_ctx_tiles)(lambda: tile(al_ref, bl_ref, cl_ref, xl_ref))


def _out_projection(mix_ctx, mix_lat, x_src, mod, w_out, l, g, b, tm=256):
    xc, xl, n_ctx, n_lat, lat_off = x_src
    N = n_ctx + n_lat
    nsub = tm // MOD_BLK
    nct = n_ctx // tm
    nlt = n_lat // tm
    row = lambda w: pl.BlockSpec((tm, w), lambda i: (i, 0))
    ctx = lambda w: _two_source_specs(tm, w, nct, nlt, 0)[0]
    lat = lambda w: _two_source_specs(tm, w, nct, nlt, 0)[1]
    xc_spec, xl_spec = _two_source_specs(tm, D_MODEL, nct, nlt, lat_off // tm)
    full = lambda r: pl.BlockSpec((r, D_MODEL), lambda i: (0, 0))
    return pl.pallas_call(
        functools.partial(_outproj_kernel, tm=tm, n_ctx_tiles=nct),
        grid=(N // tm,),
        in_specs=[ctx(RWKV_W), ctx(HYENA_W), ctx(ATTN_W), lat(RWKV_W), lat(HYENA_W), lat(ATTN_W),
                  xc_spec, xl_spec,
                  pl.BlockSpec((nsub, 6, D_MODEL), lambda i: (i, 0, 0)),
                  pl.BlockSpec((None, D_MODEL, D_MODEL), lambda i: (l, 0, 0)), full(1), full(1)],
        out_specs=[row(D_MODEL), row(D_MODEL)],
        out_shape=[jax.ShapeDtypeStruct((N, D_MODEL), F32), jax.ShapeDtypeStruct((N, D_MODEL), BF16)],
        compiler_params=pltpu.CompilerParams(dimension_semantics=("arbitrary",),
                                             vmem_limit_bytes=VMEM_LIMIT),
        name="out_projection",
    )(*mix_ctx, *mix_lat, xc, xl, mod, w_out, g.reshape(1, D_MODEL), b.reshape(1, D_MODEL))


HALO = 16


def _ffn_kernel(hp_ref, h_ref, hn_ref, x1_ref, mod_ref, wa_ref, wb_ref, cwa_ref, cwb_ref, wd_ref, g_ref, be_ref,
                o_ref, hx_scr, ua_scr, ub_scr, acc_scr, *, tm, tf, nsub, t0, n_ctx, s_ctx, s_lat):
    i = pl.program_id(0) + t0
    j = pl.program_id(1)

    @pl.when(j == 0)
    def _():
        hx_scr[0:HALO, :] = hp_ref[...]
        hx_scr[HALO:HALO + tm, :] = h_ref[...]
        hx_scr[HALO + tm:, :] = hn_ref[...]
        acc_scr[...] = jnp.zeros_like(acc_scr)

    hx = hx_scr[...]
    ua_scr[...] = jnp.dot(hx, wa_ref[...], preferred_element_type=F32)
    ub_scr[...] = jnp.dot(hx, wb_ref[...], preferred_element_type=F32)
    grow = i * tm + lax.broadcasted_iota(jnp.int32, (tm, 1), 0)
    pos = jnp.where(grow < n_ctx, grow % s_ctx, (grow - n_ctx) % s_lat)
    slen = jnp.where(grow < n_ctx, s_ctx, s_lat)
    mprev = jnp.where(pos == 0, 0.0, 1.0).astype(F32)
    mnext = jnp.where(pos == slen - 1, 0.0, 1.0).astype(F32)

    def conv(u_scr, cw_ref):
        up = u_scr[HALO - 1:HALO - 1 + tm, :] * mprev
        uc = u_scr[HALO:HALO + tm, :]
        un = u_scr[HALO + 1:HALO + 1 + tm, :] * mnext
        return up * cw_ref[0:1, :] + uc * cw_ref[1:2, :] + un * cw_ref[2:3, :]

    a = conv(ua_scr, cwa_ref)
    b = conv(ub_scr, cwb_ref)
    act = (a * _sigmoid(a) * b).astype(BF16)
    acc_scr[...] += jnp.dot(act, wd_ref[...], preferred_element_type=F32)

    @pl.when(j == pl.num_programs(1) - 1)
    def _():
        for s in range(nsub):
            rows = slice(s * MOD_BLK, (s + 1) * MOD_BLK)
            gate2 = mod_ref[s, 5:6, :]
            o_ref[rows, :] = _layer_norm(DN_ALPHA * x1_ref[rows, :] + gate2 * acc_scr[rows, :],
                                         g_ref[...], be_ref[...])


def _ffn(h2, x1, mod, wa, wb, cw, wd, l, g, b, *, n_ctx, s_ctx, s_lat, row0=0, n_rows=None, tm=512, tf=512):
    N = x1.shape[0]
    n_rows = N if n_rows is None else n_rows
    t0 = row0 // tm
    nsub = tm // MOD_BLK
    nh = tm // HALO
    last_h = N // HALO - 1
    pad = ((0, 0), (0, D_FF_PAD - D_FF))
    cwa, cwb = jnp.pad(cw[:, :D_FF], pad), jnp.pad(cw[:, D_FF:], pad)
    return pl.pallas_call(
        functools.partial(_ffn_kernel, tm=tm, tf=tf, nsub=nsub, t0=t0, n_ctx=n_ctx, s_ctx=s_ctx, s_lat=s_lat),
        grid=(n_rows // tm, D_FF_PAD // tf),
        in_specs=[pl.BlockSpec((HALO, D_MODEL), lambda i, j: (jnp.maximum((i + t0) * nh - 1, 0), 0)),
                  pl.BlockSpec((tm, D_MODEL), lambda i, j: (i + t0, 0)),
                  pl.BlockSpec((HALO, D_MODEL), lambda i, j: (jnp.minimum((i + t0 + 1) * nh, last_h), 0)),
                  pl.BlockSpec((tm, D_MODEL), lambda i, j: (i + t0, 0)),
                  pl.BlockSpec((nsub, 6, D_MODEL), lambda i, j: (i + t0, 0, 0)),
                  pl.BlockSpec((None, D_MODEL, tf), lambda i, j: (l, 0, j)),
                  pl.BlockSpec((None, D_MODEL, tf), lambda i, j: (l, 0, j)),
                  pl.BlockSpec((3, tf), lambda i, j: (0, j)),
                  pl.BlockSpec((3, tf), lambda i, j: (0, j)),
                  pl.BlockSpec((None, tf, D_MODEL), lambda i, j: (l, j, 0)),
                  pl.BlockSpec((1, D_MODEL), lambda i, j: (0, 0)),
                  pl.BlockSpec((1, D_MODEL), lambda i, j: (0, 0))],
        out_specs=pl.BlockSpec((tm, D_MODEL), lambda i, j: (i, 0)),
        out_shape=jax.ShapeDtypeStruct((n_rows, D_MODEL), F32),
        scratch_shapes=[pltpu.VMEM((tm + 2 * HALO, D_MODEL), BF16),
                        pltpu.VMEM((tm + 2 * HALO, tf), F32),
                        pltpu.VMEM((tm + 2 * HALO, tf), F32),
                        pltpu.VMEM((tm, D_MODEL), F32)],
        compiler_params=pltpu.CompilerParams(dimension_semantics=("arbitrary", "arbitrary"),
                                             vmem_limit_bytes=VMEM_LIMIT),
        name="conv_ffn",
    )(h2, h2, h2, x1, mod, wa, wb, cwa, cwb, wd, g.reshape(1, D_MODEL), b.reshape(1, D_MODEL))


def _prep_up_kernel(a_ref, b_ref, wa_ref, wb_ref):
    zeros = jnp.zeros((a_ref.shape[0], D_FF_PAD - D_FF), BF16)
    wa_ref[:, :D_FF] = a_ref[...].astype(BF16)
    wa_ref[:, D_FF:] = zeros
    wb_ref[:, :D_FF] = b_ref[...].astype(BF16)
    wb_ref[:, D_FF:] = zeros


def _prep_down_kernel(w_ref, o_ref, *, n_real):
    keep = pl.program_id(1) < n_real
    o_ref[...] = jnp.where(keep, w_ref[...], 0.0).astype(BF16)


def _prep_in_kernel(w_ref, w1_ref, a1_ref, o_ref):
    n = w_ref.shape[-1]
    o_ref[:, :C_Q] = w_ref[:, :C_Q].astype(BF16)
    for g in range(GQA_GROUP):
        for h in range(ATTN_KV_HEADS):
            src = C_Q + (h * GQA_GROUP + g) * HEAD_DIM
            dst = C_Q + (g * ATTN_KV_HEADS + h) * HEAD_DIM
            o_ref[:, dst:dst + HEAD_DIM] = w_ref[:, src:src + HEAD_DIM].astype(BF16)
    o_ref[:, C_AK:n] = w_ref[:, C_AK:n].astype(BF16)
    o_ref[:, n:] = jnp.concatenate([w1_ref[0], w1_ref[1], a1_ref[0], a1_ref[1]], axis=1).astype(BF16)


def _cast_kernel(w_ref, o_ref):
    o_ref[...] = w_ref[...].astype(BF16)


def _prep_weights(w_in, rwkv_w1, rwkv_a1, w_out, ffn_w_up, ffn_w_down):
    L, D = w_in.shape[0], D_MODEL
    cp = lambda n: pltpu.CompilerParams(dimension_semantics=("arbitrary",) * n, vmem_limit_bytes=VMEM_LIMIT)
    tr = 128
    half = jax.ShapeDtypeStruct((L, D, D_FF_PAD), BF16)
    wa, wb = pl.pallas_call(
        _prep_up_kernel,
        grid=(L, D // tr),
        in_specs=[pl.BlockSpec((None, tr, D_FF), lambda l, i: (l, i, 0)),
                  pl.BlockSpec((None, tr, D_FF), lambda l, i: (l, i, 1))],
        out_specs=[pl.BlockSpec((None, tr, D_FF_PAD), lambda l, i: (l, i, 0))] * 2,
        out_shape=[half, half], compiler_params=cp(2), name="prep_ffn_up",
    )(ffn_w_up, ffn_w_up)
    nr = D_FF // LANES
    wd = pl.pallas_call(
        functools.partial(_prep_down_kernel, n_real=nr),
        grid=(L, D_FF_PAD // LANES),
        in_specs=[pl.BlockSpec((None, LANES, D), lambda l, j: (l, jnp.minimum(j, nr - 1), 0))],
        out_specs=pl.BlockSpec((None, LANES, D), lambda l, j: (l, j, 0)),
        out_shape=jax.ShapeDtypeStruct((L, D_FF_PAD, D), BF16), compiler_params=cp(2), name="prep_ffn_down",
    )(ffn_w_down)
    tr2 = 256
    n_in = w_in.shape[-1]
    w_ext = pl.pallas_call(
        _prep_in_kernel,
        grid=(L, D // tr2),
        in_specs=[pl.BlockSpec((None, tr2, n_in), lambda l, i: (l, i, 0)),
                  pl.BlockSpec((None, 2, tr2, LORA), lambda l, i: (l, 0, i, 0)),
                  pl.BlockSpec((None, 2, tr2, LORA), lambda l, i: (l, 0, i, 0))],
        out_specs=pl.BlockSpec((None, tr2, IN_W_EXT), lambda l, i: (l, i, 0)),
        out_shape=jax.ShapeDtypeStruct((L, D, IN_W_EXT), BF16), compiler_params=cp(2), name="prep_in",
    )(w_in, rwkv_w1, rwkv_a1)
    hb0 = (RWKV_W + HYENA_W) // HEAD_DIM

    def src_rows(i):
        a = i - hb0
        perm = hb0 + (a % ATTN_KV_HEADS) * GQA_GROUP + a // ATTN_KV_HEADS
        return jnp.where((a >= 0) & (a < ATTN_HEADS), perm, i)

    wo = pl.pallas_call(
        _cast_kernel,
        grid=(L, D // HEAD_DIM),
        in_specs=[pl.BlockSpec((None, HEAD_DIM, D), lambda l, i: (l, src_rows(i), 0))],
        out_specs=pl.BlockSpec((None, HEAD_DIM, D), lambda l, i: (l, i, 0)),
        out_shape=jax.ShapeDtypeStruct((L, D, D), BF16), compiler_params=cp(2), name="prep_out",
    )(w_out)
    return w_ext, wo, wa, wb, wd


def kernel(x_prompt, x_sample, c, state_rwkv, cache_k, cache_v, c_ctx, w_mod, b_mod, w_in, rwkv_w0, rwkv_w1, rwkv_w2, rwkv_a0, rwkv_a1, rwkv_a2, rwkv_k_k, rwkv_k_a, rwkv_r_k, rwkv_lnx_g, rwkv_lnx_b, hy_short_w, hy_f_w1, hy_f_b1, hy_f_freq1, hy_f_w2, hy_f_b2, hy_f_freq2, hy_f_w3, hy_bias, attn_sink, w_out, ln1_g, ln1_b, ffn_w_up, ffn_conv_w, ffn_w_down, ln2_g, ln2_b):
    params = dict(rwkv_w0=rwkv_w0, rwkv_w2=rwkv_w2, rwkv_a0=rwkv_a0, rwkv_a2=rwkv_a2,
                  rwkv_k_k=rwkv_k_k, rwkv_k_a=rwkv_k_a, rwkv_r_k=rwkv_r_k,
                  rwkv_lnx_g=rwkv_lnx_g, rwkv_lnx_b=rwkv_lnx_b,
                  hy_short_w=hy_short_w, hy_f_w1=hy_f_w1, hy_f_b1=hy_f_b1, hy_f_freq1=hy_f_freq1,
                  hy_f_w2=hy_f_w2, hy_f_b2=hy_f_b2, hy_f_freq2=hy_f_freq2, hy_f_w3=hy_f_w3,
                  hy_bias=hy_bias)
    L = w_in.shape[0]
    Bc, Sc, D = x_prompt.shape
    Bl, Tl, _ = x_sample.shape
    P = cache_k.shape[2]
    n_ctx = Bc * Sc
    n_lat = Bl * Tl
    N = n_ctx + n_lat
    assert Sc % MOD_BLK == 0 and Tl % MOD_BLK == 0 and n_ctx % Tl == 0

    n_cond = 1 + Bl
    cond = jnp.zeros((16, D), F32).at[0].set(c_ctx).at[1:n_cond].set(c)
    mod_all = _modulation(cond, w_mod, b_mod)
    mod_ctx = jnp.broadcast_to(mod_all[:, 0:1, :], (L, n_ctx // MOD_BLK, 6 * D))
    mod_lat = jnp.broadcast_to(mod_all[:, 1:n_cond, None, :], (L, Bl, Tl // MOD_BLK, 6 * D))
    mod_blk = jnp.concatenate([mod_ctx, mod_lat.reshape(L, n_lat // MOD_BLK, 6 * D)], axis=1)
    mod_blk = mod_blk.reshape(L, N // MOD_BLK, 6, D)

    cm_c, sm_c = (jnp.asarray(m, F32) for m in _dft_mats(Sc))
    cm_l, sm_l = (jnp.asarray(m, F32) for m in _dft_mats(Tl))
    sink_pad = jnp.zeros((L, 1, LANES), F32).at[:, 0, :ATTN_HEADS].set(attn_sink)

    w_ext, wo, wa, wb, wd = _prep_weights(w_in, rwkv_w1, rwkv_a1, w_out, ffn_w_up, ffn_w_down)
    x_src = (x_prompt.reshape(n_ctx, D), x_sample.reshape(n_lat, D), n_ctx, n_lat, 0)
    states, keys, vals = [], [], []
    for l in range(L):
        p = {n: a[l] for n, a in params.items()}
        proj = _in_projection(x_src, mod_blk[l], w_ext, l)
        keys.append(proj[:n_ctx, C_AK:C_AK + KV_W].reshape(Bc, Sc, ATTN_KV_HEADS, HEAD_DIM))
        vals.append(proj[:n_ctx, C_AV:C_AV + KV_W].reshape(Bc, Sc, ATTN_KV_HEADS, HEAD_DIM))

        ya_c, s_fin = _rwkv(proj, p, None, T=Sc, n_seq=Bc, row_blk0=0, nsq=RWKV_CTX_SEQS)
        ya_l, _ = _rwkv(proj, p, state_rwkv[:, l], T=Tl, n_seq=Bl, row_blk0=n_ctx // Tl)
        states.append(s_fin)

        yb_c = _hyena(proj, p, _hyena_filter(p, Sc, cm_c, sm_c), cm_c, sm_c, L=Sc, n_seq=Bc, row_blk0=0)
        yb_l = _hyena(proj, p, _hyena_filter(p, Tl, cm_l, sm_l), cm_l, sm_l, L=Tl, n_seq=Bl,
                      row_blk0=n_ctx // Tl)

        yc_c = _ctx_attention(proj, sink_pad[l], S=Sc, n_seq=Bc)
        yc_l = _lat_attention(proj, cache_k[:, l].reshape(Bl, P, KV_W), cache_v[:, l].reshape(Bl, P, KV_W),
                              sink_pad[l], T=Tl, n_seq=Bl, row_blk0=n_ctx // Tl)

        x1, h2 = _out_projection((ya_c, yb_c, yc_c), (ya_l, yb_l, yc_l), x_src, mod_blk[l],
                                 wo, l, ln1_g[l], ln1_b[l])
        ffn = functools.partial(_ffn, h2, x1, mod_blk[l], wa, wb, ffn_conv_w[l], wd, l, ln2_g[l], ln2_b[l],
                                n_ctx=n_ctx, s_ctx=Sc, s_lat=Tl)
        if l < L - 1:
            x = ffn()
            x_src = (x, x, n_ctx, n_lat, n_ctx)
        else:
            y_prompt = ffn(row0=0, n_rows=n_ctx).reshape(Bc, Sc, D)
            y_sample = ffn(row0=n_ctx, n_rows=n_lat).reshape(Bl, Tl, D)

    new_state = jnp.stack(states, 1)
    new_k = jnp.stack(keys, 1)
    new_v = jnp.stack(vals, 1)
    return (y_prompt, y_sample, new_state, new_k, new_v)
```

```python
import functools
import math

import numpy as np
import jax
import jax.numpy as jnp
from jax import lax
from jax.experimental import pallas as pl
from jax.experimental.pallas import tpu as pltpu

F32 = jnp.float32
BF16 = jnp.bfloat16

D_MODEL = 2048
DEPTH = 4
GRID_W = 64
HEAD_DIM = 64
RWKV_W = 768
RWKV_HEADS = 12
HYENA_W = 512
ATTN_W = 768
ATTN_HEADS = 12
ATTN_KV_HEADS = 4
GQA_GROUP = 3
KV_W = 256
LORA = 64
GN_EPS = 64e-5
FILTER_EMB = 33
FILTER_HID = 64
HY_MIN_DECAY = math.log(1e-2) / 1.5
HY_MAX_DECAY = math.log(1e-2) / 0.3
WINDOW = 128
BLOCK = 128
ROPE_BASE = 10000.0
D_FF = 5504
LN_EPS = 1e-5
DN_ALPHA = (2 * DEPTH) ** 0.25

C_R, C_K, C_V, C_G = 0, 768, 1536, 2304
C_HY = 3072
C_Q, C_AK, C_AV = 4608, 5376, 5632
C_LORA = 5888
IN_W_EXT = 6144

MOD_BLK = 256
CHUNK = 64
RWKV_CB = 8
RWKV_CTX_SEQS = 4
HYENA_CTX_SEQS = 4
LANES = 128
D_FF_PAD = 5632
VMEM_LIMIT = 56 * 1024 * 1024


def _dot(a, b):
    return jnp.dot(a.astype(BF16), b.astype(BF16), preferred_element_type=F32)


def _dot_nt(a, b):
    return lax.dot_general(a.astype(BF16), b.astype(BF16), (((1,), (1,)), ((), ())),
                           preferred_element_type=F32)


def _split(x):
    hi = x.astype(BF16)
    lo = (x - hi.astype(F32)).astype(BF16)
    return hi, lo


def _dot3_dims(a, b, dims):
    ah, al = _split(a)
    bh, bl = _split(b)
    dg = functools.partial(lax.dot_general, dimension_numbers=dims, preferred_element_type=F32)
    return dg(ah, bh) + (dg(ah, bl) + dg(al, bh))


_NN = (((1,), (0,)), ((), ()))
_NT = (((1,), (1,)), ((), ()))
_TN = (((0,), (0,)), ((), ()))


def _dot3(a, b):
    return _dot3_dims(a, b, _NN)


def _sigmoid(x):
    return 1.0 / (1.0 + jnp.exp(-x))


def _softplus(x):
    return jnp.maximum(x, 0.0) + jnp.log(1.0 + jnp.exp(-jnp.abs(x)))


def _mod_kernel(c_ref, w_ref, b_ref, o_ref):
    c = c_ref[...]
    s = c * _sigmoid(c)
    o_ref[...] = _dot(s, w_ref[...]) + b_ref[...]


def _modulation(cond, w_mod, b_mod):
    L = w_mod.shape[0]
    R = cond.shape[0]
    tn = 1024
    return pl.pallas_call(
        _mod_kernel,
        grid=(L, 6 * D_MODEL // tn),
        in_specs=[pl.BlockSpec((R, D_MODEL), lambda l, j: (0, 0)),
                  pl.BlockSpec((None, D_MODEL, tn), lambda l, j: (l, 0, j)),
                  pl.BlockSpec((None, 1, tn), lambda l, j: (l, 0, j))],
        out_specs=pl.BlockSpec((None, R, tn), lambda l, j: (l, 0, j)),
        out_shape=jax.ShapeDtypeStruct((L, R, 6 * D_MODEL), F32),
        compiler_params=pltpu.CompilerParams(dimension_semantics=("arbitrary", "arbitrary"),
                                             vmem_limit_bytes=VMEM_LIMIT),
        name="modulation",
    )(cond, w_mod, b_mod.reshape(L, 1, 6 * D_MODEL))


def _two_source_specs(tm, width, nct, nlt, lat_off):
    ctx = pl.BlockSpec((tm, width), lambda i: (jnp.minimum(i, nct - 1), 0))
    lat = pl.BlockSpec((tm, width), lambda i: (jnp.clip(i - nct, 0, nlt - 1) + lat_off, 0))
    return ctx, lat


def _inproj_kernel(xc_ref, xl_ref, mod_ref, w_ref, o_ref, h_scr, *, tn, n_ctx_tiles):
    shift = mod_ref[0, 0:1, :]
    scale = mod_ref[0, 1:2, :]

    def modulate(x_ref):
        h_scr[...] = (x_ref[...] * (1.0 + scale) + shift).astype(BF16)

    pl.when(pl.program_id(0) < n_ctx_tiles)(lambda: modulate(xc_ref))
    pl.when(pl.program_id(0) >= n_ctx_tiles)(lambda: modulate(xl_ref))
    h = h_scr[...]
    for j in range(IN_W_EXT // tn):
        cols = slice(j * tn, (j + 1) * tn)
        o_ref[:, cols] = jnp.dot(h, w_ref[:, cols], preferred_element_type=F32)


def _in_projection(x_src, mod, w_ext, l, tn=512):
    xc, xl, n_ctx, n_lat, lat_off = x_src
    N = n_ctx + n_lat
    tm = MOD_BLK
    nct = n_ctx // tm
    ctx_spec, lat_spec = _two_source_specs(tm, D_MODEL, nct, n_lat // tm, lat_off // tm)
    return pl.pallas_call(
        functools.partial(_inproj_kernel, tn=tn, n_ctx_tiles=nct),
        grid=(N // tm,),
        in_specs=[ctx_spec, lat_spec,
                  pl.BlockSpec((1, 6, D_MODEL), lambda i: (i, 0, 0)),
                  pl.BlockSpec((None, D_MODEL, IN_W_EXT), lambda i: (l, 0, 0), pipeline_mode=pl.Buffered(1))],
        out_specs=pl.BlockSpec((tm, IN_W_EXT), lambda i: (i, 0)),
        out_shape=jax.ShapeDtypeStruct((N, IN_W_EXT), F32),
        scratch_shapes=[pltpu.VMEM((tm, D_MODEL), BF16)],
        compiler_params=pltpu.CompilerParams(dimension_semantics=("arbitrary",),
                                             vmem_limit_bytes=VMEM_LIMIT),
        name="in_projection",
    )(xc, xl, mod, w_ext)


PREC = dict(seg=1, cum=2, a=1, neu=1, post=1, mn=1, b_o=1, b_s=1, gn=2)


def _dotp(a, b, dims, passes):
    dg = functools.partial(lax.dot_general, dimension_numbers=dims, preferred_element_type=F32)
    if passes == 1:
        return dg(a.astype(BF16), b.astype(BF16))
    ah, al = _split(a)
    if passes == 2:
        bh = b.astype(BF16)
        return dg(ah, bh) + dg(al, bh)
    bh, bl = _split(b)
    return dg(ah, bh) + (dg(ah, bl) + dg(al, bh))


def _dotp_rhs(a, b, passes):
    dg = functools.partial(lax.dot_general, dimension_numbers=_NN, preferred_element_type=F32)
    ab = a.astype(BF16)
    if passes == 1:
        return dg(ab, b.astype(BF16))
    bh, bl = _split(b)
    return dg(ab, bh) + dg(ab, bl)


def _rwkv_kernel(*refs, T, nsq, has_init, cb):
    (r_ref, k_ref, v_ref, g_ref, lo_ref, w0_ref, w2_ref, a0_ref, a2_ref,
     kk_ref, ka_ref, rk_ref, lng_ref, lnb_ref) = refs[:14]
    pos = 14
    init_ref = None
    if has_init:
        init_ref = refs[pos]
        pos += 1
    y_ref, sfin_ref = refs[pos:pos + 2]
    rp_scr, ov_scr, m_scr, n_scr, o_scr, bonus_scr, s_scr = refs[pos + 2:]

    nC = T // CHUNK
    nCs = nC // nsq
    C = CHUNK
    H = HEAD_DIM
    P = PREC
    C2 = 2 * C
    row = lax.broadcasted_iota(jnp.int32, (C, C), 0)
    col = lax.broadcasted_iota(jnp.int32, (C, C), 1)
    tri_f = jnp.where(col <= row, 1.0, 0.0).astype(F32)
    lrow = lax.broadcasted_iota(jnp.int32, (LANES, LANES), 0)
    lcol = lax.broadcasted_iota(jnp.int32, (LANES, LANES), 1)
    same_head = (lrow // H) == (lcol // H)
    head_ones = jnp.where(same_head, 1.0, 0.0).astype(F32)
    eye = lrow == lcol
    rt_, ct_ = lrow % C, lcol % C
    strict_m = [same_head & (ct_ < rt_), same_head & (ct_ > rt_)]
    incl_m = [same_head & (ct_ <= rt_), same_head & (ct_ >= rt_)]
    lane = lax.broadcasted_iota(jnp.int32, (1, LANES), 1)
    hmask = [lane < H, lane >= H]

    def stack_heads(x):
        return jnp.concatenate([jnp.where(hmask[0], x, 0.0), jnp.where(hmask[1], x, 0.0)], axis=0)

    kkw = kk_ref[...]
    kaw = ka_ref[...]
    rkw = rk_ref[...]
    zl = jnp.zeros((LORA, LANES), F32)
    lora_w = jnp.concatenate(
        [jnp.concatenate([w if j == i else zl for j in range(4)], axis=1)
         for i, w in enumerate([w2_ref[0], w2_ref[1], a2_ref[0], a2_ref[1]])], axis=0).astype(BF16)

    def phase_a(ci, carry):
        units = []
        for cc in range(cb):
            c = ci * cb + cc
            rows = pl.ds(pl.multiple_of(c * C, C), C)
            r = r_ref[rows, :]
            k = k_ref[rows, :]
            v = v_ref[rows, :]
            lo = lo_ref[rows, :]
            kkraw = k * kkw
            ss = _dotp(kkraw * kkraw, head_ones, _NN, P['seg'])
            kkn = kkraw / jnp.maximum(jnp.sqrt(ss), 1e-12)
            bonus = jnp.zeros((C, LANES), F32)
            vm = stack_heads(v)
            lo_in = jnp.concatenate([jnp.tanh(lo[:, :2 * LORA]), lo[:, 2 * LORA:]], axis=1)
            lo_out = _dot(lo_in, lora_w)
            lds, alphas = [], []
            for d in range(2):
                wl = w0_ref[d] + lo_out[:, d * LANES:(d + 1) * LANES]
                lds.append(-jnp.exp(-_softplus(-wl) - 0.5))
                alphas.append(_sigmoid(a0_ref[d] + lo_out[:, (2 + d) * LANES:(3 + d) * LANES]))
            pref = _dotp_rhs(tri_f, jnp.concatenate(lds, axis=1), P['cum'])
            for d in range(2):
                ld, alpha = lds[d], alphas[d]
                kd = k * (1.0 + (alpha - 1.0) * kaw)
                bb = kkn * alpha
                bonus = bonus + _dotp(r * kd * rkw, head_ones, _NN, P['seg']) * v
                pf = pref[:, d * LANES:(d + 1) * LANES]
                g_incl = pf if d == 0 else pf[C - 1:C, :] - pf + ld
                g_excl = g_incl - ld
                g_end = g_incl[C - 1:C, :] if d == 0 else g_incl[0:1, :]
                ginv = jnp.exp(-g_incl)
                dec_end = jnp.exp(g_end - g_incl)
                units.append(dict(
                    d=d, c=c, vm=vm,
                    at=stack_heads(-kkn * jnp.exp(g_excl)), rt=stack_heads(r * jnp.exp(g_incl)),
                    bt=stack_heads(bb * ginv), kt=stack_heads(kd * ginv),
                    bend=stack_heads(bb * dec_end), kend=stack_heads(kd * dec_end), gend=jnp.exp(g_end)))
            bonus_scr[rows, :] = bonus
            o_scr[rows, :] = jnp.zeros((C, LANES), F32)

        for u in units:
            ar = jnp.concatenate([u['at'], u['rt']], axis=0)
            bk = jnp.concatenate([u['bt'], u['kt']], axis=0)
            u['a'] = _dotp(ar, bk, _NT, P['a'])
        for u in units:
            a = u['a']
            st, ic = strict_m[u['d']], incl_m[u['d']]
            u['p'] = jnp.where(st, a[:C2, :C2], 0.0)
            u['a_ak'] = jnp.where(st, a[:C2, C2:], 0.0)
            u['a_rb'] = jnp.where(ic, a[C2:, :C2], 0.0)
            u['a_rk'] = jnp.where(ic, a[C2:, C2:], 0.0)
        rvs = [_dotp(u['a_ak'], u['vm'], _NN, P['post']) for u in units]
        sq = [_dotp(u['p'], u['p'], _NN, P['neu']) for u in units]
        for i, u in enumerate(units):
            u['x'] = jnp.where(eye, 1.0, 0.0) + u['p']
            u['p'] = sq[i]
        for it in range(1, 6):
            res = [_dotp(u['p'], jnp.concatenate([u['x'], u['p']], axis=1) if it < 5 else u['x'], _NN, P['neu'])
                   for u in units]
            for i, u in enumerate(units):
                u['x'] = u['x'] + res[i][:, :C2]
                if it < 5:
                    u['p'] = res[i][:, C2:]
        rhs = [_dotp(u['x'], jnp.concatenate([u['at'], rvs[i]], axis=1), _NN, P['post'])
               for i, u in enumerate(units)]
        for i, u in enumerate(units):
            u['rhs'] = rhs[i]
        xs = [_dotp(jnp.concatenate([u['a_rb'], u['a_rk']], axis=1),
                    jnp.concatenate([u['rhs'], jnp.concatenate([jnp.zeros_like(u['vm']), u['vm']], axis=1)], axis=0),
                    _NN, P['post']) for u in units]
        ms = [_dotp(u['rhs'][:, :LANES], u['bend'], _TN, P['mn']) for u in units]
        ns = [_dotp(jnp.concatenate([u['rhs'][:, LANES:], u['vm']], axis=0),
                    jnp.concatenate([u['bend'], u['kend']], axis=0), _TN, P['mn']) for u in units]
        for i, u in enumerate(units):
            d, c = u['d'], u['c']
            rp = u['rt'] + xs[i][:, :LANES]
            ov = xs[i][:, LANES:]
            rp_scr[d, c] = rp[:C] + rp[C:]
            ov_scr[d, c] = ov[:C] + ov[C:]
            m_scr[d, c] = jnp.where(eye, jnp.broadcast_to(u['gend'], (LANES, LANES)), 0.0) + ms[i]
            n_scr[d, c] = ns[i]
        return carry

    lax.fori_loop(0, nC // cb, phase_a, 0)

    zero_blk = jnp.zeros((H, H), F32)
    for q in range(nsq):
        for d in range(2):
            if has_init:
                s_scr[q, d] = jnp.concatenate([jnp.concatenate([init_ref[q, d, 0], zero_blk], axis=1),
                                               jnp.concatenate([zero_blk, init_ref[q, d, 1]], axis=1)], axis=0)
            else:
                s_scr[q, d] = jnp.zeros((LANES, LANES), F32)

    def phase_b(i, carry):
        idx = [(q, d, q * nCs + (i if d == 0 else nCs - 1 - i)) for q in range(nsq) for d in range(2)]
        ss = [s_scr[q, d] for q, d, c in idx]
        os_ = [_dotp(rp_scr[d, c], ss[j], _NT, P['b_o']) for j, (q, d, c) in enumerate(idx)]
        sn = [_dotp(ss[j], m_scr[d, c], _NN, P['b_s']) for j, (q, d, c) in enumerate(idx)]
        for j, (q, d, c) in enumerate(idx):
            s_scr[q, d] = sn[j] + n_scr[d, c]
            rows = pl.ds(c * C if isinstance(c, int) else pl.multiple_of(c * C, C), C)
            o_scr[rows, :] = o_scr[rows, :] + (os_[j] + ov_scr[d, c])
        return carry

    if nCs <= 4:
        for i in range(nCs):
            phase_b(i, 0)
    else:
        lax.fori_loop(0, nCs, phase_b, 0, unroll=4)

    for q in range(nsq):
        for d in range(2):
            s = s_scr[q, d]
            sfin_ref[q, d, 0] = s[:H, :H]
            sfin_ref[q, d, 1] = s[H:, H:]

    lng = lng_ref[...]
    lnb = lnb_ref[...]

    rb = min(T, 256)

    def phase_c(c, carry):
        rows = pl.ds(pl.multiple_of(c * rb, rb), rb)
        o = o_scr[rows, :]
        st = _dotp(jnp.concatenate([o, o * o], axis=0), head_ones, _NN, P['gn']) * (1.0 / H)
        mu = st[:rb]
        var = st[rb:] - mu * mu
        on = (o - mu) * lax.rsqrt(var + GN_EPS) * lng + lnb
        y_ref[rows, :] = (on + bonus_scr[rows, :]) * _sigmoid(g_ref[rows, :])
        return carry

    lax.fori_loop(0, T // rb, phase_c, 0)


def _rwkv(proj, p, init, *, T, n_seq, row_blk0, nsq=1):
    seq_len, T = T, nsq * T
    nC = T // CHUNK
    npair = RWKV_HEADS // 2
    has_init = init is not None
    cb = lambda off: (lambda b, j: (b + row_blk0, off // LANES + j))
    vec = lambda b, j: (0, j)
    in_specs = [pl.BlockSpec((T, LANES), cb(C_R)),
                pl.BlockSpec((T, LANES), cb(C_K)),
                pl.BlockSpec((T, LANES), cb(C_V)),
                pl.BlockSpec((T, LANES), cb(C_G)),
                pl.BlockSpec((T, 4 * LORA), lambda b, j: (b + row_blk0, C_LORA // (4 * LORA))),
                pl.BlockSpec((2, 1, LANES), lambda b, j: (0, 0, j)),
                pl.BlockSpec((2, LORA, LANES), lambda b, j: (0, 0, j)),
                pl.BlockSpec((2, 1, LANES), lambda b, j: (0, 0, j)),
                pl.BlockSpec((2, LORA, LANES), lambda b, j: (0, 0, j)),
                pl.BlockSpec((1, LANES), vec), pl.BlockSpec((1, LANES), vec),
                pl.BlockSpec((1, LANES), vec), pl.BlockSpec((1, LANES), vec),
                pl.BlockSpec((1, LANES), vec)]
    args = [proj, proj, proj, proj, proj,
            p['rwkv_w0'].reshape(2, 1, RWKV_W), p['rwkv_w2'], p['rwkv_a0'].reshape(2, 1, RWKV_W), p['rwkv_a2'],
            p['rwkv_k_k'].reshape(1, RWKV_W), p['rwkv_k_a'].reshape(1, RWKV_W),
            p['rwkv_r_k'].reshape(1, RWKV_W), p['rwkv_lnx_g'].reshape(1, RWKV_W),
            p['rwkv_lnx_b'].reshape(1, RWKV_W)]
    st_spec = pl.BlockSpec((nsq, 2, 2, HEAD_DIM, HEAD_DIM), lambda b, j: (b, 0, j, 0, 0))
    if has_init:
        in_specs.append(st_spec)
        args.append(init)
    sc_o = lambda: pltpu.VMEM((2, nC, CHUNK, LANES), F32)
    sc_s = lambda: pltpu.VMEM((2, nC, LANES, LANES), F32)
    return pl.pallas_call(
        functools.partial(_rwkv_kernel, T=T, nsq=nsq, has_init=has_init, cb=min(RWKV_CB, nC)),
        grid=(n_seq // nsq, npair),
        in_specs=in_specs,
        out_specs=[pl.BlockSpec((T, LANES), lambda b, j: (b, j)), st_spec],
        out_shape=[jax.ShapeDtypeStruct((n_seq * seq_len, RWKV_W), F32),
                   jax.ShapeDtypeStruct((n_seq, 2, RWKV_HEADS, HEAD_DIM, HEAD_DIM), F32)],
        scratch_shapes=[sc_o(), sc_o(), sc_s(), sc_s(),
                        pltpu.VMEM((T, LANES), F32), pltpu.VMEM((T, LANES), F32),
                        pltpu.VMEM((nsq, 2, LANES, LANES), F32)],
        compiler_params=pltpu.CompilerParams(dimension_semantics=("arbitrary", "arbitrary"),
                                             vmem_limit_bytes=VMEM_LIMIT),
        name="rwkv_T%d" % T,
    )(*args)


def _dft_mats(L):
    f = np.arange(L, dtype=np.int64)
    ph = (np.outer(f, f) % (2 * L)).astype(np.float64) * (math.pi / L)
    return np.cos(ph), -np.sin(ph)


def _filter_features(L):
    t = np.linspace(0.0, 1.0, L, dtype=np.float32)[:, None]
    bands = (FILTER_EMB - 1) // 2
    t_res = np.arange(L, dtype=np.float32)[:, None]
    f = np.linspace(1e-4, bands - 1, bands, dtype=np.float32)[None, :]
    ang = (2.0 * math.pi * t_res * f / L).astype(np.float32)
    z = np.concatenate([t, np.cos(ang), np.sin(ang)], -1).astype(np.float32)
    zp = np.zeros((L, FILTER_HID), np.float32)
    zp[:, :FILTER_EMB] = z
    deltas = np.abs(np.linspace(HY_MIN_DECAY, HY_MAX_DECAY, HYENA_W, dtype=np.float32))
    return zp, t.astype(np.float32), deltas[None, :].astype(np.float32)


def _hyfilter_kernel(z_ref, t_ref, dl_ref, w1_ref, b1_ref, q1_ref, w2_ref, b2_ref, q2_ref, wc_ref, wa_ref,
                     hs_ref, hd_ref, knyq_ref, hm_scr, *, L):
    @pl.when((pl.program_id(0) == 0) & (pl.program_id(1) == 0))
    def _():
        h1 = jnp.sin(q1_ref[...] * (_dot3(z_ref[...], w1_ref[...]) + b1_ref[...]))
        hm_scr[...] = jnp.sin(q2_ref[...] * (_dot3(h1, w2_ref[...]) + b2_ref[...]))

    hm = hm_scr[...]
    win = jnp.exp(-t_ref[...] * dl_ref[...])
    lag = lax.broadcasted_iota(jnp.int32, (L, 1), 0)
    alt = jnp.where(lag % 2 == 0, 1.0, -1.0).astype(F32)
    hc = _dot3(hm, wc_ref[...]) * win
    ha = jnp.where(lag == 0, 0.0, _dot3(hm, wa_ref[...]) * win)
    hs_ref[...] = hc + ha
    hd_ref[...] = hc - ha
    knyq_ref[...] = jnp.sum((hc + ha) * alt, axis=0, keepdims=True) * (1.0 / (2 * L))


def _hyspec_kernel(cm_ref, sm_ref, hs_ref, hd_ref, kre_ref, kim_ref, *, L, tf):
    f = pl.program_id(2) * tf + lax.broadcasted_iota(jnp.int32, (tf, 1), 0)
    wf = jnp.where(f == 0, 1.0, 2.0).astype(F32) * (1.0 / (2 * L))
    kre_ref[...] = _dot3(cm_ref[...], hs_ref[...]) * wf
    kim_ref[...] = _dot3(sm_ref[...], hd_ref[...]) * wf


def _hyena_filter(p, L, cm, sm, cw=128):
    zp, t, dl = _filter_features(L)
    w1 = jnp.zeros((FILTER_HID, FILTER_HID), F32).at[:FILTER_EMB].set(p['hy_f_w1'])
    r2 = lambda a: a.reshape(1, -1)
    ncb = HYENA_W // cw
    full = lambda shape: pl.BlockSpec(shape, lambda n, j: (0,) * len(shape))
    shp = jax.ShapeDtypeStruct((2, L, HYENA_W), F32)
    hs, hd, knyq = pl.pallas_call(
        functools.partial(_hyfilter_kernel, L=L),
        grid=(2, ncb),
        in_specs=[full((L, FILTER_HID)), full((L, 1)),
                  pl.BlockSpec((1, cw), lambda n, j: (0, j)),
                  full((FILTER_HID, FILTER_HID)), full((1, FILTER_HID)), full((1, FILTER_HID)),
                  full((FILTER_HID, FILTER_HID)), full((1, FILTER_HID)), full((1, FILTER_HID)),
                  pl.BlockSpec((FILTER_HID, cw), lambda n, j: (0, n * 2 * ncb + j)),
                  pl.BlockSpec((FILTER_HID, cw), lambda n, j: (0, n * 2 * ncb + ncb + j))],
        out_specs=[pl.BlockSpec((None, L, cw), lambda n, j: (n, 0, j)),
                   pl.BlockSpec((None, L, cw), lambda n, j: (n, 0, j)),
                   pl.BlockSpec((None, 1, cw), lambda n, j: (n, 0, j))],
        out_shape=[shp, shp, jax.ShapeDtypeStruct((2, 1, HYENA_W), F32)],
        scratch_shapes=[pltpu.VMEM((L, FILTER_HID), F32)],
        compiler_params=pltpu.CompilerParams(dimension_semantics=("arbitrary", "arbitrary"),
                                             vmem_limit_bytes=VMEM_LIMIT),
        name="hyena_filter_L%d" % L,
    )(jnp.asarray(zp), jnp.asarray(t), jnp.asarray(dl), w1, r2(p['hy_f_b1']), r2(p['hy_f_freq1']),
      p['hy_f_w2'], r2(p['hy_f_b2']), r2(p['hy_f_freq2']), p['hy_f_w3'], p['hy_f_w3'])
    tf = min(256, L)
    sw = 256
    kre, kim = pl.pallas_call(
        functools.partial(_hyspec_kernel, L=L, tf=tf),
        grid=(2, HYENA_W // sw, L // tf),
        in_specs=[pl.BlockSpec((tf, L), lambda n, j, i: (i, 0)),
                  pl.BlockSpec((tf, L), lambda n, j, i: (i, 0)),
                  pl.BlockSpec((None, L, sw), lambda n, j, i: (n, 0, j)),
                  pl.BlockSpec((None, L, sw), lambda n, j, i: (n, 0, j))],
        out_specs=[pl.BlockSpec((None, tf, sw), lambda n, j, i: (n, i, j)),
                   pl.BlockSpec((None, tf, sw), lambda n, j, i: (n, i, j))],
        out_shape=[shp, shp],
        compiler_params=pltpu.CompilerParams(dimension_semantics=("arbitrary",) * 3,
                                             vmem_limit_bytes=VMEM_LIMIT),
        name="hyena_spectrum_L%d" % L,
    )(cm, sm, hs, hd)
    return kre, kim, knyq


def _hyena_kernel(v_ref, x1_ref, x2_ref, sw_ref, bias_ref, cm_ref, sm_ref, kre_ref, kim_ref, knyq_ref,
                  o_ref, zb_scr, zf_scr, xs_scr, yre_scr, yim_scr, *, L, rb, nsq):
    nrb = L // rb
    cw = v_ref.shape[-1]
    t = lax.broadcasted_iota(jnp.int32, (nsq * L, 1), 0) % L
    first = t == 0
    last = t == L - 1
    tl = lax.broadcasted_iota(jnp.int32, (L, 1), 0)
    alt_all = jnp.where(tl % 2 == 0, 1.0, -1.0).astype(F32)
    tb = lax.broadcasted_iota(jnp.int32, (rb, 1), 0)
    alt_blk = jnp.where(tb % 2 == 0, 1.0, -1.0).astype(F32)
    sw = sw_ref[...]
    tile = lambda a: jnp.concatenate([a] * nsq, axis=1) if nsq > 1 else a

    def short(ref, s):
        u = ref[...]
        prev = jnp.where(first, 0.0, pltpu.roll(u, 1, 0))
        nxt = jnp.where(last, 0.0, pltpu.roll(u, nsq * L - 1, 0))
        c = prev * sw[0, s][None, :] + u * sw[1, s][None, :] + nxt * sw[2, s][None, :]
        if nsq == 1:
            return c
        return jnp.concatenate([c[q * L:(q + 1) * L] for q in range(nsq)], axis=1)

    z0 = short(v_ref, 0)
    zf_scr[...] = z0
    zb_scr[...] = z0.astype(BF16)
    unyq = jnp.sum(z0 * alt_all, axis=0, keepdims=True)

    for n in range(2):
        xs_scr[...] = short(x1_ref if n == 0 else x2_ref, n + 1)

        def fwd(fb, carry):
            rows = pl.ds(pl.multiple_of(fb * rb, rb), rb)
            zb = zb_scr[...]
            ure = jnp.dot(cm_ref[rows, :], zb, preferred_element_type=F32)
            uim = jnp.dot(sm_ref[rows, :], zb, preferred_element_type=F32)
            kre = tile(kre_ref[n, rows, :])
            kim = tile(kim_ref[n, rows, :])
            yre_scr[rows, :] = (ure * kre - uim * kim).astype(BF16)
            yim_scr[rows, :] = (ure * kim + uim * kre).astype(BF16)
            return carry

        lax.fori_loop(0, nrb, fwd, 0)
        ynyq = unyq * tile(knyq_ref[n])
        bias = tile(bias_ref[n])

        def inv(ib, acc):
            r0 = pl.multiple_of(ib * rb, rb)
            rows = pl.ds(r0, rb)
            y = (jnp.dot(cm_ref[rows, :], yre_scr[...], preferred_element_type=F32)
                 + jnp.dot(sm_ref[rows, :], yim_scr[...], preferred_element_type=F32))
            y = y + alt_blk * ynyq + zf_scr[rows, :] * bias
            znew = xs_scr[rows, :] * y
            if n == 0:
                zf_scr[rows, :] = znew
                zb_scr[rows, :] = znew.astype(BF16)
                acc = acc + jnp.sum(znew * alt_blk, axis=0, keepdims=True)
            else:
                for q in range(nsq):
                    o_ref[pl.ds(q * L + r0, rb), :] = znew[:, q * cw:(q + 1) * cw]
            return acc

        unyq = lax.fori_loop(0, nrb, inv, jnp.zeros_like(unyq))


def _hyena(proj, p, filt, cm, sm, *, L, n_seq, row_blk0, nsq=1, cw=256):
    kre, kim, knyq = filt
    ncb = HYENA_W // cw
    cb = lambda s: (lambda j, b: (b + row_blk0, (C_HY + s * HYENA_W) // cw + j))
    once = pl.Buffered(1)
    sw = p['hy_short_w'].reshape(3, 3, HYENA_W)
    rb = min(256, L)
    Lb, wide = nsq * L, nsq * cw
    return pl.pallas_call(
        functools.partial(_hyena_kernel, L=L, rb=rb, nsq=nsq),
        grid=(ncb, n_seq // nsq),
        in_specs=[pl.BlockSpec((Lb, cw), cb(0)), pl.BlockSpec((Lb, cw), cb(1)), pl.BlockSpec((Lb, cw), cb(2)),
                  pl.BlockSpec((3, 3, cw), lambda j, b: (0, 0, j)),
                  pl.BlockSpec((2, 1, cw), lambda j, b: (0, 0, j)),
                  pl.BlockSpec((L, L), lambda j, b: (0, 0), pipeline_mode=once),
                  pl.BlockSpec((L, L), lambda j, b: (0, 0), pipeline_mode=once),
                  pl.BlockSpec((2, L, cw), lambda j, b: (0, 0, j), pipeline_mode=once),
                  pl.BlockSpec((2, L, cw), lambda j, b: (0, 0, j), pipeline_mode=once),
                  pl.BlockSpec((2, 1, cw), lambda j, b: (0, 0, j))],
        out_specs=pl.BlockSpec((Lb, cw), lambda j, b: (b, j)),
        out_shape=jax.ShapeDtypeStruct((n_seq * L, HYENA_W), F32),
        scratch_shapes=[pltpu.VMEM((L, wide), BF16), pltpu.VMEM((L, wide), F32), pltpu.VMEM((L, wide), F32),
                        pltpu.VMEM((L, wide), BF16), pltpu.VMEM((L, wide), BF16)],
        compiler_params=pltpu.CompilerParams(dimension_semantics=("arbitrary", "arbitrary"),
                                             vmem_limit_bytes=VMEM_LIMIT),
        name="hyena_L%d" % L,
    )(proj, proj, proj, sw, p['hy_bias'].reshape(2, 1, HYENA_W), cm.astype(BF16), sm.astype(BF16),
      kre, kim, knyq)


def _row_max(x):
    t = x[:, :LANES]
    for c in range(1, x.shape[1] // LANES):
        t = jnp.maximum(t, x[:, c * LANES:(c + 1) * LANES])
    return jnp.max(t, axis=-1, keepdims=True)


def _row_sum(x):
    t = x[:, :LANES]
    for c in range(1, x.shape[1] // LANES):
        t = t + x[:, c * LANES:(c + 1) * LANES]
    return jnp.sum(t, axis=-1, keepdims=True)


def _head_masks():
    lane = lax.broadcasted_iota(jnp.int32, (1, KV_W), 1) // HEAD_DIM
    return [lane == h for h in range(ATTN_KV_HEADS)]


def _attend_groups(qgs, keys, vals, masks, sinks, hm):
    nh = len(hm)
    un = [(g, h) for g in range(len(qgs)) for h in range(nh)]
    qm = [jnp.where(hm[h], qgs[g], 0.0).astype(BF16) for g, h in un]
    s = [[_dot_nt(qm[u], k) for k in keys] for u in range(len(un))]
    s = [[sc if mk is None else jnp.where(mk, sc, -jnp.inf) for sc, mk in zip(s[u], masks)] for u in range(len(un))]
    m = [jnp.maximum(_row_max(jnp.concatenate(s[u], axis=1)), sinks[g][h]) for u, (g, h) in enumerate(un)]
    p = [[jnp.exp(sc - m[u]) for sc in s[u]] for u in range(len(un))]
    den = [_row_sum(jnp.concatenate(p[u], axis=1)) + jnp.exp(sinks[g][h] - m[u]) for u, (g, h) in enumerate(un)]
    pv = [[_dot(pc, v) for pc, v in zip(p[u], vals)] for u in range(len(un))]
    accs = [jnp.zeros(q.shape, F32) for q in qgs]
    for u, (g, h) in enumerate(un):
        tot = pv[u][0]
        for t in pv[u][1:]:
            tot = tot + t
        accs[g] = jnp.where(hm[h], tot / den[u], accs[g])
    return accs


def _ctx_attn_kernel(q_ref, k_ref, v_ref, sink_ref, o_ref, *, S):
    scale = HEAD_DIM ** -0.5
    hm = _head_masks()
    k = k_ref[...].astype(BF16)
    v = v_ref[...].astype(BF16)
    for g in range(GQA_GROUP):
        qg = q_ref[:, g * KV_W:(g + 1) * KV_W] * scale
        sinks = [jnp.broadcast_to(sink_ref[0:1, h * GQA_GROUP + g:h * GQA_GROUP + g + 1], (S, 1))
                 for h in range(ATTN_KV_HEADS)]
        o_ref[:, g * KV_W:(g + 1) * KV_W] = _attend_groups([qg], [k], [v], [None], [sinks], hm)[0]


def _ctx_attention(proj, sink, *, S, n_seq):
    return pl.pallas_call(
        functools.partial(_ctx_attn_kernel, S=S),
        grid=(n_seq,),
        in_specs=[pl.BlockSpec((S, ATTN_W), lambda b: (b, C_Q // ATTN_W)),
                  pl.BlockSpec((S, KV_W), lambda b: (b, C_AK // KV_W)),
                  pl.BlockSpec((S, KV_W), lambda b: (b, C_AV // KV_W)),
                  pl.BlockSpec((1, LANES), lambda b: (0, 0))],
        out_specs=pl.BlockSpec((S, ATTN_W), lambda b: (b, 0)),
        out_shape=jax.ShapeDtypeStruct((n_seq * S, ATTN_W), F32),
        compiler_params=pltpu.CompilerParams(dimension_semantics=("arbitrary",),
                                             vmem_limit_bytes=VMEM_LIMIT),
        name="ctx_attention",
    )(proj, proj, proj, sink)


def _rope(x, cos, sin_signed, first_half):
    n = x.shape[-1]
    partner = jnp.where(first_half, pltpu.roll(x, n - HEAD_DIM // 2, 1), pltpu.roll(x, HEAD_DIM // 2, 1))
    return x * cos + partner * sin_signed


def _lat_attn_kernel(q_ref, kp_ref, kc_ref, kn_ref, vp_ref, vc_ref, vn_ref, ck_ref, cv_ref,
                     cq_ref, sq_ref, cp_ref, sp_ref, cc_ref, sc_ref, cn_ref, sn_ref, sink_ref, o_ref, *, T):
    i = pl.program_id(1)
    scale = HEAD_DIM ** -0.5
    hm = _head_masks()
    fh_q = (lax.broadcasted_iota(jnp.int32, (1, ATTN_W), 1) % HEAD_DIM) < HEAD_DIM // 2
    fh_k = (lax.broadcasted_iota(jnp.int32, (1, KV_W), 1) % HEAD_DIM) < HEAD_DIM // 2
    q = _rope(q_ref[...], cq_ref[...], sq_ref[...], fh_q) * scale
    kl = jnp.concatenate([_rope(kp_ref[...], cp_ref[...], sp_ref[...], fh_k),
                          _rope(kc_ref[...], cc_ref[...], sc_ref[...], fh_k),
                          _rope(kn_ref[...], cn_ref[...], sn_ref[...], fh_k)], axis=0).astype(BF16)
    vl = jnp.concatenate([vp_ref[...], vc_ref[...], vn_ref[...]], axis=0).astype(BF16)
    kc = ck_ref[...].astype(BF16)
    vc = cv_ref[...].astype(BF16)
    qpos = i * BLOCK + lax.broadcasted_iota(jnp.int32, (BLOCK, 3 * BLOCK), 0)
    kpos = (i - 1) * BLOCK + lax.broadcasted_iota(jnp.int32, (BLOCK, 3 * BLOCK), 1)
    ok = (jnp.abs(qpos - kpos) <= WINDOW) & (kpos >= 0) & (kpos < T)
    qgs = [q[:, g * KV_W:(g + 1) * KV_W] for g in range(GQA_GROUP)]
    sinks = [[jnp.broadcast_to(sink_ref[0:1, h * GQA_GROUP + g:h * GQA_GROUP + g + 1], (BLOCK, 1))
              for h in range(ATTN_KV_HEADS)] for g in range(GQA_GROUP)]
    accs = _attend_groups(qgs, [kl, kc], [vl, vc], [ok, None], sinks, hm)
    for g in range(GQA_GROUP):
        o_ref[:, g * KV_W:(g + 1) * KV_W] = accs[g]


def _rope_tables(T):
    rows = T // GRID_W
    row = np.repeat(np.arange(rows), GRID_W).astype(np.float32)
    col = np.tile(np.arange(GRID_W), rows).astype(np.float32)
    nf = HEAD_DIM // 4
    inv = (ROPE_BASE ** (-np.arange(nf, dtype=np.float32) / nf)).astype(np.float32)
    ang = np.concatenate([row[:, None] * inv, col[:, None] * inv], -1).astype(np.float32)
    cos = np.cos(ang).astype(np.float32)
    sin = np.sin(ang).astype(np.float32)
    cos_h = np.concatenate([cos, cos], -1)
    sin_h = np.concatenate([-sin, sin], -1)
    return cos_h, sin_h


def _lat_attention(proj, cache_k, cache_v, sink, *, T, n_seq, row_blk0):
    nb = T // BLOCK
    P = cache_k.shape[1]
    cos_h, sin_h = _rope_tables(T)
    cq = jnp.asarray(np.tile(cos_h, (1, ATTN_HEADS)))
    sq = jnp.asarray(np.tile(sin_h, (1, ATTN_HEADS)))
    ck = jnp.asarray(np.tile(cos_h, (1, ATTN_KV_HEADS)))
    sk = jnp.asarray(np.tile(sin_h, (1, ATTN_KV_HEADS)))
    r0 = row_blk0 * nb
    prv = lambda i: jnp.maximum(i - 1, 0)
    nxt = lambda i: jnp.minimum(i + 1, nb - 1)
    kcol, vcol = C_AK // KV_W, C_AV // KV_W
    kv = lambda col, f: pl.BlockSpec((BLOCK, KV_W), lambda b, i: (r0 + b * nb + f(i), col))
    tab = lambda w, f: pl.BlockSpec((BLOCK, w), lambda b, i: (f(i), 0))
    cur = lambda i: i
    return pl.pallas_call(
        functools.partial(_lat_attn_kernel, T=T),
        grid=(n_seq, nb),
        in_specs=[pl.BlockSpec((BLOCK, ATTN_W), lambda b, i: (r0 + b * nb + i, C_Q // ATTN_W)),
                  kv(kcol, prv), kv(kcol, cur), kv(kcol, nxt),
                  kv(vcol, prv), kv(vcol, cur), kv(vcol, nxt),
                  pl.BlockSpec((None, P, KV_W), lambda b, i: (b, 0, 0)),
                  pl.BlockSpec((None, P, KV_W), lambda b, i: (b, 0, 0)),
                  tab(ATTN_W, cur), tab(ATTN_W, cur),
                  tab(KV_W, prv), tab(KV_W, prv), tab(KV_W, cur), tab(KV_W, cur), tab(KV_W, nxt), tab(KV_W, nxt),
                  pl.BlockSpec((1, LANES), lambda b, i: (0, 0))],
        out_specs=pl.BlockSpec((BLOCK, ATTN_W), lambda b, i: (b * nb + i, 0)),
        out_shape=jax.ShapeDtypeStruct((n_seq * T, ATTN_W), F32),
        compiler_params=pltpu.CompilerParams(dimension_semantics=("arbitrary", "arbitrary"),
                                             vmem_limit_bytes=VMEM_LIMIT),
        name="latent_attention",
    )(proj, proj, proj, proj, proj, proj, proj, cache_k, cache_v, cq, sq, ck, sk, ck, sk, ck, sk, sink)


def _layer_norm(y, g, b):
    mu = jnp.mean(y, axis=-1, keepdims=True)
    d = y - mu
    var = jnp.mean(d * d, axis=-1, keepdims=True)
    return d * lax.rsqrt(var + LN_EPS) * g + b


def _outproj_kernel(ac_ref, bc_ref, cc_ref, al_ref, bl_ref, cl_ref, xc_ref, xl_ref, mod_ref, w_ref,
                    g_ref, be_ref, x1_ref, h2_ref, *, tm, n_ctx_tiles):
    r1, r2 = RWKV_W, RWKV_W + HYENA_W
    sub = 128

    def tile(a_ref, b_ref, c_ref, x_ref):
        ms = []
        for s in range(tm // sub):
            rows = slice(s * sub, (s + 1) * sub)
            ms.append(jnp.dot(a_ref[rows, :].astype(BF16), w_ref[0:r1, :], preferred_element_type=F32)
                      + jnp.dot(b_ref[rows, :].astype(BF16), w_ref[r1:r2, :], preferred_element_type=F32)
                      + jnp.dot(c_ref[rows, :].astype(BF16), w_ref[r2:, :], preferred_element_type=F32))
        for s in range(tm // sub):
            rows = slice(s * sub, (s + 1) * sub)
            blk = (s * sub) // MOD_BLK
            gate1 = mod_ref[blk, 2:3, :]
            shift2 = mod_ref[blk, 3:4, :]
            scale2 = mod_ref[blk, 4:5, :]
            x1 = _layer_norm(DN_ALPHA * x_ref[rows, :] + gate1 * ms[s], g_ref[...], be_ref[...])
            x1_ref[rows, :] = x1
            h2_ref[rows, :] = (x1 * (1.0 + scale2) + shift2).astype(BF16)

    pl.when(pl.program_id(0) < n_ctx_tiles)(lambda: tile(ac_ref, bc_ref, cc_ref, xc_ref))
    pl.when(pl.program_id(0) >= n_ctx_tiles)(lambda: tile(al_ref, bl_ref, cl_ref, xl_ref))


def _out_projection(mix_ctx, mix_lat, x_src, mod, w_out, l, g, b, tm=256):
    xc, xl, n_ctx, n_lat, lat_off = x_src
    N = n_ctx + n_lat
    nsub = tm // MOD_BLK
    nct = n_ctx // tm
    nlt = n_lat // tm
    row = lambda w: pl.BlockSpec((tm, w), lambda i: (i, 0))
    ctx = lambda w: _two_source_specs(tm, w, nct, nlt, 0)[0]
    lat = lambda w: _two_source_specs(tm, w, nct, nlt, 0)[1]
    xc_spec, xl_spec = _two_source_specs(tm, D_MODEL, nct, nlt, lat_off // tm)
    full = lambda r: pl.BlockSpec((r, D_MODEL), lambda i: (0, 0))
    return pl.pallas_call(
        functools.partial(_outproj_kernel, tm=tm, n_ctx_tiles=nct),
        grid=(N // tm,),
        in_specs=[ctx(RWKV_W), ctx(HYENA_W), ctx(ATTN_W), lat(RWKV_W), lat(HYENA_W), lat(ATTN_W),
                  xc_spec, xl_spec,
                  pl.BlockSpec((nsub, 6, D_MODEL), lambda i: (i, 0, 0)),
                  pl.BlockSpec((None, D_MODEL, D_MODEL), lambda i: (l, 0, 0)), full(1), full(1)],
        out_specs=[row(D_MODEL), row(D_MODEL)],
        out_shape=[jax.ShapeDtypeStruct((N, D_MODEL), F32), jax.ShapeDtypeStruct((N, D_MODEL), BF16)],
        compiler_params=pltpu.CompilerParams(dimension_semantics=("arbitrary",),
                                             vmem_limit_bytes=VMEM_LIMIT),
        name="out_projection",
    )(*mix_ctx, *mix_lat, xc, xl, mod, w_out, g.reshape(1, D_MODEL), b.reshape(1, D_MODEL))


HALO = 16


def _ffn_kernel(hp_ref, h_ref, hn_ref, x1_ref, mod_ref, wa_ref, wb_ref, cwa_ref, cwb_ref, wd_ref, g_ref, be_ref,
                o_ref, hx_scr, ua_scr, ub_scr, acc_scr, *, tm, tf, nsub, t0, n_ctx, s_ctx, s_lat):
    i = pl.program_id(0) + t0
    j = pl.program_id(1)

    @pl.when(j == 0)
    def _():
        hx_scr[0:HALO, :] = hp_ref[...]
        hx_scr[HALO:HALO + tm, :] = h_ref[...]
        hx_scr[HALO + tm:, :] = hn_ref[...]
        acc_scr[...] = jnp.zeros_like(acc_scr)

    hx = hx_scr[...]
    ua_scr[...] = jnp.dot(hx, wa_ref[...], preferred_element_type=F32)
    ub_scr[...] = jnp.dot(hx, wb_ref[...], preferred_element_type=F32)
    grow = i * tm + lax.broadcasted_iota(jnp.int32, (tm, 1), 0)
    pos = jnp.where(grow < n_ctx, grow % s_ctx, (grow - n_ctx) % s_lat)
    slen = jnp.where(grow < n_ctx, s_ctx, s_lat)
    mprev = jnp.where(pos == 0, 0.0, 1.0).astype(F32)
    mnext = jnp.where(pos == slen - 1, 0.0, 1.0).astype(F32)

    def conv(u_scr, cw_ref):
        up = u_scr[HALO - 1:HALO - 1 + tm, :] * mprev
        uc = u_scr[HALO:HALO + tm, :]
        un = u_scr[HALO + 1:HALO + 1 + tm, :] * mnext
        return up * cw_ref[0:1, :] + uc * cw_ref[1:2, :] + un * cw_ref[2:3, :]

    a = conv(ua_scr, cwa_ref)
    b = conv(ub_scr, cwb_ref)
    act = (a * _sigmoid(a) * b).astype(BF16)
    acc_scr[...] += jnp.dot(act, wd_ref[...], preferred_element_type=F32)

    @pl.when(j == pl.num_programs(1) - 1)
    def _():
        for s in range(nsub):
            rows = slice(s * MOD_BLK, (s + 1) * MOD_BLK)
            gate2 = mod_ref[s, 5:6, :]
            o_ref[rows, :] = _layer_norm(DN_ALPHA * x1_ref[rows, :] + gate2 * acc_scr[rows, :],
                                         g_ref[...], be_ref[...])


def _ffn(h2, x1, mod, wa, wb, cw, wd, l, g, b, *, n_ctx, s_ctx, s_lat, row0=0, n_rows=None, tm=512, tf=512):
    N = x1.shape[0]
    n_rows = N if n_rows is None else n_rows
    t0 = row0 // tm
    nsub = tm // MOD_BLK
    nh = tm // HALO
    last_h = N // HALO - 1
    pad = ((0, 0), (0, D_FF_PAD - D_FF))
    cwa, cwb = jnp.pad(cw[:, :D_FF], pad), jnp.pad(cw[:, D_FF:], pad)
    return pl.pallas_call(
        functools.partial(_ffn_kernel, tm=tm, tf=tf, nsub=nsub, t0=t0, n_ctx=n_ctx, s_ctx=s_ctx, s_lat=s_lat),
        grid=(n_rows // tm, D_FF_PAD // tf),
        in_specs=[pl.BlockSpec((HALO, D_MODEL), lambda i, j: (jnp.maximum((i + t0) * nh - 1, 0), 0)),
                  pl.BlockSpec((tm, D_MODEL), lambda i, j: (i + t0, 0)),
                  pl.BlockSpec((HALO, D_MODEL), lambda i, j: (jnp.minimum((i + t0 + 1) * nh, last_h), 0)),
                  pl.BlockSpec((tm, D_MODEL), lambda i, j: (i + t0, 0)),
                  pl.BlockSpec((nsub, 6, D_MODEL), lambda i, j: (i + t0, 0, 0)),
                  pl.BlockSpec((None, D_MODEL, tf), lambda i, j: (l, 0, j)),
                  pl.BlockSpec((None, D_MODEL, tf), lambda i, j: (l, 0, j)),
                  pl.BlockSpec((3, tf), lambda i, j: (0, j)),
                  pl.BlockSpec((3, tf), lambda i, j: (0, j)),
                  pl.BlockSpec((None, tf, D_MODEL), lambda i, j: (l, j, 0)),
                  pl.BlockSpec((1, D_MODEL), lambda i, j: (0, 0)),
                  pl.BlockSpec((1, D_MODEL), lambda i, j: (0, 0))],
        out_specs=pl.BlockSpec((tm, D_MODEL), lambda i, j: (i, 0)),
        out_shape=jax.ShapeDtypeStruct((n_rows, D_MODEL), F32),
        scratch_shapes=[pltpu.VMEM((tm + 2 * HALO, D_MODEL), BF16),
                        pltpu.VMEM((tm + 2 * HALO, tf), F32),
                        pltpu.VMEM((tm + 2 * HALO, tf), F32),
                        pltpu.VMEM((tm, D_MODEL), F32)],
        compiler_params=pltpu.CompilerParams(dimension_semantics=("arbitrary", "arbitrary"),
                                             vmem_limit_bytes=VMEM_LIMIT),
        name="conv_ffn",
    )(h2, h2, h2, x1, mod, wa, wb, cwa, cwb, wd, g.reshape(1, D_MODEL), b.reshape(1, D_MODEL))


def _prep_up_kernel(a_ref, b_ref, wa_ref, wb_ref):
    zeros = jnp.zeros((a_ref.shape[0], D_FF_PAD - D_FF), BF16)
    wa_ref[:, :D_FF] = a_ref[...].astype(BF16)
    wa_ref[:, D_FF:] = zeros
    wb_ref[:, :D_FF] = b_ref[...].astype(BF16)
    wb_ref[:, D_FF:] = zeros


def _prep_down_kernel(w_ref, o_ref, *, n_real):
    keep = pl.program_id(1) < n_real
    o_ref[...] = jnp.where(keep, w_ref[...], 0.0).astype(BF16)


def _prep_in_kernel(w_ref, w1_ref, a1_ref, o_ref):
    n = w_ref.shape[-1]
    o_ref[:, :C_Q] = w_ref[:, :C_Q].astype(BF16)
    for g in range(GQA_GROUP):
        for h in range(ATTN_KV_HEADS):
            src = C_Q + (h * GQA_GROUP + g) * HEAD_DIM
            dst = C_Q + (g * ATTN_KV_HEADS + h) * HEAD_DIM
            o_ref[:, dst:dst + HEAD_DIM] = w_ref[:, src:src + HEAD_DIM].astype(BF16)
    o_ref[:, C_AK:n] = w_ref[:, C_AK:n].astype(BF16)
    o_ref[:, n:] = jnp.concatenate([w1_ref[0], w1_ref[1], a1_ref[0], a1_ref[1]], axis=1).astype(BF16)


def _cast_kernel(w_ref, o_ref):
    o_ref[...] = w_ref[...].astype(BF16)


def _prep_weights(w_in, rwkv_w1, rwkv_a1, w_out, ffn_w_up, ffn_w_down):
    L, D = w_in.shape[0], D_MODEL
    cp = lambda n: pltpu.CompilerParams(dimension_semantics=("arbitrary",) * n, vmem_limit_bytes=VMEM_LIMIT)
    tr = 128
    half = jax.ShapeDtypeStruct((L, D, D_FF_PAD), BF16)
    wa, wb = pl.pallas_call(
        _prep_up_kernel,
        grid=(L, D // tr),
        in_specs=[pl.BlockSpec((None, tr, D_FF), lambda l, i: (l, i, 0)),
                  pl.BlockSpec((None, tr, D_FF), lambda l, i: (l, i, 1))],
        out_specs=[pl.BlockSpec((None, tr, D_FF_PAD), lambda l, i: (l, i, 0))] * 2,
        out_shape=[half, half], compiler_params=cp(2), name="prep_ffn_up",
    )(ffn_w_up, ffn_w_up)
    nr = D_FF // LANES
    wd = pl.pallas_call(
        functools.partial(_prep_down_kernel, n_real=nr),
        grid=(L, D_FF_PAD // LANES),
        in_specs=[pl.BlockSpec((None, LANES, D), lambda l, j: (l, jnp.minimum(j, nr - 1), 0))],
        out_specs=pl.BlockSpec((None, LANES, D), lambda l, j: (l, j, 0)),
        out_shape=jax.ShapeDtypeStruct((L, D_FF_PAD, D), BF16), compiler_params=cp(2), name="prep_ffn_down",
    )(ffn_w_down)
    tr2 = 256
    n_in = w_in.shape[-1]
    w_ext = pl.pallas_call(
        _prep_in_kernel,
        grid=(L, D // tr2),
        in_specs=[pl.BlockSpec((None, tr2, n_in), lambda l, i: (l, i, 0)),
                  pl.BlockSpec((None, 2, tr2, LORA), lambda l, i: (l, 0, i, 0)),
                  pl.BlockSpec((None, 2, tr2, LORA), lambda l, i: (l, 0, i, 0))],
        out_specs=pl.BlockSpec((None, tr2, IN_W_EXT), lambda l, i: (l, i, 0)),
        out_shape=jax.ShapeDtypeStruct((L, D, IN_W_EXT), BF16), compiler_params=cp(2), name="prep_in",
    )(w_in, rwkv_w1, rwkv_a1)
    hb0 = (RWKV_W + HYENA_W) // HEAD_DIM

    def src_rows(i):
        a = i - hb0
        perm = hb0 + (a % ATTN_KV_HEADS) * GQA_GROUP + a // ATTN_KV_HEADS
        return jnp.where((a >= 0) & (a < ATTN_HEADS), perm, i)

    wo = pl.pallas_call(
        _cast_kernel,
        grid=(L, D // HEAD_DIM),
        in_specs=[pl.BlockSpec((None, HEAD_DIM, D), lambda l, i: (l, src_rows(i), 0))],
        out_specs=pl.BlockSpec((None, HEAD_DIM, D), lambda l, i: (l, i, 0)),
        out_shape=jax.ShapeDtypeStruct((L, D, D), BF16), compiler_params=cp(2), name="prep_out",
    )(w_out)
    return w_ext, wo, wa, wb, wd


def kernel(x_prompt, x_sample, c, state_rwkv, cache_k, cache_v, c_ctx, w_mod, b_mod, w_in, rwkv_w0, rwkv_w1, rwkv_w2, rwkv_a0, rwkv_a1, rwkv_a2, rwkv_k_k, rwkv_k_a, rwkv_r_k, rwkv_lnx_g, rwkv_lnx_b, hy_short_w, hy_f_w1, hy_f_b1, hy_f_freq1, hy_f_w2, hy_f_b2, hy_f_freq2, hy_f_w3, hy_bias, attn_sink, w_out, ln1_g, ln1_b, ffn_w_up, ffn_conv_w, ffn_w_down, ln2_g, ln2_b):
    params = dict(rwkv_w0=rwkv_w0, rwkv_w2=rwkv_w2, rwkv_a0=rwkv_a0, rwkv_a2=rwkv_a2,
                  rwkv_k_k=rwkv_k_k, rwkv_k_a=rwkv_k_a, rwkv_r_k=rwkv_r_k,
                  rwkv_lnx_g=rwkv_lnx_g, rwkv_lnx_b=rwkv_lnx_b,
                  hy_short_w=hy_short_w, hy_f_w1=hy_f_w1, hy_f_b1=hy_f_b1, hy_f_freq1=hy_f_freq1,
                  hy_f_w2=hy_f_w2, hy_f_b2=hy_f_b2, hy_f_freq2=hy_f_freq2, hy_f_w3=hy_f_w3,
                  hy_bias=hy_bias)
    L = w_in.shape[0]
    Bc, Sc, D = x_prompt.shape
    Bl, Tl, _ = x_sample.shape
    P = cache_k.shape[2]
    n_ctx = Bc * Sc
    n_lat = Bl * Tl
    N = n_ctx + n_lat
    assert Sc % MOD_BLK == 0 and Tl % MOD_BLK == 0 and n_ctx % Tl == 0

    n_cond = 1 + Bl
    cond = jnp.zeros((16, D), F32).at[0].set(c_ctx).at[1:n_cond].set(c)
    mod_all = _modulation(cond, w_mod, b_mod)
    mod_ctx = jnp.broadcast_to(mod_all[:, 0:1, :], (L, n_ctx // MOD_BLK, 6 * D))
    mod_lat = jnp.broadcast_to(mod_all[:, 1:n_cond, None, :], (L, Bl, Tl // MOD_BLK, 6 * D))
    mod_blk = jnp.concatenate([mod_ctx, mod_lat.reshape(L, n_lat // MOD_BLK, 6 * D)], axis=1)
    mod_blk = mod_blk.reshape(L, N // MOD_BLK, 6, D)

    cm_c, sm_c = (jnp.asarray(m, F32) for m in _dft_mats(Sc))
    cm_l, sm_l = (jnp.asarray(m, F32) for m in _dft_mats(Tl))
    sink_pad = jnp.zeros((L, 1, LANES), F32).at[:, 0, :ATTN_HEADS].set(attn_sink)

    w_ext, wo, wa, wb, wd = _prep_weights(w_in, rwkv_w1, rwkv_a1, w_out, ffn_w_up, ffn_w_down)
    x_src = (x_prompt.reshape(n_ctx, D), x_sample.reshape(n_lat, D), n_ctx, n_lat, 0)
    states, keys, vals = [], [], []
    for l in range(L):
        p = {n: a[l] for n, a in params.items()}
        proj = _in_projection(x_src, mod_blk[l], w_ext, l)
        keys.append(proj[:n_ctx, C_AK:C_AK + KV_W].reshape(Bc, Sc, ATTN_KV_HEADS, HEAD_DIM))
        vals.append(proj[:n_ctx, C_AV:C_AV + KV_W].reshape(Bc, Sc, ATTN_KV_HEADS, HEAD_DIM))

        ya_c, s_fin = _rwkv(proj, p, None, T=Sc, n_seq=Bc, row_blk0=0, nsq=RWKV_CTX_SEQS)
        ya_l, _ = _rwkv(proj, p, state_rwkv[:, l], T=Tl, n_seq=Bl, row_blk0=n_ctx // Tl)
        states.append(s_fin)

        yb_c = _hyena(proj, p, _hyena_filter(p, Sc, cm_c, sm_c), cm_c, sm_c, L=Sc, n_seq=Bc, row_blk0=0,
                      nsq=HYENA_CTX_SEQS)
        yb_l = _hyena(proj, p, _hyena_filter(p, Tl, cm_l, sm_l), cm_l, sm_l, L=Tl, n_seq=Bl,
                      row_blk0=n_ctx // Tl)

        yc_c = _ctx_attention(proj, sink_pad[l], S=Sc, n_seq=Bc)
        yc_l = _lat_attention(proj, cache_k[:, l].reshape(Bl, P, KV_W), cache_v[:, l].reshape(Bl, P, KV_W),
                              sink_pad[l], T=Tl, n_seq=Bl, row_blk0=n_ctx // Tl)

        x1, h2 = _out_projection((ya_c, yb_c, yc_c), (ya_l, yb_l, yc_l), x_src, mod_blk[l],
                                 wo, l, ln1_g[l], ln1_b[l])
        ffn = functools.partial(_ffn, h2, x1, mod_blk[l], wa, wb, ffn_conv_w[l], wd, l, ln2_g[l], ln2_b[l],
                                n_ctx=n_ctx, s_ctx=Sc, s_lat=Tl)
        if l < L - 1:
            x = ffn()
            x_src = (x, x, n_ctx, n_lat, n_ctx)
        else:
            y_prompt = ffn(row0=0, n_rows=n_ctx).reshape(Bc, Sc, D)
            y_sample = ffn(row0=n_ctx, n_rows=n_lat).reshape(Bl, Tl, D)

    new_state = jnp.stack(states, 1)
    new_k = jnp.stack(keys, 1)
    new_v = jnp.stack(vals, 1)
    return (y_prompt, y_sample, new_state, new_k, new_v)
```

```python
import functools
import math

import numpy as np
import jax
import jax.numpy as jnp
from jax import lax
from jax.experimental import pallas as pl
from jax.experimental.pallas import tpu as pltpu

F32 = jnp.float32
BF16 = jnp.bfloat16

D_MODEL = 2048
DEPTH = 4
GRID_W = 64
HEAD_DIM = 64
RWKV_W = 768
RWKV_HEADS = 12
HYENA_W = 512
ATTN_W = 768
ATTN_HEADS = 12
ATTN_KV_HEADS = 4
GQA_GROUP = 3
KV_W = 256
LORA = 64
GN_EPS = 64e-5
FILTER_EMB = 33
FILTER_HID = 64
HY_MIN_DECAY = math.log(1e-2) / 1.5
HY_MAX_DECAY = math.log(1e-2) / 0.3
WINDOW = 128
BLOCK = 128
ROPE_BASE = 10000.0
D_FF = 5504
LN_EPS = 1e-5
DN_ALPHA = (2 * DEPTH) ** 0.25

C_R, C_K, C_V, C_G = 0, 768, 1536, 2304
C_HY = 3072
C_Q, C_AK, C_AV = 4608, 5376, 5632
C_LORA = 5888
IN_W_EXT = 6144

MOD_BLK = 256
CHUNK = 64
RWKV_CB = 8
RWKV_CTX_SEQS = 4
HYENA_CTX_SEQS = 4
LANES = 128
D_FF_PAD = 5632
VMEM_LIMIT = 56 * 1024 * 1024


def _dot(a, b):
    return jnp.dot(a.astype(BF16), b.astype(BF16), preferred_element_type=F32)


def _dot_nt(a, b):
    return lax.dot_general(a.astype(BF16), b.astype(BF16), (((1,), (1,)), ((), ())),
                           preferred_element_type=F32)


def _split(x):
    hi = x.astype(BF16)
    lo = (x - hi.astype(F32)).astype(BF16)
    return hi, lo


def _dot3_dims(a, b, dims):
    ah, al = _split(a)
    bh, bl = _split(b)
    dg = functools.partial(lax.dot_general, dimension_numbers=dims, preferred_element_type=F32)
    return dg(ah, bh) + (dg(ah, bl) + dg(al, bh))


_NN = (((1,), (0,)), ((), ()))
_NT = (((1,), (1,)), ((), ()))
_TN = (((0,), (0,)), ((), ()))


def _dot3(a, b):
    return _dot3_dims(a, b, _NN)


def _sigmoid(x):
    return 1.0 / (1.0 + jnp.exp(-x))


def _softplus(x):
    return jnp.maximum(x, 0.0) + jnp.log(1.0 + jnp.exp(-jnp.abs(x)))


def _mod_kernel(c_ref, w_ref, b_ref, o_ref):
    c = c_ref[...]
    s = c * _sigmoid(c)
    o_ref[...] = _dot(s, w_ref[...]) + b_ref[...]


def _modulation(cond, w_mod, b_mod):
    L = w_mod.shape[0]
    R = cond.shape[0]
    tn = 1024
    return pl.pallas_call(
        _mod_kernel,
        grid=(L, 6 * D_MODEL // tn),
        in_specs=[pl.BlockSpec((R, D_MODEL), lambda l, j: (0, 0)),
                  pl.BlockSpec((None, D_MODEL, tn), lambda l, j: (l, 0, j)),
                  pl.BlockSpec((None, 1, tn), lambda l, j: (l, 0, j))],
        out_specs=pl.BlockSpec((None, R, tn), lambda l, j: (l, 0, j)),
        out_shape=jax.ShapeDtypeStruct((L, R, 6 * D_MODEL), F32),
        compiler_params=pltpu.CompilerParams(dimension_semantics=("arbitrary", "arbitrary"),
                                             vmem_limit_bytes=VMEM_LIMIT),
        name="modulation",
    )(cond, w_mod, b_mod.reshape(L, 1, 6 * D_MODEL))


def _two_source_specs(tm, width, nct, nlt, lat_off):
    ctx = pl.BlockSpec((tm, width), lambda i: (jnp.minimum(i, nct - 1), 0))
    lat = pl.BlockSpec((tm, width), lambda i: (jnp.clip(i - nct, 0, nlt - 1) + lat_off, 0))
    return ctx, lat


def _mod_spec(l, tm, n_ctx, lat_len, t0=0, rank=1):
    def row(i):
        tok = (i + t0) * tm
        return jnp.where(tok < n_ctx, 0, 1 + (tok - n_ctx) // lat_len)

    if rank == 1:
        return pl.BlockSpec((None, 1, 6, D_MODEL), lambda i: (l, row(i), 0, 0))
    return pl.BlockSpec((None, 1, 6, D_MODEL), lambda i, j: (l, row(i), 0, 0))


def _inproj_kernel(xc_ref, xl_ref, mod_ref, w_ref, o_ref, h_scr, *, tn, n_ctx_tiles):
    shift = mod_ref[0, 0:1, :]
    scale = mod_ref[0, 1:2, :]

    def modulate(x_ref):
        h_scr[...] = (x_ref[...] * (1.0 + scale) + shift).astype(BF16)

    pl.when(pl.program_id(0) < n_ctx_tiles)(lambda: modulate(xc_ref))
    pl.when(pl.program_id(0) >= n_ctx_tiles)(lambda: modulate(xl_ref))
    h = h_scr[...]
    for j in range(IN_W_EXT // tn):
        cols = slice(j * tn, (j + 1) * tn)
        o_ref[:, cols] = jnp.dot(h, w_ref[:, cols], preferred_element_type=F32)


def _in_projection(x_src, mod, w_ext, l, tn=512):
    xc, xl, n_ctx, n_lat, lat_off, lat_len = x_src
    N = n_ctx + n_lat
    tm = MOD_BLK
    nct = n_ctx // tm
    ctx_spec, lat_spec = _two_source_specs(tm, D_MODEL, nct, n_lat // tm, lat_off // tm)
    return pl.pallas_call(
        functools.partial(_inproj_kernel, tn=tn, n_ctx_tiles=nct),
        grid=(N // tm,),
        in_specs=[ctx_spec, lat_spec,
                  _mod_spec(l, tm, n_ctx, lat_len),
                  pl.BlockSpec((None, D_MODEL, IN_W_EXT), lambda i: (l, 0, 0), pipeline_mode=pl.Buffered(1))],
        out_specs=pl.BlockSpec((tm, IN_W_EXT), lambda i: (i, 0)),
        out_shape=jax.ShapeDtypeStruct((N, IN_W_EXT), F32),
        scratch_shapes=[pltpu.VMEM((tm, D_MODEL), BF16)],
        compiler_params=pltpu.CompilerParams(dimension_semantics=("arbitrary",),
                                             vmem_limit_bytes=VMEM_LIMIT),
        name="in_projection",
    )(xc, xl, mod, w_ext)


PREC = dict(seg=1, cum=2, a=1, neu=1, post=1, mn=1, b_o=1, b_s=1, gn=2)


def _dotp(a, b, dims, passes):
    dg = functools.partial(lax.dot_general, dimension_numbers=dims, preferred_element_type=F32)
    if passes == 1:
        return dg(a.astype(BF16), b.astype(BF16))
    ah, al = _split(a)
    if passes == 2:
        bh = b.astype(BF16)
        return dg(ah, bh) + dg(al, bh)
    bh, bl = _split(b)
    return dg(ah, bh) + (dg(ah, bl) + dg(al, bh))


def _dotp_rhs(a, b, passes):
    dg = functools.partial(lax.dot_general, dimension_numbers=_NN, preferred_element_type=F32)
    ab = a.astype(BF16)
    if passes == 1:
        return dg(ab, b.astype(BF16))
    bh, bl = _split(b)
    return dg(ab, bh) + dg(ab, bl)


def _rwkv_kernel(*refs, T, nsq, has_init, cb):
    (r_ref, k_ref, v_ref, g_ref, lo_ref, w0_ref, w2_ref, a0_ref, a2_ref,
     kk_ref, ka_ref, rk_ref, lng_ref, lnb_ref) = refs[:14]
    pos = 14
    init_ref = None
    if has_init:
        init_ref = refs[pos]
        pos += 1
    y_ref, sfin_ref = refs[pos:pos + 2]
    rp_scr, ov_scr, m_scr, n_scr, o_scr, bonus_scr, s_scr = refs[pos + 2:]

    nC = T // CHUNK
    nCs = nC // nsq
    C = CHUNK
    H = HEAD_DIM
    P = PREC
    C2 = 2 * C
    row = lax.broadcasted_iota(jnp.int32, (C, C), 0)
    col = lax.broadcasted_iota(jnp.int32, (C, C), 1)
    tri_f = jnp.where(col <= row, 1.0, 0.0).astype(F32)
    lrow = lax.broadcasted_iota(jnp.int32, (LANES, LANES), 0)
    lcol = lax.broadcasted_iota(jnp.int32, (LANES, LANES), 1)
    same_head = (lrow // H) == (lcol // H)
    head_ones = jnp.where(same_head, 1.0, 0.0).astype(F32)
    eye = lrow == lcol
    rt_, ct_ = lrow % C, lcol % C
    strict_m = [same_head & (ct_ < rt_), same_head & (ct_ > rt_)]
    incl_m = [same_head & (ct_ <= rt_), same_head & (ct_ >= rt_)]
    lane = lax.broadcasted_iota(jnp.int32, (1, LANES), 1)
    hmask = [lane < H, lane >= H]

    def stack_heads(x):
        return jnp.concatenate([jnp.where(hmask[0], x, 0.0), jnp.where(hmask[1], x, 0.0)], axis=0)

    kkw = kk_ref[...]
    kaw = ka_ref[...]
    rkw = rk_ref[...]
    zl = jnp.zeros((LORA, LANES), F32)
    lora_w = jnp.concatenate(
        [jnp.concatenate([w if j == i else zl for j in range(4)], axis=1)
         for i, w in enumerate([w2_ref[0], w2_ref[1], a2_ref[0], a2_ref[1]])], axis=0).astype(BF16)

    def phase_a(ci, carry):
        units = []
        for cc in range(cb):
            c = ci * cb + cc
            rows = pl.ds(pl.multiple_of(c * C, C), C)
            r = r_ref[rows, :]
            k = k_ref[rows, :]
            v = v_ref[rows, :]
            lo = lo_ref[rows, :]
            kkraw = k * kkw
            ss = _dotp(kkraw * kkraw, head_ones, _NN, P['seg'])
            kkn = kkraw / jnp.maximum(jnp.sqrt(ss), 1e-12)
            bonus = jnp.zeros((C, LANES), F32)
            vm = stack_heads(v)
            lo_in = jnp.concatenate([jnp.tanh(lo[:, :2 * LORA]), lo[:, 2 * LORA:]], axis=1)
            lo_out = _dot(lo_in, lora_w)
            lds, alphas = [], []
            for d in range(2):
                wl = w0_ref[d] + lo_out[:, d * LANES:(d + 1) * LANES]
                lds.append(-jnp.exp(-_softplus(-wl) - 0.5))
                alphas.append(_sigmoid(a0_ref[d] + lo_out[:, (2 + d) * LANES:(3 + d) * LANES]))
            pref = _dotp_rhs(tri_f, jnp.concatenate(lds, axis=1), P['cum'])
            for d in range(2):
                ld, alpha = lds[d], alphas[d]
                kd = k * (1.0 + (alpha - 1.0) * kaw)
                bb = kkn * alpha
                bonus = bonus + _dotp(r * kd * rkw, head_ones, _NN, P['seg']) * v
                pf = pref[:, d * LANES:(d + 1) * LANES]
                g_incl = pf if d == 0 else pf[C - 1:C, :] - pf + ld
                g_excl = g_incl - ld
                g_end = g_incl[C - 1:C, :] if d == 0 else g_incl[0:1, :]
                ginv = jnp.exp(-g_incl)
                dec_end = jnp.exp(g_end - g_incl)
                units.append(dict(
                    d=d, c=c, vm=vm,
                    at=stack_heads(-kkn * jnp.exp(g_excl)), rt=stack_heads(r * jnp.exp(g_incl)),
                    bt=stack_heads(bb * ginv), kt=stack_heads(kd * ginv),
                    bend=stack_heads(bb * dec_end), kend=stack_heads(kd * dec_end), gend=jnp.exp(g_end)))
            bonus_scr[rows, :] = bonus
            o_scr[rows, :] = jnp.zeros((C, LANES), F32)

        for u in units:
            ar = jnp.concatenate([u['at'], u['rt']], axis=0)
            bk = jnp.concatenate([u['bt'], u['kt']], axis=0)
            u['a'] = _dotp(ar, bk, _NT, P['a'])
        for u in units:
            a = u['a']
            st, ic = strict_m[u['d']], incl_m[u['d']]
            u['p'] = jnp.where(st, a[:C2, :C2], 0.0)
            u['a_ak'] = jnp.where(st, a[:C2, C2:], 0.0)
            u['a_rb'] = jnp.where(ic, a[C2:, :C2], 0.0)
            u['a_rk'] = jnp.where(ic, a[C2:, C2:], 0.0)
        rvs = [_dotp(u['a_ak'], u['vm'], _NN, P['post']) for u in units]
        sq = [_dotp(u['p'], u['p'], _NN, P['neu']) for u in units]
        for i, u in enumerate(units):
            u['x'] = jnp.where(eye, 1.0, 0.0) + u['p']
            u['p'] = sq[i]
        for it in range(1, 6):
            res = [_dotp(u['p'], jnp.concatenate([u['x'], u['p']], axis=1) if it < 5 else u['x'], _NN, P['neu'])
                   for u in units]
            for i, u in enumerate(units):
                u['x'] = u['x'] + res[i][:, :C2]
                if it < 5:
                    u['p'] = res[i][:, C2:]
        rhs = [_dotp(u['x'], jnp.concatenate([u['at'], rvs[i]], axis=1), _NN, P['post'])
               for i, u in enumerate(units)]
        for i, u in enumerate(units):
            u['rhs'] = rhs[i]
        xs = [_dotp(jnp.concatenate([u['a_rb'], u['a_rk']], axis=1),
                    jnp.concatenate([u['rhs'], jnp.concatenate([jnp.zeros_like(u['vm']), u['vm']], axis=1)], axis=0),
                    _NN, P['post']) for u in units]
        ms = [_dotp(u['rhs'][:, :LANES], u['bend'], _TN, P['mn']) for u in units]
        ns = [_dotp(jnp.concatenate([u['rhs'][:, LANES:], u['vm']], axis=0),
                    jnp.concatenate([u['bend'], u['kend']], axis=0), _TN, P['mn']) for u in units]
        for i, u in enumerate(units):
            d, c = u['d'], u['c']
            rp = u['rt'] + xs[i][:, :LANES]
            ov = xs[i][:, LANES:]
            rp_scr[d, c] = rp[:C] + rp[C:]
            ov_scr[d, c] = ov[:C] + ov[C:]
            m_scr[d, c] = jnp.where(eye, jnp.broadcast_to(u['gend'], (LANES, LANES)), 0.0) + ms[i]
            n_scr[d, c] = ns[i]
        return carry

    lax.fori_loop(0, nC // cb, phase_a, 0)

    zero_blk = jnp.zeros((H, H), F32)
    for q in range(nsq):
        for d in range(2):
            if has_init:
                s_scr[q, d] = jnp.concatenate([jnp.concatenate([init_ref[q, d, 0], zero_blk], axis=1),
                                               jnp.concatenate([zero_blk, init_ref[q, d, 1]], axis=1)], axis=0)
            else:
                s_scr[q, d] = jnp.zeros((LANES, LANES), F32)

    def phase_b(i, carry):
        idx = [(q, d, q * nCs + (i if d == 0 else nCs - 1 - i)) for q in range(nsq) for d in range(2)]
        ss = [s_scr[q, d] for q, d, c in idx]
        os_ = [_dotp(rp_scr[d, c], ss[j], _NT, P['b_o']) for j, (q, d, c) in enumerate(idx)]
        sn = [_dotp(ss[j], m_scr[d, c], _NN, P['b_s']) for j, (q, d, c) in enumerate(idx)]
        for j, (q, d, c) in enumerate(idx):
            s_scr[q, d] = sn[j] + n_scr[d, c]
            rows = pl.ds(c * C if isinstance(c, int) else pl.multiple_of(c * C, C), C)
            o_scr[rows, :] = o_scr[rows, :] + (os_[j] + ov_scr[d, c])
        return carry

    if nCs <= 4:
        for i in range(nCs):
            phase_b(i, 0)
    else:
        lax.fori_loop(0, nCs, phase_b, 0, unroll=4)

    for q in range(nsq):
        for d in range(2):
            s = s_scr[q, d]
            sfin_ref[q, d, 0] = s[:H, :H]
            sfin_ref[q, d, 1] = s[H:, H:]

    lng = lng_ref[...]
    lnb = lnb_ref[...]

    rb = min(T, 256)

    def phase_c(c, carry):
        rows = pl.ds(pl.multiple_of(c * rb, rb), rb)
        o = o_scr[rows, :]
        st = _dotp(jnp.concatenate([o, o * o], axis=0), head_ones, _NN, P['gn']) * (1.0 / H)
        mu = st[:rb]
        var = st[rb:] - mu * mu
        on = (o - mu) * lax.rsqrt(var + GN_EPS) * lng + lnb
        y_ref[rows, :] = (on + bonus_scr[rows, :]) * _sigmoid(g_ref[rows, :])
        return carry

    lax.fori_loop(0, T // rb, phase_c, 0)


def _rwkv(proj, p, init, *, T, n_seq, row_blk0, nsq=1):
    seq_len, T = T, nsq * T
    nC = T // CHUNK
    npair = RWKV_HEADS // 2
    has_init = init is not None
    cb = lambda off: (lambda b, j: (b + row_blk0, off // LANES + j))
    vec = lambda b, j: (0, j)
    in_specs = [pl.BlockSpec((T, LANES), cb(C_R)),
                pl.BlockSpec((T, LANES), cb(C_K)),
                pl.BlockSpec((T, LANES), cb(C_V)),
                pl.BlockSpec((T, LANES), cb(C_G)),
                pl.BlockSpec((T, 4 * LORA), lambda b, j: (b + row_blk0, C_LORA // (4 * LORA))),
                pl.BlockSpec((2, 1, LANES), lambda b, j: (0, 0, j)),
                pl.BlockSpec((2, LORA, LANES), lambda b, j: (0, 0, j)),
                pl.BlockSpec((2, 1, LANES), lambda b, j: (0, 0, j)),
                pl.BlockSpec((2, LORA, LANES), lambda b, j: (0, 0, j)),
                pl.BlockSpec((1, LANES), vec), pl.BlockSpec((1, LANES), vec),
                pl.BlockSpec((1, LANES), vec), pl.BlockSpec((1, LANES), vec),
                pl.BlockSpec((1, LANES), vec)]
    args = [proj, proj, proj, proj, proj,
            p['rwkv_w0'].reshape(2, 1, RWKV_W), p['rwkv_w2'], p['rwkv_a0'].reshape(2, 1, RWKV_W), p['rwkv_a2'],
            p['rwkv_k_k'].reshape(1, RWKV_W), p['rwkv_k_a'].reshape(1, RWKV_W),
            p['rwkv_r_k'].reshape(1, RWKV_W), p['rwkv_lnx_g'].reshape(1, RWKV_W),
            p['rwkv_lnx_b'].reshape(1, RWKV_W)]
    st_spec = pl.BlockSpec((nsq, 2, 2, HEAD_DIM, HEAD_DIM), lambda b, j: (b, 0, j, 0, 0))
    if has_init:
        in_specs.append(st_spec)
        args.append(init)
    sc_o = lambda: pltpu.VMEM((2, nC, CHUNK, LANES), F32)
    sc_s = lambda: pltpu.VMEM((2, nC, LANES, LANES), F32)
    return pl.pallas_call(
        functools.partial(_rwkv_kernel, T=T, nsq=nsq, has_init=has_init, cb=min(RWKV_CB, nC)),
        grid=(n_seq // nsq, npair),
        in_specs=in_specs,
        out_specs=[pl.BlockSpec((T, LANES), lambda b, j: (b, j)), st_spec],
        out_shape=[jax.ShapeDtypeStruct((n_seq * seq_len, RWKV_W), F32),
                   jax.ShapeDtypeStruct((n_seq, 2, RWKV_HEADS, HEAD_DIM, HEAD_DIM), F32)],
        scratch_shapes=[sc_o(), sc_o(), sc_s(), sc_s(),
                        pltpu.VMEM((T, LANES), F32), pltpu.VMEM((T, LANES), F32),
                        pltpu.VMEM((nsq, 2, LANES, LANES), F32)],
        compiler_params=pltpu.CompilerParams(dimension_semantics=("arbitrary", "arbitrary"),
                                             vmem_limit_bytes=VMEM_LIMIT),
        name="rwkv_T%d" % T,
    )(*args)


def _dft_mats(L):
    f = np.arange(L, dtype=np.int64)
    ph = (np.outer(f, f) % (2 * L)).astype(np.float64) * (math.pi / L)
    return np.cos(ph), -np.sin(ph)


def _filter_features(L):
    t = np.linspace(0.0, 1.0, L, dtype=np.float32)[:, None]
    bands = (FILTER_EMB - 1) // 2
    t_res = np.arange(L, dtype=np.float32)[:, None]
    f = np.linspace(1e-4, bands - 1, bands, dtype=np.float32)[None, :]
    ang = (2.0 * math.pi * t_res * f / L).astype(np.float32)
    z = np.concatenate([t, np.cos(ang), np.sin(ang)], -1).astype(np.float32)
    zp = np.zeros((L, FILTER_HID), np.float32)
    zp[:, :FILTER_EMB] = z
    deltas = np.abs(np.linspace(HY_MIN_DECAY, HY_MAX_DECAY, HYENA_W, dtype=np.float32))
    return zp, t.astype(np.float32), deltas[None, :].astype(np.float32)


def _hyfilter_kernel(z_ref, t_ref, dl_ref, w1_ref, b1_ref, q1_ref, w2_ref, b2_ref, q2_ref, wc_ref, wa_ref,
                     hs_ref, hd_ref, knyq_ref, hm_scr, *, L):
    @pl.when((pl.program_id(0) == 0) & (pl.program_id(1) == 0))
    def _():
        h1 = jnp.sin(q1_ref[...] * (_dot3(z_ref[...], w1_ref[...]) + b1_ref[...]))
        hm_scr[...] = jnp.sin(q2_ref[...] * (_dot3(h1, w2_ref[...]) + b2_ref[...]))

    hm = hm_scr[...]
    win = jnp.exp(-t_ref[...] * dl_ref[...])
    lag = lax.broadcasted_iota(jnp.int32, (L, 1), 0)
    alt = jnp.where(lag % 2 == 0, 1.0, -1.0).astype(F32)
    hc = _dot3(hm, wc_ref[...]) * win
    ha = jnp.where(lag == 0, 0.0, _dot3(hm, wa_ref[...]) * win)
    hs_ref[...] = hc + ha
    hd_ref[...] = hc - ha
    knyq_ref[...] = jnp.sum((hc + ha) * alt, axis=0, keepdims=True) * (1.0 / (2 * L))


def _hyspec_kernel(cm_ref, sm_ref, hs_ref, hd_ref, kre_ref, kim_ref, *, L, tf):
    f = pl.program_id(2) * tf + lax.broadcasted_iota(jnp.int32, (tf, 1), 0)
    wf = jnp.where(f == 0, 1.0, 2.0).astype(F32) * (1.0 / (2 * L))
    kre_ref[...] = _dot(cm_ref[...], hs_ref[...]) * wf
    kim_ref[...] = _dot(sm_ref[...], hd_ref[...]) * wf


def _hyena_filter(p, L, cm, sm, cw=128):
    zp, t, dl = _filter_features(L)
    w1 = jnp.zeros((FILTER_HID, FILTER_HID), F32).at[:FILTER_EMB].set(p['hy_f_w1'])
    r2 = lambda a: a.reshape(1, -1)
    ncb = HYENA_W // cw
    full = lambda shape: pl.BlockSpec(shape, lambda n, j: (0,) * len(shape))
    shp = jax.ShapeDtypeStruct((2, L, HYENA_W), F32)
    hs, hd, knyq = pl.pallas_call(
        functools.partial(_hyfilter_kernel, L=L),
        grid=(2, ncb),
        in_specs=[full((L, FILTER_HID)), full((L, 1)),
                  pl.BlockSpec((1, cw), lambda n, j: (0, j)),
                  full((FILTER_HID, FILTER_HID)), full((1, FILTER_HID)), full((1, FILTER_HID)),
                  full((FILTER_HID, FILTER_HID)), full((1, FILTER_HID)), full((1, FILTER_HID)),
                  pl.BlockSpec((FILTER_HID, cw), lambda n, j: (0, n * 2 * ncb + j)),
                  pl.BlockSpec((FILTER_HID, cw), lambda n, j: (0, n * 2 * ncb + ncb + j))],
        out_specs=[pl.BlockSpec((None, L, cw), lambda n, j: (n, 0, j)),
                   pl.BlockSpec((None, L, cw), lambda n, j: (n, 0, j)),
                   pl.BlockSpec((None, 1, cw), lambda n, j: (n, 0, j))],
        out_shape=[shp, shp, jax.ShapeDtypeStruct((2, 1, HYENA_W), F32)],
        scratch_shapes=[pltpu.VMEM((L, FILTER_HID), F32)],
        compiler_params=pltpu.CompilerParams(dimension_semantics=("arbitrary", "arbitrary"),
                                             vmem_limit_bytes=VMEM_LIMIT),
        name="hyena_filter_L%d" % L,
    )(jnp.asarray(zp), jnp.asarray(t), jnp.asarray(dl), w1, r2(p['hy_f_b1']), r2(p['hy_f_freq1']),
      p['hy_f_w2'], r2(p['hy_f_b2']), r2(p['hy_f_freq2']), p['hy_f_w3'], p['hy_f_w3'])
    tf = min(256, L)
    sw = 256
    kre, kim = pl.pallas_call(
        functools.partial(_hyspec_kernel, L=L, tf=tf),
        grid=(2, HYENA_W // sw, L // tf),
        in_specs=[pl.BlockSpec((tf, L), lambda n, j, i: (i, 0)),
                  pl.BlockSpec((tf, L), lambda n, j, i: (i, 0)),
                  pl.BlockSpec((None, L, sw), lambda n, j, i: (n, 0, j)),
                  pl.BlockSpec((None, L, sw), lambda n, j, i: (n, 0, j))],
        out_specs=[pl.BlockSpec((None, tf, sw), lambda n, j, i: (n, i, j)),
                   pl.BlockSpec((None, tf, sw), lambda n, j, i: (n, i, j))],
        out_shape=[shp, shp],
        compiler_params=pltpu.CompilerParams(dimension_semantics=("arbitrary",) * 3,
                                             vmem_limit_bytes=VMEM_LIMIT),
        name="hyena_spectrum_L%d" % L,
    )(cm, sm, hs, hd)
    return kre, kim, knyq


def _hyena_kernel(v_ref, x1_ref, x2_ref, sw_ref, bias_ref, cm_ref, sm_ref, kre_ref, kim_ref, knyq_ref,
                  o_ref, zb_scr, zf_scr, xs_scr, yre_scr, yim_scr, *, L, rb, nsq):
    nrb = L // rb
    cw = v_ref.shape[-1]
    t = lax.broadcasted_iota(jnp.int32, (nsq * L, 1), 0) % L
    first = t == 0
    last = t == L - 1
    tl = lax.broadcasted_iota(jnp.int32, (L, 1), 0)
    alt_all = jnp.where(tl % 2 == 0, 1.0, -1.0).astype(F32)
    tb = lax.broadcasted_iota(jnp.int32, (rb, 1), 0)
    alt_blk = jnp.where(tb % 2 == 0, 1.0, -1.0).astype(F32)
    sw = sw_ref[...]
    tile = lambda a: jnp.concatenate([a] * nsq, axis=1) if nsq > 1 else a

    def short(ref, s):
        u = ref[...]
        prev = jnp.where(first, 0.0, pltpu.roll(u, 1, 0))
        nxt = jnp.where(last, 0.0, pltpu.roll(u, nsq * L - 1, 0))
        c = prev * sw[0, s][None, :] + u * sw[1, s][None, :] + nxt * sw[2, s][None, :]
        if nsq == 1:
            return c
        return jnp.concatenate([c[q * L:(q + 1) * L] for q in range(nsq)], axis=1)

    z0 = short(v_ref, 0)
    zf_scr[...] = z0
    zb_scr[...] = z0.astype(BF16)
    unyq = jnp.sum(z0 * alt_all, axis=0, keepdims=True)

    for n in range(2):
        xs_scr[...] = short(x1_ref if n == 0 else x2_ref, n + 1)

        def fwd(fb, carry):
            rows = pl.ds(pl.multiple_of(fb * rb, rb), rb)
            zb = zb_scr[...]
            ure = jnp.dot(cm_ref[rows, :], zb, preferred_element_type=F32)
            uim = jnp.dot(sm_ref[rows, :], zb, preferred_element_type=F32)
            kre = tile(kre_ref[n, rows, :])
            kim = tile(kim_ref[n, rows, :])
            yre_scr[rows, :] = (ure * kre - uim * kim).astype(BF16)
            yim_scr[rows, :] = (ure * kim + uim * kre).astype(BF16)
            return carry

        lax.fori_loop(0, nrb, fwd, 0)
        ynyq = unyq * tile(knyq_ref[n])
        bias = tile(bias_ref[n])

        def inv(ib, acc):
            r0 = pl.multiple_of(ib * rb, rb)
            rows = pl.ds(r0, rb)
            y = (jnp.dot(cm_ref[rows, :], yre_scr[...], preferred_element_type=F32)
                 + jnp.dot(sm_ref[rows, :], yim_scr[...], preferred_element_type=F32))
            y = y + alt_blk * ynyq + zf_scr[rows, :] * bias
            znew = xs_scr[rows, :] * y
            if n == 0:
                zf_scr[rows, :] = znew
                zb_scr[rows, :] = znew.astype(BF16)
                acc = acc + jnp.sum(znew * alt_blk, axis=0, keepdims=True)
            else:
                for q in range(nsq):
                    o_ref[pl.ds(q * L + r0, rb), :] = znew[:, q * cw:(q + 1) * cw]
            return acc

        unyq = lax.fori_loop(0, nrb, inv, jnp.zeros_like(unyq))


def _hyena(proj, p, filt, cm, sm, *, L, n_seq, row_blk0, nsq=1, cw=256):
    kre, kim, knyq = filt
    ncb = HYENA_W // cw
    cb = lambda s: (lambda j, b: (b + row_blk0, (C_HY + s * HYENA_W) // cw + j))
    once = pl.Buffered(1)
    sw = p['hy_short_w'].reshape(3, 3, HYENA_W)
    rb = min(256, L)
    Lb, wide = nsq * L, nsq * cw
    return pl.pallas_call(
        functools.partial(_hyena_kernel, L=L, rb=rb, nsq=nsq),
        grid=(ncb, n_seq // nsq),
        in_specs=[pl.BlockSpec((Lb, cw), cb(0)), pl.BlockSpec((Lb, cw), cb(1)), pl.BlockSpec((Lb, cw), cb(2)),
                  pl.BlockSpec((3, 3, cw), lambda j, b: (0, 0, j)),
                  pl.BlockSpec((2, 1, cw), lambda j, b: (0, 0, j)),
                  pl.BlockSpec((L, L), lambda j, b: (0, 0), pipeline_mode=once),
                  pl.BlockSpec((L, L), lambda j, b: (0, 0), pipeline_mode=once),
                  pl.BlockSpec((2, L, cw), lambda j, b: (0, 0, j), pipeline_mode=once),
                  pl.BlockSpec((2, L, cw), lambda j, b: (0, 0, j), pipeline_mode=once),
                  pl.BlockSpec((2, 1, cw), lambda j, b: (0, 0, j))],
        out_specs=pl.BlockSpec((Lb, cw), lambda j, b: (b, j)),
        out_shape=jax.ShapeDtypeStruct((n_seq * L, HYENA_W), F32),
        scratch_shapes=[pltpu.VMEM((L, wide), BF16), pltpu.VMEM((L, wide), F32), pltpu.VMEM((L, wide), F32),
                        pltpu.VMEM((L, wide), BF16), pltpu.VMEM((L, wide), BF16)],
        compiler_params=pltpu.CompilerParams(dimension_semantics=("arbitrary", "arbitrary"),
                                             vmem_limit_bytes=VMEM_LIMIT),
        name="hyena_L%d" % L,
    )(proj, proj, proj, sw, p['hy_bias'].reshape(2, 1, HYENA_W), cm.astype(BF16), sm.astype(BF16),
      kre, kim, knyq)


def _row_max(x):
    t = x[:, :LANES]
    for c in range(1, x.shape[1] // LANES):
        t = jnp.maximum(t, x[:, c * LANES:(c + 1) * LANES])
    return jnp.max(t, axis=-1, keepdims=True)


def _row_sum(x):
    t = x[:, :LANES]
    for c in range(1, x.shape[1] // LANES):
        t = t + x[:, c * LANES:(c + 1) * LANES]
    return jnp.sum(t, axis=-1, keepdims=True)


def _head_masks():
    lane = lax.broadcasted_iota(jnp.int32, (1, KV_W), 1) // HEAD_DIM
    return [lane == h for h in range(ATTN_KV_HEADS)]


def _attend_groups(qgs, keys, vals, masks, sinks, hm):
    nh = len(hm)
    un = [(g, h) for g in range(len(qgs)) for h in range(nh)]
    qm = [jnp.where(hm[h], qgs[g], 0.0).astype(BF16) for g, h in un]
    s = [[_dot_nt(qm[u], k) for k in keys] for u in range(len(un))]
    s = [[sc if mk is None else jnp.where(mk, sc, -jnp.inf) for sc, mk in zip(s[u], masks)] for u in range(len(un))]
    m = [jnp.maximum(_row_max(jnp.concatenate(s[u], axis=1)), sinks[g][h]) for u, (g, h) in enumerate(un)]
    p = [[jnp.exp(sc - m[u]) for sc in s[u]] for u in range(len(un))]
    den = [_row_sum(jnp.concatenate(p[u], axis=1)) + jnp.exp(sinks[g][h] - m[u]) for u, (g, h) in enumerate(un)]
    pv = [[_dot(pc, v) for pc, v in zip(p[u], vals)] for u in range(len(un))]
    accs = [jnp.zeros(q.shape, F32) for q in qgs]
    for u, (g, h) in enumerate(un):
        tot = pv[u][0]
        for t in pv[u][1:]:
            tot = tot + t
        accs[g] = jnp.where(hm[h], tot / den[u], accs[g])
    return accs


def _ctx_attn_kernel(q_ref, k_ref, v_ref, sink_ref, o_ref, *, S):
    scale = HEAD_DIM ** -0.5
    hm = _head_masks()
    k = k_ref[...].astype(BF16)
    v = v_ref[...].astype(BF16)
    for g in range(GQA_GROUP):
        qg = q_ref[:, g * KV_W:(g + 1) * KV_W] * scale
        sinks = [jnp.broadcast_to(sink_ref[0:1, h * GQA_GROUP + g:h * GQA_GROUP + g + 1], (S, 1))
                 for h in range(ATTN_KV_HEADS)]
        o_ref[:, g * KV_W:(g + 1) * KV_W] = _attend_groups([qg], [k], [v], [None], [sinks], hm)[0]


def _ctx_attention(proj, sink, *, S, n_seq):
    return pl.pallas_call(
        functools.partial(_ctx_attn_kernel, S=S),
        grid=(n_seq,),
        in_specs=[pl.BlockSpec((S, ATTN_W), lambda b: (b, C_Q // ATTN_W)),
                  pl.BlockSpec((S, KV_W), lambda b: (b, C_AK // KV_W)),
                  pl.BlockSpec((S, KV_W), lambda b: (b, C_AV // KV_W)),
                  pl.BlockSpec((1, LANES), lambda b: (0, 0))],
        out_specs=pl.BlockSpec((S, ATTN_W), lambda b: (b, 0)),
        out_shape=jax.ShapeDtypeStruct((n_seq * S, ATTN_W), F32),
        compiler_params=pltpu.CompilerParams(dimension_semantics=("arbitrary",),
                                             vmem_limit_bytes=VMEM_LIMIT),
        name="ctx_attention",
    )(proj, proj, proj, sink)


def _rope(x, cos, sin_signed, first_half):
    n = x.shape[-1]
    partner = jnp.where(first_half, pltpu.roll(x, n - HEAD_DIM // 2, 1), pltpu.roll(x, HEAD_DIM // 2, 1))
    return x * cos + partner * sin_signed


def _lat_attn_kernel(q_ref, kp_ref, kc_ref, kn_ref, vp_ref, vc_ref, vn_ref, ck_ref, cv_ref,
                     cq_ref, sq_ref, cp_ref, sp_ref, cc_ref, sc_ref, cn_ref, sn_ref, sink_ref, o_ref, *, T):
    i = pl.program_id(1)
    scale = HEAD_DIM ** -0.5
    hm = _head_masks()
    fh_q = (lax.broadcasted_iota(jnp.int32, (1, ATTN_W), 1) % HEAD_DIM) < HEAD_DIM // 2
    fh_k = (lax.broadcasted_iota(jnp.int32, (1, KV_W), 1) % HEAD_DIM) < HEAD_DIM // 2
    q = _rope(q_ref[...], cq_ref[...], sq_ref[...], fh_q) * scale
    kl = jnp.concatenate([_rope(kp_ref[...], cp_ref[...], sp_ref[...], fh_k),
                          _rope(kc_ref[...], cc_ref[...], sc_ref[...], fh_k),
                          _rope(kn_ref[...], cn_ref[...], sn_ref[...], fh_k)], axis=0).astype(BF16)
    vl = jnp.concatenate([vp_ref[...], vc_ref[...], vn_ref[...]], axis=0).astype(BF16)
    kc = ck_ref[...].astype(BF16)
    vc = cv_ref[...].astype(BF16)
    qpos = i * BLOCK + lax.broadcasted_iota(jnp.int32, (BLOCK, 3 * BLOCK), 0)
    kpos = (i - 1) * BLOCK + lax.broadcasted_iota(jnp.int32, (BLOCK, 3 * BLOCK), 1)
    ok = (jnp.abs(qpos - kpos) <= WINDOW) & (kpos >= 0) & (kpos < T)
    qgs = [q[:, g * KV_W:(g + 1) * KV_W] for g in range(GQA_GROUP)]
    sinks = [[jnp.broadcast_to(sink_ref[0:1, h * GQA_GROUP + g:h * GQA_GROUP + g + 1], (BLOCK, 1))
              for h in range(ATTN_KV_HEADS)] for g in range(GQA_GROUP)]
    accs = _attend_groups(qgs, [kl, kc], [vl, vc], [ok, None], sinks, hm)
    for g in range(GQA_GROUP):
        o_ref[:, g * KV_W:(g + 1) * KV_W] = accs[g]


def _rope_tables(T):
    rows = T // GRID_W
    row = np.repeat(np.arange(rows), GRID_W).astype(np.float32)
    col = np.tile(np.arange(GRID_W), rows).astype(np.float32)
    nf = HEAD_DIM // 4
    inv = (ROPE_BASE ** (-np.arange(nf, dtype=np.float32) / nf)).astype(np.float32)
    ang = np.concatenate([row[:, None] * inv, col[:, None] * inv], -1).astype(np.float32)
    cos = np.cos(ang).astype(np.float32)
    sin = np.sin(ang).astype(np.float32)
    cos_h = np.concatenate([cos, cos], -1)
    sin_h = np.concatenate([-sin, sin], -1)
    return cos_h, sin_h


def _lat_attention(proj, cache_k, cache_v, sink, *, T, n_seq, row_blk0):
    nb = T // BLOCK
    P = cache_k.shape[1]
    cos_h, sin_h = _rope_tables(T)
    cq = jnp.asarray(np.tile(cos_h, (1, ATTN_HEADS)))
    sq = jnp.asarray(np.tile(sin_h, (1, ATTN_HEADS)))
    ck = jnp.asarray(np.tile(cos_h, (1, ATTN_KV_HEADS)))
    sk = jnp.asarray(np.tile(sin_h, (1, ATTN_KV_HEADS)))
    r0 = row_blk0 * nb
    prv = lambda i: jnp.maximum(i - 1, 0)
    nxt = lambda i: jnp.minimum(i + 1, nb - 1)
    kcol, vcol = C_AK // KV_W, C_AV // KV_W
    kv = lambda col, f: pl.BlockSpec((BLOCK, KV_W), lambda b, i: (r0 + b * nb + f(i), col))
    tab = lambda w, f: pl.BlockSpec((BLOCK, w), lambda b, i: (f(i), 0))
    cur = lambda i: i
    return pl.pallas_call(
        functools.partial(_lat_attn_kernel, T=T),
        grid=(n_seq, nb),
        in_specs=[pl.BlockSpec((BLOCK, ATTN_W), lambda b, i: (r0 + b * nb + i, C_Q // ATTN_W)),
                  kv(kcol, prv), kv(kcol, cur), kv(kcol, nxt),
                  kv(vcol, prv), kv(vcol, cur), kv(vcol, nxt),
                  pl.BlockSpec((None, P, KV_W), lambda b, i: (b, 0, 0)),
                  pl.BlockSpec((None, P, KV_W), lambda b, i: (b, 0, 0)),
                  tab(ATTN_W, cur), tab(ATTN_W, cur),
                  tab(KV_W, prv), tab(KV_W, prv), tab(KV_W, cur), tab(KV_W, cur), tab(KV_W, nxt), tab(KV_W, nxt),
                  pl.BlockSpec((1, LANES), lambda b, i: (0, 0))],
        out_specs=pl.BlockSpec((BLOCK, ATTN_W), lambda b, i: (b * nb + i, 0)),
        out_shape=jax.ShapeDtypeStruct((n_seq * T, ATTN_W), F32),
        compiler_params=pltpu.CompilerParams(dimension_semantics=("arbitrary", "arbitrary"),
                                             vmem_limit_bytes=VMEM_LIMIT),
        name="latent_attention",
    )(proj, proj, proj, proj, proj, proj, proj, cache_k, cache_v, cq, sq, ck, sk, ck, sk, ck, sk, sink)


def _layer_norm(y, g, b):
    mu = jnp.mean(y, axis=-1, keepdims=True)
    d = y - mu
    var = jnp.mean(d * d, axis=-1, keepdims=True)
    return d * lax.rsqrt(var + LN_EPS) * g + b


def _outproj_kernel(ac_ref, bc_ref, cc_ref, al_ref, bl_ref, cl_ref, xc_ref, xl_ref, mod_ref, w_ref,
                    g_ref, be_ref, x1_ref, h2_ref, *, tm, n_ctx_tiles):
    r1, r2 = RWKV_W, RWKV_W + HYENA_W
    sub = 128

    def tile(a_ref, b_ref, c_ref, x_ref):
        ms = []
        for s in range(tm // sub):
            rows = slice(s * sub, (s + 1) * sub)
            ms.append(jnp.dot(a_ref[rows, :].astype(BF16), w_ref[0:r1, :], preferred_element_type=F32)
                      + jnp.dot(b_ref[rows, :].astype(BF16), w_ref[r1:r2, :], preferred_element_type=F32)
                      + jnp.dot(c_ref[rows, :].astype(BF16), w_ref[r2:, :], preferred_element_type=F32))
        for s in range(tm // sub):
            rows = slice(s * sub, (s + 1) * sub)
            gate1 = mod_ref[0, 2:3, :]
            shift2 = mod_ref[0, 3:4, :]
            scale2 = mod_ref[0, 4:5, :]
            x1 = _layer_norm(DN_ALPHA * x_ref[rows, :] + gate1 * ms[s], g_ref[...], be_ref[...])
            x1_ref[rows, :] = x1
            h2_ref[rows, :] = (x1 * (1.0 + scale2) + shift2).astype(BF16)

    pl.when(pl.program_id(0) < n_ctx_tiles)(lambda: tile(ac_ref, bc_ref, cc_ref, xc_ref))
    pl.when(pl.program_id(0) >= n_ctx_tiles)(lambda: tile(al_ref, bl_ref, cl_ref, xl_ref))


def _out_projection(mix_ctx, mix_lat, x_src, mod, w_out, l, g, b, tm=256):
    xc, xl, n_ctx, n_lat, lat_off, lat_len = x_src
    N = n_ctx + n_lat
    nct = n_ctx // tm
    nlt = n_lat // tm
    row = lambda w: pl.BlockSpec((tm, w), lambda i: (i, 0))
    ctx = lambda w: _two_source_specs(tm, w, nct, nlt, 0)[0]
    lat = lambda w: _two_source_specs(tm, w, nct, nlt, 0)[1]
    xc_spec, xl_spec = _two_source_specs(tm, D_MODEL, nct, nlt, lat_off // tm)
    full = lambda r: pl.BlockSpec((r, D_MODEL), lambda i: (0, 0))
    return pl.pallas_call(
        functools.partial(_outproj_kernel, tm=tm, n_ctx_tiles=nct),
        grid=(N // tm,),
        in_specs=[ctx(RWKV_W), ctx(HYENA_W), ctx(ATTN_W), lat(RWKV_W), lat(HYENA_W), lat(ATTN_W),
                  xc_spec, xl_spec,
                  _mod_spec(l, tm, n_ctx, lat_len),
                  pl.BlockSpec((None, D_MODEL, D_MODEL), lambda i: (l, 0, 0)), full(1), full(1)],
        out_specs=[row(D_MODEL), row(D_MODEL)],
        out_shape=[jax.ShapeDtypeStruct((N, D_MODEL), F32), jax.ShapeDtypeStruct((N, D_MODEL), BF16)],
        compiler_params=pltpu.CompilerParams(dimension_semantics=("arbitrary",),
                                             vmem_limit_bytes=VMEM_LIMIT),
        name="out_projection",
    )(*mix_ctx, *mix_lat, xc, xl, mod, w_out, g.reshape(1, D_MODEL), b.reshape(1, D_MODEL))


HALO = 16


def _ffn_kernel(hp_ref, h_ref, hn_ref, x1_ref, mod_ref, wa_ref, wb_ref, cwa_ref, cwb_ref, wd_ref, g_ref, be_ref,
                o_ref, hx_scr, ua_scr, ub_scr, acc_scr, *, tm, tf, nsub, t0, n_ctx, s_ctx, s_lat):
    i = pl.program_id(0) + t0
    j = pl.program_id(1)

    @pl.when(j == 0)
    def _():
        hx_scr[0:HALO, :] = hp_ref[...]
        hx_scr[HALO:HALO + tm, :] = h_ref[...]
        hx_scr[HALO + tm:, :] = hn_ref[...]
        acc_scr[...] = jnp.zeros_like(acc_scr)

    hx = hx_scr[...]
    ua_scr[...] = jnp.dot(hx, wa_ref[...], preferred_element_type=F32)
    ub_scr[...] = jnp.dot(hx, wb_ref[...], preferred_element_type=F32)
    grow = i * tm + lax.broadcasted_iota(jnp.int32, (tm, 1), 0)
    pos = jnp.where(grow < n_ctx, grow % s_ctx, (grow - n_ctx) % s_lat)
    slen = jnp.where(grow < n_ctx, s_ctx, s_lat)
    mprev = jnp.where(pos == 0, 0.0, 1.0).astype(F32)
    mnext = jnp.where(pos == slen - 1, 0.0, 1.0).astype(F32)

    def conv(u_scr, cw_ref):
        up = u_scr[HALO - 1:HALO - 1 + tm, :] * mprev
        uc = u_scr[HALO:HALO + tm, :]
        un = u_scr[HALO + 1:HALO + 1 + tm, :] * mnext
        return up * cw_ref[0:1, :] + uc * cw_ref[1:2, :] + un * cw_ref[2:3, :]

    a = conv(ua_scr, cwa_ref)
    b = conv(ub_scr, cwb_ref)
    act = (a * _sigmoid(a) * b).astype(BF16)
    acc_scr[...] += jnp.dot(act, wd_ref[...], preferred_element_type=F32)

    @pl.when(j == pl.num_programs(1) - 1)
    def _():
        for s in range(nsub):
            rows = slice(s * MOD_BLK, (s + 1) * MOD_BLK)
            gate2 = mod_ref[0, 5:6, :]
            o_ref[rows, :] = _layer_norm(DN_ALPHA * x1_ref[rows, :] + gate2 * acc_scr[rows, :],
                                         g_ref[...], be_ref[...])


def _ffn(h2, x1, mod, wa, wb, cw, wd, l, g, b, *, n_ctx, s_ctx, s_lat, row0=0, n_rows=None, tm=512, tf=512):
    N = x1.shape[0]
    n_rows = N if n_rows is None else n_rows
    t0 = row0 // tm
    nsub = tm // MOD_BLK
    nh = tm // HALO
    last_h = N // HALO - 1
    pad = ((0, 0), (0, D_FF_PAD - D_FF))
    cwa, cwb = jnp.pad(cw[:, :D_FF], pad), jnp.pad(cw[:, D_FF:], pad)
    return pl.pallas_call(
        functools.partial(_ffn_kernel, tm=tm, tf=tf, nsub=nsub, t0=t0, n_ctx=n_ctx, s_ctx=s_ctx, s_lat=s_lat),
        grid=(n_rows // tm, D_FF_PAD // tf),
        in_specs=[pl.BlockSpec((HALO, D_MODEL), lambda i, j: (jnp.maximum((i + t0) * nh - 1, 0), 0)),
                  pl.BlockSpec((tm, D_MODEL), lambda i, j: (i + t0, 0)),
                  pl.BlockSpec((HALO, D_MODEL), lambda i, j: (jnp.minimum((i + t0 + 1) * nh, last_h), 0)),
                  pl.BlockSpec((tm, D_MODEL), lambda i, j: (i + t0, 0)),
                  _mod_spec(l, tm, n_ctx, s_lat, t0=t0, rank=2),
                  pl.BlockSpec((None, D_MODEL, tf), lambda i, j: (l, 0, j)),
                  pl.BlockSpec((None, D_MODEL, tf), lambda i, j: (l, 0, j)),
                  pl.BlockSpec((3, tf), lambda i, j: (0, j)),
                  pl.BlockSpec((3, tf), lambda i, j: (0, j)),
                  pl.BlockSpec((None, tf, D_MODEL), lambda i, j: (l, j, 0)),
                  pl.BlockSpec((1, D_MODEL), lambda i, j: (0, 0)),
                  pl.BlockSpec((1, D_MODEL), lambda i, j: (0, 0))],
        out_specs=pl.BlockSpec((tm, D_MODEL), lambda i, j: (i, 0)),
        out_shape=jax.ShapeDtypeStruct((n_rows, D_MODEL), F32),
        scratch_shapes=[pltpu.VMEM((tm + 2 * HALO, D_MODEL), BF16),
                        pltpu.VMEM((tm + 2 * HALO, tf), F32),
                        pltpu.VMEM((tm + 2 * HALO, tf), F32),
                        pltpu.VMEM((tm, D_MODEL), F32)],
        compiler_params=pltpu.CompilerParams(dimension_semantics=("arbitrary", "arbitrary"),
                                             vmem_limit_bytes=VMEM_LIMIT),
        name="conv_ffn",
    )(h2, h2, h2, x1, mod, wa, wb, cwa, cwb, wd, g.reshape(1, D_MODEL), b.reshape(1, D_MODEL))


def _prep_up_kernel(a_ref, b_ref, wa_ref, wb_ref):
    zeros = jnp.zeros((a_ref.shape[0], D_FF_PAD - D_FF), BF16)
    wa_ref[:, :D_FF] = a_ref[...].astype(BF16)
    wa_ref[:, D_FF:] = zeros
    wb_ref[:, :D_FF] = b_ref[...].astype(BF16)
    wb_ref[:, D_FF:] = zeros


def _prep_down_kernel(w_ref, o_ref, *, n_real):
    keep = pl.program_id(1) < n_real
    o_ref[...] = jnp.where(keep, w_ref[...], 0.0).astype(BF16)


def _prep_in_kernel(w_ref, w1_ref, a1_ref, o_ref):
    n = w_ref.shape[-1]
    o_ref[:, :C_Q] = w_ref[:, :C_Q].astype(BF16)
    for g in range(GQA_GROUP):
        for h in range(ATTN_KV_HEADS):
            src = C_Q + (h * GQA_GROUP + g) * HEAD_DIM
            dst = C_Q + (g * ATTN_KV_HEADS + h) * HEAD_DIM
            o_ref[:, dst:dst + HEAD_DIM] = w_ref[:, src:src + HEAD_DIM].astype(BF16)
    o_ref[:, C_AK:n] = w_ref[:, C_AK:n].astype(BF16)
    o_ref[:, n:] = jnp.concatenate([w1_ref[0], w1_ref[1], a1_ref[0], a1_ref[1]], axis=1).astype(BF16)


def _prep_out_kernel(w_ref, o_ref):
    r0 = RWKV_W + HYENA_W
    o_ref[:r0, :] = w_ref[:r0, :].astype(BF16)
    for g in range(GQA_GROUP):
        for h in range(ATTN_KV_HEADS):
            src = r0 + (h * GQA_GROUP + g) * HEAD_DIM
            dst = r0 + (g * ATTN_KV_HEADS + h) * HEAD_DIM
            o_ref[dst:dst + HEAD_DIM, :] = w_ref[src:src + HEAD_DIM, :].astype(BF16)


def _prep_weights(w_in, rwkv_w1, rwkv_a1, w_out, ffn_w_up, ffn_w_down):
    L, D = w_in.shape[0], D_MODEL
    cp = lambda n: pltpu.CompilerParams(dimension_semantics=("arbitrary",) * n, vmem_limit_bytes=VMEM_LIMIT)
    tr = 128
    half = jax.ShapeDtypeStruct((L, D, D_FF_PAD), BF16)
    wa, wb = pl.pallas_call(
        _prep_up_kernel,
        grid=(L, D // tr),
        in_specs=[pl.BlockSpec((None, tr, D_FF), lambda l, i: (l, i, 0)),
                  pl.BlockSpec((None, tr, D_FF), lambda l, i: (l, i, 1))],
        out_specs=[pl.BlockSpec((None, tr, D_FF_PAD), lambda l, i: (l, i, 0))] * 2,
        out_shape=[half, half], compiler_params=cp(2), name="prep_ffn_up",
    )(ffn_w_up, ffn_w_up)
    nr = D_FF // LANES
    wd = pl.pallas_call(
        functools.partial(_prep_down_kernel, n_real=nr),
        grid=(L, D_FF_PAD // LANES),
        in_specs=[pl.BlockSpec((None, LANES, D), lambda l, j: (l, jnp.minimum(j, nr - 1), 0))],
        out_specs=pl.BlockSpec((None, LANES, D), lambda l, j: (l, j, 0)),
        out_shape=jax.ShapeDtypeStruct((L, D_FF_PAD, D), BF16), compiler_params=cp(2), name="prep_ffn_down",
    )(ffn_w_down)
    tr2 = 256
    n_in = w_in.shape[-1]
    w_ext = pl.pallas_call(
        _prep_in_kernel,
        grid=(L, D // tr2),
        in_specs=[pl.BlockSpec((None, tr2, n_in), lambda l, i: (l, i, 0)),
                  pl.BlockSpec((None, 2, tr2, LORA), lambda l, i: (l, 0, i, 0)),
                  pl.BlockSpec((None, 2, tr2, LORA), lambda l, i: (l, 0, i, 0))],
        out_specs=pl.BlockSpec((None, tr2, IN_W_EXT), lambda l, i: (l, i, 0)),
        out_shape=jax.ShapeDtypeStruct((L, D, IN_W_EXT), BF16), compiler_params=cp(2), name="prep_in",
    )(w_in, rwkv_w1, rwkv_a1)
    wo = pl.pallas_call(
        _prep_out_kernel,
        grid=(L,),
        in_specs=[pl.BlockSpec((None, D, D), lambda l: (l, 0, 0))],
        out_specs=pl.BlockSpec((None, D, D), lambda l: (l, 0, 0)),
        out_shape=jax.ShapeDtypeStruct((L, D, D), BF16), compiler_params=cp(1), name="prep_out",
    )(w_out)
    return w_ext, wo, wa, wb, wd


def kernel(x_prompt, x_sample, c, state_rwkv, cache_k, cache_v, c_ctx, w_mod, b_mod, w_in, rwkv_w0, rwkv_w1, rwkv_w2, rwkv_a0, rwkv_a1, rwkv_a2, rwkv_k_k, rwkv_k_a, rwkv_r_k, rwkv_lnx_g, rwkv_lnx_b, hy_short_w, hy_f_w1, hy_f_b1, hy_f_freq1, hy_f_w2, hy_f_b2, hy_f_freq2, hy_f_w3, hy_bias, attn_sink, w_out, ln1_g, ln1_b, ffn_w_up, ffn_conv_w, ffn_w_down, ln2_g, ln2_b):
    params = dict(rwkv_w0=rwkv_w0, rwkv_w2=rwkv_w2, rwkv_a0=rwkv_a0, rwkv_a2=rwkv_a2,
                  rwkv_k_k=rwkv_k_k, rwkv_k_a=rwkv_k_a, rwkv_r_k=rwkv_r_k,
                  rwkv_lnx_g=rwkv_lnx_g, rwkv_lnx_b=rwkv_lnx_b,
                  hy_short_w=hy_short_w, hy_f_w1=hy_f_w1, hy_f_b1=hy_f_b1, hy_f_freq1=hy_f_freq1,
                  hy_f_w2=hy_f_w2, hy_f_b2=hy_f_b2, hy_f_freq2=hy_f_freq2, hy_f_w3=hy_f_w3,
                  hy_bias=hy_bias)
    L = w_in.shape[0]
    Bc, Sc, D = x_prompt.shape
    Bl, Tl, _ = x_sample.shape
    P = cache_k.shape[2]
    n_ctx = Bc * Sc
    n_lat = Bl * Tl
    N = n_ctx + n_lat
    assert Sc % MOD_BLK == 0 and Tl % MOD_BLK == 0 and n_ctx % Tl == 0

    n_cond = 1 + Bl
    cond = jnp.zeros((16, D), F32).at[0].set(c_ctx).at[1:n_cond].set(c)
    mod_all = _modulation(cond, w_mod, b_mod)
    mod_tab = mod_all.reshape(L, 16, 6, D)

    cm_c, sm_c = (jnp.asarray(m, F32) for m in _dft_mats(Sc))
    cm_l, sm_l = (jnp.asarray(m, F32) for m in _dft_mats(Tl))
    sink_pad = jnp.zeros((L, 1, LANES), F32).at[:, 0, :ATTN_HEADS].set(attn_sink)

    w_ext, wo, wa, wb, wd = _prep_weights(w_in, rwkv_w1, rwkv_a1, w_out, ffn_w_up, ffn_w_down)
    x_src = (x_prompt.reshape(n_ctx, D), x_sample.reshape(n_lat, D), n_ctx, n_lat, 0, Tl)
    states, keys, vals = [], [], []
    for l in range(L):
        p = {n: a[l] for n, a in params.items()}
        proj = _in_projection(x_src, mod_tab, w_ext, l)
        keys.append(proj[:n_ctx, C_AK:C_AK + KV_W].reshape(Bc, Sc, ATTN_KV_HEADS, HEAD_DIM))
        vals.append(proj[:n_ctx, C_AV:C_AV + KV_W].reshape(Bc, Sc, ATTN_KV_HEADS, HEAD_DIM))

        ya_c, s_fin = _rwkv(proj, p, None, T=Sc, n_seq=Bc, row_blk0=0, nsq=RWKV_CTX_SEQS)
        ya_l, _ = _rwkv(proj, p, state_rwkv[:, l], T=Tl, n_seq=Bl, row_blk0=n_ctx // Tl)
        states.append(s_fin)

        yb_c = _hyena(proj, p, _hyena_filter(p, Sc, cm_c, sm_c), cm_c, sm_c, L=Sc, n_seq=Bc, row_blk0=0,
                      nsq=HYENA_CTX_SEQS)
        yb_l = _hyena(proj, p, _hyena_filter(p, Tl, cm_l, sm_l), cm_l, sm_l, L=Tl, n_seq=Bl,
                      row_blk0=n_ctx // Tl)

        yc_c = _ctx_attention(proj, sink_pad[l], S=Sc, n_seq=Bc)
        yc_l = _lat_attention(proj, cache_k[:, l].reshape(Bl, P, KV_W), cache_v[:, l].reshape(Bl, P, KV_W),
                              sink_pad[l], T=Tl, n_seq=Bl, row_blk0=n_ctx // Tl)

        x1, h2 = _out_projection((ya_c, yb_c, yc_c), (ya_l, yb_l, yc_l), x_src, mod_tab,
                                 wo, l, ln1_g[l], ln1_b[l])
        ffn = functools.partial(_ffn, h2, x1, mod_tab, wa, wb, ffn_conv_w[l], wd, l, ln2_g[l], ln2_b[l],
                                n_ctx=n_ctx, s_ctx=Sc, s_lat=Tl)
        if l < L - 1:
            x = ffn()
            x_src = (x, x, n_ctx, n_lat, n_ctx, Tl)
        else:
            y_prompt = ffn(row0=0, n_rows=n_ctx).reshape(Bc, Sc, D)
            y_sample = ffn(row0=n_ctx, n_rows=n_lat).reshape(Bl, Tl, D)

    new_state = jnp.stack(states, 1)
    new_k = jnp.stack(keys, 1)
    new_v = jnp.stack(vals, 1)
    return (y_prompt, y_sample, new_state, new_k, new_v)
```

```python
import functools
import math

import numpy as np
import jax
import jax.numpy as jnp
from jax import lax
from jax.experimental import pallas as pl
from jax.experimental.pallas import tpu as pltpu

F32 = jnp.float32
BF16 = jnp.bfloat16

D_MODEL = 2048
DEPTH = 4
GRID_W = 64
HEAD_DIM = 64
RWKV_W = 768
RWKV_HEADS = 12
HYENA_W = 512
ATTN_W = 768
ATTN_HEADS = 12
ATTN_KV_HEADS = 4
GQA_GROUP = 3
KV_W = 256
LORA = 64
GN_EPS = 64e-5
FILTER_EMB = 33
FILTER_HID = 64
HY_MIN_DECAY = math.log(1e-2) / 1.5
HY_MAX_DECAY = math.log(1e-2) / 0.3
WINDOW = 128
BLOCK = 128
ROPE_BASE = 10000.0
D_FF = 5504
LN_EPS = 1e-5
DN_ALPHA = (2 * DEPTH) ** 0.25

C_R, C_K, C_V, C_G = 0, 768, 1536, 2304
C_HY = 3072
C_Q, C_AK, C_AV = 4608, 5376, 5632
C_LORA = 5888
IN_W_EXT = 6144

MOD_BLK = 256
CHUNK = 64
RWKV_CB = 8
RWKV_CTX_SEQS = 4
HYENA_CTX_SEQS = 4
HYENA_ROW_BLK = 1024
LANES = 128
D_FF_PAD = 5632
VMEM_LIMIT = 56 * 1024 * 1024


def _dot(a, b):
    return jnp.dot(a.astype(BF16), b.astype(BF16), preferred_element_type=F32)


def _dot_nt(a, b):
    return lax.dot_general(a.astype(BF16), b.astype(BF16), (((1,), (1,)), ((), ())),
                           preferred_element_type=F32)


def _split(x):
    hi = x.astype(BF16)
    lo = (x - hi.astype(F32)).astype(BF16)
    return hi, lo


def _dot3_dims(a, b, dims):
    ah, al = _split(a)
    bh, bl = _split(b)
    dg = functools.partial(lax.dot_general, dimension_numbers=dims, preferred_element_type=F32)
    return dg(ah, bh) + (dg(ah, bl) + dg(al, bh))


_NN = (((1,), (0,)), ((), ()))
_NT = (((1,), (1,)), ((), ()))
_TN = (((0,), (0,)), ((), ()))


def _dot3(a, b):
    return _dot3_dims(a, b, _NN)


def _sigmoid(x):
    return 1.0 / (1.0 + jnp.exp(-x))


def _softplus(x):
    return jnp.maximum(x, 0.0) + jnp.log(1.0 + jnp.exp(-jnp.abs(x)))


def _mod_kernel(c_ref, w_ref, b_ref, o_ref):
    c = c_ref[...]
    s = c * _sigmoid(c)
    o_ref[...] = _dot(s, w_ref[...]) + b_ref[...]


def _modulation(cond, w_mod, b_mod):
    L = w_mod.shape[0]
    R = cond.shape[0]
    tn = 1024
    return pl.pallas_call(
        _mod_kernel,
        grid=(L, 6 * D_MODEL // tn),
        in_specs=[pl.BlockSpec((R, D_MODEL), lambda l, j: (0, 0)),
                  pl.BlockSpec((None, D_MODEL, tn), lambda l, j: (l, 0, j)),
                  pl.BlockSpec((None, 1, tn), lambda l, j: (l, 0, j))],
        out_specs=pl.BlockSpec((None, R, tn), lambda l, j: (l, 0, j)),
        out_shape=jax.ShapeDtypeStruct((L, R, 6 * D_MODEL), F32),
        compiler_params=pltpu.CompilerParams(dimension_semantics=("arbitrary", "arbitrary"),
                                             vmem_limit_bytes=VMEM_LIMIT),
        name="modulation",
    )(cond, w_mod, b_mod.reshape(L, 1, 6 * D_MODEL))


def _two_source_specs(tm, width, nct, nlt, lat_off):
    ctx = pl.BlockSpec((tm, width), lambda i: (jnp.minimum(i, nct - 1), 0))
    lat = pl.BlockSpec((tm, width), lambda i: (jnp.clip(i - nct, 0, nlt - 1) + lat_off, 0))
    return ctx, lat


def _mod_spec(l, tm, n_ctx, lat_len, t0=0, rank=1):
    def row(i):
        tok = (i + t0) * tm
        return jnp.where(tok < n_ctx, 0, 1 + (tok - n_ctx) // lat_len)

    if rank == 1:
        return pl.BlockSpec((None, 1, 6, D_MODEL), lambda i: (l, row(i), 0, 0))
    return pl.BlockSpec((None, 1, 6, D_MODEL), lambda i, j: (l, row(i), 0, 0))


def _inproj_kernel(xc_ref, xl_ref, mod_ref, w_ref, o_ref, h_scr, *, tn, n_ctx_tiles):
    shift = mod_ref[0, 0:1, :]
    scale = mod_ref[0, 1:2, :]

    def modulate(x_ref):
        h_scr[...] = (x_ref[...] * (1.0 + scale) + shift).astype(BF16)

    pl.when(pl.program_id(0) < n_ctx_tiles)(lambda: modulate(xc_ref))
    pl.when(pl.program_id(0) >= n_ctx_tiles)(lambda: modulate(xl_ref))
    h = h_scr[...]
    for j in range(IN_W_EXT // tn):
        cols = slice(j * tn, (j + 1) * tn)
        o_ref[:, cols] = jnp.dot(h, w_ref[:, cols], preferred_element_type=F32)


def _in_projection(x_src, mod, w_ext, l, tn=512):
    xc, xl, n_ctx, n_lat, lat_off, lat_len = x_src
    N = n_ctx + n_lat
    tm = MOD_BLK
    nct = n_ctx // tm
    ctx_spec, lat_spec = _two_source_specs(tm, D_MODEL, nct, n_lat // tm, lat_off // tm)
    return pl.pallas_call(
        functools.partial(_inproj_kernel, tn=tn, n_ctx_tiles=nct),
        grid=(N // tm,),
        in_specs=[ctx_spec, lat_spec,
                  _mod_spec(l, tm, n_ctx, lat_len),
                  pl.BlockSpec((None, D_MODEL, IN_W_EXT), lambda i: (l, 0, 0), pipeline_mode=pl.Buffered(1))],
        out_specs=pl.BlockSpec((tm, IN_W_EXT), lambda i: (i, 0)),
        out_shape=jax.ShapeDtypeStruct((N, IN_W_EXT), F32),
        scratch_shapes=[pltpu.VMEM((tm, D_MODEL), BF16)],
        compiler_params=pltpu.CompilerParams(dimension_semantics=("arbitrary",),
                                             vmem_limit_bytes=VMEM_LIMIT),
        name="in_projection",
    )(xc, xl, mod, w_ext)


PREC = dict(seg=1, cum=2, a=1, neu=1, post=1, mn=1, b_o=1, b_s=1, gn=2)


def _dotp(a, b, dims, passes):
    dg = functools.partial(lax.dot_general, dimension_numbers=dims, preferred_element_type=F32)
    if passes == 1:
        return dg(a.astype(BF16), b.astype(BF16))
    ah, al = _split(a)
    if passes == 2:
        bh = b.astype(BF16)
        return dg(ah, bh) + dg(al, bh)
    bh, bl = _split(b)
    return dg(ah, bh) + (dg(ah, bl) + dg(al, bh))


def _dotp_rhs(a, b, passes):
    dg = functools.partial(lax.dot_general, dimension_numbers=_NN, preferred_element_type=F32)
    ab = a.astype(BF16)
    if passes == 1:
        return dg(ab, b.astype(BF16))
    bh, bl = _split(b)
    return dg(ab, bh) + dg(ab, bl)


def _rwkv_kernel(*refs, T, nsq, has_init, cb):
    (r_ref, k_ref, v_ref, g_ref, lo_ref, w0_ref, w2_ref, a0_ref, a2_ref,
     kk_ref, ka_ref, rk_ref, lng_ref, lnb_ref) = refs[:14]
    pos = 14
    init_ref = None
    if has_init:
        init_ref = refs[pos]
        pos += 1
    y_ref, sfin_ref = refs[pos:pos + 2]
    rp_scr, ov_scr, m_scr, n_scr, o_scr, bonus_scr, s_scr = refs[pos + 2:]

    nC = T // CHUNK
    nCs = nC // nsq
    C = CHUNK
    H = HEAD_DIM
    P = PREC
    C2 = 2 * C
    row = lax.broadcasted_iota(jnp.int32, (C, C), 0)
    col = lax.broadcasted_iota(jnp.int32, (C, C), 1)
    tri_f = jnp.where(col <= row, 1.0, 0.0).astype(F32)
    lrow = lax.broadcasted_iota(jnp.int32, (LANES, LANES), 0)
    lcol = lax.broadcasted_iota(jnp.int32, (LANES, LANES), 1)
    same_head = (lrow // H) == (lcol // H)
    head_ones = jnp.where(same_head, 1.0, 0.0).astype(F32)
    eye = lrow == lcol
    rt_, ct_ = lrow % C, lcol % C
    strict_m = [same_head & (ct_ < rt_), same_head & (ct_ > rt_)]
    incl_m = [same_head & (ct_ <= rt_), same_head & (ct_ >= rt_)]
    lane = lax.broadcasted_iota(jnp.int32, (1, LANES), 1)
    hmask = [lane < H, lane >= H]

    def stack_heads(x):
        return jnp.concatenate([jnp.where(hmask[0], x, 0.0), jnp.where(hmask[1], x, 0.0)], axis=0)

    kkw = kk_ref[...]
    kaw = ka_ref[...]
    rkw = rk_ref[...]
    zl = jnp.zeros((LORA, LANES), F32)
    lora_w = jnp.concatenate(
        [jnp.concatenate([w if j == i else zl for j in range(4)], axis=1)
         for i, w in enumerate([w2_ref[0], w2_ref[1], a2_ref[0], a2_ref[1]])], axis=0).astype(BF16)

    def phase_a(ci, carry):
        units = []
        for cc in range(cb):
            c = ci * cb + cc
            rows = pl.ds(pl.multiple_of(c * C, C), C)
            r = r_ref[rows, :]
            k = k_ref[rows, :]
            v = v_ref[rows, :]
            lo = lo_ref[rows, :]
            kkraw = k * kkw
            ss = _dotp(kkraw * kkraw, head_ones, _NN, P['seg'])
            kkn = kkraw / jnp.maximum(jnp.sqrt(ss), 1e-12)
            bonus = jnp.zeros((C, LANES), F32)
            vm = stack_heads(v)
            lo_in = jnp.concatenate([jnp.tanh(lo[:, :2 * LORA]), lo[:, 2 * LORA:]], axis=1)
            lo_out = _dot(lo_in, lora_w)
            lds, alphas = [], []
            for d in range(2):
                wl = w0_ref[d] + lo_out[:, d * LANES:(d + 1) * LANES]
                lds.append(-jnp.exp(-_softplus(-wl) - 0.5))
                alphas.append(_sigmoid(a0_ref[d] + lo_out[:, (2 + d) * LANES:(3 + d) * LANES]))
            pref = _dotp_rhs(tri_f, jnp.concatenate(lds, axis=1), P['cum'])
            for d in range(2):
                ld, alpha = lds[d], alphas[d]
                kd = k * (1.0 + (alpha - 1.0) * kaw)
                bb = kkn * alpha
                bonus = bonus + _dotp(r * kd * rkw, head_ones, _NN, P['seg']) * v
                pf = pref[:, d * LANES:(d + 1) * LANES]
                g_incl = pf if d == 0 else pf[C - 1:C, :] - pf + ld
                g_excl = g_incl - ld
                g_end = g_incl[C - 1:C, :] if d == 0 else g_incl[0:1, :]
                ginv = jnp.exp(-g_incl)
                dec_end = jnp.exp(g_end - g_incl)
                units.append(dict(
                    d=d, c=c, vm=vm,
                    at=stack_heads(-kkn * jnp.exp(g_excl)), rt=stack_heads(r * jnp.exp(g_incl)),
                    bt=stack_heads(bb * ginv), kt=stack_heads(kd * ginv),
                    bend=stack_heads(bb * dec_end), kend=stack_heads(kd * dec_end), gend=jnp.exp(g_end)))
            bonus_scr[rows, :] = bonus
            o_scr[rows, :] = jnp.zeros((C, LANES), F32)

        for u in units:
            ar = jnp.concatenate([u['at'], u['rt']], axis=0)
            bk = jnp.concatenate([u['bt'], u['kt']], axis=0)
            u['a'] = _dotp(ar, bk, _NT, P['a'])
        for u in units:
            a = u['a']
            st, ic = strict_m[u['d']], incl_m[u['d']]
            u['p'] = jnp.where(st, a[:C2, :C2], 0.0)
            u['a_ak'] = jnp.where(st, a[:C2, C2:], 0.0)
            u['a_rb'] = jnp.where(ic, a[C2:, :C2], 0.0)
            u['a_rk'] = jnp.where(ic, a[C2:, C2:], 0.0)
        rvs = [_dotp(u['a_ak'], u['vm'], _NN, P['post']) for u in units]
        sq = [_dotp(u['p'], u['p'], _NN, P['neu']) for u in units]
        for i, u in enumerate(units):
            u['x'] = jnp.where(eye, 1.0, 0.0) + u['p']
            u['p'] = sq[i]
        for it in range(1, 6):
            res = [_dotp(u['p'], jnp.concatenate([u['x'], u['p']], axis=1) if it < 5 else u['x'], _NN, P['neu'])
                   for u in units]
            for i, u in enumerate(units):
                u['x'] = u['x'] + res[i][:, :C2]
                if it < 5:
                    u['p'] = res[i][:, C2:]
        rhs = [_dotp(u['x'], jnp.concatenate([u['at'], rvs[i]], axis=1), _NN, P['post'])
               for i, u in enumerate(units)]
        for i, u in enumerate(units):
            u['rhs'] = rhs[i]
        xs = [_dotp(jnp.concatenate([u['a_rb'], u['a_rk']], axis=1),
                    jnp.concatenate([u['rhs'], jnp.concatenate([jnp.zeros_like(u['vm']), u['vm']], axis=1)], axis=0),
                    _NN, P['post']) for u in units]
        ms = [_dotp(u['rhs'][:, :LANES], u['bend'], _TN, P['mn']) for u in units]
        ns = [_dotp(jnp.concatenate([u['rhs'][:, LANES:], u['vm']], axis=0),
                    jnp.concatenate([u['bend'], u['kend']], axis=0), _TN, P['mn']) for u in units]
        for i, u in enumerate(units):
            d, c = u['d'], u['c']
            rp = u['rt'] + xs[i][:, :LANES]
            ov = xs[i][:, LANES:]
            rp_scr[d, c] = rp[:C] + rp[C:]
            ov_scr[d, c] = ov[:C] + ov[C:]
            m_scr[d, c] = jnp.where(eye, jnp.broadcast_to(u['gend'], (LANES, LANES)), 0.0) + ms[i]
            n_scr[d, c] = ns[i]
        return carry

    lax.fori_loop(0, nC // cb, phase_a, 0)

    zero_blk = jnp.zeros((H, H), F32)
    for q in range(nsq):
        for d in range(2):
            if has_init:
                s_scr[q, d] = jnp.concatenate([jnp.concatenate([init_ref[q, d, 0], zero_blk], axis=1),
                                               jnp.concatenate([zero_blk, init_ref[q, d, 1]], axis=1)], axis=0)
            else:
                s_scr[q, d] = jnp.zeros((LANES, LANES), F32)

    def phase_b(i, carry):
        idx = [(q, d, q * nCs + (i if d == 0 else nCs - 1 - i)) for q in range(nsq) for d in range(2)]
        ss = [s_scr[q, d] for q, d, c in idx]
        os_ = [_dotp(rp_scr[d, c], ss[j], _NT, P['b_o']) for j, (q, d, c) in enumerate(idx)]
        sn = [_dotp(ss[j], m_scr[d, c], _NN, P['b_s']) for j, (q, d, c) in enumerate(idx)]
        for j, (q, d, c) in enumerate(idx):
            s_scr[q, d] = sn[j] + n_scr[d, c]
            rows = pl.ds(c * C if isinstance(c, int) else pl.multiple_of(c * C, C), C)
            o_scr[rows, :] = o_scr[rows, :] + (os_[j] + ov_scr[d, c])
        return carry

    if nCs <= 4:
        for i in range(nCs):
            phase_b(i, 0)
    else:
        lax.fori_loop(0, nCs, phase_b, 0, unroll=4)

    for q in range(nsq):
        for d in range(2):
            s = s_scr[q, d]
            sfin_ref[q, d, 0] = s[:H, :H]
            sfin_ref[q, d, 1] = s[H:, H:]

    lng = lng_ref[...]
    lnb = lnb_ref[...]

    rb = min(T, 256)

    def phase_c(c, carry):
        rows = pl.ds(pl.multiple_of(c * rb, rb), rb)
        o = o_scr[rows, :]
        st = _dotp(jnp.concatenate([o, o * o], axis=0), head_ones, _NN, P['gn']) * (1.0 / H)
        mu = st[:rb]
        var = st[rb:] - mu * mu
        on = (o - mu) * lax.rsqrt(var + GN_EPS) * lng + lnb
        y_ref[rows, :] = (on + bonus_scr[rows, :]) * _sigmoid(g_ref[rows, :])
        return carry

    lax.fori_loop(0, T // rb, phase_c, 0)


def _rwkv(proj, p, init, *, T, n_seq, row_blk0, nsq=1):
    seq_len, T = T, nsq * T
    nC = T // CHUNK
    npair = RWKV_HEADS // 2
    has_init = init is not None
    cb = lambda off: (lambda b, j: (b + row_blk0, off // LANES + j))
    vec = lambda b, j: (0, j)
    in_specs = [pl.BlockSpec((T, LANES), cb(C_R)),
                pl.BlockSpec((T, LANES), cb(C_K)),
                pl.BlockSpec((T, LANES), cb(C_V)),
                pl.BlockSpec((T, LANES), cb(C_G)),
                pl.BlockSpec((T, 4 * LORA), lambda b, j: (b + row_blk0, C_LORA // (4 * LORA))),
                pl.BlockSpec((2, 1, LANES), lambda b, j: (0, 0, j)),
                pl.BlockSpec((2, LORA, LANES), lambda b, j: (0, 0, j)),
                pl.BlockSpec((2, 1, LANES), lambda b, j: (0, 0, j)),
                pl.BlockSpec((2, LORA, LANES), lambda b, j: (0, 0, j)),
                pl.BlockSpec((1, LANES), vec), pl.BlockSpec((1, LANES), vec),
                pl.BlockSpec((1, LANES), vec), pl.BlockSpec((1, LANES), vec),
                pl.BlockSpec((1, LANES), vec)]
    args = [proj, proj, proj, proj, proj,
            p['rwkv_w0'].reshape(2, 1, RWKV_W), p['rwkv_w2'], p['rwkv_a0'].reshape(2, 1, RWKV_W), p['rwkv_a2'],
            p['rwkv_k_k'].reshape(1, RWKV_W), p['rwkv_k_a'].reshape(1, RWKV_W),
            p['rwkv_r_k'].reshape(1, RWKV_W), p['rwkv_lnx_g'].reshape(1, RWKV_W),
            p['rwkv_lnx_b'].reshape(1, RWKV_W)]
    st_spec = pl.BlockSpec((nsq, 2, 2, HEAD_DIM, HEAD_DIM), lambda b, j: (b, 0, j, 0, 0))
    if has_init:
        in_specs.append(st_spec)
        args.append(init)
    sc_o = lambda: pltpu.VMEM((2, nC, CHUNK, LANES), F32)
    sc_s = lambda: pltpu.VMEM((2, nC, LANES, LANES), F32)
    return pl.pallas_call(
        functools.partial(_rwkv_kernel, T=T, nsq=nsq, has_init=has_init, cb=min(RWKV_CB, nC)),
        grid=(n_seq // nsq, npair),
        in_specs=in_specs,
        out_specs=[pl.BlockSpec((T, LANES), lambda b, j: (b, j)), st_spec],
        out_shape=[jax.ShapeDtypeStruct((n_seq * seq_len, RWKV_W), F32),
                   jax.ShapeDtypeStruct((n_seq, 2, RWKV_HEADS, HEAD_DIM, HEAD_DIM), F32)],
        scratch_shapes=[sc_o(), sc_o(), sc_s(), sc_s(),
                        pltpu.VMEM((T, LANES), F32), pltpu.VMEM((T, LANES), F32),
                        pltpu.VMEM((nsq, 2, LANES, LANES), F32)],
        compiler_params=pltpu.CompilerParams(dimension_semantics=("arbitrary", "arbitrary"),
                                             vmem_limit_bytes=VMEM_LIMIT),
        name="rwkv_T%d" % T,
    )(*args)


def _dft_mats(L):
    f = np.arange(L, dtype=np.int64)
    ph = (np.outer(f, f) % (2 * L)).astype(np.float64) * (math.pi / L)
    return np.cos(ph), -np.sin(ph)


def _filter_features(L):
    t = np.linspace(0.0, 1.0, L, dtype=np.float32)[:, None]
    bands = (FILTER_EMB - 1) // 2
    t_res = np.arange(L, dtype=np.float32)[:, None]
    f = np.linspace(1e-4, bands - 1, bands, dtype=np.float32)[None, :]
    ang = (2.0 * math.pi * t_res * f / L).astype(np.float32)
    z = np.concatenate([t, np.cos(ang), np.sin(ang)], -1).astype(np.float32)
    zp = np.zeros((L, FILTER_HID), np.float32)
    zp[:, :FILTER_EMB] = z
    deltas = np.abs(np.linspace(HY_MIN_DECAY, HY_MAX_DECAY, HYENA_W, dtype=np.float32))
    return zp, t.astype(np.float32), deltas[None, :].astype(np.float32)


def _hyfilter_kernel(z_ref, t_ref, dl_ref, w1_ref, b1_ref, q1_ref, w2_ref, b2_ref, q2_ref, wc_ref, wa_ref,
                     hs_ref, hd_ref, knyq_ref, hm_scr, *, L):
    @pl.when((pl.program_id(0) == 0) & (pl.program_id(1) == 0))
    def _():
        h1 = jnp.sin(q1_ref[...] * (_dot3(z_ref[...], w1_ref[...]) + b1_ref[...]))
        hm_scr[...] = jnp.sin(q2_ref[...] * (_dot3(h1, w2_ref[...]) + b2_ref[...]))

    hm = hm_scr[...]
    win = jnp.exp(-t_ref[...] * dl_ref[...])
    lag = lax.broadcasted_iota(jnp.int32, (L, 1), 0)
    alt = jnp.where(lag % 2 == 0, 1.0, -1.0).astype(F32)
    hc = _dot3(hm, wc_ref[...]) * win
    ha = jnp.where(lag == 0, 0.0, _dot3(hm, wa_ref[...]) * win)
    hs_ref[...] = hc + ha
    hd_ref[...] = hc - ha
    knyq_ref[...] = jnp.sum((hc + ha) * alt, axis=0, keepdims=True) * (1.0 / (2 * L))


def _hyspec_kernel(cm_ref, sm_ref, hs_ref, hd_ref, kre_ref, kim_ref, *, L, tf):
    f = pl.program_id(2) * tf + lax.broadcasted_iota(jnp.int32, (tf, 1), 0)
    wf = jnp.where(f == 0, 1.0, 2.0).astype(F32) * (1.0 / (2 * L))
    kre_ref[...] = _dot(cm_ref[...], hs_ref[...]) * wf
    kim_ref[...] = _dot(sm_ref[...], hd_ref[...]) * wf


def _hyena_filter(p, L, cm, sm, cw=128):
    zp, t, dl = _filter_features(L)
    w1 = jnp.zeros((FILTER_HID, FILTER_HID), F32).at[:FILTER_EMB].set(p['hy_f_w1'])
    r2 = lambda a: a.reshape(1, -1)
    ncb = HYENA_W // cw
    full = lambda shape: pl.BlockSpec(shape, lambda n, j: (0,) * len(shape))
    shp = jax.ShapeDtypeStruct((2, L, HYENA_W), F32)
    hs, hd, knyq = pl.pallas_call(
        functools.partial(_hyfilter_kernel, L=L),
        grid=(2, ncb),
        in_specs=[full((L, FILTER_HID)), full((L, 1)),
                  pl.BlockSpec((1, cw), lambda n, j: (0, j)),
                  full((FILTER_HID, FILTER_HID)), full((1, FILTER_HID)), full((1, FILTER_HID)),
                  full((FILTER_HID, FILTER_HID)), full((1, FILTER_HID)), full((1, FILTER_HID)),
                  pl.BlockSpec((FILTER_HID, cw), lambda n, j: (0, n * 2 * ncb + j)),
                  pl.BlockSpec((FILTER_HID, cw), lambda n, j: (0, n * 2 * ncb + ncb + j))],
        out_specs=[pl.BlockSpec((None, L, cw), lambda n, j: (n, 0, j)),
                   pl.BlockSpec((None, L, cw), lambda n, j: (n, 0, j)),
                   pl.BlockSpec((None, 1, cw), lambda n, j: (n, 0, j))],
        out_shape=[shp, shp, jax.ShapeDtypeStruct((2, 1, HYENA_W), F32)],
        scratch_shapes=[pltpu.VMEM((L, FILTER_HID), F32)],
        compiler_params=pltpu.CompilerParams(dimension_semantics=("arbitrary", "arbitrary"),
                                             vmem_limit_bytes=VMEM_LIMIT),
        name="hyena_filter_L%d" % L,
    )(jnp.asarray(zp), jnp.asarray(t), jnp.asarray(dl), w1, r2(p['hy_f_b1']), r2(p['hy_f_freq1']),
      p['hy_f_w2'], r2(p['hy_f_b2']), r2(p['hy_f_freq2']), p['hy_f_w3'], p['hy_f_w3'])
    tf = min(512, L)
    sw = 256
    kre, kim = pl.pallas_call(
        functools.partial(_hyspec_kernel, L=L, tf=tf),
        grid=(2, HYENA_W // sw, L // tf),
        in_specs=[pl.BlockSpec((tf, L), lambda n, j, i: (i, 0)),
                  pl.BlockSpec((tf, L), lambda n, j, i: (i, 0)),
                  pl.BlockSpec((None, L, sw), lambda n, j, i: (n, 0, j)),
                  pl.BlockSpec((None, L, sw), lambda n, j, i: (n, 0, j))],
        out_specs=[pl.BlockSpec((None, tf, sw), lambda n, j, i: (n, i, j)),
                   pl.BlockSpec((None, tf, sw), lambda n, j, i: (n, i, j))],
        out_shape=[shp, shp],
        compiler_params=pltpu.CompilerParams(dimension_semantics=("arbitrary",) * 3,
                                             vmem_limit_bytes=VMEM_LIMIT),
        name="hyena_spectrum_L%d" % L,
    )(cm, sm, hs, hd)
    return kre, kim, knyq


def _hyena_kernel(v_ref, x1_ref, x2_ref, sw_ref, bias_ref, cm_ref, sm_ref, kre_ref, kim_ref, knyq_ref,
                  o_ref, zb_scr, zf_scr, xs_scr, yre_scr, yim_scr, *, L, rb, nsq):
    nrb = L // rb
    cw = v_ref.shape[-1]
    t = lax.broadcasted_iota(jnp.int32, (nsq * L, 1), 0) % L
    first = t == 0
    last = t == L - 1
    tl = lax.broadcasted_iota(jnp.int32, (L, 1), 0)
    alt_all = jnp.where(tl % 2 == 0, 1.0, -1.0).astype(F32)
    tb = lax.broadcasted_iota(jnp.int32, (rb, 1), 0)
    alt_blk = jnp.where(tb % 2 == 0, 1.0, -1.0).astype(F32)
    sw = sw_ref[...]
    tile = lambda a: jnp.concatenate([a] * nsq, axis=1) if nsq > 1 else a

    def short(ref, s):
        u = ref[...]
        prev = jnp.where(first, 0.0, pltpu.roll(u, 1, 0))
        nxt = jnp.where(last, 0.0, pltpu.roll(u, nsq * L - 1, 0))
        c = prev * sw[0, s][None, :] + u * sw[1, s][None, :] + nxt * sw[2, s][None, :]
        if nsq == 1:
            return c
        return jnp.concatenate([c[q * L:(q + 1) * L] for q in range(nsq)], axis=1)

    z0 = short(v_ref, 0)
    zf_scr[...] = z0
    zb_scr[...] = z0.astype(BF16)
    unyq = jnp.sum(z0 * alt_all, axis=0, keepdims=True)

    for n in range(2):
        xs_scr[...] = short(x1_ref if n == 0 else x2_ref, n + 1)

        def fwd(fb, carry):
            rows = pl.ds(pl.multiple_of(fb * rb, rb), rb)
            zb = zb_scr[...]
            ure = jnp.dot(cm_ref[rows, :], zb, preferred_element_type=F32)
            uim = jnp.dot(sm_ref[rows, :], zb, preferred_element_type=F32)
            kre = tile(kre_ref[n, rows, :])
            kim = tile(kim_ref[n, rows, :])
            yre_scr[rows, :] = (ure * kre - uim * kim).astype(BF16)
            yim_scr[rows, :] = (ure * kim + uim * kre).astype(BF16)
            return carry

        lax.fori_loop(0, nrb, fwd, 0)
        ynyq = unyq * tile(knyq_ref[n])
        bias = tile(bias_ref[n])

        def inv(ib, acc):
            r0 = pl.multiple_of(ib * rb, rb)
            rows = pl.ds(r0, rb)
            y = (jnp.dot(cm_ref[rows, :], yre_scr[...], preferred_element_type=F32)
                 + jnp.dot(sm_ref[rows, :], yim_scr[...], preferred_element_type=F32))
            y = y + alt_blk * ynyq + zf_scr[rows, :] * bias
            znew = xs_scr[rows, :] * y
            if n == 0:
                zf_scr[rows, :] = znew
                zb_scr[rows, :] = znew.astype(BF16)
                acc = acc + jnp.sum(znew * alt_blk, axis=0, keepdims=True)
            else:
                for q in range(nsq):
                    o_ref[pl.ds(q * L + r0, rb), :] = znew[:, q * cw:(q + 1) * cw]
            return acc

        unyq = lax.fori_loop(0, nrb, inv, jnp.zeros_like(unyq))


def _hyena(proj, p, filt, cm, sm, *, L, n_seq, row_blk0, nsq=1, cw=256):
    kre, kim, knyq = filt
    ncb = HYENA_W // cw
    cb = lambda s: (lambda j, b: (b + row_blk0, (C_HY + s * HYENA_W) // cw + j))
    once = pl.Buffered(1)
    sw = p['hy_short_w'].reshape(3, 3, HYENA_W)
    rb = min(HYENA_ROW_BLK, L)
    Lb, wide = nsq * L, nsq * cw
    return pl.pallas_call(
        functools.partial(_hyena_kernel, L=L, rb=rb, nsq=nsq),
        grid=(ncb, n_seq // nsq),
        in_specs=[pl.BlockSpec((Lb, cw), cb(0)), pl.BlockSpec((Lb, cw), cb(1)), pl.BlockSpec((Lb, cw), cb(2)),
                  pl.BlockSpec((3, 3, cw), lambda j, b: (0, 0, j)),
                  pl.BlockSpec((2, 1, cw), lambda j, b: (0, 0, j)),
                  pl.BlockSpec((L, L), lambda j, b: (0, 0), pipeline_mode=once),
                  pl.BlockSpec((L, L), lambda j, b: (0, 0), pipeline_mode=once),
                  pl.BlockSpec((2, L, cw), lambda j, b: (0, 0, j), pipeline_mode=once),
                  pl.BlockSpec((2, L, cw), lambda j, b: (0, 0, j), pipeline_mode=once),
                  pl.BlockSpec((2, 1, cw), lambda j, b: (0, 0, j))],
        out_specs=pl.BlockSpec((Lb, cw), lambda j, b: (b, j)),
        out_shape=jax.ShapeDtypeStruct((n_seq * L, HYENA_W), F32),
        scratch_shapes=[pltpu.VMEM((L, wide), BF16), pltpu.VMEM((L, wide), F32), pltpu.VMEM((L, wide), F32),
                        pltpu.VMEM((L, wide), BF16), pltpu.VMEM((L, wide), BF16)],
        compiler_params=pltpu.CompilerParams(dimension_semantics=("arbitrary", "arbitrary"),
                                             vmem_limit_bytes=VMEM_LIMIT),
        name="hyena_L%d" % L,
    )(proj, proj, proj, sw, p['hy_bias'].reshape(2, 1, HYENA_W), cm.astype(BF16), sm.astype(BF16),
      kre, kim, knyq)


def _row_max(x):
    t = x[:, :LANES]
    for c in range(1, x.shape[1] // LANES):
        t = jnp.maximum(t, x[:, c * LANES:(c + 1) * LANES])
    return jnp.max(t, axis=-1, keepdims=True)


def _row_sum(x):
    t = x[:, :LANES]
    for c in range(1, x.shape[1] // LANES):
        t = t + x[:, c * LANES:(c + 1) * LANES]
    return jnp.sum(t, axis=-1, keepdims=True)


def _head_masks():
    lane = lax.broadcasted_iota(jnp.int32, (1, KV_W), 1) // HEAD_DIM
    return [lane == h for h in range(ATTN_KV_HEADS)]


def _attend_groups(qgs, keys, vals, masks, sinks, hm):
    nh = len(hm)
    un = [(g, h) for g in range(len(qgs)) for h in range(nh)]
    qm = [jnp.where(hm[h], qgs[g], 0.0).astype(BF16) for g, h in un]
    s = [[_dot_nt(qm[u], k) for k in keys] for u in range(len(un))]
    s = [[sc if mk is None else jnp.where(mk, sc, -jnp.inf) for sc, mk in zip(s[u], masks)] for u in range(len(un))]
    m = [jnp.maximum(_row_max(jnp.concatenate(s[u], axis=1)), sinks[g][h]) for u, (g, h) in enumerate(un)]
    p = [[jnp.exp(sc - m[u]) for sc in s[u]] for u in range(len(un))]
    den = [_row_sum(jnp.concatenate(p[u], axis=1)) + jnp.exp(sinks[g][h] - m[u]) for u, (g, h) in enumerate(un)]
    pv = [[_dot(pc, v) for pc, v in zip(p[u], vals)] for u in range(len(un))]
    accs = [jnp.zeros(q.shape, F32) for q in qgs]
    for u, (g, h) in enumerate(un):
        tot = pv[u][0]
        for t in pv[u][1:]:
            tot = tot + t
        accs[g] = jnp.where(hm[h], tot / den[u], accs[g])
    return accs


def _ctx_attn_kernel(q_ref, k_ref, v_ref, sink_ref, o_ref, *, S):
    scale = HEAD_DIM ** -0.5
    hm = _head_masks()
    k = k_ref[...].astype(BF16)
    v = v_ref[...].astype(BF16)
    for g in range(GQA_GROUP):
        qg = q_ref[:, g * KV_W:(g + 1) * KV_W] * scale
        sinks = [jnp.broadcast_to(sink_ref[0:1, h * GQA_GROUP + g:h * GQA_GROUP + g + 1], (S, 1))
                 for h in range(ATTN_KV_HEADS)]
        o_ref[:, g * KV_W:(g + 1) * KV_W] = _attend_groups([qg], [k], [v], [None], [sinks], hm)[0]


def _ctx_attention(proj, sink, *, S, n_seq):
    return pl.pallas_call(
        functools.partial(_ctx_attn_kernel, S=S),
        grid=(n_seq,),
        in_specs=[pl.BlockSpec((S, ATTN_W), lambda b: (b, C_Q // ATTN_W)),
                  pl.BlockSpec((S, KV_W), lambda b: (b, C_AK // KV_W)),
                  pl.BlockSpec((S, KV_W), lambda b: (b, C_AV // KV_W)),
                  pl.BlockSpec((1, LANES), lambda b: (0, 0))],
        out_specs=pl.BlockSpec((S, ATTN_W), lambda b: (b, 0)),
        out_shape=jax.ShapeDtypeStruct((n_seq * S, ATTN_W), F32),
        compiler_params=pltpu.CompilerParams(dimension_semantics=("arbitrary",),
                                             vmem_limit_bytes=VMEM_LIMIT),
        name="ctx_attention",
    )(proj, proj, proj, sink)


def _rope(x, cos, sin_signed, first_half):
    n = x.shape[-1]
    partner = jnp.where(first_half, pltpu.roll(x, n - HEAD_DIM // 2, 1), pltpu.roll(x, HEAD_DIM // 2, 1))
    return x * cos + partner * sin_signed


def _lat_attn_kernel(q_ref, kp_ref, kc_ref, kn_ref, vp_ref, vc_ref, vn_ref, ck_ref, cv_ref,
                     cq_ref, sq_ref, cp_ref, sp_ref, cc_ref, sc_ref, cn_ref, sn_ref, sink_ref, o_ref, *, T):
    i = pl.program_id(1)
    scale = HEAD_DIM ** -0.5
    hm = _head_masks()
    fh_q = (lax.broadcasted_iota(jnp.int32, (1, ATTN_W), 1) % HEAD_DIM) < HEAD_DIM // 2
    fh_k = (lax.broadcasted_iota(jnp.int32, (1, KV_W), 1) % HEAD_DIM) < HEAD_DIM // 2
    q = _rope(q_ref[...], cq_ref[...], sq_ref[...], fh_q) * scale
    kl = jnp.concatenate([_rope(kp_ref[...], cp_ref[...], sp_ref[...], fh_k),
                          _rope(kc_ref[...], cc_ref[...], sc_ref[...], fh_k),
                          _rope(kn_ref[...], cn_ref[...], sn_ref[...], fh_k)], axis=0).astype(BF16)
    vl = jnp.concatenate([vp_ref[...], vc_ref[...], vn_ref[...]], axis=0).astype(BF16)
    kc = ck_ref[...].astype(BF16)
    vc = cv_ref[...].astype(BF16)
    qpos = i * BLOCK + lax.broadcasted_iota(jnp.int32, (BLOCK, 3 * BLOCK), 0)
    kpos = (i - 1) * BLOCK + lax.broadcasted_iota(jnp.int32, (BLOCK, 3 * BLOCK), 1)
    ok = (jnp.abs(qpos - kpos) <= WINDOW) & (kpos >= 0) & (kpos < T)
    qgs = [q[:, g * KV_W:(g + 1) * KV_W] for g in range(GQA_GROUP)]
    sinks = [[jnp.broadcast_to(sink_ref[0:1, h * GQA_GROUP + g:h * GQA_GROUP + g + 1], (BLOCK, 1))
              for h in range(ATTN_KV_HEADS)] for g in range(GQA_GROUP)]
    accs = _attend_groups(qgs, [kl, kc], [vl, vc], [ok, None], sinks, hm)
    for g in range(GQA_GROUP):
        o_ref[:, g * KV_W:(g + 1) * KV_W] = accs[g]


def _rope_tables(T):
    rows = T // GRID_W
    row = np.repeat(np.arange(rows), GRID_W).astype(np.float32)
    col = np.tile(np.arange(GRID_W), rows).astype(np.float32)
    nf = HEAD_DIM // 4
    inv = (ROPE_BASE ** (-np.arange(nf, dtype=np.float32) / nf)).astype(np.float32)
    ang = np.concatenate([row[:, None] * inv, col[:, None] * inv], -1).astype(np.float32)
    cos = np.cos(ang).astype(np.float32)
    sin = np.sin(ang).astype(np.float32)
    cos_h = np.concatenate([cos, cos], -1)
    sin_h = np.concatenate([-sin, sin], -1)
    return cos_h, sin_h


def _lat_attention(proj, cache_k, cache_v, sink, *, T, n_seq, row_blk0):
    nb = T // BLOCK
    P = cache_k.shape[1]
    cos_h, sin_h = _rope_tables(T)
    cq = jnp.asarray(np.tile(cos_h, (1, ATTN_HEADS)))
    sq = jnp.asarray(np.tile(sin_h, (1, ATTN_HEADS)))
    ck = jnp.asarray(np.tile(cos_h, (1, ATTN_KV_HEADS)))
    sk = jnp.asarray(np.tile(sin_h, (1, ATTN_KV_HEADS)))
    r0 = row_blk0 * nb
    prv = lambda i: jnp.maximum(i - 1, 0)
    nxt = lambda i: jnp.minimum(i + 1, nb - 1)
    kcol, vcol = C_AK // KV_W, C_AV // KV_W
    kv = lambda col, f: pl.BlockSpec((BLOCK, KV_W), lambda b, i: (r0 + b * nb + f(i), col))
    tab = lambda w, f: pl.BlockSpec((BLOCK, w), lambda b, i: (f(i), 0))
    cur = lambda i: i
    return pl.pallas_call(
        functools.partial(_lat_attn_kernel, T=T),
        grid=(n_seq, nb),
        in_specs=[pl.BlockSpec((BLOCK, ATTN_W), lambda b, i: (r0 + b * nb + i, C_Q // ATTN_W)),
                  kv(kcol, prv), kv(kcol, cur), kv(kcol, nxt),
                  kv(vcol, prv), kv(vcol, cur), kv(vcol, nxt),
                  pl.BlockSpec((None, P, KV_W), lambda b, i: (b, 0, 0)),
                  pl.BlockSpec((None, P, KV_W), lambda b, i: (b, 0, 0)),
                  tab(ATTN_W, cur), tab(ATTN_W, cur),
                  tab(KV_W, prv), tab(KV_W, prv), tab(KV_W, cur), tab(KV_W, cur), tab(KV_W, nxt), tab(KV_W, nxt),
                  pl.BlockSpec((1, LANES), lambda b, i: (0, 0))],
        out_specs=pl.BlockSpec((BLOCK, ATTN_W), lambda b, i: (b * nb + i, 0)),
        out_shape=jax.ShapeDtypeStruct((n_seq * T, ATTN_W), F32),
        compiler_params=pltpu.CompilerParams(dimension_semantics=("arbitrary", "arbitrary"),
                                             vmem_limit_bytes=VMEM_LIMIT),
        name="latent_attention",
    )(proj, proj, proj, proj, proj, proj, proj, cache_k, cache_v, cq, sq, ck, sk, ck, sk, ck, sk, sink)


def _layer_norm(y, g, b):
    mu = jnp.mean(y, axis=-1, keepdims=True)
    d = y - mu
    var = jnp.mean(d * d, axis=-1, keepdims=True)
    return d * lax.rsqrt(var + LN_EPS) * g + b


def _outproj_kernel(ac_ref, bc_ref, cc_ref, al_ref, bl_ref, cl_ref, xc_ref, xl_ref, mod_ref, w_ref,
                    g_ref, be_ref, x1_ref, h2_ref, *, tm, n_ctx_tiles):
    r1, r2 = RWKV_W, RWKV_W + HYENA_W
    sub = 128

    def tile(a_ref, b_ref, c_ref, x_ref):
        ms = []
        for s in range(tm // sub):
            rows = slice(s * sub, (s + 1) * sub)
            ms.append(jnp.dot(a_ref[rows, :].astype(BF16), w_ref[0:r1, :], preferred_element_type=F32)
                      + jnp.dot(b_ref[rows, :].astype(BF16), w_ref[r1:r2, :], preferred_element_type=F32)
                      + jnp.dot(c_ref[rows, :].astype(BF16), w_ref[r2:, :], preferred_element_type=F32))
        for s in range(tm // sub):
            rows = slice(s * sub, (s + 1) * sub)
            gate1 = mod_ref[0, 2:3, :]
            shift2 = mod_ref[0, 3:4, :]
            scale2 = mod_ref[0, 4:5, :]
            x1 = _layer_norm(DN_ALPHA * x_ref[rows, :] + gate1 * ms[s], g_ref[...], be_ref[...])
            x1_ref[rows, :] = x1
            h2_ref[rows, :] = (x1 * (1.0 + scale2) + shift2).astype(BF16)

    pl.when(pl.program_id(0) < n_ctx_tiles)(lambda: tile(ac_ref, bc_ref, cc_ref, xc_ref))
    pl.when(pl.program_id(0) >= n_ctx_tiles)(lambda: tile(al_ref, bl_ref, cl_ref, xl_ref))


def _out_projection(mix_ctx, mix_lat, x_src, mod, w_out, l, g, b, tm=256):
    xc, xl, n_ctx, n_lat, lat_off, lat_len = x_src
    N = n_ctx + n_lat
    nct = n_ctx // tm
    nlt = n_lat // tm
    row = lambda w: pl.BlockSpec((tm, w), lambda i: (i, 0))
    ctx = lambda w: _two_source_specs(tm, w, nct, nlt, 0)[0]
    lat = lambda w: _two_source_specs(tm, w, nct, nlt, 0)[1]
    xc_spec, xl_spec = _two_source_specs(tm, D_MODEL, nct, nlt, lat_off // tm)
    full = lambda r: pl.BlockSpec((r, D_MODEL), lambda i: (0, 0))
    return pl.pallas_call(
        functools.partial(_outproj_kernel, tm=tm, n_ctx_tiles=nct),
        grid=(N // tm,),
        in_specs=[ctx(RWKV_W), ctx(HYENA_W), ctx(ATTN_W), lat(RWKV_W), lat(HYENA_W), lat(ATTN_W),
                  xc_spec, xl_spec,
                  _mod_spec(l, tm, n_ctx, lat_len),
                  pl.BlockSpec((None, D_MODEL, D_MODEL), lambda i: (l, 0, 0)), full(1), full(1)],
        out_specs=[row(D_MODEL), row(D_MODEL)],
        out_shape=[jax.ShapeDtypeStruct((N, D_MODEL), F32), jax.ShapeDtypeStruct((N, D_MODEL), BF16)],
        compiler_params=pltpu.CompilerParams(dimension_semantics=("arbitrary",),
                                             vmem_limit_bytes=VMEM_LIMIT),
        name="out_projection",
    )(*mix_ctx, *mix_lat, xc, xl, mod, w_out, g.reshape(1, D_MODEL), b.reshape(1, D_MODEL))


HALO = 16


def _ffn_kernel(hp_ref, h_ref, hn_ref, x1_ref, mod_ref, wa_ref, wb_ref, cwa_ref, cwb_ref, wd_ref, g_ref, be_ref,
                o_ref, hx_scr, ua_scr, ub_scr, acc_scr, *, tm, tf, nsub, t0, n_ctx, s_ctx, s_lat):
    i = pl.program_id(0) + t0
    j = pl.program_id(1)

    @pl.when(j == 0)
    def _():
        hx_scr[0:HALO, :] = hp_ref[...]
        hx_scr[HALO:HALO + tm, :] = h_ref[...]
        hx_scr[HALO + tm:, :] = hn_ref[...]
        acc_scr[...] = jnp.zeros_like(acc_scr)

    hx = hx_scr[...]
    ua_scr[...] = jnp.dot(hx, wa_ref[...], preferred_element_type=F32)
    ub_scr[...] = jnp.dot(hx, wb_ref[...], preferred_element_type=F32)
    grow = i * tm + lax.broadcasted_iota(jnp.int32, (tm, 1), 0)
    pos = jnp.where(grow < n_ctx, grow % s_ctx, (grow - n_ctx) % s_lat)
    slen = jnp.where(grow < n_ctx, s_ctx, s_lat)
    mprev = jnp.where(pos == 0, 0.0, 1.0).astype(F32)
    mnext = jnp.where(pos == slen - 1, 0.0, 1.0).astype(F32)

    def conv(u_scr, cw_ref):
        up = u_scr[HALO - 1:HALO - 1 + tm, :] * mprev
        uc = u_scr[HALO:HALO + tm, :]
        un = u_scr[HALO + 1:HALO + 1 + tm, :] * mnext
        return up * cw_ref[0:1, :] + uc * cw_ref[1:2, :] + un * cw_ref[2:3, :]

    a = conv(ua_scr, cwa_ref)
    b = conv(ub_scr, cwb_ref)
    act = (a * _sigmoid(a) * b).astype(BF16)
    acc_scr[...] += jnp.dot(act, wd_ref[...], preferred_element_type=F32)

    @pl.when(j == pl.num_programs(1) - 1)
    def _():
        for s in range(nsub):
            rows = slice(s * MOD_BLK, (s + 1) * MOD_BLK)
            gate2 = mod_ref[0, 5:6, :]
            o_ref[rows, :] = _layer_norm(DN_ALPHA * x1_ref[rows, :] + gate2 * acc_scr[rows, :],
                                         g_ref[...], be_ref[...])


def _ffn(h2, x1, mod, wa, wb, cw, wd, l, g, b, *, n_ctx, s_ctx, s_lat, row0=0, n_rows=None, tm=512, tf=512):
    N = x1.shape[0]
    n_rows = N if n_rows is None else n_rows
    t0 = row0 // tm
    nsub = tm // MOD_BLK
    nh = tm // HALO
    last_h = N // HALO - 1
    pad = ((0, 0), (0, D_FF_PAD - D_FF))
    cwa, cwb = jnp.pad(cw[:, :D_FF], pad), jnp.pad(cw[:, D_FF:], pad)
    return pl.pallas_call(
        functools.partial(_ffn_kernel, tm=tm, tf=tf, nsub=nsub, t0=t0, n_ctx=n_ctx, s_ctx=s_ctx, s_lat=s_lat),
        grid=(n_rows // tm, D_FF_PAD // tf),
        in_specs=[pl.BlockSpec((HALO, D_MODEL), lambda i, j: (jnp.maximum((i + t0) * nh - 1, 0), 0)),
                  pl.BlockSpec((tm, D_MODEL), lambda i, j: (i + t0, 0)),
                  pl.BlockSpec((HALO, D_MODEL), lambda i, j: (jnp.minimum((i + t0 + 1) * nh, last_h), 0)),
                  pl.BlockSpec((tm, D_MODEL), lambda i, j: (i + t0, 0)),
                  _mod_spec(l, tm, n_ctx, s_lat, t0=t0, rank=2),
                  pl.BlockSpec((None, D_MODEL, tf), lambda i, j: (l, 0, j)),
                  pl.BlockSpec((None, D_MODEL, tf), lambda i, j: (l, 0, j)),
                  pl.BlockSpec((3, tf), lambda i, j: (0, j)),
                  pl.BlockSpec((3, tf), lambda i, j: (0, j)),
                  pl.BlockSpec((None, tf, D_MODEL), lambda i, j: (l, j, 0)),
                  pl.BlockSpec((1, D_MODEL), lambda i, j: (0, 0)),
                  pl.BlockSpec((1, D_MODEL), lambda i, j: (0, 0))],
        out_specs=pl.BlockSpec((tm, D_MODEL), lambda i, j: (i, 0)),
        out_shape=jax.ShapeDtypeStruct((n_rows, D_MODEL), F32),
        scratch_shapes=[pltpu.VMEM((tm + 2 * HALO, D_MODEL), BF16),
                        pltpu.VMEM((tm + 2 * HALO, tf), F32),
                        pltpu.VMEM((tm + 2 * HALO, tf), F32),
                        pltpu.VMEM((tm, D_MODEL), F32)],
        compiler_params=pltpu.CompilerParams(dimension_semantics=("arbitrary", "arbitrary"),
                                             vmem_limit_bytes=VMEM_LIMIT),
        name="conv_ffn",
    )(h2, h2, h2, x1, mod, wa, wb, cwa, cwb, wd, g.reshape(1, D_MODEL), b.reshape(1, D_MODEL))


def _prep_up_kernel(a_ref, b_ref, wa_ref, wb_ref):
    zeros = jnp.zeros((a_ref.shape[0], D_FF_PAD - D_FF), BF16)
    wa_ref[:, :D_FF] = a_ref[...].astype(BF16)
    wa_ref[:, D_FF:] = zeros
    wb_ref[:, :D_FF] = b_ref[...].astype(BF16)
    wb_ref[:, D_FF:] = zeros


def _prep_down_kernel(w_ref, o_ref, *, n_real):
    keep = pl.program_id(1) < n_real
    o_ref[...] = jnp.where(keep, w_ref[...], 0.0).astype(BF16)


def _prep_in_kernel(w_ref, w1_ref, a1_ref, o_ref):
    n = w_ref.shape[-1]
    o_ref[:, :C_Q] = w_ref[:, :C_Q].astype(BF16)
    for g in range(GQA_GROUP):
        for h in range(ATTN_KV_HEADS):
            src = C_Q + (h * GQA_GROUP + g) * HEAD_DIM
            dst = C_Q + (g * ATTN_KV_HEADS + h) * HEAD_DIM
            o_ref[:, dst:dst + HEAD_DIM] = w_ref[:, src:src + HEAD_DIM].astype(BF16)
    o_ref[:, C_AK:n] = w_ref[:, C_AK:n].astype(BF16)
    o_ref[:, n:] = jnp.concatenate([w1_ref[0], w1_ref[1], a1_ref[0], a1_ref[1]], axis=1).astype(BF16)


def _prep_out_kernel(w_ref, o_ref):
    r0 = RWKV_W + HYENA_W
    o_ref[:r0, :] = w_ref[:r0, :].astype(BF16)
    for g in range(GQA_GROUP):
        for h in range(ATTN_KV_HEADS):
            src = r0 + (h * GQA_GROUP + g) * HEAD_DIM
            dst = r0 + (g * ATTN_KV_HEADS + h) * HEAD_DIM
            o_ref[dst:dst + HEAD_DIM, :] = w_ref[src:src + HEAD_DIM, :].astype(BF16)


def _prep_weights(w_in, rwkv_w1, rwkv_a1, w_out, ffn_w_up, ffn_w_down):
    L, D = w_in.shape[0], D_MODEL
    cp = lambda n: pltpu.CompilerParams(dimension_semantics=("arbitrary",) * n, vmem_limit_bytes=VMEM_LIMIT)
    tr = 128
    half = jax.ShapeDtypeStruct((L, D, D_FF_PAD), BF16)
    wa, wb = pl.pallas_call(
        _prep_up_kernel,
        grid=(L, D // tr),
        in_specs=[pl.BlockSpec((None, tr, D_FF), lambda l, i: (l, i, 0)),
                  pl.BlockSpec((None, tr, D_FF), lambda l, i: (l, i, 1))],
        out_specs=[pl.BlockSpec((None, tr, D_FF_PAD), lambda l, i: (l, i, 0))] * 2,
        out_shape=[half, half], compiler_params=cp(2), name="prep_ffn_up",
    )(ffn_w_up, ffn_w_up)
    nr = D_FF // LANES
    wd = pl.pallas_call(
        functools.partial(_prep_down_kernel, n_real=nr),
        grid=(L, D_FF_PAD // LANES),
        in_specs=[pl.BlockSpec((None, LANES, D), lambda l, j: (l, jnp.minimum(j, nr - 1), 0))],
        out_specs=pl.BlockSpec((None, LANES, D), lambda l, j: (l, j, 0)),
        out_shape=jax.ShapeDtypeStruct((L, D_FF_PAD, D), BF16), compiler_params=cp(2), name="prep_ffn_down",
    )(ffn_w_down)
    tr2 = 256
    n_in = w_in.shape[-1]
    w_ext = pl.pallas_call(
        _prep_in_kernel,
        grid=(L, D // tr2),
        in_specs=[pl.BlockSpec((None, tr2, n_in), lambda l, i: (l, i, 0)),
                  pl.BlockSpec((None, 2, tr2, LORA), lambda l, i: (l, 0, i, 0)),
                  pl.BlockSpec((None, 2, tr2, LORA), lambda l, i: (l, 0, i, 0))],
        out_specs=pl.BlockSpec((None, tr2, IN_W_EXT), lambda l, i: (l, i, 0)),
        out_shape=jax.ShapeDtypeStruct((L, D, IN_W_EXT), BF16), compiler_params=cp(2), name="prep_in",
    )(w_in, rwkv_w1, rwkv_a1)
    wo = pl.pallas_call(
        _prep_out_kernel,
        grid=(L,),
        in_specs=[pl.BlockSpec((None, D, D), lambda l: (l, 0, 0))],
        out_specs=pl.BlockSpec((None, D, D), lambda l: (l, 0, 0)),
        out_shape=jax.ShapeDtypeStruct((L, D, D), BF16), compiler_params=cp(1), name="prep_out",
    )(w_out)
    return w_ext, wo, wa, wb, wd


def kernel(x_prompt, x_sample, c, state_rwkv, cache_k, cache_v, c_ctx, w_mod, b_mod, w_in, rwkv_w0, rwkv_w1, rwkv_w2, rwkv_a0, rwkv_a1, rwkv_a2, rwkv_k_k, rwkv_k_a, rwkv_r_k, rwkv_lnx_g, rwkv_lnx_b, hy_short_w, hy_f_w1, hy_f_b1, hy_f_freq1, hy_f_w2, hy_f_b2, hy_f_freq2, hy_f_w3, hy_bias, attn_sink, w_out, ln1_g, ln1_b, ffn_w_up, ffn_conv_w, ffn_w_down, ln2_g, ln2_b):
    params = dict(rwkv_w0=rwkv_w0, rwkv_w2=rwkv_w2, rwkv_a0=rwkv_a0, rwkv_a2=rwkv_a2,
                  rwkv_k_k=rwkv_k_k, rwkv_k_a=rwkv_k_a, rwkv_r_k=rwkv_r_k,
                  rwkv_lnx_g=rwkv_lnx_g, rwkv_lnx_b=rwkv_lnx_b,
                  hy_short_w=hy_short_w, hy_f_w1=hy_f_w1, hy_f_b1=hy_f_b1, hy_f_freq1=hy_f_freq1,
                  hy_f_w2=hy_f_w2, hy_f_b2=hy_f_b2, hy_f_freq2=hy_f_freq2, hy_f_w3=hy_f_w3,
                  hy_bias=hy_bias)
    L = w_in.shape[0]
    Bc, Sc, D = x_prompt.shape
    Bl, Tl, _ = x_sample.shape
    P = cache_k.shape[2]
    n_ctx = Bc * Sc
    n_lat = Bl * Tl
    N = n_ctx + n_lat
    assert Sc % MOD_BLK == 0 and Tl % MOD_BLK == 0 and n_ctx % Tl == 0

    n_cond = 1 + Bl
    cond = jnp.zeros((16, D), F32).at[0].set(c_ctx).at[1:n_cond].set(c)
    mod_all = _modulation(cond, w_mod, b_mod)
    mod_tab = mod_all.reshape(L, 16, 6, D)

    cm_c, sm_c = (jnp.asarray(m, F32) for m in _dft_mats(Sc))
    cm_l, sm_l = (jnp.asarray(m, F32) for m in _dft_mats(Tl))
    sink_pad = jnp.zeros((L, 1, LANES), F32).at[:, 0, :ATTN_HEADS].set(attn_sink)

    w_ext, wo, wa, wb, wd = _prep_weights(w_in, rwkv_w1, rwkv_a1, w_out, ffn_w_up, ffn_w_down)
    x_src = (x_prompt.reshape(n_ctx, D), x_sample.reshape(n_lat, D), n_ctx, n_lat, 0, Tl)
    states, keys, vals = [], [], []
    for l in range(L):
        p = {n: a[l] for n, a in params.items()}
        proj = _in_projection(x_src, mod_tab, w_ext, l)
        keys.append(proj[:n_ctx, C_AK:C_AK + KV_W].reshape(Bc, Sc, ATTN_KV_HEADS, HEAD_DIM))
        vals.append(proj[:n_ctx, C_AV:C_AV + KV_W].reshape(Bc, Sc, ATTN_KV_HEADS, HEAD_DIM))

        ya_c, s_fin = _rwkv(proj, p, None, T=Sc, n_seq=Bc, row_blk0=0, nsq=RWKV_CTX_SEQS)
        ya_l, _ = _rwkv(proj, p, state_rwkv[:, l], T=Tl, n_seq=Bl, row_blk0=n_ctx // Tl)
        states.append(s_fin)

        yb_c = _hyena(proj, p, _hyena_filter(p, Sc, cm_c, sm_c), cm_c, sm_c, L=Sc, n_seq=Bc, row_blk0=0,
                      nsq=HYENA_CTX_SEQS)
        yb_l = _hyena(proj, p, _hyena_filter(p, Tl, cm_l, sm_l), cm_l, sm_l, L=Tl, n_seq=Bl,
                      row_blk0=n_ctx // Tl)

        yc_c = _ctx_attention(proj, sink_pad[l], S=Sc, n_seq=Bc)
        yc_l = _lat_attention(proj, cache_k[:, l].reshape(Bl, P, KV_W), cache_v[:, l].reshape(Bl, P, KV_W),
                              sink_pad[l], T=Tl, n_seq=Bl, row_blk0=n_ctx // Tl)

        x1, h2 = _out_projection((ya_c, yb_c, yc_c), (ya_l, yb_l, yc_l), x_src, mod_tab,
                                 wo, l, ln1_g[l], ln1_b[l])
        ffn = functools.partial(_ffn, h2, x1, mod_tab, wa, wb, ffn_conv_w[l], wd, l, ln2_g[l], ln2_b[l],
                                n_ctx=n_ctx, s_ctx=Sc, s_lat=Tl)
        if l < L - 1:
            x = ffn()
            x_src = (x, x, n_ctx, n_lat, n_ctx, Tl)
        else:
            y_prompt = ffn(row0=0, n_rows=n_ctx).reshape(Bc, Sc, D)
            y_sample = ffn(row0=n_ctx, n_rows=n_lat).reshape(Bl, Tl, D)

    new_state = jnp.stack(states, 1)
    new_k = jnp.stack(keys, 1)
    new_v = jnp.stack(vals, 1)
    return (y_prompt, y_sample, new_state, new_k, new_v)
```

```python
import functools
import math

import numpy as np
import jax
import jax.numpy as jnp
from jax import lax
from jax.experimental import pallas as pl
from jax.experimental.pallas import tpu as pltpu

F32 = jnp.float32
BF16 = jnp.bfloat16

D_MODEL = 2048
DEPTH = 4
GRID_W = 64
HEAD_DIM = 64
RWKV_W = 768
RWKV_HEADS = 12
HYENA_W = 512
ATTN_W = 768
ATTN_HEADS = 12
ATTN_KV_HEADS = 4
GQA_GROUP = 3
KV_W = 256
LORA = 64
GN_EPS = 64e-5
FILTER_EMB = 33
FILTER_HID = 64
HY_MIN_DECAY = math.log(1e-2) / 1.5
HY_MAX_DECAY = math.log(1e-2) / 0.3
WINDOW = 128
BLOCK = 128
ROPE_BASE = 10000.0
D_FF = 5504
LN_EPS = 1e-5
DN_ALPHA = (2 * DEPTH) ** 0.25

C_R, C_K, C_V, C_G = 0, 768, 1536, 2304
C_HY = 3072
C_Q, C_AK, C_AV = 4608, 5376, 5632
C_LORA = 5888
IN_W_EXT = 6144

MOD_BLK = 256
CHUNK = 64
RWKV_CB = 8
RWKV_CTX_SEQS = 4
HYENA_CTX_SEQS = 4
HYENA_ROW_BLK = 1024
LANES = 128
D_FF_PAD = 5632
VMEM_LIMIT = 56 * 1024 * 1024


def _dot(a, b):
    return jnp.dot(a.astype(BF16), b.astype(BF16), preferred_element_type=F32)


def _dot_nt(a, b):
    return lax.dot_general(a.astype(BF16), b.astype(BF16), (((1,), (1,)), ((), ())),
                           preferred_element_type=F32)


def _split(x):
    hi = x.astype(BF16)
    lo = (x - hi.astype(F32)).astype(BF16)
    return hi, lo


def _dot3_dims(a, b, dims):
    ah, al = _split(a)
    bh, bl = _split(b)
    dg = functools.partial(lax.dot_general, dimension_numbers=dims, preferred_element_type=F32)
    return dg(ah, bh) + (dg(ah, bl) + dg(al, bh))


_NN = (((1,), (0,)), ((), ()))
_NT = (((1,), (1,)), ((), ()))
_TN = (((0,), (0,)), ((), ()))


def _dot3(a, b):
    return _dot3_dims(a, b, _NN)


def _sigmoid(x):
    return 1.0 / (1.0 + jnp.exp(-x))


def _softplus(x):
    return jnp.maximum(x, 0.0) + jnp.log(1.0 + jnp.exp(-jnp.abs(x)))


def _mod_kernel(c_ref, w_ref, b_ref, o_ref):
    c = c_ref[...]
    s = c * _sigmoid(c)
    o_ref[...] = _dot(s, w_ref[...]) + b_ref[...]


def _modulation(cond, w_mod, b_mod):
    L = w_mod.shape[0]
    R = cond.shape[0]
    tn = 1024
    return pl.pallas_call(
        _mod_kernel,
        grid=(L, 6 * D_MODEL // tn),
        in_specs=[pl.BlockSpec((R, D_MODEL), lambda l, j: (0, 0)),
                  pl.BlockSpec((None, D_MODEL, tn), lambda l, j: (l, 0, j)),
                  pl.BlockSpec((None, 1, tn), lambda l, j: (l, 0, j))],
        out_specs=pl.BlockSpec((None, R, tn), lambda l, j: (l, 0, j)),
        out_shape=jax.ShapeDtypeStruct((L, R, 6 * D_MODEL), F32),
        compiler_params=pltpu.CompilerParams(dimension_semantics=("arbitrary", "arbitrary"),
                                             vmem_limit_bytes=VMEM_LIMIT),
        name="modulation",
    )(cond, w_mod, b_mod.reshape(L, 1, 6 * D_MODEL))


def _two_source_specs(tm, width, nct, nlt, lat_off):
    ctx = pl.BlockSpec((tm, width), lambda i: (jnp.minimum(i, nct - 1), 0))
    lat = pl.BlockSpec((tm, width), lambda i: (jnp.clip(i - nct, 0, nlt - 1) + lat_off, 0))
    return ctx, lat


def _mod_spec(l, tm, n_ctx, lat_len, t0=0, rank=1):
    def row(i):
        tok = (i + t0) * tm
        return jnp.where(tok < n_ctx, 0, 1 + (tok - n_ctx) // lat_len)

    if rank == 1:
        return pl.BlockSpec((None, 1, 6, D_MODEL), lambda i: (l, row(i), 0, 0))
    return pl.BlockSpec((None, 1, 6, D_MODEL), lambda i, j: (l, row(i), 0, 0))


def _inproj_kernel(xc_ref, xl_ref, mod_ref, w_ref, o_ref, h_scr, *, tn, n_ctx_tiles):
    shift = mod_ref[0, 0:1, :]
    scale = mod_ref[0, 1:2, :]

    def modulate(x_ref):
        h_scr[...] = (x_ref[...] * (1.0 + scale) + shift).astype(BF16)

    pl.when(pl.program_id(0) < n_ctx_tiles)(lambda: modulate(xc_ref))
    pl.when(pl.program_id(0) >= n_ctx_tiles)(lambda: modulate(xl_ref))
    h = h_scr[...]
    for j in range(IN_W_EXT // tn):
        cols = slice(j * tn, (j + 1) * tn)
        o_ref[:, cols] = jnp.dot(h, w_ref[:, cols], preferred_element_type=F32)


def _in_projection(x_src, mod, w_ext, l, tn=512):
    xc, xl, n_ctx, n_lat, lat_off, lat_len = x_src
    N = n_ctx + n_lat
    tm = MOD_BLK
    nct = n_ctx // tm
    ctx_spec, lat_spec = _two_source_specs(tm, D_MODEL, nct, n_lat // tm, lat_off // tm)
    return pl.pallas_call(
        functools.partial(_inproj_kernel, tn=tn, n_ctx_tiles=nct),
        grid=(N // tm,),
        in_specs=[ctx_spec, lat_spec,
                  _mod_spec(l, tm, n_ctx, lat_len),
                  pl.BlockSpec((None, D_MODEL, IN_W_EXT), lambda i: (l, 0, 0), pipeline_mode=pl.Buffered(1))],
        out_specs=pl.BlockSpec((tm, IN_W_EXT), lambda i: (i, 0)),
        out_shape=jax.ShapeDtypeStruct((N, IN_W_EXT), F32),
        scratch_shapes=[pltpu.VMEM((tm, D_MODEL), BF16)],
        compiler_params=pltpu.CompilerParams(dimension_semantics=("arbitrary",),
                                             vmem_limit_bytes=VMEM_LIMIT),
        name="in_projection",
    )(xc, xl, mod, w_ext)


PREC = dict(seg=1, cum=2, a=1, neu=1, post=1, mn=1, b_o=1, b_s=1, gn=2)


def _dotp(a, b, dims, passes):
    dg = functools.partial(lax.dot_general, dimension_numbers=dims, preferred_element_type=F32)
    if passes == 1:
        return dg(a.astype(BF16), b.astype(BF16))
    ah, al = _split(a)
    if passes == 2:
        bh = b.astype(BF16)
        return dg(ah, bh) + dg(al, bh)
    bh, bl = _split(b)
    return dg(ah, bh) + (dg(ah, bl) + dg(al, bh))


def _dotp_rhs(a, b, passes):
    dg = functools.partial(lax.dot_general, dimension_numbers=_NN, preferred_element_type=F32)
    ab = a.astype(BF16)
    if passes == 1:
        return dg(ab, b.astype(BF16))
    bh, bl = _split(b)
    return dg(ab, bh) + dg(ab, bl)


def _rwkv_kernel(*refs, T, nsq, has_init, cb):
    (r_ref, k_ref, v_ref, g_ref, lo_ref, w0_ref, w2_ref, a0_ref, a2_ref,
     kk_ref, ka_ref, rk_ref, lng_ref, lnb_ref) = refs[:14]
    pos = 14
    init_ref = None
    if has_init:
        init_ref = refs[pos]
        pos += 1
    y_ref, sfin_ref = refs[pos:pos + 2]
    rp_scr, ov_scr, m_scr, n_scr, o_scr, bonus_scr, s_scr = refs[pos + 2:]

    nC = T // CHUNK
    nCs = nC // nsq
    C = CHUNK
    H = HEAD_DIM
    P = PREC
    C2 = 2 * C
    row = lax.broadcasted_iota(jnp.int32, (C, C), 0)
    col = lax.broadcasted_iota(jnp.int32, (C, C), 1)
    tri_f = jnp.where(col <= row, 1.0, 0.0).astype(F32)
    lrow = lax.broadcasted_iota(jnp.int32, (LANES, LANES), 0)
    lcol = lax.broadcasted_iota(jnp.int32, (LANES, LANES), 1)
    same_head = (lrow // H) == (lcol // H)
    head_ones = jnp.where(same_head, 1.0, 0.0).astype(F32)
    eye = lrow == lcol
    rt_, ct_ = lrow % C, lcol % C
    strict_m = [same_head & (ct_ < rt_), same_head & (ct_ > rt_)]
    incl_m = [same_head & (ct_ <= rt_), same_head & (ct_ >= rt_)]
    lane = lax.broadcasted_iota(jnp.int32, (1, LANES), 1)
    hmask = [lane < H, lane >= H]

    def stack_heads(x):
        return jnp.concatenate([jnp.where(hmask[0], x, 0.0), jnp.where(hmask[1], x, 0.0)], axis=0)

    kkw = kk_ref[...]
    kaw = ka_ref[...]
    rkw = rk_ref[...]
    zl = jnp.zeros((LORA, LANES), F32)
    lora_w = jnp.concatenate(
        [jnp.concatenate([w if j == i else zl for j in range(4)], axis=1)
         for i, w in enumerate([w2_ref[0], w2_ref[1], a2_ref[0], a2_ref[1]])], axis=0).astype(BF16)

    def phase_a(ci, carry):
        units = []
        for cc in range(cb):
            c = ci * cb + cc
            rows = pl.ds(pl.multiple_of(c * C, C), C)
            r = r_ref[rows, :]
            k = k_ref[rows, :]
            v = v_ref[rows, :]
            lo = lo_ref[rows, :]
            kkraw = k * kkw
            ss = _dotp(kkraw * kkraw, head_ones, _NN, P['seg'])
            kkn = kkraw / jnp.maximum(jnp.sqrt(ss), 1e-12)
            bonus = jnp.zeros((C, LANES), F32)
            vm = stack_heads(v)
            lo_in = jnp.concatenate([jnp.tanh(lo[:, :2 * LORA]), lo[:, 2 * LORA:]], axis=1)
            lo_out = _dot(lo_in, lora_w)
            lds, alphas = [], []
            for d in range(2):
                wl = w0_ref[d] + lo_out[:, d * LANES:(d + 1) * LANES]
                lds.append(-jnp.exp(-_softplus(-wl) - 0.5))
                alphas.append(_sigmoid(a0_ref[d] + lo_out[:, (2 + d) * LANES:(3 + d) * LANES]))
            pref = _dotp_rhs(tri_f, jnp.concatenate(lds, axis=1), P['cum'])
            for d in range(2):
                ld, alpha = lds[d], alphas[d]
                kd = k * (1.0 + (alpha - 1.0) * kaw)
                bb = kkn * alpha
                bonus = bonus + _dotp(r * kd * rkw, head_ones, _NN, P['seg']) * v
                pf = pref[:, d * LANES:(d + 1) * LANES]
                g_incl = pf if d == 0 else pf[C - 1:C, :] - pf + ld
                g_excl = g_incl - ld
                g_end = g_incl[C - 1:C, :] if d == 0 else g_incl[0:1, :]
                ginv = jnp.exp(-g_incl)
                dec_end = jnp.exp(g_end - g_incl)
                units.append(dict(
                    d=d, c=c, vm=vm,
                    at=stack_heads(-kkn * jnp.exp(g_excl)), rt=stack_heads(r * jnp.exp(g_incl)),
                    bt=stack_heads(bb * ginv), kt=stack_heads(kd * ginv),
                    bend=stack_heads(bb * dec_end), kend=stack_heads(kd * dec_end), gend=jnp.exp(g_end)))
            bonus_scr[rows, :] = bonus
            o_scr[rows, :] = jnp.zeros((C, LANES), F32)

        for u in units:
            ar = jnp.concatenate([u['at'], u['rt']], axis=0)
            bk = jnp.concatenate([u['bt'], u['kt']], axis=0)
            u['a'] = _dotp(ar, bk, _NT, P['a'])
        for u in units:
            a = u['a']
            st, ic = strict_m[u['d']], incl_m[u['d']]
            u['p'] = jnp.where(st, a[:C2, :C2], 0.0)
            u['a_ak'] = jnp.where(st, a[:C2, C2:], 0.0)
            u['a_rb'] = jnp.where(ic, a[C2:, :C2], 0.0)
            u['a_rk'] = jnp.where(ic, a[C2:, C2:], 0.0)
        rvs = [_dotp(u['a_ak'], u['vm'], _NN, P['post']) for u in units]
        sq = [_dotp(u['p'], u['p'], _NN, P['neu']) for u in units]
        for i, u in enumerate(units):
            u['x'] = jnp.where(eye, 1.0, 0.0) + u['p']
            u['p'] = sq[i]
        for it in range(1, 6):
            res = [_dotp(u['p'], jnp.concatenate([u['x'], u['p']], axis=1) if it < 5 else u['x'], _NN, P['neu'])
                   for u in units]
            for i, u in enumerate(units):
                u['x'] = u['x'] + res[i][:, :C2]
                if it < 5:
                    u['p'] = res[i][:, C2:]
        rhs = [_dotp(u['x'], jnp.concatenate([u['at'], rvs[i]], axis=1), _NN, P['post'])
               for i, u in enumerate(units)]
        for i, u in enumerate(units):
            u['rhs'] = rhs[i]
        xs = [_dotp(jnp.concatenate([u['a_rb'], u['a_rk']], axis=1),
                    jnp.concatenate([u['rhs'], jnp.concatenate([jnp.zeros_like(u['vm']), u['vm']], axis=1)], axis=0),
                    _NN, P['post']) for u in units]
        ms = [_dotp(u['rhs'][:, :LANES], u['bend'], _TN, P['mn']) for u in units]
        ns = [_dotp(jnp.concatenate([u['rhs'][:, LANES:], u['vm']], axis=0),
                    jnp.concatenate([u['bend'], u['kend']], axis=0), _TN, P['mn']) for u in units]
        for i, u in enumerate(units):
            d, c = u['d'], u['c']
            rp = u['rt'] + xs[i][:, :LANES]
            ov = xs[i][:, LANES:]
            rp_scr[d, c] = rp[:C] + rp[C:]
            ov_scr[d, c] = ov[:C] + ov[C:]
            m_scr[d, c] = jnp.where(eye, jnp.broadcast_to(u['gend'], (LANES, LANES)), 0.0) + ms[i]
            n_scr[d, c] = ns[i]
        return carry

    lax.fori_loop(0, nC // cb, phase_a, 0)

    zero_blk = jnp.zeros((H, H), F32)
    for q in range(nsq):
        for d in range(2):
            if has_init:
                s_scr[q, d] = jnp.concatenate([jnp.concatenate([init_ref[q, d, 0], zero_blk], axis=1),
                                               jnp.concatenate([zero_blk, init_ref[q, d, 1]], axis=1)], axis=0)
            else:
                s_scr[q, d] = jnp.zeros((LANES, LANES), F32)

    def phase_b(i, carry):
        idx = [(q, d, q * nCs + (i if d == 0 else nCs - 1 - i)) for q in range(nsq) for d in range(2)]
        ss = [s_scr[q, d] for q, d, c in idx]
        os_ = [_dotp(rp_scr[d, c], ss[j], _NT, P['b_o']) for j, (q, d, c) in enumerate(idx)]
        sn = [_dotp(ss[j], m_scr[d, c], _NN, P['b_s']) for j, (q, d, c) in enumerate(idx)]
        for j, (q, d, c) in enumerate(idx):
            s_scr[q, d] = sn[j] + n_scr[d, c]
            rows = pl.ds(c * C if isinstance(c, int) else pl.multiple_of(c * C, C), C)
            o_scr[rows, :] = o_scr[rows, :] + (os_[j] + ov_scr[d, c])
        return carry

    if nCs <= 4:
        for i in range(nCs):
            phase_b(i, 0)
    else:
        lax.fori_loop(0, nCs, phase_b, 0, unroll=4)

    for q in range(nsq):
        for d in range(2):
            s = s_scr[q, d]
            sfin_ref[q, d, 0] = s[:H, :H]
            sfin_ref[q, d, 1] = s[H:, H:]

    lng = lng_ref[...]
    lnb = lnb_ref[...]

    rb = min(T, 256)

    def phase_c(c, carry):
        rows = pl.ds(pl.multiple_of(c * rb, rb), rb)
        o = o_scr[rows, :]
        mu = _dotp(o, head_ones, _NN, P['gn']) * (1.0 / H)
        dlt = o - mu
        var = _dotp(dlt * dlt, head_ones, _NN, P['gn']) * (1.0 / H)
        on = dlt * lax.rsqrt(var + GN_EPS) * lng + lnb
        y_ref[rows, :] = (on + bonus_scr[rows, :]) * _sigmoid(g_ref[rows, :])
        return carry

    lax.fori_loop(0, T // rb, phase_c, 0)


def _rwkv(proj, p, init, *, T, n_seq, row_blk0, nsq=1):
    seq_len, T = T, nsq * T
    nC = T // CHUNK
    npair = RWKV_HEADS // 2
    has_init = init is not None
    cb = lambda off: (lambda b, j: (b + row_blk0, off // LANES + j))
    vec = lambda b, j: (0, j)
    in_specs = [pl.BlockSpec((T, LANES), cb(C_R)),
                pl.BlockSpec((T, LANES), cb(C_K)),
                pl.BlockSpec((T, LANES), cb(C_V)),
                pl.BlockSpec((T, LANES), cb(C_G)),
                pl.BlockSpec((T, 4 * LORA), lambda b, j: (b + row_blk0, C_LORA // (4 * LORA))),
                pl.BlockSpec((2, 1, LANES), lambda b, j: (0, 0, j)),
                pl.BlockSpec((2, LORA, LANES), lambda b, j: (0, 0, j)),
                pl.BlockSpec((2, 1, LANES), lambda b, j: (0, 0, j)),
                pl.BlockSpec((2, LORA, LANES), lambda b, j: (0, 0, j)),
                pl.BlockSpec((1, LANES), vec), pl.BlockSpec((1, LANES), vec),
                pl.BlockSpec((1, LANES), vec), pl.BlockSpec((1, LANES), vec),
                pl.BlockSpec((1, LANES), vec)]
    args = [proj, proj, proj, proj, proj,
            p['rwkv_w0'].reshape(2, 1, RWKV_W), p['rwkv_w2'], p['rwkv_a0'].reshape(2, 1, RWKV_W), p['rwkv_a2'],
            p['rwkv_k_k'].reshape(1, RWKV_W), p['rwkv_k_a'].reshape(1, RWKV_W),
            p['rwkv_r_k'].reshape(1, RWKV_W), p['rwkv_lnx_g'].reshape(1, RWKV_W),
            p['rwkv_lnx_b'].reshape(1, RWKV_W)]
    st_spec = pl.BlockSpec((nsq, 2, 2, HEAD_DIM, HEAD_DIM), lambda b, j: (b, 0, j, 0, 0))
    if has_init:
        in_specs.append(st_spec)
        args.append(init)
    sc_o = lambda: pltpu.VMEM((2, nC, CHUNK, LANES), F32)
    sc_s = lambda: pltpu.VMEM((2, nC, LANES, LANES), F32)
    return pl.pallas_call(
        functools.partial(_rwkv_kernel, T=T, nsq=nsq, has_init=has_init, cb=min(RWKV_CB, nC)),
        grid=(n_seq // nsq, npair),
        in_specs=in_specs,
        out_specs=[pl.BlockSpec((T, LANES), lambda b, j: (b, j)), st_spec],
        out_shape=[jax.ShapeDtypeStruct((n_seq * seq_len, RWKV_W), F32),
                   jax.ShapeDtypeStruct((n_seq, 2, RWKV_HEADS, HEAD_DIM, HEAD_DIM), F32)],
        scratch_shapes=[sc_o(), sc_o(), sc_s(), sc_s(),
                        pltpu.VMEM((T, LANES), F32), pltpu.VMEM((T, LANES), F32),
                        pltpu.VMEM((nsq, 2, LANES, LANES), F32)],
        compiler_params=pltpu.CompilerParams(dimension_semantics=("arbitrary", "arbitrary"),
                                             vmem_limit_bytes=VMEM_LIMIT),
        name="rwkv_T%d" % T,
    )(*args)


def _dft_mats(L):
    f = np.arange(L, dtype=np.int64)
    ph = (np.outer(f, f) % (2 * L)).astype(np.float64) * (math.pi / L)
    return np.cos(ph), -np.sin(ph)


def _filter_features(L):
    t = np.linspace(0.0, 1.0, L, dtype=np.float32)[:, None]
    bands = (FILTER_EMB - 1) // 2
    t_res = np.arange(L, dtype=np.float32)[:, None]
    f = np.linspace(1e-4, bands - 1, bands, dtype=np.float32)[None, :]
    ang = (2.0 * math.pi * t_res * f / L).astype(np.float32)
    z = np.concatenate([t, np.cos(ang), np.sin(ang)], -1).astype(np.float32)
    zp = np.zeros((L, FILTER_HID), np.float32)
    zp[:, :FILTER_EMB] = z
    deltas = np.abs(np.linspace(HY_MIN_DECAY, HY_MAX_DECAY, HYENA_W, dtype=np.float32))
    return zp, t.astype(np.float32), deltas[None, :].astype(np.float32)


def _hyfilter_kernel(z_ref, t_ref, dl_ref, w1_ref, b1_ref, q1_ref, w2_ref, b2_ref, q2_ref, wc_ref, wa_ref,
                     hs_ref, hd_ref, knyq_ref, hm_scr, *, L):
    @pl.when((pl.program_id(0) == 0) & (pl.program_id(1) == 0))
    def _():
        h1 = jnp.sin(q1_ref[...] * (_dot3(z_ref[...], w1_ref[...]) + b1_ref[...]))
        hm_scr[...] = jnp.sin(q2_ref[...] * (_dot3(h1, w2_ref[...]) + b2_ref[...]))

    hm = hm_scr[...]
    win = jnp.exp(-t_ref[...] * dl_ref[...])
    lag = lax.broadcasted_iota(jnp.int32, (L, 1), 0)
    alt = jnp.where(lag % 2 == 0, 1.0, -1.0).astype(F32)
    hc = _dot3(hm, wc_ref[...]) * win
    ha = jnp.where(lag == 0, 0.0, _dot3(hm, wa_ref[...]) * win)
    hs_ref[...] = hc + ha
    hd_ref[...] = hc - ha
    knyq_ref[...] = jnp.sum((hc + ha) * alt, axis=0, keepdims=True) * (1.0 / (2 * L))


def _hyspec_kernel(cm_ref, sm_ref, hs_ref, hd_ref, kre_ref, kim_ref, *, L, tf):
    f = pl.program_id(2) * tf + lax.broadcasted_iota(jnp.int32, (tf, 1), 0)
    wf = jnp.where(f == 0, 1.0, 2.0).astype(F32) * (1.0 / (2 * L))
    kre_ref[...] = _dot(cm_ref[...], hs_ref[...]) * wf
    kim_ref[...] = _dot(sm_ref[...], hd_ref[...]) * wf


def _hyena_filter(p, L, cm, sm, cw=128):
    zp, t, dl = _filter_features(L)
    w1 = jnp.zeros((FILTER_HID, FILTER_HID), F32).at[:FILTER_EMB].set(p['hy_f_w1'])
    r2 = lambda a: a.reshape(1, -1)
    ncb = HYENA_W // cw
    full = lambda shape: pl.BlockSpec(shape, lambda n, j: (0,) * len(shape))
    shp = jax.ShapeDtypeStruct((2, L, HYENA_W), F32)
    hs, hd, knyq = pl.pallas_call(
        functools.partial(_hyfilter_kernel, L=L),
        grid=(2, ncb),
        in_specs=[full((L, FILTER_HID)), full((L, 1)),
                  pl.BlockSpec((1, cw), lambda n, j: (0, j)),
                  full((FILTER_HID, FILTER_HID)), full((1, FILTER_HID)), full((1, FILTER_HID)),
                  full((FILTER_HID, FILTER_HID)), full((1, FILTER_HID)), full((1, FILTER_HID)),
                  pl.BlockSpec((FILTER_HID, cw), lambda n, j: (0, n * 2 * ncb + j)),
                  pl.BlockSpec((FILTER_HID, cw), lambda n, j: (0, n * 2 * ncb + ncb + j))],
        out_specs=[pl.BlockSpec((None, L, cw), lambda n, j: (n, 0, j)),
                   pl.BlockSpec((None, L, cw), lambda n, j: (n, 0, j)),
                   pl.BlockSpec((None, 1, cw), lambda n, j: (n, 0, j))],
        out_shape=[shp, shp, jax.ShapeDtypeStruct((2, 1, HYENA_W), F32)],
        scratch_shapes=[pltpu.VMEM((L, FILTER_HID), F32)],
        compiler_params=pltpu.CompilerParams(dimension_semantics=("arbitrary", "arbitrary"),
                                             vmem_limit_bytes=VMEM_LIMIT),
        name="hyena_filter_L%d" % L,
    )(jnp.asarray(zp), jnp.asarray(t), jnp.asarray(dl), w1, r2(p['hy_f_b1']), r2(p['hy_f_freq1']),
      p['hy_f_w2'], r2(p['hy_f_b2']), r2(p['hy_f_freq2']), p['hy_f_w3'], p['hy_f_w3'])
    tf = min(512, L)
    sw = 256
    kre, kim = pl.pallas_call(
        functools.partial(_hyspec_kernel, L=L, tf=tf),
        grid=(2, HYENA_W // sw, L // tf),
        in_specs=[pl.BlockSpec((tf, L), lambda n, j, i: (i, 0)),
                  pl.BlockSpec((tf, L), lambda n, j, i: (i, 0)),
                  pl.BlockSpec((None, L, sw), lambda n, j, i: (n, 0, j)),
                  pl.BlockSpec((None, L, sw), lambda n, j, i: (n, 0, j))],
        out_specs=[pl.BlockSpec((None, tf, sw), lambda n, j, i: (n, i, j)),
                   pl.BlockSpec((None, tf, sw), lambda n, j, i: (n, i, j))],
        out_shape=[shp, shp],
        compiler_params=pltpu.CompilerParams(dimension_semantics=("arbitrary",) * 3,
                                             vmem_limit_bytes=VMEM_LIMIT),
        name="hyena_spectrum_L%d" % L,
    )(cm, sm, hs, hd)
    return kre, kim, knyq


def _hyena_kernel(v_ref, x1_ref, x2_ref, sw_ref, bias_ref, cm_ref, sm_ref, kre_ref, kim_ref, knyq_ref,
                  o_ref, zb_scr, zf_scr, xs_scr, yre_scr, yim_scr, *, L, rb, nsq):
    nrb = L // rb
    cw = v_ref.shape[-1]
    t = lax.broadcasted_iota(jnp.int32, (nsq * L, 1), 0) % L
    first = t == 0
    last = t == L - 1
    tl = lax.broadcasted_iota(jnp.int32, (L, 1), 0)
    alt_all = jnp.where(tl % 2 == 0, 1.0, -1.0).astype(F32)
    tb = lax.broadcasted_iota(jnp.int32, (rb, 1), 0)
    alt_blk = jnp.where(tb % 2 == 0, 1.0, -1.0).astype(F32)
    sw = sw_ref[...]
    tile = lambda a: jnp.concatenate([a] * nsq, axis=1) if nsq > 1 else a

    def short(ref, s):
        u = ref[...]
        prev = jnp.where(first, 0.0, pltpu.roll(u, 1, 0))
        nxt = jnp.where(last, 0.0, pltpu.roll(u, nsq * L - 1, 0))
        c = prev * sw[0, s][None, :] + u * sw[1, s][None, :] + nxt * sw[2, s][None, :]
        if nsq == 1:
            return c
        return jnp.concatenate([c[q * L:(q + 1) * L] for q in range(nsq)], axis=1)

    z0 = short(v_ref, 0)
    zf_scr[...] = z0
    zb_scr[...] = z0.astype(BF16)
    unyq = jnp.sum(z0 * alt_all, axis=0, keepdims=True)

    for n in range(2):
        xs_scr[...] = short(x1_ref if n == 0 else x2_ref, n + 1)

        def fwd(fb, carry):
            rows = pl.ds(pl.multiple_of(fb * rb, rb), rb)
            zb = zb_scr[...]
            ure = jnp.dot(cm_ref[rows, :], zb, preferred_element_type=F32)
            uim = jnp.dot(sm_ref[rows, :], zb, preferred_element_type=F32)
            kre = tile(kre_ref[n, rows, :])
            kim = tile(kim_ref[n, rows, :])
            yre_scr[rows, :] = (ure * kre - uim * kim).astype(BF16)
            yim_scr[rows, :] = (ure * kim + uim * kre).astype(BF16)
            return carry

        lax.fori_loop(0, nrb, fwd, 0)
        ynyq = unyq * tile(knyq_ref[n])
        bias = tile(bias_ref[n])

        def inv(ib, acc):
            r0 = pl.multiple_of(ib * rb, rb)
            rows = pl.ds(r0, rb)
            y = (jnp.dot(cm_ref[rows, :], yre_scr[...], preferred_element_type=F32)
                 + jnp.dot(sm_ref[rows, :], yim_scr[...], preferred_element_type=F32))
            y = y + alt_blk * ynyq + zf_scr[rows, :] * bias
            znew = xs_scr[rows, :] * y
            if n == 0:
                zf_scr[rows, :] = znew
                zb_scr[rows, :] = znew.astype(BF16)
                acc = acc + jnp.sum(znew * alt_blk, axis=0, keepdims=True)
            else:
                for q in range(nsq):
                    o_ref[pl.ds(q * L + r0, rb), :] = znew[:, q * cw:(q + 1) * cw]
            return acc

        unyq = lax.fori_loop(0, nrb, inv, jnp.zeros_like(unyq))


def _hyena(proj, p, filt, cm, sm, *, L, n_seq, row_blk0, nsq=1, cw=256):
    kre, kim, knyq = filt
    ncb = HYENA_W // cw
    cb = lambda s: (lambda j, b: (b + row_blk0, (C_HY + s * HYENA_W) // cw + j))
    once = pl.Buffered(1)
    sw = p['hy_short_w'].reshape(3, 3, HYENA_W)
    rb = min(HYENA_ROW_BLK, L)
    Lb, wide = nsq * L, nsq * cw
    return pl.pallas_call(
        functools.partial(_hyena_kernel, L=L, rb=rb, nsq=nsq),
        grid=(ncb, n_seq // nsq),
        in_specs=[pl.BlockSpec((Lb, cw), cb(0)), pl.BlockSpec((Lb, cw), cb(1)), pl.BlockSpec((Lb, cw), cb(2)),
                  pl.BlockSpec((3, 3, cw), lambda j, b: (0, 0, j)),
                  pl.BlockSpec((2, 1, cw), lambda j, b: (0, 0, j)),
                  pl.BlockSpec((L, L), lambda j, b: (0, 0), pipeline_mode=once),
                  pl.BlockSpec((L, L), lambda j, b: (0, 0), pipeline_mode=once),
                  pl.BlockSpec((2, L, cw), lambda j, b: (0, 0, j), pipeline_mode=once),
                  pl.BlockSpec((2, L, cw), lambda j, b: (0, 0, j), pipeline_mode=once),
                  pl.BlockSpec((2, 1, cw), lambda j, b: (0, 0, j))],
        out_specs=pl.BlockSpec((Lb, cw), lambda j, b: (b, j)),
        out_shape=jax.ShapeDtypeStruct((n_seq * L, HYENA_W), F32),
        scratch_shapes=[pltpu.VMEM((L, wide), BF16), pltpu.VMEM((L, wide), F32), pltpu.VMEM((L, wide), F32),
                        pltpu.VMEM((L, wide), BF16), pltpu.VMEM((L, wide), BF16)],
        compiler_params=pltpu.CompilerParams(dimension_semantics=("arbitrary", "arbitrary"),
                                             vmem_limit_bytes=VMEM_LIMIT),
        name="hyena_L%d" % L,
    )(proj, proj, proj, sw, p['hy_bias'].reshape(2, 1, HYENA_W), cm.astype(BF16), sm.astype(BF16),
      kre, kim, knyq)


def _row_max(x):
    t = x[:, :LANES]
    for c in range(1, x.shape[1] // LANES):
        t = jnp.maximum(t, x[:, c * LANES:(c + 1) * LANES])
    return jnp.max(t, axis=-1, keepdims=True)


def _row_sum(x):
    t = x[:, :LANES]
    for c in range(1, x.shape[1] // LANES):
        t = t + x[:, c * LANES:(c + 1) * LANES]
    return jnp.sum(t, axis=-1, keepdims=True)


def _head_masks():
    lane = lax.broadcasted_iota(jnp.int32, (1, KV_W), 1) // HEAD_DIM
    return [lane == h for h in range(ATTN_KV_HEADS)]


def _attend_groups(qgs, keys, vals, masks, sinks, hm):
    nh = len(hm)
    un = [(g, h) for g in range(len(qgs)) for h in range(nh)]
    qm = [jnp.where(hm[h], qgs[g], 0.0).astype(BF16) for g, h in un]
    s = [[_dot_nt(qm[u], k) for k in keys] for u in range(len(un))]
    s = [[sc if mk is None else jnp.where(mk, sc, -jnp.inf) for sc, mk in zip(s[u], masks)] for u in range(len(un))]
    m = [jnp.maximum(_row_max(jnp.concatenate(s[u], axis=1)), sinks[g][h]) for u, (g, h) in enumerate(un)]
    p = [[jnp.exp(sc - m[u]) for sc in s[u]] for u in range(len(un))]
    den = [_row_sum(jnp.concatenate(p[u], axis=1)) + jnp.exp(sinks[g][h] - m[u]) for u, (g, h) in enumerate(un)]
    pv = [[_dot(pc, v) for pc, v in zip(p[u], vals)] for u in range(len(un))]
    accs = [jnp.zeros(q.shape, F32) for q in qgs]
    for u, (g, h) in enumerate(un):
        tot = pv[u][0]
        for t in pv[u][1:]:
            tot = tot + t
        accs[g] = jnp.where(hm[h], tot / den[u], accs[g])
    return accs


def _ctx_attn_kernel(q_ref, k_ref, v_ref, sink_ref, o_ref, *, S):
    scale = HEAD_DIM ** -0.5
    hm = _head_masks()
    k = k_ref[...].astype(BF16)
    v = v_ref[...].astype(BF16)
    for g in range(GQA_GROUP):
        qg = q_ref[:, g * KV_W:(g + 1) * KV_W] * scale
        sinks = [jnp.broadcast_to(sink_ref[0:1, h * GQA_GROUP + g:h * GQA_GROUP + g + 1], (S, 1))
                 for h in range(ATTN_KV_HEADS)]
        o_ref[:, g * KV_W:(g + 1) * KV_W] = _attend_groups([qg], [k], [v], [None], [sinks], hm)[0]


def _ctx_attention(proj, sink, *, S, n_seq):
    return pl.pallas_call(
        functools.partial(_ctx_attn_kernel, S=S),
        grid=(n_seq,),
        in_specs=[pl.BlockSpec((S, ATTN_W), lambda b: (b, C_Q // ATTN_W)),
                  pl.BlockSpec((S, KV_W), lambda b: (b, C_AK // KV_W)),
                  pl.BlockSpec((S, KV_W), lambda b: (b, C_AV // KV_W)),
                  pl.BlockSpec((1, LANES), lambda b: (0, 0))],
        out_specs=pl.BlockSpec((S, ATTN_W), lambda b: (b, 0)),
        out_shape=jax.ShapeDtypeStruct((n_seq * S, ATTN_W), F32),
        compiler_params=pltpu.CompilerParams(dimension_semantics=("arbitrary",),
                                             vmem_limit_bytes=VMEM_LIMIT),
        name="ctx_attention",
    )(proj, proj, proj, sink)


def _rope(x, cos, sin_signed, first_half):
    n = x.shape[-1]
    partner = jnp.where(first_half, pltpu.roll(x, n - HEAD_DIM // 2, 1), pltpu.roll(x, HEAD_DIM // 2, 1))
    return x * cos + partner * sin_signed


def _lat_attn_kernel(q_ref, kp_ref, kc_ref, kn_ref, vp_ref, vc_ref, vn_ref, ck_ref, cv_ref,
                     cq_ref, sq_ref, cp_ref, sp_ref, cc_ref, sc_ref, cn_ref, sn_ref, sink_ref, o_ref, *, T):
    i = pl.program_id(1)
    scale = HEAD_DIM ** -0.5
    hm = _head_masks()
    fh_q = (lax.broadcasted_iota(jnp.int32, (1, ATTN_W), 1) % HEAD_DIM) < HEAD_DIM // 2
    fh_k = (lax.broadcasted_iota(jnp.int32, (1, KV_W), 1) % HEAD_DIM) < HEAD_DIM // 2
    q = _rope(q_ref[...], cq_ref[...], sq_ref[...], fh_q) * scale
    kl = jnp.concatenate([_rope(kp_ref[...], cp_ref[...], sp_ref[...], fh_k),
                          _rope(kc_ref[...], cc_ref[...], sc_ref[...], fh_k),
                          _rope(kn_ref[...], cn_ref[...], sn_ref[...], fh_k)], axis=0).astype(BF16)
    vl = jnp.concatenate([vp_ref[...], vc_ref[...], vn_ref[...]], axis=0).astype(BF16)
    kc = ck_ref[...].astype(BF16)
    vc = cv_ref[...].astype(BF16)
    qpos = i * BLOCK + lax.broadcasted_iota(jnp.int32, (BLOCK, 3 * BLOCK), 0)
    kpos = (i - 1) * BLOCK + lax.broadcasted_iota(jnp.int32, (BLOCK, 3 * BLOCK), 1)
    ok = (jnp.abs(qpos - kpos) <= WINDOW) & (kpos >= 0) & (kpos < T)
    qgs = [q[:, g * KV_W:(g + 1) * KV_W] for g in range(GQA_GROUP)]
    sinks = [[jnp.broadcast_to(sink_ref[0:1, h * GQA_GROUP + g:h * GQA_GROUP + g + 1], (BLOCK, 1))
              for h in range(ATTN_KV_HEADS)] for g in range(GQA_GROUP)]
    accs = _attend_groups(qgs, [kl, kc], [vl, vc], [ok, None], sinks, hm)
    for g in range(GQA_GROUP):
        o_ref[:, g * KV_W:(g + 1) * KV_W] = accs[g]


def _rope_tables(T):
    rows = T // GRID_W
    row = np.repeat(np.arange(rows), GRID_W).astype(np.float32)
    col = np.tile(np.arange(GRID_W), rows).astype(np.float32)
    nf = HEAD_DIM // 4
    inv = (ROPE_BASE ** (-np.arange(nf, dtype=np.float32) / nf)).astype(np.float32)
    ang = np.concatenate([row[:, None] * inv, col[:, None] * inv], -1).astype(np.float32)
    cos = np.cos(ang).astype(np.float32)
    sin = np.sin(ang).astype(np.float32)
    cos_h = np.concatenate([cos, cos], -1)
    sin_h = np.concatenate([-sin, sin], -1)
    return cos_h, sin_h


def _lat_attention(proj, cache_k, cache_v, sink, *, T, n_seq, row_blk0):
    nb = T // BLOCK
    P = cache_k.shape[1]
    cos_h, sin_h = _rope_tables(T)
    cq = jnp.asarray(np.tile(cos_h, (1, ATTN_HEADS)))
    sq = jnp.asarray(np.tile(sin_h, (1, ATTN_HEADS)))
    ck = jnp.asarray(np.tile(cos_h, (1, ATTN_KV_HEADS)))
    sk = jnp.asarray(np.tile(sin_h, (1, ATTN_KV_HEADS)))
    r0 = row_blk0 * nb
    prv = lambda i: jnp.maximum(i - 1, 0)
    nxt = lambda i: jnp.minimum(i + 1, nb - 1)
    kcol, vcol = C_AK // KV_W, C_AV // KV_W
    kv = lambda col, f: pl.BlockSpec((BLOCK, KV_W), lambda b, i: (r0 + b * nb + f(i), col))
    tab = lambda w, f: pl.BlockSpec((BLOCK, w), lambda b, i: (f(i), 0))
    cur = lambda i: i
    return pl.pallas_call(
        functools.partial(_lat_attn_kernel, T=T),
        grid=(n_seq, nb),
        in_specs=[pl.BlockSpec((BLOCK, ATTN_W), lambda b, i: (r0 + b * nb + i, C_Q // ATTN_W)),
                  kv(kcol, prv), kv(kcol, cur), kv(kcol, nxt),
                  kv(vcol, prv), kv(vcol, cur), kv(vcol, nxt),
                  pl.BlockSpec((None, P, KV_W), lambda b, i: (b, 0, 0)),
                  pl.BlockSpec((None, P, KV_W), lambda b, i: (b, 0, 0)),
                  tab(ATTN_W, cur), tab(ATTN_W, cur),
                  tab(KV_W, prv), tab(KV_W, prv), tab(KV_W, cur), tab(KV_W, cur), tab(KV_W, nxt), tab(KV_W, nxt),
                  pl.BlockSpec((1, LANES), lambda b, i: (0, 0))],
        out_specs=pl.BlockSpec((BLOCK, ATTN_W), lambda b, i: (b * nb + i, 0)),
        out_shape=jax.ShapeDtypeStruct((n_seq * T, ATTN_W), F32),
        compiler_params=pltpu.CompilerParams(dimension_semantics=("arbitrary", "arbitrary"),
                                             vmem_limit_bytes=VMEM_LIMIT),
        name="latent_attention",
    )(proj, proj, proj, proj, proj, proj, proj, cache_k, cache_v, cq, sq, ck, sk, ck, sk, ck, sk, sink)


def _layer_norm(y, g, b):
    mu = jnp.mean(y, axis=-1, keepdims=True)
    d = y - mu
    var = jnp.mean(d * d, axis=-1, keepdims=True)
    return d * lax.rsqrt(var + LN_EPS) * g + b


def _outproj_kernel(ac_ref, bc_ref, cc_ref, al_ref, bl_ref, cl_ref, xc_ref, xl_ref, mod_ref, w_ref,
                    g_ref, be_ref, x1_ref, h2_ref, *, tm, n_ctx_tiles):
    r1, r2 = RWKV_W, RWKV_W + HYENA_W
    sub = 128

    def tile(a_ref, b_ref, c_ref, x_ref):
        ms = []
        for s in range(tm // sub):
            rows = slice(s * sub, (s + 1) * sub)
            ms.append(jnp.dot(a_ref[rows, :].astype(BF16), w_ref[0:r1, :], preferred_element_type=F32)
                      + jnp.dot(b_ref[rows, :].astype(BF16), w_ref[r1:r2, :], preferred_element_type=F32)
                      + jnp.dot(c_ref[rows, :].astype(BF16), w_ref[r2:, :], preferred_element_type=F32))
        for s in range(tm // sub):
            rows = slice(s * sub, (s + 1) * sub)
            gate1 = mod_ref[0, 2:3, :]
            shift2 = mod_ref[0, 3:4, :]
            scale2 = mod_ref[0, 4:5, :]
            x1 = _layer_norm(DN_ALPHA * x_ref[rows, :] + gate1 * ms[s], g_ref[...], be_ref[...])
            x1_ref[rows, :] = x1
            h2_ref[rows, :] = (x1 * (1.0 + scale2) + shift2).astype(BF16)

    pl.when(pl.program_id(0) < n_ctx_tiles)(lambda: tile(ac_ref, bc_ref, cc_ref, xc_ref))
    pl.when(pl.program_id(0) >= n_ctx_tiles)(lambda: tile(al_ref, bl_ref, cl_ref, xl_ref))


def _out_projection(mix_ctx, mix_lat, x_src, mod, w_out, l, g, b, tm=256):
    xc, xl, n_ctx, n_lat, lat_off, lat_len = x_src
    N = n_ctx + n_lat
    nct = n_ctx // tm
    nlt = n_lat // tm
    row = lambda w: pl.BlockSpec((tm, w), lambda i: (i, 0))
    ctx = lambda w: _two_source_specs(tm, w, nct, nlt, 0)[0]
    lat = lambda w: _two_source_specs(tm, w, nct, nlt, 0)[1]
    xc_spec, xl_spec = _two_source_specs(tm, D_MODEL, nct, nlt, lat_off // tm)
    full = lambda r: pl.BlockSpec((r, D_MODEL), lambda i: (0, 0))
    return pl.pallas_call(
        functools.partial(_outproj_kernel, tm=tm, n_ctx_tiles=nct),
        grid=(N // tm,),
        in_specs=[ctx(RWKV_W), ctx(HYENA_W), ctx(ATTN_W), lat(RWKV_W), lat(HYENA_W), lat(ATTN_W),
                  xc_spec, xl_spec,
                  _mod_spec(l, tm, n_ctx, lat_len),
                  pl.BlockSpec((None, D_MODEL, D_MODEL), lambda i: (l, 0, 0)), full(1), full(1)],
        out_specs=[row(D_MODEL), row(D_MODEL)],
        out_shape=[jax.ShapeDtypeStruct((N, D_MODEL), F32), jax.ShapeDtypeStruct((N, D_MODEL), BF16)],
        compiler_params=pltpu.CompilerParams(dimension_semantics=("arbitrary",),
                                             vmem_limit_bytes=VMEM_LIMIT),
        name="out_projection",
    )(*mix_ctx, *mix_lat, xc, xl, mod, w_out, g.reshape(1, D_MODEL), b.reshape(1, D_MODEL))


HALO = 16


def _ffn_kernel(hp_ref, h_ref, hn_ref, x1_ref, mod_ref, wa_ref, wb_ref, cwa_ref, cwb_ref, wd_ref, g_ref, be_ref,
                o_ref, hx_scr, ua_scr, ub_scr, acc_scr, *, tm, tf, nsub, t0, n_ctx, s_ctx, s_lat):
    i = pl.program_id(0) + t0
    j = pl.program_id(1)

    @pl.when(j == 0)
    def _():
        hx_scr[0:HALO, :] = hp_ref[...]
        hx_scr[HALO:HALO + tm, :] = h_ref[...]
        hx_scr[HALO + tm:, :] = hn_ref[...]
        acc_scr[...] = jnp.zeros_like(acc_scr)

    hx = hx_scr[...]
    ua_scr[...] = jnp.dot(hx, wa_ref[...], preferred_element_type=F32)
    ub_scr[...] = jnp.dot(hx, wb_ref[...], preferred_element_type=F32)
    grow = i * tm + lax.broadcasted_iota(jnp.int32, (tm, 1), 0)
    pos = jnp.where(grow < n_ctx, grow % s_ctx, (grow - n_ctx) % s_lat)
    slen = jnp.where(grow < n_ctx, s_ctx, s_lat)
    mprev = jnp.where(pos == 0, 0.0, 1.0).astype(F32)
    mnext = jnp.where(pos == slen - 1, 0.0, 1.0).astype(F32)

    def conv(u_scr, cw_ref):
        up = u_scr[HALO - 1:HALO - 1 + tm, :] * mprev
        uc = u_scr[HALO:HALO + tm, :]
        un = u_scr[HALO + 1:HALO + 1 + tm, :] * mnext
        return up * cw_ref[0:1, :] + uc * cw_ref[1:2, :] + un * cw_ref[2:3, :]

    a = conv(ua_scr, cwa_ref)
    b = conv(ub_scr, cwb_ref)
    act = (a * _sigmoid(a) * b).astype(BF16)
    acc_scr[...] += jnp.dot(act, wd_ref[...], preferred_element_type=F32)

    @pl.when(j == pl.num_programs(1) - 1)
    def _():
        for s in range(nsub):
            rows = slice(s * MOD_BLK, (s + 1) * MOD_BLK)
            gate2 = mod_ref[0, 5:6, :]
            o_ref[rows, :] = _layer_norm(DN_ALPHA * x1_ref[rows, :] + gate2 * acc_scr[rows, :],
                                         g_ref[...], be_ref[...])


def _ffn(h2, x1, mod, wa, wb, cw, wd, l, g, b, *, n_ctx, s_ctx, s_lat, row0=0, n_rows=None, tm=512, tf=512):
    N = x1.shape[0]
    n_rows = N if n_rows is None else n_rows
    t0 = row0 // tm
    nsub = tm // MOD_BLK
    nh = tm // HALO
    last_h = N // HALO - 1
    pad = ((0, 0), (0, D_FF_PAD - D_FF))
    cwa, cwb = jnp.pad(cw[:, :D_FF], pad), jnp.pad(cw[:, D_FF:], pad)
    return pl.pallas_call(
        functools.partial(_ffn_kernel, tm=tm, tf=tf, nsub=nsub, t0=t0, n_ctx=n_ctx, s_ctx=s_ctx, s_lat=s_lat),
        grid=(n_rows // tm, D_FF_PAD // tf),
        in_specs=[pl.BlockSpec((HALO, D_MODEL), lambda i, j: (jnp.maximum((i + t0) * nh - 1, 0), 0)),
                  pl.BlockSpec((tm, D_MODEL), lambda i, j: (i + t0, 0)),
                  pl.BlockSpec((HALO, D_MODEL), lambda i, j: (jnp.minimum((i + t0 + 1) * nh, last_h), 0)),
                  pl.BlockSpec((tm, D_MODEL), lambda i, j: (i + t0, 0)),
                  _mod_spec(l, tm, n_ctx, s_lat, t0=t0, rank=2),
                  pl.BlockSpec((None, D_MODEL, tf), lambda i, j: (l, 0, j)),
                  pl.BlockSpec((None, D_MODEL, tf), lambda i, j: (l, 0, j)),
                  pl.BlockSpec((3, tf), lambda i, j: (0, j)),
                  pl.BlockSpec((3, tf), lambda i, j: (0, j)),
                  pl.BlockSpec((None, tf, D_MODEL), lambda i, j: (l, j, 0)),
                  pl.BlockSpec((1, D_MODEL), lambda i, j: (0, 0)),
                  pl.BlockSpec((1, D_MODEL), lambda i, j: (0, 0))],
        out_specs=pl.BlockSpec((tm, D_MODEL), lambda i, j: (i, 0)),
        out_shape=jax.ShapeDtypeStruct((n_rows, D_MODEL), F32),
        scratch_shapes=[pltpu.VMEM((tm + 2 * HALO, D_MODEL), BF16),
                        pltpu.VMEM((tm + 2 * HALO, tf), F32),
                        pltpu.VMEM((tm + 2 * HALO, tf), F32),
                        pltpu.VMEM((tm, D_MODEL), F32)],
        compiler_params=pltpu.CompilerParams(dimension_semantics=("arbitrary", "arbitrary"),
                                             vmem_limit_bytes=VMEM_LIMIT),
        name="conv_ffn",
    )(h2, h2, h2, x1, mod, wa, wb, cwa, cwb, wd, g.reshape(1, D_MODEL), b.reshape(1, D_MODEL))


def _prep_up_kernel(a_ref, b_ref, wa_ref, wb_ref):
    zeros = jnp.zeros((a_ref.shape[0], D_FF_PAD - D_FF), BF16)
    wa_ref[:, :D_FF] = a_ref[...].astype(BF16)
    wa_ref[:, D_FF:] = zeros
    wb_ref[:, :D_FF] = b_ref[...].astype(BF16)
    wb_ref[:, D_FF:] = zeros


def _prep_down_kernel(w_ref, o_ref, *, n_real):
    keep = pl.program_id(1) < n_real
    o_ref[...] = jnp.where(keep, w_ref[...], 0.0).astype(BF16)


def _prep_in_kernel(w_ref, w1_ref, a1_ref, o_ref):
    n = w_ref.shape[-1]
    o_ref[:, :C_Q] = w_ref[:, :C_Q].astype(BF16)
    for g in range(GQA_GROUP):
        for h in range(ATTN_KV_HEADS):
            src = C_Q + (h * GQA_GROUP + g) * HEAD_DIM
            dst = C_Q + (g * ATTN_KV_HEADS + h) * HEAD_DIM
            o_ref[:, dst:dst + HEAD_DIM] = w_ref[:, src:src + HEAD_DIM].astype(BF16)
    o_ref[:, C_AK:n] = w_ref[:, C_AK:n].astype(BF16)
    o_ref[:, n:] = jnp.concatenate([w1_ref[0], w1_ref[1], a1_ref[0], a1_ref[1]], axis=1).astype(BF16)


def _prep_out_kernel(w_ref, o_ref):
    r0 = RWKV_W + HYENA_W
    o_ref[:r0, :] = w_ref[:r0, :].astype(BF16)
    for g in range(GQA_GROUP):
        for h in range(ATTN_KV_HEADS):
            src = r0 + (h * GQA_GROUP + g) * HEAD_DIM
            dst = r0 + (g * ATTN_KV_HEADS + h) * HEAD_DIM
            o_ref[dst:dst + HEAD_DIM, :] = w_ref[src:src + HEAD_DIM, :].astype(BF16)


def _prep_weights(w_in, rwkv_w1, rwkv_a1, w_out, ffn_w_up, ffn_w_down):
    L, D = w_in.shape[0], D_MODEL
    cp = lambda n: pltpu.CompilerParams(dimension_semantics=("arbitrary",) * n, vmem_limit_bytes=VMEM_LIMIT)
    tr = 128
    half = jax.ShapeDtypeStruct((L, D, D_FF_PAD), BF16)
    wa, wb = pl.pallas_call(
        _prep_up_kernel,
        grid=(L, D // tr),
        in_specs=[pl.BlockSpec((None, tr, D_FF), lambda l, i: (l, i, 0)),
                  pl.BlockSpec((None, tr, D_FF), lambda l, i: (l, i, 1))],
        out_specs=[pl.BlockSpec((None, tr, D_FF_PAD), lambda l, i: (l, i, 0))] * 2,
        out_shape=[half, half], compiler_params=cp(2), name="prep_ffn_up",
    )(ffn_w_up, ffn_w_up)
    nr = D_FF // LANES
    wd = pl.pallas_call(
        functools.partial(_prep_down_kernel, n_real=nr),
        grid=(L, D_FF_PAD // LANES),
        in_specs=[pl.BlockSpec((None, LANES, D), lambda l, j: (l, jnp.minimum(j, nr - 1), 0))],
        out_specs=pl.BlockSpec((None, LANES, D), lambda l, j: (l, j, 0)),
        out_shape=jax.ShapeDtypeStruct((L, D_FF_PAD, D), BF16), compiler_params=cp(2), name="prep_ffn_down",
    )(ffn_w_down)
    tr2 = 256
    n_in = w_in.shape[-1]
    w_ext = pl.pallas_call(
        _prep_in_kernel,
        grid=(L, D // tr2),
        in_specs=[pl.BlockSpec((None, tr2, n_in), lambda l, i: (l, i, 0)),
                  pl.BlockSpec((None, 2, tr2, LORA), lambda l, i: (l, 0, i, 0)),
                  pl.BlockSpec((None, 2, tr2, LORA), lambda l, i: (l, 0, i, 0))],
        out_specs=pl.BlockSpec((None, tr2, IN_W_EXT), lambda l, i: (l, i, 0)),
        out_shape=jax.ShapeDtypeStruct((L, D, IN_W_EXT), BF16), compiler_params=cp(2), name="prep_in",
    )(w_in, rwkv_w1, rwkv_a1)
    wo = pl.pallas_call(
        _prep_out_kernel,
        grid=(L,),
        in_specs=[pl.BlockSpec((None, D, D), lambda l: (l, 0, 0))],
        out_specs=pl.BlockSpec((None, D, D), lambda l: (l, 0, 0)),
        out_shape=jax.ShapeDtypeStruct((L, D, D), BF16), compiler_params=cp(1), name="prep_out",
    )(w_out)
    return w_ext, wo, wa, wb, wd


def kernel(x_prompt, x_sample, c, state_rwkv, cache_k, cache_v, c_ctx, w_mod, b_mod, w_in, rwkv_w0, rwkv_w1, rwkv_w2, rwkv_a0, rwkv_a1, rwkv_a2, rwkv_k_k, rwkv_k_a, rwkv_r_k, rwkv_lnx_g, rwkv_lnx_b, hy_short_w, hy_f_w1, hy_f_b1, hy_f_freq1, hy_f_w2, hy_f_b2, hy_f_freq2, hy_f_w3, hy_bias, attn_sink, w_out, ln1_g, ln1_b, ffn_w_up, ffn_conv_w, ffn_w_down, ln2_g, ln2_b):
    params = dict(rwkv_w0=rwkv_w0, rwkv_w2=rwkv_w2, rwkv_a0=rwkv_a0, rwkv_a2=rwkv_a2,
                  rwkv_k_k=rwkv_k_k, rwkv_k_a=rwkv_k_a, rwkv_r_k=rwkv_r_k,
                  rwkv_lnx_g=rwkv_lnx_g, rwkv_lnx_b=rwkv_lnx_b,
                  hy_short_w=hy_short_w, hy_f_w1=hy_f_w1, hy_f_b1=hy_f_b1, hy_f_freq1=hy_f_freq1,
                  hy_f_w2=hy_f_w2, hy_f_b2=hy_f_b2, hy_f_freq2=hy_f_freq2, hy_f_w3=hy_f_w3,
                  hy_bias=hy_bias)
    L = w_in.shape[0]
    Bc, Sc, D = x_prompt.shape
    Bl, Tl, _ = x_sample.shape
    P = cache_k.shape[2]
    n_ctx = Bc * Sc
    n_lat = Bl * Tl
    N = n_ctx + n_lat
    assert Sc % MOD_BLK == 0 and Tl % MOD_BLK == 0 and n_ctx % Tl == 0

    n_cond = 1 + Bl
    cond = jnp.zeros((16, D), F32).at[0].set(c_ctx).at[1:n_cond].set(c)
    mod_all = _modulation(cond, w_mod, b_mod)
    mod_tab = mod_all.reshape(L, 16, 6, D)

    cm_c, sm_c = (jnp.asarray(m, F32) for m in _dft_mats(Sc))
    cm_l, sm_l = (jnp.asarray(m, F32) for m in _dft_mats(Tl))
    sink_pad = jnp.zeros((L, 1, LANES), F32).at[:, 0, :ATTN_HEADS].set(attn_sink)

    w_ext, wo, wa, wb, wd = _prep_weights(w_in, rwkv_w1, rwkv_a1, w_out, ffn_w_up, ffn_w_down)
    x_src = (x_prompt.reshape(n_ctx, D), x_sample.reshape(n_lat, D), n_ctx, n_lat, 0, Tl)
    states, keys, vals = [], [], []
    for l in range(L):
        p = {n: a[l] for n, a in params.items()}
        proj = _in_projection(x_src, mod_tab, w_ext, l)
        keys.append(proj[:n_ctx, C_AK:C_AK + KV_W].reshape(Bc, Sc, ATTN_KV_HEADS, HEAD_DIM))
        vals.append(proj[:n_ctx, C_AV:C_AV + KV_W].reshape(Bc, Sc, ATTN_KV_HEADS, HEAD_DIM))

        ya_c, s_fin = _rwkv(proj, p, None, T=Sc, n_seq=Bc, row_blk0=0, nsq=RWKV_CTX_SEQS)
        ya_l, _ = _rwkv(proj, p, state_rwkv[:, l], T=Tl, n_seq=Bl, row_blk0=n_ctx // Tl)
        states.append(s_fin)

        yb_c = _hyena(proj, p, _hyena_filter(p, Sc, cm_c, sm_c), cm_c, sm_c, L=Sc, n_seq=Bc, row_blk0=0,
                      nsq=HYENA_CTX_SEQS)
        yb_l = _hyena(proj, p, _hyena_filter(p, Tl, cm_l, sm_l), cm_l, sm_l, L=Tl, n_seq=Bl,
                      row_blk0=n_ctx // Tl)

        yc_c = _ctx_attention(proj, sink_pad[l], S=Sc, n_seq=Bc)
        yc_l = _lat_attention(proj, cache_k[:, l].reshape(Bl, P, KV_W), cache_v[:, l].reshape(Bl, P, KV_W),
                              sink_pad[l], T=Tl, n_seq=Bl, row_blk0=n_ctx // Tl)

        x1, h2 = _out_projection((ya_c, yb_c, yc_c), (ya_l, yb_l, yc_l), x_src, mod_tab,
                                 wo, l, ln1_g[l], ln1_b[l])
        ffn = functools.partial(_ffn, h2, x1, mod_tab, wa, wb, ffn_conv_w[l], wd, l, ln2_g[l], ln2_b[l],
                                n_ctx=n_ctx, s_ctx=Sc, s_lat=Tl)
        if l < L - 1:
            x = ffn()
            x_src = (x, x, n_ctx, n_lat, n_ctx, Tl)
        else:
            y_prompt = ffn(row0=0, n_rows=n_ctx).reshape(Bc, Sc, D)
            y_sample = ffn(row0=n_ctx, n_rows=n_lat).reshape(Bl, Tl, D)

    new_state = jnp.stack(states, 1)
    new_k = jnp.stack(keys, 1)
    new_v = jnp.stack(vals, 1)
    return (y_prompt, y_sample, new_state, new_k, new_v)
```

```python
import functools
import math

import numpy as np
import jax
import jax.numpy as jnp
from jax import lax
from jax.experimental import pallas as pl
from jax.experimental.pallas import tpu as pltpu

F32 = jnp.float32
BF16 = jnp.bfloat16

D_MODEL = 2048
DEPTH = 4
GRID_W = 64
HEAD_DIM = 64
RWKV_W = 768
RWKV_HEADS = 12
HYENA_W = 512
ATTN_W = 768
ATTN_HEADS = 12
ATTN_KV_HEADS = 4
GQA_GROUP = 3
KV_W = 256
LORA = 64
GN_EPS = 64e-5
FILTER_EMB = 33
FILTER_HID = 64
HY_MIN_DECAY = math.log(1e-2) / 1.5
HY_MAX_DECAY = math.log(1e-2) / 0.3
WINDOW = 128
BLOCK = 128
ROPE_BASE = 10000.0
D_FF = 5504
LN_EPS = 1e-5
DN_ALPHA = (2 * DEPTH) ** 0.25

C_R, C_K, C_V, C_G = 0, 768, 1536, 2304
C_HY = 3072
C_Q, C_AK, C_AV = 4608, 5376, 5632
C_LORA = 5888
IN_W_EXT = 6144

MOD_BLK = 256
CHUNK = 64
RWKV_CB = 8
RWKV_CTX_SEQS = 4
HYENA_CTX_SEQS = 4
HYENA_ROW_BLK = 1024
LANES = 128
D_FF_PAD = 5632
VMEM_LIMIT = 56 * 1024 * 1024


def _dot(a, b):
    return jnp.dot(a.astype(BF16), b.astype(BF16), preferred_element_type=F32)


def _dot_nt(a, b):
    return lax.dot_general(a.astype(BF16), b.astype(BF16), (((1,), (1,)), ((), ())),
                           preferred_element_type=F32)


def _split(x):
    hi = x.astype(BF16)
    lo = (x - hi.astype(F32)).astype(BF16)
    return hi, lo


def _dot3_dims(a, b, dims):
    ah, al = _split(a)
    bh, bl = _split(b)
    dg = functools.partial(lax.dot_general, dimension_numbers=dims, preferred_element_type=F32)
    return dg(ah, bh) + (dg(ah, bl) + dg(al, bh))


_NN = (((1,), (0,)), ((), ()))
_NT = (((1,), (1,)), ((), ()))
_TN = (((0,), (0,)), ((), ()))


def _dot3(a, b):
    return _dot3_dims(a, b, _NN)


def _sigmoid(x):
    return 1.0 / (1.0 + jnp.exp(-x))


def _softplus(x):
    return jnp.maximum(x, 0.0) + jnp.log(1.0 + jnp.exp(-jnp.abs(x)))


def _mod_kernel(c_ref, w_ref, b_ref, o_ref):
    c = c_ref[...]
    s = c * _sigmoid(c)
    o_ref[...] = _dot(s, w_ref[...]) + b_ref[...]


def _modulation(cond, w_mod, b_mod):
    L = w_mod.shape[0]
    R = cond.shape[0]
    tn = 1024
    return pl.pallas_call(
        _mod_kernel,
        grid=(L, 6 * D_MODEL // tn),
        in_specs=[pl.BlockSpec((R, D_MODEL), lambda l, j: (0, 0)),
                  pl.BlockSpec((None, D_MODEL, tn), lambda l, j: (l, 0, j)),
                  pl.BlockSpec((None, 1, tn), lambda l, j: (l, 0, j))],
        out_specs=pl.BlockSpec((None, R, tn), lambda l, j: (l, 0, j)),
        out_shape=jax.ShapeDtypeStruct((L, R, 6 * D_MODEL), F32),
        compiler_params=pltpu.CompilerParams(dimension_semantics=("arbitrary", "arbitrary"),
                                             vmem_limit_bytes=VMEM_LIMIT),
        name="modulation",
    )(cond, w_mod, b_mod.reshape(L, 1, 6 * D_MODEL))


def _two_source_specs(tm, width, nct, nlt, lat_off):
    ctx = pl.BlockSpec((tm, width), lambda i: (jnp.minimum(i, nct - 1), 0))
    lat = pl.BlockSpec((tm, width), lambda i: (jnp.clip(i - nct, 0, nlt - 1) + lat_off, 0))
    return ctx, lat


def _mod_spec(l, tm, n_ctx, lat_len, t0=0, rank=1):
    def row(i):
        tok = (i + t0) * tm
        return jnp.where(tok < n_ctx, 0, 1 + (tok - n_ctx) // lat_len)

    if rank == 1:
        return pl.BlockSpec((None, 1, 6, D_MODEL), lambda i: (l, row(i), 0, 0))
    return pl.BlockSpec((None, 1, 6, D_MODEL), lambda i, j: (l, row(i), 0, 0))


def _inproj_kernel(xc_ref, xl_ref, mod_ref, w_ref, o_ref, h_scr, *, tn, n_ctx_tiles):
    shift = mod_ref[0, 0:1, :]
    scale = mod_ref[0, 1:2, :]

    def modulate(x_ref):
        h_scr[...] = (x_ref[...] * (1.0 + scale) + shift).astype(BF16)

    pl.when(pl.program_id(0) < n_ctx_tiles)(lambda: modulate(xc_ref))
    pl.when(pl.program_id(0) >= n_ctx_tiles)(lambda: modulate(xl_ref))
    h = h_scr[...]
    for j in range(IN_W_EXT // tn):
        cols = slice(j * tn, (j + 1) * tn)
        o_ref[:, cols] = jnp.dot(h, w_ref[:, cols], preferred_element_type=F32)


def _in_projection(x_src, mod, w_ext, l, tn=512):
    xc, xl, n_ctx, n_lat, lat_off, lat_len = x_src
    N = n_ctx + n_lat
    tm = MOD_BLK
    nct = n_ctx // tm
    ctx_spec, lat_spec = _two_source_specs(tm, D_MODEL, nct, n_lat // tm, lat_off // tm)
    return pl.pallas_call(
        functools.partial(_inproj_kernel, tn=tn, n_ctx_tiles=nct),
        grid=(N // tm,),
        in_specs=[ctx_spec, lat_spec,
                  _mod_spec(l, tm, n_ctx, lat_len),
                  pl.BlockSpec((None, D_MODEL, IN_W_EXT), lambda i: (l, 0, 0), pipeline_mode=pl.Buffered(1))],
        out_specs=pl.BlockSpec((tm, IN_W_EXT), lambda i: (i, 0)),
        out_shape=jax.ShapeDtypeStruct((N, IN_W_EXT), F32),
        scratch_shapes=[pltpu.VMEM((tm, D_MODEL), BF16)],
        compiler_params=pltpu.CompilerParams(dimension_semantics=("arbitrary",),
                                             vmem_limit_bytes=VMEM_LIMIT),
        name="in_projection",
    )(xc, xl, mod, w_ext)


PREC = dict(seg=1, cum=2, a=1, neu=1, post=1, mn=1, b_o=1, b_s=1, gn=2)


def _dotp(a, b, dims, passes):
    dg = functools.partial(lax.dot_general, dimension_numbers=dims, preferred_element_type=F32)
    if passes == 1:
        return dg(a.astype(BF16), b.astype(BF16))
    ah, al = _split(a)
    if passes == 2:
        bh = b.astype(BF16)
        return dg(ah, bh) + dg(al, bh)
    bh, bl = _split(b)
    return dg(ah, bh) + (dg(ah, bl) + dg(al, bh))


def _dotp_rhs(a, b, passes):
    dg = functools.partial(lax.dot_general, dimension_numbers=_NN, preferred_element_type=F32)
    ab = a.astype(BF16)
    if passes == 1:
        return dg(ab, b.astype(BF16))
    bh, bl = _split(b)
    return dg(ab, bh) + dg(ab, bl)


def _rwkv_kernel(*refs, T, nsq, has_init, cb):
    (r_ref, k_ref, v_ref, g_ref, lo_ref, w0_ref, w2_ref, a0_ref, a2_ref,
     kk_ref, ka_ref, rk_ref, lng_ref, lnb_ref) = refs[:14]
    pos = 14
    init_ref = None
    if has_init:
        init_ref = refs[pos]
        pos += 1
    y_ref, sfin_ref = refs[pos:pos + 2]
    rp_scr, ov_scr, m_scr, n_scr, o_scr, bonus_scr, s_scr = refs[pos + 2:]

    nC = T // CHUNK
    nCs = nC // nsq
    C = CHUNK
    H = HEAD_DIM
    P = PREC
    C2 = 2 * C
    row = lax.broadcasted_iota(jnp.int32, (C, C), 0)
    col = lax.broadcasted_iota(jnp.int32, (C, C), 1)
    tri_f = jnp.where(col <= row, 1.0, 0.0).astype(F32)
    lrow = lax.broadcasted_iota(jnp.int32, (LANES, LANES), 0)
    lcol = lax.broadcasted_iota(jnp.int32, (LANES, LANES), 1)
    same_head = (lrow // H) == (lcol // H)
    head_ones = jnp.where(same_head, 1.0, 0.0).astype(F32)
    eye = lrow == lcol
    rt_, ct_ = lrow % C, lcol % C
    strict_m = [same_head & (ct_ < rt_), same_head & (ct_ > rt_)]
    incl_m = [same_head & (ct_ <= rt_), same_head & (ct_ >= rt_)]
    lane = lax.broadcasted_iota(jnp.int32, (1, LANES), 1)
    hmask = [lane < H, lane >= H]

    def stack_heads(x):
        return jnp.concatenate([jnp.where(hmask[0], x, 0.0), jnp.where(hmask[1], x, 0.0)], axis=0)

    kkw = kk_ref[...]
    kaw = ka_ref[...]
    rkw = rk_ref[...]
    zl = jnp.zeros((LORA, LANES), F32)
    lora_w = jnp.concatenate(
        [jnp.concatenate([w if j == i else zl for j in range(4)], axis=1)
         for i, w in enumerate([w2_ref[0], w2_ref[1], a2_ref[0], a2_ref[1]])], axis=0).astype(BF16)

    def phase_a(ci, carry):
        units = []
        for cc in range(cb):
            c = ci * cb + cc
            rows = pl.ds(pl.multiple_of(c * C, C), C)
            r = r_ref[rows, :]
            k = k_ref[rows, :]
            v = v_ref[rows, :]
            lo = lo_ref[rows, :]
            kkraw = k * kkw
            ss = _dotp(kkraw * kkraw, head_ones, _NN, P['seg'])
            kkn = kkraw / jnp.maximum(jnp.sqrt(ss), 1e-12)
            bonus = jnp.zeros((C, LANES), F32)
            vm = stack_heads(v)
            lo_in = jnp.concatenate([jnp.tanh(lo[:, :2 * LORA]), lo[:, 2 * LORA:]], axis=1)
            lo_out = _dot(lo_in, lora_w)
            lds, alphas = [], []
            for d in range(2):
                wl = w0_ref[d] + lo_out[:, d * LANES:(d + 1) * LANES]
                lds.append(-jnp.exp(-_softplus(-wl) - 0.5))
                alphas.append(_sigmoid(a0_ref[d] + lo_out[:, (2 + d) * LANES:(3 + d) * LANES]))
            pref = _dotp_rhs(tri_f, jnp.concatenate(lds, axis=1), P['cum'])
            for d in range(2):
                ld, alpha = lds[d], alphas[d]
                kd = k * (1.0 + (alpha - 1.0) * kaw)
                bb = kkn * alpha
                bonus = bonus + _dotp(r * kd * rkw, head_ones, _NN, P['seg']) * v
                pf = pref[:, d * LANES:(d + 1) * LANES]
                g_incl = pf if d == 0 else pf[C - 1:C, :] - pf + ld
                g_excl = g_incl - ld
                g_end = g_incl[C - 1:C, :] if d == 0 else g_incl[0:1, :]
                ginv = jnp.exp(-g_incl)
                dec_end = jnp.exp(g_end - g_incl)
                units.append(dict(
                    d=d, c=c, vm=vm,
                    at=stack_heads(-kkn * jnp.exp(g_excl)), rt=stack_heads(r * jnp.exp(g_incl)),
                    bt=stack_heads(bb * ginv), kt=stack_heads(kd * ginv),
                    bend=stack_heads(bb * dec_end), kend=stack_heads(kd * dec_end), gend=jnp.exp(g_end)))
            bonus_scr[rows, :] = bonus
            o_scr[rows, :] = jnp.zeros((C, LANES), F32)

        for u in units:
            ar = jnp.concatenate([u['at'], u['rt']], axis=0)
            bk = jnp.concatenate([u['bt'], u['kt']], axis=0)
            u['a'] = _dotp(ar, bk, _NT, P['a'])
        for u in units:
            a = u['a']
            st, ic = strict_m[u['d']], incl_m[u['d']]
            u['p'] = jnp.where(st, a[:C2, :C2], 0.0)
            u['a_ak'] = jnp.where(st, a[:C2, C2:], 0.0)
            u['a_rb'] = jnp.where(ic, a[C2:, :C2], 0.0)
            u['a_rk'] = jnp.where(ic, a[C2:, C2:], 0.0)
        rvs = [_dotp(u['a_ak'], u['vm'], _NN, P['post']) for u in units]
        sq = [_dotp(u['p'], u['p'], _NN, P['neu']) for u in units]
        for i, u in enumerate(units):
            u['x'] = jnp.where(eye, 1.0, 0.0) + u['p']
            u['p'] = sq[i]
        for it in range(1, 6):
            res = [_dotp(u['p'], jnp.concatenate([u['x'], u['p']], axis=1) if it < 5 else u['x'], _NN, P['neu'])
                   for u in units]
            for i, u in enumerate(units):
                u['x'] = u['x'] + res[i][:, :C2]
                if it < 5:
                    u['p'] = res[i][:, C2:]
        rhs = [_dotp(u['x'], jnp.concatenate([u['at'], rvs[i]], axis=1), _NN, P['post'])
               for i, u in enumerate(units)]
        for i, u in enumerate(units):
            u['rhs'] = rhs[i]
        xs = [_dotp(jnp.concatenate([u['a_rb'], u['a_rk']], axis=1),
                    jnp.concatenate([u['rhs'], jnp.concatenate([jnp.zeros_like(u['vm']), u['vm']], axis=1)], axis=0),
                    _NN, P['post']) for u in units]
        ms = [_dotp(u['rhs'][:, :LANES], u['bend'], _TN, P['mn']) for u in units]
        ns = [_dotp(jnp.concatenate([u['rhs'][:, LANES:], u['vm']], axis=0),
                    jnp.concatenate([u['bend'], u['kend']], axis=0), _TN, P['mn']) for u in units]
        for i, u in enumerate(units):
            d, c = u['d'], u['c']
            rp = u['rt'] + xs[i][:, :LANES]
            ov = xs[i][:, LANES:]
            rp_scr[d, c] = rp[:C] + rp[C:]
            ov_scr[d, c] = ov[:C] + ov[C:]
            m_scr[d, c] = jnp.where(eye, jnp.broadcast_to(u['gend'], (LANES, LANES)), 0.0) + ms[i]
            n_scr[d, c] = ns[i]
        return carry

    lax.fori_loop(0, nC // cb, phase_a, 0)

    zero_blk = jnp.zeros((H, H), F32)
    for q in range(nsq):
        for d in range(2):
            if has_init:
                s_scr[q, d] = jnp.concatenate([jnp.concatenate([init_ref[q, d, 0], zero_blk], axis=1),
                                               jnp.concatenate([zero_blk, init_ref[q, d, 1]], axis=1)], axis=0)
            else:
                s_scr[q, d] = jnp.zeros((LANES, LANES), F32)

    def phase_b(i, carry):
        idx = [(q, d, q * nCs + (i if d == 0 else nCs - 1 - i)) for q in range(nsq) for d in range(2)]
        ss = [s_scr[q, d] for q, d, c in idx]
        os_ = [_dotp(rp_scr[d, c], ss[j], _NT, P['b_o']) for j, (q, d, c) in enumerate(idx)]
        sn = [_dotp(ss[j], m_scr[d, c], _NN, P['b_s']) for j, (q, d, c) in enumerate(idx)]
        for j, (q, d, c) in enumerate(idx):
            s_scr[q, d] = sn[j] + n_scr[d, c]
            rows = pl.ds(c * C if isinstance(c, int) else pl.multiple_of(c * C, C), C)
            o_scr[rows, :] = o_scr[rows, :] + (os_[j] + ov_scr[d, c])
        return carry

    if nCs <= 4:
        for i in range(nCs):
            phase_b(i, 0)
    else:
        lax.fori_loop(0, nCs, phase_b, 0, unroll=4)

    for q in range(nsq):
        for d in range(2):
            s = s_scr[q, d]
            sfin_ref[q, d, 0] = s[:H, :H]
            sfin_ref[q, d, 1] = s[H:, H:]

    lng = lng_ref[...]
    lnb = lnb_ref[...]

    rb = min(T, 512)

    def phase_c(c, carry):
        rows = pl.ds(pl.multiple_of(c * rb, rb), rb)
        o = o_scr[rows, :]
        mu = _dotp(o, head_ones, _NN, P['gn']) * (1.0 / H)
        dlt = o - mu
        var = _dotp(dlt * dlt, head_ones, _NN, P['gn']) * (1.0 / H)
        on = dlt * lax.rsqrt(var + GN_EPS) * lng + lnb
        y_ref[rows, :] = (on + bonus_scr[rows, :]) * _sigmoid(g_ref[rows, :])
        return carry

    lax.fori_loop(0, T // rb, phase_c, 0)


def _rwkv(proj, p, init, *, T, n_seq, row_blk0, nsq=1):
    seq_len, T = T, nsq * T
    nC = T // CHUNK
    npair = RWKV_HEADS // 2
    has_init = init is not None
    cb = lambda off: (lambda b, j: (b + row_blk0, off // LANES + j))
    vec = lambda b, j: (0, j)
    in_specs = [pl.BlockSpec((T, LANES), cb(C_R)),
                pl.BlockSpec((T, LANES), cb(C_K)),
                pl.BlockSpec((T, LANES), cb(C_V)),
                pl.BlockSpec((T, LANES), cb(C_G)),
                pl.BlockSpec((T, 4 * LORA), lambda b, j: (b + row_blk0, C_LORA // (4 * LORA))),
                pl.BlockSpec((2, 1, LANES), lambda b, j: (0, 0, j)),
                pl.BlockSpec((2, LORA, LANES), lambda b, j: (0, 0, j)),
                pl.BlockSpec((2, 1, LANES), lambda b, j: (0, 0, j)),
                pl.BlockSpec((2, LORA, LANES), lambda b, j: (0, 0, j)),
                pl.BlockSpec((1, LANES), vec), pl.BlockSpec((1, LANES), vec),
                pl.BlockSpec((1, LANES), vec), pl.BlockSpec((1, LANES), vec),
                pl.BlockSpec((1, LANES), vec)]
    args = [proj, proj, proj, proj, proj,
            p['rwkv_w0'].reshape(2, 1, RWKV_W), p['rwkv_w2'], p['rwkv_a0'].reshape(2, 1, RWKV_W), p['rwkv_a2'],
            p['rwkv_k_k'].reshape(1, RWKV_W), p['rwkv_k_a'].reshape(1, RWKV_W),
            p['rwkv_r_k'].reshape(1, RWKV_W), p['rwkv_lnx_g'].reshape(1, RWKV_W),
            p['rwkv_lnx_b'].reshape(1, RWKV_W)]
    st_spec = pl.BlockSpec((nsq, 2, 2, HEAD_DIM, HEAD_DIM), lambda b, j: (b, 0, j, 0, 0))
    if has_init:
        in_specs.append(st_spec)
        args.append(init)
    sc_o = lambda: pltpu.VMEM((2, nC, CHUNK, LANES), F32)
    sc_s = lambda: pltpu.VMEM((2, nC, LANES, LANES), F32)
    return pl.pallas_call(
        functools.partial(_rwkv_kernel, T=T, nsq=nsq, has_init=has_init, cb=min(RWKV_CB, nC)),
        grid=(n_seq // nsq, npair),
        in_specs=in_specs,
        out_specs=[pl.BlockSpec((T, LANES), lambda b, j: (b, j)), st_spec],
        out_shape=[jax.ShapeDtypeStruct((n_seq * seq_len, RWKV_W), F32),
                   jax.ShapeDtypeStruct((n_seq, 2, RWKV_HEADS, HEAD_DIM, HEAD_DIM), F32)],
        scratch_shapes=[sc_o(), sc_o(), sc_s(), sc_s(),
                        pltpu.VMEM((T, LANES), F32), pltpu.VMEM((T, LANES), F32),
                        pltpu.VMEM((nsq, 2, LANES, LANES), F32)],
        compiler_params=pltpu.CompilerParams(dimension_semantics=("arbitrary", "arbitrary"),
                                             vmem_limit_bytes=VMEM_LIMIT),
        name="rwkv_T%d" % T,
    )(*args)


def _dft_mats(L):
    f = np.arange(L, dtype=np.int64)
    ph = (np.outer(f, f) % (2 * L)).astype(np.float64) * (math.pi / L)
    return np.cos(ph), -np.sin(ph)


def _filter_features(L):
    t = np.linspace(0.0, 1.0, L, dtype=np.float32)[:, None]
    bands = (FILTER_EMB - 1) // 2
    t_res = np.arange(L, dtype=np.float32)[:, None]
    f = np.linspace(1e-4, bands - 1, bands, dtype=np.float32)[None, :]
    ang = (2.0 * math.pi * t_res * f / L).astype(np.float32)
    z = np.concatenate([t, np.cos(ang), np.sin(ang)], -1).astype(np.float32)
    zp = np.zeros((L, FILTER_HID), np.float32)
    zp[:, :FILTER_EMB] = z
    deltas = np.abs(np.linspace(HY_MIN_DECAY, HY_MAX_DECAY, HYENA_W, dtype=np.float32))
    return zp, t.astype(np.float32), deltas[None, :].astype(np.float32)


def _hyfilter_kernel(z_ref, t_ref, dl_ref, w1_ref, b1_ref, q1_ref, w2_ref, b2_ref, q2_ref, wc_ref, wa_ref,
                     hs_ref, hd_ref, knyq_ref, hm_scr, *, L):
    @pl.when((pl.program_id(0) == 0) & (pl.program_id(1) == 0))
    def _():
        h1 = jnp.sin(q1_ref[...] * (_dot3(z_ref[...], w1_ref[...]) + b1_ref[...]))
        hm_scr[...] = jnp.sin(q2_ref[...] * (_dot3(h1, w2_ref[...]) + b2_ref[...]))

    hm = hm_scr[...]
    win = jnp.exp(-t_ref[...] * dl_ref[...])
    lag = lax.broadcasted_iota(jnp.int32, (L, 1), 0)
    alt = jnp.where(lag % 2 == 0, 1.0, -1.0).astype(F32)
    hc = _dot3(hm, wc_ref[...]) * win
    ha = jnp.where(lag == 0, 0.0, _dot3(hm, wa_ref[...]) * win)
    hs_ref[...] = hc + ha
    hd_ref[...] = hc - ha
    knyq_ref[...] = jnp.sum((hc + ha) * alt, axis=0, keepdims=True) * (1.0 / (2 * L))


def _hyspec_kernel(cm_ref, sm_ref, hs_ref, hd_ref, kre_ref, kim_ref, *, L, tf):
    f = pl.program_id(2) * tf + lax.broadcasted_iota(jnp.int32, (tf, 1), 0)
    wf = jnp.where(f == 0, 1.0, 2.0).astype(F32) * (1.0 / (2 * L))
    kre_ref[...] = _dot(cm_ref[...], hs_ref[...]) * wf
    kim_ref[...] = _dot(sm_ref[...], hd_ref[...]) * wf


def _hyena_filter(p, L, cm, sm, cw=128):
    zp, t, dl = _filter_features(L)
    w1 = jnp.zeros((FILTER_HID, FILTER_HID), F32).at[:FILTER_EMB].set(p['hy_f_w1'])
    r2 = lambda a: a.reshape(1, -1)
    ncb = HYENA_W // cw
    full = lambda shape: pl.BlockSpec(shape, lambda n, j: (0,) * len(shape))
    shp = jax.ShapeDtypeStruct((2, L, HYENA_W), F32)
    hs, hd, knyq = pl.pallas_call(
        functools.partial(_hyfilter_kernel, L=L),
        grid=(2, ncb),
        in_specs=[full((L, FILTER_HID)), full((L, 1)),
                  pl.BlockSpec((1, cw), lambda n, j: (0, j)),
                  full((FILTER_HID, FILTER_HID)), full((1, FILTER_HID)), full((1, FILTER_HID)),
                  full((FILTER_HID, FILTER_HID)), full((1, FILTER_HID)), full((1, FILTER_HID)),
                  pl.BlockSpec((FILTER_HID, cw), lambda n, j: (0, n * 2 * ncb + j)),
                  pl.BlockSpec((FILTER_HID, cw), lambda n, j: (0, n * 2 * ncb + ncb + j))],
        out_specs=[pl.BlockSpec((None, L, cw), lambda n, j: (n, 0, j)),
                   pl.BlockSpec((None, L, cw), lambda n, j: (n, 0, j)),
                   pl.BlockSpec((None, 1, cw), lambda n, j: (n, 0, j))],
        out_shape=[shp, shp, jax.ShapeDtypeStruct((2, 1, HYENA_W), F32)],
        scratch_shapes=[pltpu.VMEM((L, FILTER_HID), F32)],
        compiler_params=pltpu.CompilerParams(dimension_semantics=("arbitrary", "arbitrary"),
                                             vmem_limit_bytes=VMEM_LIMIT),
        name="hyena_filter_L%d" % L,
    )(jnp.asarray(zp), jnp.asarray(t), jnp.asarray(dl), w1, r2(p['hy_f_b1']), r2(p['hy_f_freq1']),
      p['hy_f_w2'], r2(p['hy_f_b2']), r2(p['hy_f_freq2']), p['hy_f_w3'], p['hy_f_w3'])
    tf = min(512, L)
    sw = 256
    kre, kim = pl.pallas_call(
        functools.partial(_hyspec_kernel, L=L, tf=tf),
        grid=(2, HYENA_W // sw, L // tf),
        in_specs=[pl.BlockSpec((tf, L), lambda n, j, i: (i, 0)),
                  pl.BlockSpec((tf, L), lambda n, j, i: (i, 0)),
                  pl.BlockSpec((None, L, sw), lambda n, j, i: (n, 0, j)),
                  pl.BlockSpec((None, L, sw), lambda n, j, i: (n, 0, j))],
        out_specs=[pl.BlockSpec((None, tf, sw), lambda n, j, i: (n, i, j)),
                   pl.BlockSpec((None, tf, sw), lambda n, j, i: (n, i, j))],
        out_shape=[shp, shp],
        compiler_params=pltpu.CompilerParams(dimension_semantics=("arbitrary",) * 3,
                                             vmem_limit_bytes=VMEM_LIMIT),
        name="hyena_spectrum_L%d" % L,
    )(cm, sm, hs, hd)
    return kre, kim, knyq


def _hyena_kernel(v_ref, x1_ref, x2_ref, sw_ref, bias_ref, cm_ref, sm_ref, kre_ref, kim_ref, knyq_ref,
                  o_ref, zb_scr, zf_scr, xs_scr, yre_scr, yim_scr, *, L, rb, nsq):
    nrb = L // rb
    cw = v_ref.shape[-1]
    t = lax.broadcasted_iota(jnp.int32, (nsq * L, 1), 0) % L
    first = t == 0
    last = t == L - 1
    tl = lax.broadcasted_iota(jnp.int32, (L, 1), 0)
    alt_all = jnp.where(tl % 2 == 0, 1.0, -1.0).astype(F32)
    tb = lax.broadcasted_iota(jnp.int32, (rb, 1), 0)
    alt_blk = jnp.where(tb % 2 == 0, 1.0, -1.0).astype(F32)
    sw = sw_ref[...]
    tile = lambda a: jnp.concatenate([a] * nsq, axis=1) if nsq > 1 else a

    def short(ref, s):
        u = ref[...]
        prev = jnp.where(first, 0.0, pltpu.roll(u, 1, 0))
        nxt = jnp.where(last, 0.0, pltpu.roll(u, nsq * L - 1, 0))
        c = prev * sw[0, s][None, :] + u * sw[1, s][None, :] + nxt * sw[2, s][None, :]
        if nsq == 1:
            return c
        return jnp.concatenate([c[q * L:(q + 1) * L] for q in range(nsq)], axis=1)

    z0 = short(v_ref, 0)
    zf_scr[...] = z0
    zb_scr[...] = z0.astype(BF16)
    unyq = jnp.sum(z0 * alt_all, axis=0, keepdims=True)

    for n in range(2):
        xs_scr[...] = short(x1_ref if n == 0 else x2_ref, n + 1)

        def fwd(fb, carry):
            rows = pl.ds(pl.multiple_of(fb * rb, rb), rb)
            zb = zb_scr[...]
            ure = jnp.dot(cm_ref[rows, :], zb, preferred_element_type=F32)
            uim = jnp.dot(sm_ref[rows, :], zb, preferred_element_type=F32)
            kre = tile(kre_ref[n, rows, :])
            kim = tile(kim_ref[n, rows, :])
            yre_scr[rows, :] = (ure * kre - uim * kim).astype(BF16)
            yim_scr[rows, :] = (ure * kim + uim * kre).astype(BF16)
            return carry

        lax.fori_loop(0, nrb, fwd, 0)
        ynyq = unyq * tile(knyq_ref[n])
        bias = tile(bias_ref[n])

        def inv(ib, acc):
            r0 = pl.multiple_of(ib * rb, rb)
            rows = pl.ds(r0, rb)
            y = (jnp.dot(cm_ref[rows, :], yre_scr[...], preferred_element_type=F32)
                 + jnp.dot(sm_ref[rows, :], yim_scr[...], preferred_element_type=F32))
            y = y + alt_blk * ynyq + zf_scr[rows, :] * bias
            znew = xs_scr[rows, :] * y
            if n == 0:
                zf_scr[rows, :] = znew
                zb_scr[rows, :] = znew.astype(BF16)
                acc = acc + jnp.sum(znew * alt_blk, axis=0, keepdims=True)
            else:
                for q in range(nsq):
                    o_ref[pl.ds(q * L + r0, rb), :] = znew[:, q * cw:(q + 1) * cw]
            return acc

        unyq = lax.fori_loop(0, nrb, inv, jnp.zeros_like(unyq))


def _hyena(proj, p, filt, cm, sm, *, L, n_seq, row_blk0, nsq=1, cw=256):
    kre, kim, knyq = filt
    ncb = HYENA_W // cw
    cb = lambda s: (lambda j, b: (b + row_blk0, (C_HY + s * HYENA_W) // cw + j))
    once = pl.Buffered(1)
    sw = p['hy_short_w'].reshape(3, 3, HYENA_W)
    rb = min(HYENA_ROW_BLK, L)
    Lb, wide = nsq * L, nsq * cw
    return pl.pallas_call(
        functools.partial(_hyena_kernel, L=L, rb=rb, nsq=nsq),
        grid=(ncb, n_seq // nsq),
        in_specs=[pl.BlockSpec((Lb, cw), cb(0)), pl.BlockSpec((Lb, cw), cb(1)), pl.BlockSpec((Lb, cw), cb(2)),
                  pl.BlockSpec((3, 3, cw), lambda j, b: (0, 0, j)),
                  pl.BlockSpec((2, 1, cw), lambda j, b: (0, 0, j)),
                  pl.BlockSpec((L, L), lambda j, b: (0, 0), pipeline_mode=once),
                  pl.BlockSpec((L, L), lambda j, b: (0, 0), pipeline_mode=once),
                  pl.BlockSpec((2, L, cw), lambda j, b: (0, 0, j), pipeline_mode=once),
                  pl.BlockSpec((2, L, cw), lambda j, b: (0, 0, j), pipeline_mode=once),
                  pl.BlockSpec((2, 1, cw), lambda j, b: (0, 0, j))],
        out_specs=pl.BlockSpec((Lb, cw), lambda j, b: (b, j)),
        out_shape=jax.ShapeDtypeStruct((n_seq * L, HYENA_W), F32),
        scratch_shapes=[pltpu.VMEM((L, wide), BF16), pltpu.VMEM((L, wide), F32), pltpu.VMEM((L, wide), F32),
                        pltpu.VMEM((L, wide), BF16), pltpu.VMEM((L, wide), BF16)],
        compiler_params=pltpu.CompilerParams(dimension_semantics=("arbitrary", "arbitrary"),
                                             vmem_limit_bytes=VMEM_LIMIT),
        name="hyena_L%d" % L,
    )(proj, proj, proj, sw, p['hy_bias'].reshape(2, 1, HYENA_W), cm.astype(BF16), sm.astype(BF16),
      kre, kim, knyq)


def _row_max(x):
    t = x[:, :LANES]
    for c in range(1, x.shape[1] // LANES):
        t = jnp.maximum(t, x[:, c * LANES:(c + 1) * LANES])
    return jnp.max(t, axis=-1, keepdims=True)


def _row_sum(x):
    t = x[:, :LANES]
    for c in range(1, x.shape[1] // LANES):
        t = t + x[:, c * LANES:(c + 1) * LANES]
    return jnp.sum(t, axis=-1, keepdims=True)


def _head_masks():
    lane = lax.broadcasted_iota(jnp.int32, (1, KV_W), 1) // HEAD_DIM
    return [lane == h for h in range(ATTN_KV_HEADS)]


def _attend_groups(qgs, keys, vals, masks, sinks, hm):
    nh = len(hm)
    un = [(g, h) for g in range(len(qgs)) for h in range(nh)]
    qm = [jnp.where(hm[h], qgs[g], 0.0).astype(BF16) for g, h in un]
    s = [[_dot_nt(qm[u], k) for k in keys] for u in range(len(un))]
    s = [[sc if mk is None else jnp.where(mk, sc, -jnp.inf) for sc, mk in zip(s[u], masks)] for u in range(len(un))]
    m = [jnp.maximum(_row_max(jnp.concatenate(s[u], axis=1)), sinks[g][h]) for u, (g, h) in enumerate(un)]
    p = [[jnp.exp(sc - m[u]) for sc in s[u]] for u in range(len(un))]
    den = [_row_sum(jnp.concatenate(p[u], axis=1)) + jnp.exp(sinks[g][h] - m[u]) for u, (g, h) in enumerate(un)]
    pv = [[_dot(pc, v) for pc, v in zip(p[u], vals)] for u in range(len(un))]
    accs = [jnp.zeros(q.shape, F32) for q in qgs]
    for u, (g, h) in enumerate(un):
        tot = pv[u][0]
        for t in pv[u][1:]:
            tot = tot + t
        accs[g] = jnp.where(hm[h], tot / den[u], accs[g])
    return accs


def _ctx_attn_kernel(q_ref, k_ref, v_ref, sink_ref, o_ref, *, S):
    scale = HEAD_DIM ** -0.5
    hm = _head_masks()
    k = k_ref[...].astype(BF16)
    v = v_ref[...].astype(BF16)
    for g in range(GQA_GROUP):
        qg = q_ref[:, g * KV_W:(g + 1) * KV_W] * scale
        sinks = [jnp.broadcast_to(sink_ref[0:1, h * GQA_GROUP + g:h * GQA_GROUP + g + 1], (S, 1))
                 for h in range(ATTN_KV_HEADS)]
        o_ref[:, g * KV_W:(g + 1) * KV_W] = _attend_groups([qg], [k], [v], [None], [sinks], hm)[0]


def _ctx_attention(proj, sink, *, S, n_seq):
    return pl.pallas_call(
        functools.partial(_ctx_attn_kernel, S=S),
        grid=(n_seq,),
        in_specs=[pl.BlockSpec((S, ATTN_W), lambda b: (b, C_Q // ATTN_W)),
                  pl.BlockSpec((S, KV_W), lambda b: (b, C_AK // KV_W)),
                  pl.BlockSpec((S, KV_W), lambda b: (b, C_AV // KV_W)),
                  pl.BlockSpec((1, LANES), lambda b: (0, 0))],
        out_specs=pl.BlockSpec((S, ATTN_W), lambda b: (b, 0)),
        out_shape=jax.ShapeDtypeStruct((n_seq * S, ATTN_W), F32),
        compiler_params=pltpu.CompilerParams(dimension_semantics=("arbitrary",),
                                             vmem_limit_bytes=VMEM_LIMIT),
        name="ctx_attention",
    )(proj, proj, proj, sink)


def _rope(x, cos, sin_signed, first_half):
    n = x.shape[-1]
    partner = jnp.where(first_half, pltpu.roll(x, n - HEAD_DIM // 2, 1), pltpu.roll(x, HEAD_DIM // 2, 1))
    return x * cos + partner * sin_signed


def _lat_attn_kernel(q_ref, kp_ref, kc_ref, kn_ref, vp_ref, vc_ref, vn_ref, ck_ref, cv_ref,
                     cq_ref, sq_ref, cp_ref, sp_ref, cc_ref, sc_ref, cn_ref, sn_ref, sink_ref, o_ref, *, T):
    i = pl.program_id(1)
    scale = HEAD_DIM ** -0.5
    hm = _head_masks()
    fh_q = (lax.broadcasted_iota(jnp.int32, (1, ATTN_W), 1) % HEAD_DIM) < HEAD_DIM // 2
    fh_k = (lax.broadcasted_iota(jnp.int32, (1, KV_W), 1) % HEAD_DIM) < HEAD_DIM // 2
    q = _rope(q_ref[...], cq_ref[...], sq_ref[...], fh_q) * scale
    kl = jnp.concatenate([_rope(kp_ref[...], cp_ref[...], sp_ref[...], fh_k),
                          _rope(kc_ref[...], cc_ref[...], sc_ref[...], fh_k),
                          _rope(kn_ref[...], cn_ref[...], sn_ref[...], fh_k)], axis=0).astype(BF16)
    vl = jnp.concatenate([vp_ref[...], vc_ref[...], vn_ref[...]], axis=0).astype(BF16)
    kc = ck_ref[...].astype(BF16)
    vc = cv_ref[...].astype(BF16)
    qpos = i * BLOCK + lax.broadcasted_iota(jnp.int32, (BLOCK, 3 * BLOCK), 0)
    kpos = (i - 1) * BLOCK + lax.broadcasted_iota(jnp.int32, (BLOCK, 3 * BLOCK), 1)
    ok = (jnp.abs(qpos - kpos) <= WINDOW) & (kpos >= 0) & (kpos < T)
    qgs = [q[:, g * KV_W:(g + 1) * KV_W] for g in range(GQA_GROUP)]
    sinks = [[jnp.broadcast_to(sink_ref[0:1, h * GQA_GROUP + g:h * GQA_GROUP + g + 1], (BLOCK, 1))
              for h in range(ATTN_KV_HEADS)] for g in range(GQA_GROUP)]
    accs = _attend_groups(qgs, [kl, kc], [vl, vc], [ok, None], sinks, hm)
    for g in range(GQA_GROUP):
        o_ref[:, g * KV_W:(g + 1) * KV_W] = accs[g]


def _rope_tables(T):
    rows = T // GRID_W
    row = np.repeat(np.arange(rows), GRID_W).astype(np.float32)
    col = np.tile(np.arange(GRID_W), rows).astype(np.float32)
    nf = HEAD_DIM // 4
    inv = (ROPE_BASE ** (-np.arange(nf, dtype=np.float32) / nf)).astype(np.float32)
    ang = np.concatenate([row[:, None] * inv, col[:, None] * inv], -1).astype(np.float32)
    cos = np.cos(ang).astype(np.float32)
    sin = np.sin(ang).astype(np.float32)
    cos_h = np.concatenate([cos, cos], -1)
    sin_h = np.concatenate([-sin, sin], -1)
    return cos_h, sin_h


def _lat_attention(proj, cache_k, cache_v, sink, *, T, n_seq, row_blk0):
    nb = T // BLOCK
    P = cache_k.shape[1]
    cos_h, sin_h = _rope_tables(T)
    cq = jnp.asarray(np.tile(cos_h, (1, ATTN_HEADS)))
    sq = jnp.asarray(np.tile(sin_h, (1, ATTN_HEADS)))
    ck = jnp.asarray(np.tile(cos_h, (1, ATTN_KV_HEADS)))
    sk = jnp.asarray(np.tile(sin_h, (1, ATTN_KV_HEADS)))
    r0 = row_blk0 * nb
    prv = lambda i: jnp.maximum(i - 1, 0)
    nxt = lambda i: jnp.minimum(i + 1, nb - 1)
    kcol, vcol = C_AK // KV_W, C_AV // KV_W
    kv = lambda col, f: pl.BlockSpec((BLOCK, KV_W), lambda b, i: (r0 + b * nb + f(i), col))
    tab = lambda w, f: pl.BlockSpec((BLOCK, w), lambda b, i: (f(i), 0))
    cur = lambda i: i
    return pl.pallas_call(
        functools.partial(_lat_attn_kernel, T=T),
        grid=(n_seq, nb),
        in_specs=[pl.BlockSpec((BLOCK, ATTN_W), lambda b, i: (r0 + b * nb + i, C_Q // ATTN_W)),
                  kv(kcol, prv), kv(kcol, cur), kv(kcol, nxt),
                  kv(vcol, prv), kv(vcol, cur), kv(vcol, nxt),
                  pl.BlockSpec((None, P, KV_W), lambda b, i: (b, 0, 0)),
                  pl.BlockSpec((None, P, KV_W), lambda b, i: (b, 0, 0)),
                  tab(ATTN_W, cur), tab(ATTN_W, cur),
                  tab(KV_W, prv), tab(KV_W, prv), tab(KV_W, cur), tab(KV_W, cur), tab(KV_W, nxt), tab(KV_W, nxt),
                  pl.BlockSpec((1, LANES), lambda b, i: (0, 0))],
        out_specs=pl.BlockSpec((BLOCK, ATTN_W), lambda b, i: (b * nb + i, 0)),
        out_shape=jax.ShapeDtypeStruct((n_seq * T, ATTN_W), F32),
        compiler_params=pltpu.CompilerParams(dimension_semantics=("arbitrary", "arbitrary"),
                                             vmem_limit_bytes=VMEM_LIMIT),
        name="latent_attention",
    )(proj, proj, proj, proj, proj, proj, proj, cache_k, cache_v, cq, sq, ck, sk, ck, sk, ck, sk, sink)


def _layer_norm(y, g, b):
    mu = jnp.mean(y, axis=-1, keepdims=True)
    d = y - mu
    var = jnp.mean(d * d, axis=-1, keepdims=True)
    return d * lax.rsqrt(var + LN_EPS) * g + b


def _outproj_kernel(ac_ref, bc_ref, cc_ref, al_ref, bl_ref, cl_ref, xc_ref, xl_ref, mod_ref, w_ref,
                    g_ref, be_ref, x1_ref, h2_ref, *, tm, n_ctx_tiles):
    r1, r2 = RWKV_W, RWKV_W + HYENA_W
    sub = 128

    def tile(a_ref, b_ref, c_ref, x_ref):
        ms = []
        for s in range(tm // sub):
            rows = slice(s * sub, (s + 1) * sub)
            ms.append(jnp.dot(a_ref[rows, :].astype(BF16), w_ref[0:r1, :], preferred_element_type=F32)
                      + jnp.dot(b_ref[rows, :].astype(BF16), w_ref[r1:r2, :], preferred_element_type=F32)
                      + jnp.dot(c_ref[rows, :].astype(BF16), w_ref[r2:, :], preferred_element_type=F32))
        for s in range(tm // sub):
            rows = slice(s * sub, (s + 1) * sub)
            gate1 = mod_ref[0, 2:3, :]
            shift2 = mod_ref[0, 3:4, :]
            scale2 = mod_ref[0, 4:5, :]
            x1 = _layer_norm(DN_ALPHA * x_ref[rows, :] + gate1 * ms[s], g_ref[...], be_ref[...])
            x1_ref[rows, :] = x1
            h2_ref[rows, :] = (x1 * (1.0 + scale2) + shift2).astype(BF16)

    pl.when(pl.program_id(0) < n_ctx_tiles)(lambda: tile(ac_ref, bc_ref, cc_ref, xc_ref))
    pl.when(pl.program_id(0) >= n_ctx_tiles)(lambda: tile(al_ref, bl_ref, cl_ref, xl_ref))


def _out_projection(mix_ctx, mix_lat, x_src, mod, w_out, l, g, b, tm=256):
    xc, xl, n_ctx, n_lat, lat_off, lat_len = x_src
    N = n_ctx + n_lat
    nct = n_ctx // tm
    nlt = n_lat // tm
    row = lambda w: pl.BlockSpec((tm, w), lambda i: (i, 0))
    ctx = lambda w: _two_source_specs(tm, w, nct, nlt, 0)[0]
    lat = lambda w: _two_source_specs(tm, w, nct, nlt, 0)[1]
    xc_spec, xl_spec = _two_source_specs(tm, D_MODEL, nct, nlt, lat_off // tm)
    full = lambda r: pl.BlockSpec((r, D_MODEL), lambda i: (0, 0))
    return pl.pallas_call(
        functools.partial(_outproj_kernel, tm=tm, n_ctx_tiles=nct),
        grid=(N // tm,),
        in_specs=[ctx(RWKV_W), ctx(HYENA_W), ctx(ATTN_W), lat(RWKV_W), lat(HYENA_W), lat(ATTN_W),
                  xc_spec, xl_spec,
                  _mod_spec(l, tm, n_ctx, lat_len),
                  pl.BlockSpec((None, D_MODEL, D_MODEL), lambda i: (l, 0, 0)), full(1), full(1)],
        out_specs=[row(D_MODEL), row(D_MODEL)],
        out_shape=[jax.ShapeDtypeStruct((N, D_MODEL), F32), jax.ShapeDtypeStruct((N, D_MODEL), BF16)],
        compiler_params=pltpu.CompilerParams(dimension_semantics=("arbitrary",),
                                             vmem_limit_bytes=VMEM_LIMIT),
        name="out_projection",
    )(*mix_ctx, *mix_lat, xc, xl, mod, w_out, g.reshape(1, D_MODEL), b.reshape(1, D_MODEL))


HALO = 16


def _ffn_kernel(hp_ref, h_ref, hn_ref, x1_ref, mod_ref, wa_ref, wb_ref, cwa_ref, cwb_ref, wd_ref, g_ref, be_ref,
                o_ref, hx_scr, ua_scr, ub_scr, acc_scr, *, tm, tf, nsub, t0, n_ctx, s_ctx, s_lat):
    i = pl.program_id(0) + t0
    j = pl.program_id(1)

    @pl.when(j == 0)
    def _():
        hx_scr[0:HALO, :] = hp_ref[...]
        hx_scr[HALO:HALO + tm, :] = h_ref[...]
        hx_scr[HALO + tm:, :] = hn_ref[...]
        acc_scr[...] = jnp.zeros_like(acc_scr)

    hx = hx_scr[...]
    ua_scr[...] = jnp.dot(hx, wa_ref[...], preferred_element_type=F32)
    ub_scr[...] = jnp.dot(hx, wb_ref[...], preferred_element_type=F32)
    grow = i * tm + lax.broadcasted_iota(jnp.int32, (tm, 1), 0)
    pos = jnp.where(grow < n_ctx, grow % s_ctx, (grow - n_ctx) % s_lat)
    slen = jnp.where(grow < n_ctx, s_ctx, s_lat)
    mprev = jnp.where(pos == 0, 0.0, 1.0).astype(F32)
    mnext = jnp.where(pos == slen - 1, 0.0, 1.0).astype(F32)

    def conv(u_scr, cw_ref):
        up = u_scr[HALO - 1:HALO - 1 + tm, :] * mprev
        uc = u_scr[HALO:HALO + tm, :]
        un = u_scr[HALO + 1:HALO + 1 + tm, :] * mnext
        return up * cw_ref[0:1, :] + uc * cw_ref[1:2, :] + un * cw_ref[2:3, :]

    a = conv(ua_scr, cwa_ref)
    b = conv(ub_scr, cwb_ref)
    act = (a * _sigmoid(a) * b).astype(BF16)
    acc_scr[...] += jnp.dot(act, wd_ref[...], preferred_element_type=F32)

    @pl.when(j == pl.num_programs(1) - 1)
    def _():
        for s in range(nsub):
            rows = slice(s * MOD_BLK, (s + 1) * MOD_BLK)
            gate2 = mod_ref[0, 5:6, :]
            o_ref[rows, :] = _layer_norm(DN_ALPHA * x1_ref[rows, :] + gate2 * acc_scr[rows, :],
                                         g_ref[...], be_ref[...])


def _ffn(h2, x1, mod, wa, wb, cw, wd, l, g, b, *, n_ctx, s_ctx, s_lat, row0=0, n_rows=None, tm=512, tf=512):
    N = x1.shape[0]
    n_rows = N if n_rows is None else n_rows
    t0 = row0 // tm
    nsub = tm // MOD_BLK
    nh = tm // HALO
    last_h = N // HALO - 1
    pad = ((0, 0), (0, D_FF_PAD - D_FF))
    cwa, cwb = jnp.pad(cw[:, :D_FF], pad), jnp.pad(cw[:, D_FF:], pad)
    return pl.pallas_call(
        functools.partial(_ffn_kernel, tm=tm, tf=tf, nsub=nsub, t0=t0, n_ctx=n_ctx, s_ctx=s_ctx, s_lat=s_lat),
        grid=(n_rows // tm, D_FF_PAD // tf),
        in_specs=[pl.BlockSpec((HALO, D_MODEL), lambda i, j: (jnp.maximum((i + t0) * nh - 1, 0), 0)),
                  pl.BlockSpec((tm, D_MODEL), lambda i, j: (i + t0, 0)),
                  pl.BlockSpec((HALO, D_MODEL), lambda i, j: (jnp.minimum((i + t0 + 1) * nh, last_h), 0)),
                  pl.BlockSpec((tm, D_MODEL), lambda i, j: (i + t0, 0)),
                  _mod_spec(l, tm, n_ctx, s_lat, t0=t0, rank=2),
                  pl.BlockSpec((None, D_MODEL, tf), lambda i, j: (l, 0, j)),
                  pl.BlockSpec((None, D_MODEL, tf), lambda i, j: (l, 0, j)),
                  pl.BlockSpec((3, tf), lambda i, j: (0, j)),
                  pl.BlockSpec((3, tf), lambda i, j: (0, j)),
                  pl.BlockSpec((None, tf, D_MODEL), lambda i, j: (l, j, 0)),
                  pl.BlockSpec((1, D_MODEL), lambda i, j: (0, 0)),
                  pl.BlockSpec((1, D_MODEL), lambda i, j: (0, 0))],
        out_specs=pl.BlockSpec((tm, D_MODEL), lambda i, j: (i, 0)),
        out_shape=jax.ShapeDtypeStruct((n_rows, D_MODEL), F32),
        scratch_shapes=[pltpu.VMEM((tm + 2 * HALO, D_MODEL), BF16),
                        pltpu.VMEM((tm + 2 * HALO, tf), F32),
                        pltpu.VMEM((tm + 2 * HALO, tf), F32),
                        pltpu.VMEM((tm, D_MODEL), F32)],
        compiler_params=pltpu.CompilerParams(dimension_semantics=("arbitrary", "arbitrary"),
                                             vmem_limit_bytes=VMEM_LIMIT),
        name="conv_ffn",
    )(h2, h2, h2, x1, mod, wa, wb, cwa, cwb, wd, g.reshape(1, D_MODEL), b.reshape(1, D_MODEL))


def _prep_up_kernel(a_ref, b_ref, wa_ref, wb_ref):
    zeros = jnp.zeros((a_ref.shape[0], D_FF_PAD - D_FF), BF16)
    wa_ref[:, :D_FF] = a_ref[...].astype(BF16)
    wa_ref[:, D_FF:] = zeros
    wb_ref[:, :D_FF] = b_ref[...].astype(BF16)
    wb_ref[:, D_FF:] = zeros


def _prep_down_kernel(w_ref, o_ref, *, n_real):
    keep = pl.program_id(1) < n_real
    o_ref[...] = jnp.where(keep, w_ref[...], 0.0).astype(BF16)


def _prep_in_kernel(w_ref, w1_ref, a1_ref, o_ref):
    n = w_ref.shape[-1]
    o_ref[:, :C_Q] = w_ref[:, :C_Q].astype(BF16)
    for g in range(GQA_GROUP):
        for h in range(ATTN_KV_HEADS):
            src = C_Q + (h * GQA_GROUP + g) * HEAD_DIM
            dst = C_Q + (g * ATTN_KV_HEADS + h) * HEAD_DIM
            o_ref[:, dst:dst + HEAD_DIM] = w_ref[:, src:src + HEAD_DIM].astype(BF16)
    o_ref[:, C_AK:n] = w_ref[:, C_AK:n].astype(BF16)
    o_ref[:, n:] = jnp.concatenate([w1_ref[0], w1_ref[1], a1_ref[0], a1_ref[1]], axis=1).astype(BF16)


def _prep_out_kernel(w_ref, o_ref):
    r0 = RWKV_W + HYENA_W
    o_ref[:r0, :] = w_ref[:r0, :].astype(BF16)
    for g in range(GQA_GROUP):
        for h in range(ATTN_KV_HEADS):
            src = r0 + (h * GQA_GROUP + g) * HEAD_DIM
            dst = r0 + (g * ATTN_KV_HEADS + h) * HEAD_DIM
            o_ref[dst:dst + HEAD_DIM, :] = w_ref[src:src + HEAD_DIM, :].astype(BF16)


def _prep_weights(w_in, rwkv_w1, rwkv_a1, w_out, ffn_w_up, ffn_w_down):
    L, D = w_in.shape[0], D_MODEL
    cp = lambda n: pltpu.CompilerParams(dimension_semantics=("arbitrary",) * n, vmem_limit_bytes=VMEM_LIMIT)
    tr = 128
    half = jax.ShapeDtypeStruct((L, D, D_FF_PAD), BF16)
    wa, wb = pl.pallas_call(
        _prep_up_kernel,
        grid=(L, D // tr),
        in_specs=[pl.BlockSpec((None, tr, D_FF), lambda l, i: (l, i, 0)),
                  pl.BlockSpec((None, tr, D_FF), lambda l, i: (l, i, 1))],
        out_specs=[pl.BlockSpec((None, tr, D_FF_PAD), lambda l, i: (l, i, 0))] * 2,
        out_shape=[half, half], compiler_params=cp(2), name="prep_ffn_up",
    )(ffn_w_up, ffn_w_up)
    nr = D_FF // LANES
    wd = pl.pallas_call(
        functools.partial(_prep_down_kernel, n_real=nr),
        grid=(L, D_FF_PAD // LANES),
        in_specs=[pl.BlockSpec((None, LANES, D), lambda l, j: (l, jnp.minimum(j, nr - 1), 0))],
        out_specs=pl.BlockSpec((None, LANES, D), lambda l, j: (l, j, 0)),
        out_shape=jax.ShapeDtypeStruct((L, D_FF_PAD, D), BF16), compiler_params=cp(2), name="prep_ffn_down",
    )(ffn_w_down)
    tr2 = 256
    n_in = w_in.shape[-1]
    w_ext = pl.pallas_call(
        _prep_in_kernel,
        grid=(L, D // tr2),
        in_specs=[pl.BlockSpec((None, tr2, n_in), lambda l, i: (l, i, 0)),
                  pl.BlockSpec((None, 2, tr2, LORA), lambda l, i: (l, 0, i, 0)),
                  pl.BlockSpec((None, 2, tr2, LORA), lambda l, i: (l, 0, i, 0))],
        out_specs=pl.BlockSpec((None, tr2, IN_W_EXT), lambda l, i: (l, i, 0)),
        out_shape=jax.ShapeDtypeStruct((L, D, IN_W_EXT), BF16), compiler_params=cp(2), name="prep_in",
    )(w_in, rwkv_w1, rwkv_a1)
    wo = pl.pallas_call(
        _prep_out_kernel,
        grid=(L,),
        in_specs=[pl.BlockSpec((None, D, D), lambda l: (l, 0, 0))],
        out_specs=pl.BlockSpec((None, D, D), lambda l: (l, 0, 0)),
        out_shape=jax.ShapeDtypeStruct((L, D, D), BF16), compiler_params=cp(1), name="prep_out",
    )(w_out)
    return w_ext, wo, wa, wb, wd


def kernel(x_prompt, x_sample, c, state_rwkv, cache_k, cache_v, c_ctx, w_mod, b_mod, w_in, rwkv_w0, rwkv_w1, rwkv_w2, rwkv_a0, rwkv_a1, rwkv_a2, rwkv_k_k, rwkv_k_a, rwkv_r_k, rwkv_lnx_g, rwkv_lnx_b, hy_short_w, hy_f_w1, hy_f_b1, hy_f_freq1, hy_f_w2, hy_f_b2, hy_f_freq2, hy_f_w3, hy_bias, attn_sink, w_out, ln1_g, ln1_b, ffn_w_up, ffn_conv_w, ffn_w_down, ln2_g, ln2_b):
    params = dict(rwkv_w0=rwkv_w0, rwkv_w2=rwkv_w2, rwkv_a0=rwkv_a0, rwkv_a2=rwkv_a2,
                  rwkv_k_k=rwkv_k_k, rwkv_k_a=rwkv_k_a, rwkv_r_k=rwkv_r_k,
                  rwkv_lnx_g=rwkv_lnx_g, rwkv_lnx_b=rwkv_lnx_b,
                  hy_short_w=hy_short_w, hy_f_w1=hy_f_w1, hy_f_b1=hy_f_b1, hy_f_freq1=hy_f_freq1,
                  hy_f_w2=hy_f_w2, hy_f_b2=hy_f_b2, hy_f_freq2=hy_f_freq2, hy_f_w3=hy_f_w3,
                  hy_bias=hy_bias)
    L = w_in.shape[0]
    Bc, Sc, D = x_prompt.shape
    Bl, Tl, _ = x_sample.shape
    P = cache_k.shape[2]
    n_ctx = Bc * Sc
    n_lat = Bl * Tl
    N = n_ctx + n_lat
    assert Sc % MOD_BLK == 0 and Tl % MOD_BLK == 0 and n_ctx % Tl == 0

    n_cond = 1 + Bl
    cond = jnp.zeros((16, D), F32).at[0].set(c_ctx).at[1:n_cond].set(c)
    mod_all = _modulation(cond, w_mod, b_mod)
    mod_tab = mod_all.reshape(L, 16, 6, D)

    cm_c, sm_c = (jnp.asarray(m, F32) for m in _dft_mats(Sc))
    cm_l, sm_l = (jnp.asarray(m, F32) for m in _dft_mats(Tl))
    sink_pad = jnp.zeros((L, 1, LANES), F32).at[:, 0, :ATTN_HEADS].set(attn_sink)

    w_ext, wo, wa, wb, wd = _prep_weights(w_in, rwkv_w1, rwkv_a1, w_out, ffn_w_up, ffn_w_down)
    x_src = (x_prompt.reshape(n_ctx, D), x_sample.reshape(n_lat, D), n_ctx, n_lat, 0, Tl)
    states, keys, vals = [], [], []
    for l in range(L):
        p = {n: a[l] for n, a in params.items()}
        proj = _in_projection(x_src, mod_tab, w_ext, l)
        keys.append(proj[:n_ctx, C_AK:C_AK + KV_W].reshape(Bc, Sc, ATTN_KV_HEADS, HEAD_DIM))
        vals.append(proj[:n_ctx, C_AV:C_AV + KV_W].reshape(Bc, Sc, ATTN_KV_HEADS, HEAD_DIM))

        ya_c, s_fin = _rwkv(proj, p, None, T=Sc, n_seq=Bc, row_blk0=0, nsq=RWKV_CTX_SEQS)
        ya_l, _ = _rwkv(proj, p, state_rwkv[:, l], T=Tl, n_seq=Bl, row_blk0=n_ctx // Tl)
        states.append(s_fin)

        yb_c = _hyena(proj, p, _hyena_filter(p, Sc, cm_c, sm_c), cm_c, sm_c, L=Sc, n_seq=Bc, row_blk0=0,
                      nsq=HYENA_CTX_SEQS)
        yb_l = _hyena(proj, p, _hyena_filter(p, Tl, cm_l, sm_l), cm_l, sm_l, L=Tl, n_seq=Bl,
                      row_blk0=n_ctx // Tl)

        yc_c = _ctx_attention(proj, sink_pad[l], S=Sc, n_seq=Bc)
        yc_l = _lat_attention(proj, cache_k[:, l].reshape(Bl, P, KV_W), cache_v[:, l].reshape(Bl, P, KV_W),
                              sink_pad[l], T=Tl, n_seq=Bl, row_blk0=n_ctx // Tl)

        x1, h2 = _out_projection((ya_c, yb_c, yc_c), (ya_l, yb_l, yc_l), x_src, mod_tab,
                                 wo, l, ln1_g[l], ln1_b[l])
        ffn = functools.partial(_ffn, h2, x1, mod_tab, wa, wb, ffn_conv_w[l], wd, l, ln2_g[l], ln2_b[l],
                                n_ctx=n_ctx, s_ctx=Sc, s_lat=Tl)
        if l < L - 1:
            x = ffn()
            x_src = (x, x, n_ctx, n_lat, n_ctx, Tl)
        else:
            y_prompt = ffn(row0=0, n_rows=n_ctx).reshape(Bc, Sc, D)
            y_sample = ffn(row0=n_ctx, n_rows=n_lat).reshape(Bl, Tl, D)

    new_state = jnp.stack(states, 1)
    new_k = jnp.stack(keys, 1)
    new_v = jnp.stack(vals, 1)
    return (y_prompt, y_sample, new_state, new_k, new_v)
```

```python
import functools
import math

import numpy as np
import jax
import jax.numpy as jnp
from jax import lax
from jax.experimental import pallas as pl
from jax.experimental.pallas import tpu as pltpu

F32 = jnp.float32
BF16 = jnp.bfloat16

D_MODEL = 2048
DEPTH = 4
GRID_W = 64
HEAD_DIM = 64
RWKV_W = 768
RWKV_HEADS = 12
HYENA_W = 512
ATTN_W = 768
ATTN_HEADS = 12
ATTN_KV_HEADS = 4
GQA_GROUP = 3
KV_W = 256
LORA = 64
GN_EPS = 64e-5
FILTER_EMB = 33
FILTER_HID = 64
HY_MIN_DECAY = math.log(1e-2) / 1.5
HY_MAX_DECAY = math.log(1e-2) / 0.3
WINDOW = 128
BLOCK = 128
ROPE_BASE = 10000.0
D_FF = 5504
LN_EPS = 1e-5
DN_ALPHA = (2 * DEPTH) ** 0.25

C_R, C_K, C_V, C_G = 0, 768, 1536, 2304
C_HY = 3072
C_Q, C_AK, C_AV = 4608, 5376, 5632
C_LORA = 5888
IN_W_EXT = 6144

MOD_BLK = 256
CHUNK = 64
RWKV_CB = 8
RWKV_CTX_SEQS = 4
HYENA_CTX_SEQS = 4
HYENA_ROW_BLK = 1024
LANES = 128
D_FF_PAD = 5632
VMEM_LIMIT = 56 * 1024 * 1024


def _dot(a, b):
    return jnp.dot(a.astype(BF16), b.astype(BF16), preferred_element_type=F32)


def _dot_nt(a, b):
    return lax.dot_general(a.astype(BF16), b.astype(BF16), (((1,), (1,)), ((), ())),
                           preferred_element_type=F32)


def _split(x):
    hi = x.astype(BF16)
    lo = (x - hi.astype(F32)).astype(BF16)
    return hi, lo


def _dot3_dims(a, b, dims):
    ah, al = _split(a)
    bh, bl = _split(b)
    dg = functools.partial(lax.dot_general, dimension_numbers=dims, preferred_element_type=F32)
    return dg(ah, bh) + (dg(ah, bl) + dg(al, bh))


_NN = (((1,), (0,)), ((), ()))
_NT = (((1,), (1,)), ((), ()))
_TN = (((0,), (0,)), ((), ()))


def _dot3(a, b):
    return _dot3_dims(a, b, _NN)


def _sigmoid(x):
    return 1.0 / (1.0 + jnp.exp(-x))


def _softplus(x):
    return jnp.maximum(x, 0.0) + jnp.log(1.0 + jnp.exp(-jnp.abs(x)))


def _mod_kernel(c_ref, w_ref, b_ref, o_ref):
    c = c_ref[...]
    s = c * _sigmoid(c)
    o_ref[...] = _dot(s, w_ref[...]) + b_ref[...]


def _modulation(cond, w_mod, b_mod):
    L = w_mod.shape[0]
    R = cond.shape[0]
    tn = 1024
    return pl.pallas_call(
        _mod_kernel,
        grid=(L, 6 * D_MODEL // tn),
        in_specs=[pl.BlockSpec((R, D_MODEL), lambda l, j: (0, 0)),
                  pl.BlockSpec((None, D_MODEL, tn), lambda l, j: (l, 0, j)),
                  pl.BlockSpec((None, 1, tn), lambda l, j: (l, 0, j))],
        out_specs=pl.BlockSpec((None, R, tn), lambda l, j: (l, 0, j)),
        out_shape=jax.ShapeDtypeStruct((L, R, 6 * D_MODEL), F32),
        compiler_params=pltpu.CompilerParams(dimension_semantics=("arbitrary", "arbitrary"),
                                             vmem_limit_bytes=VMEM_LIMIT),
        name="modulation",
    )(cond, w_mod, b_mod.reshape(L, 1, 6 * D_MODEL))


def _two_source_specs(tm, width, nct, nlt, lat_off):
    ctx = pl.BlockSpec((tm, width), lambda i: (jnp.minimum(i, nct - 1), 0))
    lat = pl.BlockSpec((tm, width), lambda i: (jnp.clip(i - nct, 0, nlt - 1) + lat_off, 0))
    return ctx, lat


def _mod_spec(l, tm, n_ctx, lat_len, t0=0, rank=1):
    def row(i):
        tok = (i + t0) * tm
        return jnp.where(tok < n_ctx, 0, 1 + (tok - n_ctx) // lat_len)

    if rank == 1:
        return pl.BlockSpec((None, 1, 6, D_MODEL), lambda i: (l, row(i), 0, 0))
    return pl.BlockSpec((None, 1, 6, D_MODEL), lambda i, j: (l, row(i), 0, 0))


def _inproj_kernel(xc_ref, xl_ref, mod_ref, w_ref, o_ref, h_scr, *, tn, n_ctx_tiles):
    shift = mod_ref[0, 0:1, :]
    scale = mod_ref[0, 1:2, :]

    def modulate(x_ref):
        h_scr[...] = (x_ref[...] * (1.0 + scale) + shift).astype(BF16)

    pl.when(pl.program_id(0) < n_ctx_tiles)(lambda: modulate(xc_ref))
    pl.when(pl.program_id(0) >= n_ctx_tiles)(lambda: modulate(xl_ref))
    h = h_scr[...]
    for j in range(IN_W_EXT // tn):
        cols = slice(j * tn, (j + 1) * tn)
        o_ref[:, cols] = jnp.dot(h, w_ref[:, cols], preferred_element_type=F32)


def _in_projection(x_src, mod, w_ext, l, tn=512):
    xc, xl, n_ctx, n_lat, lat_off, lat_len = x_src
    N = n_ctx + n_lat
    tm = MOD_BLK
    nct = n_ctx // tm
    ctx_spec, lat_spec = _two_source_specs(tm, D_MODEL, nct, n_lat // tm, lat_off // tm)
    return pl.pallas_call(
        functools.partial(_inproj_kernel, tn=tn, n_ctx_tiles=nct),
        grid=(N // tm,),
        in_specs=[ctx_spec, lat_spec,
                  _mod_spec(l, tm, n_ctx, lat_len),
                  pl.BlockSpec((None, D_MODEL, IN_W_EXT), lambda i: (l, 0, 0), pipeline_mode=pl.Buffered(1))],
        out_specs=pl.BlockSpec((tm, IN_W_EXT), lambda i: (i, 0)),
        out_shape=jax.ShapeDtypeStruct((N, IN_W_EXT), F32),
        scratch_shapes=[pltpu.VMEM((tm, D_MODEL), BF16)],
        compiler_params=pltpu.CompilerParams(dimension_semantics=("arbitrary",),
                                             vmem_limit_bytes=VMEM_LIMIT),
        name="in_projection",
    )(xc, xl, mod, w_ext)


PREC = dict(seg=1, cum=2, a=1, neu=1, post=1, mn=1, b_o=1, b_s=1, gn=2)


def _dotp(a, b, dims, passes):
    dg = functools.partial(lax.dot_general, dimension_numbers=dims, preferred_element_type=F32)
    if passes == 1:
        return dg(a.astype(BF16), b.astype(BF16))
    ah, al = _split(a)
    if passes == 2:
        bh = b.astype(BF16)
        return dg(ah, bh) + dg(al, bh)
    bh, bl = _split(b)
    return dg(ah, bh) + (dg(ah, bl) + dg(al, bh))


def _dotp_rhs(a, b, passes):
    dg = functools.partial(lax.dot_general, dimension_numbers=_NN, preferred_element_type=F32)
    ab = a.astype(BF16)
    if passes == 1:
        return dg(ab, b.astype(BF16))
    bh, bl = _split(b)
    return dg(ab, bh) + dg(ab, bl)


def _rwkv_kernel(*refs, T, nsq, has_init, cb):
    (r_ref, k_ref, v_ref, g_ref, lo_ref, w0_ref, w2_ref, a0_ref, a2_ref,
     kk_ref, ka_ref, rk_ref, lng_ref, lnb_ref) = refs[:14]
    pos = 14
    init_ref = None
    if has_init:
        init_ref = refs[pos]
        pos += 1
    y_ref, sfin_ref = refs[pos:pos + 2]
    rp_scr, ov_scr, m_scr, n_scr, o_scr, bonus_scr, s_scr = refs[pos + 2:]

    nC = T // CHUNK
    nCs = nC // nsq
    C = CHUNK
    H = HEAD_DIM
    P = PREC
    C2 = 2 * C
    row = lax.broadcasted_iota(jnp.int32, (C, C), 0)
    col = lax.broadcasted_iota(jnp.int32, (C, C), 1)
    tri_f = jnp.where(col <= row, 1.0, 0.0).astype(F32)
    lrow = lax.broadcasted_iota(jnp.int32, (LANES, LANES), 0)
    lcol = lax.broadcasted_iota(jnp.int32, (LANES, LANES), 1)
    same_head = (lrow // H) == (lcol // H)
    head_ones = jnp.where(same_head, 1.0, 0.0).astype(F32)
    eye = lrow == lcol
    rt_, ct_ = lrow % C, lcol % C
    strict_m = [same_head & (ct_ < rt_), same_head & (ct_ > rt_)]
    incl_m = [same_head & (ct_ <= rt_), same_head & (ct_ >= rt_)]
    lane = lax.broadcasted_iota(jnp.int32, (1, LANES), 1)
    hmask = [lane < H, lane >= H]

    def stack_heads(x):
        return jnp.concatenate([jnp.where(hmask[0], x, 0.0), jnp.where(hmask[1], x, 0.0)], axis=0)

    kkw = kk_ref[...]
    kaw = ka_ref[...]
    rkw = rk_ref[...]
    zl = jnp.zeros((LORA, LANES), F32)
    lora_w = jnp.concatenate(
        [jnp.concatenate([w if j == i else zl for j in range(4)], axis=1)
         for i, w in enumerate([w2_ref[0], w2_ref[1], a2_ref[0], a2_ref[1]])], axis=0).astype(BF16)

    def phase_a(ci, carry):
        units = []
        for cc in range(cb):
            c = ci * cb + cc
            rows = pl.ds(pl.multiple_of(c * C, C), C)
            r = r_ref[rows, :]
            k = k_ref[rows, :]
            v = v_ref[rows, :]
            lo = lo_ref[rows, :]
            kkraw = k * kkw
            ss = _dotp(kkraw * kkraw, head_ones, _NN, P['seg'])
            kkn = kkraw / jnp.maximum(jnp.sqrt(ss), 1e-12)
            bonus = jnp.zeros((C, LANES), F32)
            vm = stack_heads(v)
            lo_in = jnp.concatenate([jnp.tanh(lo[:, :2 * LORA]), lo[:, 2 * LORA:]], axis=1)
            lo_out = _dot(lo_in, lora_w)
            lds, alphas = [], []
            for d in range(2):
                wl = w0_ref[d] + lo_out[:, d * LANES:(d + 1) * LANES]
                lds.append(-jnp.exp(-_softplus(-wl) - 0.5))
                alphas.append(_sigmoid(a0_ref[d] + lo_out[:, (2 + d) * LANES:(3 + d) * LANES]))
            pref = _dotp_rhs(tri_f, jnp.concatenate(lds, axis=1), P['cum'])
            for d in range(2):
                ld, alpha = lds[d], alphas[d]
                kd = k * (1.0 + (alpha - 1.0) * kaw)
                bb = kkn * alpha
                bonus = bonus + _dotp(r * kd * rkw, head_ones, _NN, P['seg']) * v
                pf = pref[:, d * LANES:(d + 1) * LANES]
                g_incl = pf if d == 0 else pf[C - 1:C, :] - pf + ld
                g_excl = g_incl - ld
                g_end = g_incl[C - 1:C, :] if d == 0 else g_incl[0:1, :]
                ginv = jnp.exp(-g_incl)
                dec_end = jnp.exp(g_end - g_incl)
                units.append(dict(
                    d=d, c=c, vm=vm,
                    at=stack_heads(-kkn * jnp.exp(g_excl)), rt=stack_heads(r * jnp.exp(g_incl)),
                    bt=stack_heads(bb * ginv), kt=stack_heads(kd * ginv),
                    bend=stack_heads(bb * dec_end), kend=stack_heads(kd * dec_end), gend=jnp.exp(g_end)))
            bonus_scr[rows, :] = bonus
            o_scr[rows, :] = jnp.zeros((C, LANES), F32)

        for u in units:
            ar = jnp.concatenate([u['at'], u['rt']], axis=0)
            bk = jnp.concatenate([u['bt'], u['kt']], axis=0)
            u['a'] = _dotp(ar, bk, _NT, P['a'])
        for u in units:
            a = u['a']
            st, ic = strict_m[u['d']], incl_m[u['d']]
            u['p'] = jnp.where(st, a[:C2, :C2], 0.0)
            u['a_ak'] = jnp.where(st, a[:C2, C2:], 0.0)
            u['a_rb'] = jnp.where(ic, a[C2:, :C2], 0.0)
            u['a_rk'] = jnp.where(ic, a[C2:, C2:], 0.0)
        rvs = [_dotp(u['a_ak'], u['vm'], _NN, P['post']) for u in units]
        sq = [_dotp(u['p'], u['p'], _NN, P['neu']) for u in units]
        for i, u in enumerate(units):
            u['x'] = jnp.where(eye, 1.0, 0.0) + u['p']
            u['p'] = sq[i]
        for it in range(1, 6):
            res = [_dotp(u['p'], jnp.concatenate([u['x'], u['p']], axis=1) if it < 5 else u['x'], _NN, P['neu'])
                   for u in units]
            for i, u in enumerate(units):
                u['x'] = u['x'] + res[i][:, :C2]
                if it < 5:
                    u['p'] = res[i][:, C2:]
        rhs = [_dotp(u['x'], jnp.concatenate([u['at'], rvs[i]], axis=1), _NN, P['post'])
               for i, u in enumerate(units)]
        for i, u in enumerate(units):
            u['rhs'] = rhs[i]
        xs = [_dotp(jnp.concatenate([u['a_rb'], u['a_rk']], axis=1),
                    jnp.concatenate([u['rhs'], jnp.concatenate([jnp.zeros_like(u['vm']), u['vm']], axis=1)], axis=0),
                    _NN, P['post']) for u in units]
        ms = [_dotp(u['rhs'][:, :LANES], u['bend'], _TN, P['mn']) for u in units]
        ns = [_dotp(jnp.concatenate([u['rhs'][:, LANES:], u['vm']], axis=0),
                    jnp.concatenate([u['bend'], u['kend']], axis=0), _TN, P['mn']) for u in units]
        for i, u in enumerate(units):
            d, c = u['d'], u['c']
            rp = u['rt'] + xs[i][:, :LANES]
            ov = xs[i][:, LANES:]
            rp_scr[d, c] = rp[:C] + rp[C:]
            ov_scr[d, c] = ov[:C] + ov[C:]
            m_scr[d, c] = jnp.where(eye, jnp.broadcast_to(u['gend'], (LANES, LANES)), 0.0) + ms[i]
            n_scr[d, c] = ns[i]
        return carry

    lax.fori_loop(0, nC // cb, phase_a, 0)

    zero_blk = jnp.zeros((H, H), F32)
    for q in range(nsq):
        for d in range(2):
            if has_init:
                s_scr[q, d] = jnp.concatenate([jnp.concatenate([init_ref[q, d, 0], zero_blk], axis=1),
                                               jnp.concatenate([zero_blk, init_ref[q, d, 1]], axis=1)], axis=0)
            else:
                s_scr[q, d] = jnp.zeros((LANES, LANES), F32)

    def phase_b(i, carry):
        idx = [(q, d, q * nCs + (i if d == 0 else nCs - 1 - i)) for q in range(nsq) for d in range(2)]
        ss = [s_scr[q, d] for q, d, c in idx]
        os_ = [_dotp(rp_scr[d, c], ss[j], _NT, P['b_o']) for j, (q, d, c) in enumerate(idx)]
        sn = [_dotp(ss[j], m_scr[d, c], _NN, P['b_s']) for j, (q, d, c) in enumerate(idx)]
        for j, (q, d, c) in enumerate(idx):
            s_scr[q, d] = sn[j] + n_scr[d, c]
            rows = pl.ds(c * C if isinstance(c, int) else pl.multiple_of(c * C, C), C)
            o_scr[rows, :] = o_scr[rows, :] + (os_[j] + ov_scr[d, c])
        return carry

    if nCs <= 4:
        for i in range(nCs):
            phase_b(i, 0)
    else:
        lax.fori_loop(0, nCs, phase_b, 0, unroll=4)

    for q in range(nsq):
        for d in range(2):
            s = s_scr[q, d]
            sfin_ref[q, d, 0] = s[:H, :H]
            sfin_ref[q, d, 1] = s[H:, H:]

    lng = lng_ref[...]
    lnb = lnb_ref[...]

    rb = min(T, 1024)

    def phase_c(c, carry):
        rows = pl.ds(pl.multiple_of(c * rb, rb), rb)
        o = o_scr[rows, :]
        mu = _dotp(o, head_ones, _NN, P['gn']) * (1.0 / H)
        dlt = o - mu
        var = _dotp(dlt * dlt, head_ones, _NN, P['gn']) * (1.0 / H)
        on = dlt * lax.rsqrt(var + GN_EPS) * lng + lnb
        y_ref[rows, :] = (on + bonus_scr[rows, :]) * _sigmoid(g_ref[rows, :])
        return carry

    lax.fori_loop(0, T // rb, phase_c, 0)


def _rwkv(proj, p, init, *, T, n_seq, row_blk0, nsq=1):
    seq_len, T = T, nsq * T
    nC = T // CHUNK
    npair = RWKV_HEADS // 2
    has_init = init is not None
    cb = lambda off: (lambda b, j: (b + row_blk0, off // LANES + j))
    vec = lambda b, j: (0, j)
    in_specs = [pl.BlockSpec((T, LANES), cb(C_R)),
                pl.BlockSpec((T, LANES), cb(C_K)),
                pl.BlockSpec((T, LANES), cb(C_V)),
                pl.BlockSpec((T, LANES), cb(C_G)),
                pl.BlockSpec((T, 4 * LORA), lambda b, j: (b + row_blk0, C_LORA // (4 * LORA))),
                pl.BlockSpec((2, 1, LANES), lambda b, j: (0, 0, j)),
                pl.BlockSpec((2, LORA, LANES), lambda b, j: (0, 0, j)),
                pl.BlockSpec((2, 1, LANES), lambda b, j: (0, 0, j)),
                pl.BlockSpec((2, LORA, LANES), lambda b, j: (0, 0, j)),
                pl.BlockSpec((1, LANES), vec), pl.BlockSpec((1, LANES), vec),
                pl.BlockSpec((1, LANES), vec), pl.BlockSpec((1, LANES), vec),
                pl.BlockSpec((1, LANES), vec)]
    args = [proj, proj, proj, proj, proj,
            p['rwkv_w0'].reshape(2, 1, RWKV_W), p['rwkv_w2'], p['rwkv_a0'].reshape(2, 1, RWKV_W), p['rwkv_a2'],
            p['rwkv_k_k'].reshape(1, RWKV_W), p['rwkv_k_a'].reshape(1, RWKV_W),
            p['rwkv_r_k'].reshape(1, RWKV_W), p['rwkv_lnx_g'].reshape(1, RWKV_W),
            p['rwkv_lnx_b'].reshape(1, RWKV_W)]
    st_spec = pl.BlockSpec((nsq, 2, 2, HEAD_DIM, HEAD_DIM), lambda b, j: (b, 0, j, 0, 0))
    if has_init:
        in_specs.append(st_spec)
        args.append(init)
    sc_o = lambda: pltpu.VMEM((2, nC, CHUNK, LANES), F32)
    sc_s = lambda: pltpu.VMEM((2, nC, LANES, LANES), F32)
    return pl.pallas_call(
        functools.partial(_rwkv_kernel, T=T, nsq=nsq, has_init=has_init, cb=min(RWKV_CB, nC)),
        grid=(n_seq // nsq, npair),
        in_specs=in_specs,
        out_specs=[pl.BlockSpec((T, LANES), lambda b, j: (b, j)), st_spec],
        out_shape=[jax.ShapeDtypeStruct((n_seq * seq_len, RWKV_W), F32),
                   jax.ShapeDtypeStruct((n_seq, 2, RWKV_HEADS, HEAD_DIM, HEAD_DIM), F32)],
        scratch_shapes=[sc_o(), sc_o(), sc_s(), sc_s(),
                        pltpu.VMEM((T, LANES), F32), pltpu.VMEM((T, LANES), F32),
                        pltpu.VMEM((nsq, 2, LANES, LANES), F32)],
        compiler_params=pltpu.CompilerParams(dimension_semantics=("arbitrary", "arbitrary"),
                                             vmem_limit_bytes=VMEM_LIMIT),
        name="rwkv_T%d" % T,
    )(*args)


def _dft_mats(L):
    f = np.arange(L, dtype=np.int64)
    ph = (np.outer(f, f) % (2 * L)).astype(np.float64) * (math.pi / L)
    return np.cos(ph), -np.sin(ph)


def _filter_features(L):
    t = np.linspace(0.0, 1.0, L, dtype=np.float32)[:, None]
    bands = (FILTER_EMB - 1) // 2
    t_res = np.arange(L, dtype=np.float32)[:, None]
    f = np.linspace(1e-4, bands - 1, bands, dtype=np.float32)[None, :]
    ang = (2.0 * math.pi * t_res * f / L).astype(np.float32)
    z = np.concatenate([t, np.cos(ang), np.sin(ang)], -1).astype(np.float32)
    zp = np.zeros((L, FILTER_HID), np.float32)
    zp[:, :FILTER_EMB] = z
    deltas = np.abs(np.linspace(HY_MIN_DECAY, HY_MAX_DECAY, HYENA_W, dtype=np.float32))
    return zp, t.astype(np.float32), deltas[None, :].astype(np.float32)


def _hyfilter_kernel(z_ref, t_ref, dl_ref, w1_ref, b1_ref, q1_ref, w2_ref, b2_ref, q2_ref, wc_ref, wa_ref,
                     hs_ref, hd_ref, knyq_ref, hm_scr, *, L):
    @pl.when((pl.program_id(0) == 0) & (pl.program_id(1) == 0))
    def _():
        h1 = jnp.sin(q1_ref[...] * (_dot3(z_ref[...], w1_ref[...]) + b1_ref[...]))
        hm_scr[...] = jnp.sin(q2_ref[...] * (_dot3(h1, w2_ref[...]) + b2_ref[...]))

    hm = hm_scr[...]
    win = jnp.exp(-t_ref[...] * dl_ref[...])
    lag = lax.broadcasted_iota(jnp.int32, (L, 1), 0)
    alt = jnp.where(lag % 2 == 0, 1.0, -1.0).astype(F32)
    hc = _dot3(hm, wc_ref[...]) * win
    ha = jnp.where(lag == 0, 0.0, _dot3(hm, wa_ref[...]) * win)
    hs_ref[...] = hc + ha
    hd_ref[...] = hc - ha
    knyq_ref[...] = jnp.sum((hc + ha) * alt, axis=0, keepdims=True) * (1.0 / (2 * L))


def _hyspec_kernel(cm_ref, sm_ref, hs_ref, hd_ref, kre_ref, kim_ref, *, L, tf):
    f = pl.program_id(2) * tf + lax.broadcasted_iota(jnp.int32, (tf, 1), 0)
    wf = jnp.where(f == 0, 1.0, 2.0).astype(F32) * (1.0 / (2 * L))
    kre_ref[...] = _dot(cm_ref[...], hs_ref[...]) * wf
    kim_ref[...] = _dot(sm_ref[...], hd_ref[...]) * wf


def _hyena_filter(p, L, cm, sm, cw=128):
    zp, t, dl = _filter_features(L)
    w1 = jnp.zeros((FILTER_HID, FILTER_HID), F32).at[:FILTER_EMB].set(p['hy_f_w1'])
    r2 = lambda a: a.reshape(1, -1)
    ncb = HYENA_W // cw
    full = lambda shape: pl.BlockSpec(shape, lambda n, j: (0,) * len(shape))
    shp = jax.ShapeDtypeStruct((2, L, HYENA_W), F32)
    hs, hd, knyq = pl.pallas_call(
        functools.partial(_hyfilter_kernel, L=L),
        grid=(2, ncb),
        in_specs=[full((L, FILTER_HID)), full((L, 1)),
                  pl.BlockSpec((1, cw), lambda n, j: (0, j)),
                  full((FILTER_HID, FILTER_HID)), full((1, FILTER_HID)), full((1, FILTER_HID)),
                  full((FILTER_HID, FILTER_HID)), full((1, FILTER_HID)), full((1, FILTER_HID)),
                  pl.BlockSpec((FILTER_HID, cw), lambda n, j: (0, n * 2 * ncb + j)),
                  pl.BlockSpec((FILTER_HID, cw), lambda n, j: (0, n * 2 * ncb + ncb + j))],
        out_specs=[pl.BlockSpec((None, L, cw), lambda n, j: (n, 0, j)),
                   pl.BlockSpec((None, L, cw), lambda n, j: (n, 0, j)),
                   pl.BlockSpec((None, 1, cw), lambda n, j: (n, 0, j))],
        out_shape=[shp, shp, jax.ShapeDtypeStruct((2, 1, HYENA_W), F32)],
        scratch_shapes=[pltpu.VMEM((L, FILTER_HID), F32)],
        compiler_params=pltpu.CompilerParams(dimension_semantics=("arbitrary", "arbitrary"),
                                             vmem_limit_bytes=VMEM_LIMIT),
        name="hyena_filter_L%d" % L,
    )(jnp.asarray(zp), jnp.asarray(t), jnp.asarray(dl), w1, r2(p['hy_f_b1']), r2(p['hy_f_freq1']),
      p['hy_f_w2'], r2(p['hy_f_b2']), r2(p['hy_f_freq2']), p['hy_f_w3'], p['hy_f_w3'])
    tf = min(512, L)
    sw = 256
    kre, kim = pl.pallas_call(
        functools.partial(_hyspec_kernel, L=L, tf=tf),
        grid=(2, HYENA_W // sw, L // tf),
        in_specs=[pl.BlockSpec((tf, L), lambda n, j, i: (i, 0)),
                  pl.BlockSpec((tf, L), lambda n, j, i: (i, 0)),
                  pl.BlockSpec((None, L, sw), lambda n, j, i: (n, 0, j)),
                  pl.BlockSpec((None, L, sw), lambda n, j, i: (n, 0, j))],
        out_specs=[pl.BlockSpec((None, tf, sw), lambda n, j, i: (n, i, j)),
                   pl.BlockSpec((None, tf, sw), lambda n, j, i: (n, i, j))],
        out_shape=[shp, shp],
        compiler_params=pltpu.CompilerParams(dimension_semantics=("arbitrary",) * 3,
                                             vmem_limit_bytes=VMEM_LIMIT),
        name="hyena_spectrum_L%d" % L,
    )(cm, sm, hs, hd)
    return kre, kim, knyq


def _hyena_kernel(v_ref, x1_ref, x2_ref, sw_ref, bias_ref, cm_ref, sm_ref, kre_ref, kim_ref, knyq_ref,
                  o_ref, zb_scr, zf_scr, xs_scr, yre_scr, yim_scr, *, L, rb, nsq):
    nrb = L // rb
    cw = v_ref.shape[-1]
    t = lax.broadcasted_iota(jnp.int32, (nsq * L, 1), 0) % L
    first = t == 0
    last = t == L - 1
    tl = lax.broadcasted_iota(jnp.int32, (L, 1), 0)
    alt_all = jnp.where(tl % 2 == 0, 1.0, -1.0).astype(F32)
    tb = lax.broadcasted_iota(jnp.int32, (rb, 1), 0)
    alt_blk = jnp.where(tb % 2 == 0, 1.0, -1.0).astype(F32)
    sw = sw_ref[...]
    tile = lambda a: jnp.concatenate([a] * nsq, axis=1) if nsq > 1 else a

    def short(ref, s):
        u = ref[...]
        prev = jnp.where(first, 0.0, pltpu.roll(u, 1, 0))
        nxt = jnp.where(last, 0.0, pltpu.roll(u, nsq * L - 1, 0))
        c = prev * sw[0, s][None, :] + u * sw[1, s][None, :] + nxt * sw[2, s][None, :]
        if nsq == 1:
            return c
        return jnp.concatenate([c[q * L:(q + 1) * L] for q in range(nsq)], axis=1)

    z0 = short(v_ref, 0)
    zf_scr[...] = z0
    zb_scr[...] = z0.astype(BF16)
    unyq = jnp.sum(z0 * alt_all, axis=0, keepdims=True)

    for n in range(2):
        xs_scr[...] = short(x1_ref if n == 0 else x2_ref, n + 1)

        def fwd(fb, carry):
            rows = pl.ds(pl.multiple_of(fb * rb, rb), rb)
            zb = zb_scr[...]
            ure = jnp.dot(cm_ref[rows, :], zb, preferred_element_type=F32)
            uim = jnp.dot(sm_ref[rows, :], zb, preferred_element_type=F32)
            kre = tile(kre_ref[n, rows, :])
            kim = tile(kim_ref[n, rows, :])
            yre_scr[rows, :] = (ure * kre - uim * kim).astype(BF16)
            yim_scr[rows, :] = (ure * kim + uim * kre).astype(BF16)
            return carry

        lax.fori_loop(0, nrb, fwd, 0)
        ynyq = unyq * tile(knyq_ref[n])
        bias = tile(bias_ref[n])

        def inv(ib, acc):
            r0 = pl.multiple_of(ib * rb, rb)
            rows = pl.ds(r0, rb)
            y = (jnp.dot(cm_ref[rows, :], yre_scr[...], preferred_element_type=F32)
                 + jnp.dot(sm_ref[rows, :], yim_scr[...], preferred_element_type=F32))
            y = y + alt_blk * ynyq + zf_scr[rows, :] * bias
            znew = xs_scr[rows, :] * y
            if n == 0:
                zf_scr[rows, :] = znew
                zb_scr[rows, :] = znew.astype(BF16)
                acc = acc + jnp.sum(znew * alt_blk, axis=0, keepdims=True)
            else:
                for q in range(nsq):
                    o_ref[pl.ds(q * L + r0, rb), :] = znew[:, q * cw:(q + 1) * cw]
            return acc

        unyq = lax.fori_loop(0, nrb, inv, jnp.zeros_like(unyq))


def _hyena(proj, p, filt, cm, sm, *, L, n_seq, row_blk0, nsq=1, cw=256):
    kre, kim, knyq = filt
    ncb = HYENA_W // cw
    cb = lambda s: (lambda j, b: (b + row_blk0, (C_HY + s * HYENA_W) // cw + j))
    once = pl.Buffered(1)
    sw = p['hy_short_w'].reshape(3, 3, HYENA_W)
    rb = min(HYENA_ROW_BLK, L)
    Lb, wide = nsq * L, nsq * cw
    return pl.pallas_call(
        functools.partial(_hyena_kernel, L=L, rb=rb, nsq=nsq),
        grid=(ncb, n_seq // nsq),
        in_specs=[pl.BlockSpec((Lb, cw), cb(0)), pl.BlockSpec((Lb, cw), cb(1)), pl.BlockSpec((Lb, cw), cb(2)),
                  pl.BlockSpec((3, 3, cw), lambda j, b: (0, 0, j)),
                  pl.BlockSpec((2, 1, cw), lambda j, b: (0, 0, j)),
                  pl.BlockSpec((L, L), lambda j, b: (0, 0), pipeline_mode=once),
                  pl.BlockSpec((L, L), lambda j, b: (0, 0), pipeline_mode=once),
                  pl.BlockSpec((2, L, cw), lambda j, b: (0, 0, j), pipeline_mode=once),
                  pl.BlockSpec((2, L, cw), lambda j, b: (0, 0, j), pipeline_mode=once),
                  pl.BlockSpec((2, 1, cw), lambda j, b: (0, 0, j))],
        out_specs=pl.BlockSpec((Lb, cw), lambda j, b: (b, j)),
        out_shape=jax.ShapeDtypeStruct((n_seq * L, HYENA_W), F32),
        scratch_shapes=[pltpu.VMEM((L, wide), BF16), pltpu.VMEM((L, wide), F32), pltpu.VMEM((L, wide), F32),
                        pltpu.VMEM((L, wide), BF16), pltpu.VMEM((L, wide), BF16)],
        compiler_params=pltpu.CompilerParams(dimension_semantics=("arbitrary", "arbitrary"),
                                             vmem_limit_bytes=VMEM_LIMIT),
        name="hyena_L%d" % L,
    )(proj, proj, proj, sw, p['hy_bias'].reshape(2, 1, HYENA_W), cm.astype(BF16), sm.astype(BF16),
      kre, kim, knyq)


def _row_max(x):
    t = x[:, :LANES]
    for c in range(1, x.shape[1] // LANES):
        t = jnp.maximum(t, x[:, c * LANES:(c + 1) * LANES])
    return jnp.max(t, axis=-1, keepdims=True)


def _row_sum(x):
    t = x[:, :LANES]
    for c in range(1, x.shape[1] // LANES):
        t = t + x[:, c * LANES:(c + 1) * LANES]
    return jnp.sum(t, axis=-1, keepdims=True)


def _head_masks():
    lane = lax.broadcasted_iota(jnp.int32, (1, KV_W), 1) // HEAD_DIM
    return [lane == h for h in range(ATTN_KV_HEADS)]


def _attend_groups(qgs, keys, vals, masks, sinks, hm):
    nh = len(hm)
    un = [(g, h) for g in range(len(qgs)) for h in range(nh)]
    qm = [jnp.where(hm[h], qgs[g], 0.0).astype(BF16) for g, h in un]
    s = [[_dot_nt(qm[u], k) for k in keys] for u in range(len(un))]
    s = [[sc if mk is None else jnp.where(mk, sc, -jnp.inf) for sc, mk in zip(s[u], masks)] for u in range(len(un))]
    m = [jnp.maximum(_row_max(jnp.concatenate(s[u], axis=1)), sinks[g][h]) for u, (g, h) in enumerate(un)]
    p = [[jnp.exp(sc - m[u]) for sc in s[u]] for u in range(len(un))]
    den = [_row_sum(jnp.concatenate(p[u], axis=1)) + jnp.exp(sinks[g][h] - m[u]) for u, (g, h) in enumerate(un)]
    pv = [[_dot(pc, v) for pc, v in zip(p[u], vals)] for u in range(len(un))]
    accs = [jnp.zeros(q.shape, F32) for q in qgs]
    for u, (g, h) in enumerate(un):
        tot = pv[u][0]
        for t in pv[u][1:]:
            tot = tot + t
        accs[g] = jnp.where(hm[h], tot / den[u], accs[g])
    return accs


def _ctx_attn_kernel(q_ref, k_ref, v_ref, sink_ref, o_ref, *, S):
    scale = HEAD_DIM ** -0.5
    hm = _head_masks()
    k = k_ref[...].astype(BF16)
    v = v_ref[...].astype(BF16)
    for g in range(GQA_GROUP):
        qg = q_ref[:, g * KV_W:(g + 1) * KV_W] * scale
        sinks = [jnp.broadcast_to(sink_ref[0:1, h * GQA_GROUP + g:h * GQA_GROUP + g + 1], (S, 1))
                 for h in range(ATTN_KV_HEADS)]
        o_ref[:, g * KV_W:(g + 1) * KV_W] = _attend_groups([qg], [k], [v], [None], [sinks], hm)[0]


def _ctx_attention(proj, sink, *, S, n_seq):
    return pl.pallas_call(
        functools.partial(_ctx_attn_kernel, S=S),
        grid=(n_seq,),
        in_specs=[pl.BlockSpec((S, ATTN_W), lambda b: (b, C_Q // ATTN_W)),
                  pl.BlockSpec((S, KV_W), lambda b: (b, C_AK // KV_W)),
                  pl.BlockSpec((S, KV_W), lambda b: (b, C_AV // KV_W)),
                  pl.BlockSpec((1, LANES), lambda b: (0, 0))],
        out_specs=pl.BlockSpec((S, ATTN_W), lambda b: (b, 0)),
        out_shape=jax.ShapeDtypeStruct((n_seq * S, ATTN_W), F32),
        compiler_params=pltpu.CompilerParams(dimension_semantics=("arbitrary",),
                                             vmem_limit_bytes=VMEM_LIMIT),
        name="ctx_attention",
    )(proj, proj, proj, sink)


def _rope(x, cos, sin_signed, first_half):
    n = x.shape[-1]
    partner = jnp.where(first_half, pltpu.roll(x, n - HEAD_DIM // 2, 1), pltpu.roll(x, HEAD_DIM // 2, 1))
    return x * cos + partner * sin_signed


def _lat_attn_kernel(q_ref, kp_ref, kc_ref, kn_ref, vp_ref, vc_ref, vn_ref, ck_ref, cv_ref,
                     cq_ref, sq_ref, cp_ref, sp_ref, cc_ref, sc_ref, cn_ref, sn_ref, sink_ref, o_ref, *, T):
    i = pl.program_id(1)
    scale = HEAD_DIM ** -0.5
    hm = _head_masks()
    fh_q = (lax.broadcasted_iota(jnp.int32, (1, ATTN_W), 1) % HEAD_DIM) < HEAD_DIM // 2
    fh_k = (lax.broadcasted_iota(jnp.int32, (1, KV_W), 1) % HEAD_DIM) < HEAD_DIM // 2
    q = _rope(q_ref[...], cq_ref[...], sq_ref[...], fh_q) * scale
    kl = jnp.concatenate([_rope(kp_ref[...], cp_ref[...], sp_ref[...], fh_k),
                          _rope(kc_ref[...], cc_ref[...], sc_ref[...], fh_k),
                          _rope(kn_ref[...], cn_ref[...], sn_ref[...], fh_k)], axis=0).astype(BF16)
    vl = jnp.concatenate([vp_ref[...], vc_ref[...], vn_ref[...]], axis=0).astype(BF16)
    kc = ck_ref[...].astype(BF16)
    vc = cv_ref[...].astype(BF16)
    qpos = i * BLOCK + lax.broadcasted_iota(jnp.int32, (BLOCK, 3 * BLOCK), 0)
    kpos = (i - 1) * BLOCK + lax.broadcasted_iota(jnp.int32, (BLOCK, 3 * BLOCK), 1)
    ok = (jnp.abs(qpos - kpos) <= WINDOW) & (kpos >= 0) & (kpos < T)
    qgs = [q[:, g * KV_W:(g + 1) * KV_W] for g in range(GQA_GROUP)]
    sinks = [[jnp.broadcast_to(sink_ref[0:1, h * GQA_GROUP + g:h * GQA_GROUP + g + 1], (BLOCK, 1))
              for h in range(ATTN_KV_HEADS)] for g in range(GQA_GROUP)]
    accs = _attend_groups(qgs, [kl, kc], [vl, vc], [ok, None], sinks, hm)
    for g in range(GQA_GROUP):
        o_ref[:, g * KV_W:(g + 1) * KV_W] = accs[g]


def _rope_tables(T):
    rows = T // GRID_W
    row = np.repeat(np.arange(rows), GRID_W).astype(np.float32)
    col = np.tile(np.arange(GRID_W), rows).astype(np.float32)
    nf = HEAD_DIM // 4
    inv = (ROPE_BASE ** (-np.arange(nf, dtype=np.float32) / nf)).astype(np.float32)
    ang = np.concatenate([row[:, None] * inv, col[:, None] * inv], -1).astype(np.float32)
    cos = np.cos(ang).astype(np.float32)
    sin = np.sin(ang).astype(np.float32)
    cos_h = np.concatenate([cos, cos], -1)
    sin_h = np.concatenate([-sin, sin], -1)
    return cos_h, sin_h


def _lat_attention(proj, cache_k, cache_v, sink, *, T, n_seq, row_blk0):
    nb = T // BLOCK
    P = cache_k.shape[1]
    cos_h, sin_h = _rope_tables(T)
    cq = jnp.asarray(np.tile(cos_h, (1, ATTN_HEADS)))
    sq = jnp.asarray(np.tile(sin_h, (1, ATTN_HEADS)))
    ck = jnp.asarray(np.tile(cos_h, (1, ATTN_KV_HEADS)))
    sk = jnp.asarray(np.tile(sin_h, (1, ATTN_KV_HEADS)))
    r0 = row_blk0 * nb
    prv = lambda i: jnp.maximum(i - 1, 0)
    nxt = lambda i: jnp.minimum(i + 1, nb - 1)
    kcol, vcol = C_AK // KV_W, C_AV // KV_W
    kv = lambda col, f: pl.BlockSpec((BLOCK, KV_W), lambda b, i: (r0 + b * nb + f(i), col))
    tab = lambda w, f: pl.BlockSpec((BLOCK, w), lambda b, i: (f(i), 0))
    cur = lambda i: i
    return pl.pallas_call(
        functools.partial(_lat_attn_kernel, T=T),
        grid=(n_seq, nb),
        in_specs=[pl.BlockSpec((BLOCK, ATTN_W), lambda b, i: (r0 + b * nb + i, C_Q // ATTN_W)),
                  kv(kcol, prv), kv(kcol, cur), kv(kcol, nxt),
                  kv(vcol, prv), kv(vcol, cur), kv(vcol, nxt),
                  pl.BlockSpec((None, P, KV_W), lambda b, i: (b, 0, 0)),
                  pl.BlockSpec((None, P, KV_W), lambda b, i: (b, 0, 0)),
                  tab(ATTN_W, cur), tab(ATTN_W, cur),
                  tab(KV_W, prv), tab(KV_W, prv), tab(KV_W, cur), tab(KV_W, cur), tab(KV_W, nxt), tab(KV_W, nxt),
                  pl.BlockSpec((1, LANES), lambda b, i: (0, 0))],
        out_specs=pl.BlockSpec((BLOCK, ATTN_W), lambda b, i: (b * nb + i, 0)),
        out_shape=jax.ShapeDtypeStruct((n_seq * T, ATTN_W), F32),
        compiler_params=pltpu.CompilerParams(dimension_semantics=("arbitrary", "arbitrary"),
                                             vmem_limit_bytes=VMEM_LIMIT),
        name="latent_attention",
    )(proj, proj, proj, proj, proj, proj, proj, cache_k, cache_v, cq, sq, ck, sk, ck, sk, ck, sk, sink)


def _layer_norm(y, g, b):
    mu = jnp.mean(y, axis=-1, keepdims=True)
    d = y - mu
    var = jnp.mean(d * d, axis=-1, keepdims=True)
    return d * lax.rsqrt(var + LN_EPS) * g + b


def _outproj_kernel(ac_ref, bc_ref, cc_ref, al_ref, bl_ref, cl_ref, xc_ref, xl_ref, mod_ref, w_ref,
                    g_ref, be_ref, x1_ref, h2_ref, *, tm, n_ctx_tiles):
    r1, r2 = RWKV_W, RWKV_W + HYENA_W
    sub = 128

    def tile(a_ref, b_ref, c_ref, x_ref):
        ms = []
        for s in range(tm // sub):
            rows = slice(s * sub, (s + 1) * sub)
            ms.append(jnp.dot(a_ref[rows, :].astype(BF16), w_ref[0:r1, :], preferred_element_type=F32)
                      + jnp.dot(b_ref[rows, :].astype(BF16), w_ref[r1:r2, :], preferred_element_type=F32)
                      + jnp.dot(c_ref[rows, :].astype(BF16), w_ref[r2:, :], preferred_element_type=F32))
        for s in range(tm // sub):
            rows = slice(s * sub, (s + 1) * sub)
            gate1 = mod_ref[0, 2:3, :]
            shift2 = mod_ref[0, 3:4, :]
            scale2 = mod_ref[0, 4:5, :]
            x1 = _layer_norm(DN_ALPHA * x_ref[rows, :] + gate1 * ms[s], g_ref[...], be_ref[...])
            x1_ref[rows, :] = x1
            h2_ref[rows, :] = (x1 * (1.0 + scale2) + shift2).astype(BF16)

    pl.when(pl.program_id(0) < n_ctx_tiles)(lambda: tile(ac_ref, bc_ref, cc_ref, xc_ref))
    pl.when(pl.program_id(0) >= n_ctx_tiles)(lambda: tile(al_ref, bl_ref, cl_ref, xl_ref))


def _out_projection(mix_ctx, mix_lat, x_src, mod, w_out, l, g, b, tm=256):
    xc, xl, n_ctx, n_lat, lat_off, lat_len = x_src
    N = n_ctx + n_lat
    nct = n_ctx // tm
    nlt = n_lat // tm
    row = lambda w: pl.BlockSpec((tm, w), lambda i: (i, 0))
    ctx = lambda w: _two_source_specs(tm, w, nct, nlt, 0)[0]
    lat = lambda w: _two_source_specs(tm, w, nct, nlt, 0)[1]
    xc_spec, xl_spec = _two_source_specs(tm, D_MODEL, nct, nlt, lat_off // tm)
    full = lambda r: pl.BlockSpec((r, D_MODEL), lambda i: (0, 0))
    return pl.pallas_call(
        functools.partial(_outproj_kernel, tm=tm, n_ctx_tiles=nct),
        grid=(N // tm,),
        in_specs=[ctx(RWKV_W), ctx(HYENA_W), ctx(ATTN_W), lat(RWKV_W), lat(HYENA_W), lat(ATTN_W),
                  xc_spec, xl_spec,
                  _mod_spec(l, tm, n_ctx, lat_len),
                  pl.BlockSpec((None, D_MODEL, D_MODEL), lambda i: (l, 0, 0)), full(1), full(1)],
        out_specs=[row(D_MODEL), row(D_MODEL)],
        out_shape=[jax.ShapeDtypeStruct((N, D_MODEL), F32), jax.ShapeDtypeStruct((N, D_MODEL), BF16)],
        compiler_params=pltpu.CompilerParams(dimension_semantics=("arbitrary",),
                                             vmem_limit_bytes=VMEM_LIMIT),
        name="out_projection",
    )(*mix_ctx, *mix_lat, xc, xl, mod, w_out, g.reshape(1, D_MODEL), b.reshape(1, D_MODEL))


HALO = 16


def _ffn_kernel(hp_ref, h_ref, hn_ref, x1_ref, mod_ref, wa_ref, wb_ref, cwa_ref, cwb_ref, wd_ref, g_ref, be_ref,
                o_ref, hx_scr, ua_scr, ub_scr, acc_scr, *, tm, tf, nsub, t0, n_ctx, s_ctx, s_lat):
    i = pl.program_id(0) + t0
    j = pl.program_id(1)

    @pl.when(j == 0)
    def _():
        hx_scr[0:HALO, :] = hp_ref[...]
        hx_scr[HALO:HALO + tm, :] = h_ref[...]
        hx_scr[HALO + tm:, :] = hn_ref[...]
        acc_scr[...] = jnp.zeros_like(acc_scr)

    hx = hx_scr[...]
    ua_scr[...] = jnp.dot(hx, wa_ref[...], preferred_element_type=F32)
    ub_scr[...] = jnp.dot(hx, wb_ref[...], preferred_element_type=F32)
    grow = i * tm + lax.broadcasted_iota(jnp.int32, (tm, 1), 0)
    pos = jnp.where(grow < n_ctx, grow % s_ctx, (grow - n_ctx) % s_lat)
    slen = jnp.where(grow < n_ctx, s_ctx, s_lat)
    mprev = jnp.where(pos == 0, 0.0, 1.0).astype(F32)
    mnext = jnp.where(pos == slen - 1, 0.0, 1.0).astype(F32)

    def conv(u_scr, cw_ref):
        up = u_scr[HALO - 1:HALO - 1 + tm, :] * mprev
        uc = u_scr[HALO:HALO + tm, :]
        un = u_scr[HALO + 1:HALO + 1 + tm, :] * mnext
        return up * cw_ref[0:1, :] + uc * cw_ref[1:2, :] + un * cw_ref[2:3, :]

    a = conv(ua_scr, cwa_ref)
    b = conv(ub_scr, cwb_ref)
    act = (a * _sigmoid(a) * b).astype(BF16)
    acc_scr[...] += jnp.dot(act, wd_ref[...], preferred_element_type=F32)

    @pl.when(j == pl.num_programs(1) - 1)
    def _():
        for s in range(nsub):
            rows = slice(s * MOD_BLK, (s + 1) * MOD_BLK)
            gate2 = mod_ref[0, 5:6, :]
            o_ref[rows, :] = _layer_norm(DN_ALPHA * x1_ref[rows, :] + gate2 * acc_scr[rows, :],
                                         g_ref[...], be_ref[...])


def _ffn(h2, x1, mod, wa, wb, cw, wd, l, g, b, *, n_ctx, s_ctx, s_lat, row0=0, n_rows=None, tm=512, tf=512):
    N = x1.shape[0]
    n_rows = N if n_rows is None else n_rows
    t0 = row0 // tm
    nsub = tm // MOD_BLK
    nh = tm // HALO
    last_h = N // HALO - 1
    pad = ((0, 0), (0, D_FF_PAD - D_FF))
    cwa, cwb = jnp.pad(cw[:, :D_FF], pad), jnp.pad(cw[:, D_FF:], pad)
    return pl.pallas_call(
        functools.partial(_ffn_kernel, tm=tm, tf=tf, nsub=nsub, t0=t0, n_ctx=n_ctx, s_ctx=s_ctx, s_lat=s_lat),
        grid=(n_rows // tm, D_FF_PAD // tf),
        in_specs=[pl.BlockSpec((HALO, D_MODEL), lambda i, j: (jnp.maximum((i + t0) * nh - 1, 0), 0)),
                  pl.BlockSpec((tm, D_MODEL), lambda i, j: (i + t0, 0)),
                  pl.BlockSpec((HALO, D_MODEL), lambda i, j: (jnp.minimum((i + t0 + 1) * nh, last_h), 0)),
                  pl.BlockSpec((tm, D_MODEL), lambda i, j: (i + t0, 0)),
                  _mod_spec(l, tm, n_ctx, s_lat, t0=t0, rank=2),
                  pl.BlockSpec((None, D_MODEL, tf), lambda i, j: (l, 0, j)),
                  pl.BlockSpec((None, D_MODEL, tf), lambda i, j: (l, 0, j)),
                  pl.BlockSpec((3, tf), lambda i, j: (0, j)),
                  pl.BlockSpec((3, tf), lambda i, j: (0, j)),
                  pl.BlockSpec((None, tf, D_MODEL), lambda i, j: (l, j, 0)),
                  pl.BlockSpec((1, D_MODEL), lambda i, j: (0, 0)),
                  pl.BlockSpec((1, D_MODEL), lambda i, j: (0, 0))],
        out_specs=pl.BlockSpec((tm, D_MODEL), lambda i, j: (i, 0)),
        out_shape=jax.ShapeDtypeStruct((n_rows, D_MODEL), F32),
        scratch_shapes=[pltpu.VMEM((tm + 2 * HALO, D_MODEL), BF16),
                        pltpu.VMEM((tm + 2 * HALO, tf), F32),
                        pltpu.VMEM((tm + 2 * HALO, tf), F32),
                        pltpu.VMEM((tm, D_MODEL), F32)],
        compiler_params=pltpu.CompilerParams(dimension_semantics=("arbitrary", "arbitrary"),
                                             vmem_limit_bytes=VMEM_LIMIT),
        name="conv_ffn",
    )(h2, h2, h2, x1, mod, wa, wb, cwa, cwb, wd, g.reshape(1, D_MODEL), b.reshape(1, D_MODEL))


def _prep_up_kernel(a_ref, b_ref, wa_ref, wb_ref):
    zeros = jnp.zeros((a_ref.shape[0], D_FF_PAD - D_FF), BF16)
    wa_ref[:, :D_FF] = a_ref[...].astype(BF16)
    wa_ref[:, D_FF:] = zeros
    wb_ref[:, :D_FF] = b_ref[...].astype(BF16)
    wb_ref[:, D_FF:] = zeros


def _prep_down_kernel(w_ref, o_ref, *, n_real):
    keep = pl.program_id(1) < n_real
    o_ref[...] = jnp.where(keep, w_ref[...], 0.0).astype(BF16)


def _prep_in_kernel(w_ref, w1_ref, a1_ref, o_ref):
    n = w_ref.shape[-1]
    o_ref[:, :C_Q] = w_ref[:, :C_Q].astype(BF16)
    for g in range(GQA_GROUP):
        for h in range(ATTN_KV_HEADS):
            src = C_Q + (h * GQA_GROUP + g) * HEAD_DIM
            dst = C_Q + (g * ATTN_KV_HEADS + h) * HEAD_DIM
            o_ref[:, dst:dst + HEAD_DIM] = w_ref[:, src:src + HEAD_DIM].astype(BF16)
    o_ref[:, C_AK:n] = w_ref[:, C_AK:n].astype(BF16)
    o_ref[:, n:] = jnp.concatenate([w1_ref[0], w1_ref[1], a1_ref[0], a1_ref[1]], axis=1).astype(BF16)


def _prep_out_kernel(w_ref, o_ref):
    r0 = RWKV_W + HYENA_W
    o_ref[:r0, :] = w_ref[:r0, :].astype(BF16)
    for g in range(GQA_GROUP):
        for h in range(ATTN_KV_HEADS):
            src = r0 + (h * GQA_GROUP + g) * HEAD_DIM
            dst = r0 + (g * ATTN_KV_HEADS + h) * HEAD_DIM
            o_ref[dst:dst + HEAD_DIM, :] = w_ref[src:src + HEAD_DIM, :].astype(BF16)


def _prep_weights(w_in, rwkv_w1, rwkv_a1, w_out, ffn_w_up, ffn_w_down):
    L, D = w_in.shape[0], D_MODEL
    cp = lambda n: pltpu.CompilerParams(dimension_semantics=("arbitrary",) * n, vmem_limit_bytes=VMEM_LIMIT)
    tr = 128
    half = jax.ShapeDtypeStruct((L, D, D_FF_PAD), BF16)
    wa, wb = pl.pallas_call(
        _prep_up_kernel,
        grid=(L, D // tr),
        in_specs=[pl.BlockSpec((None, tr, D_FF), lambda l, i: (l, i, 0)),
                  pl.BlockSpec((None, tr, D_FF), lambda l, i: (l, i, 1))],
        out_specs=[pl.BlockSpec((None, tr, D_FF_PAD), lambda l, i: (l, i, 0))] * 2,
        out_shape=[half, half], compiler_params=cp(2), name="prep_ffn_up",
    )(ffn_w_up, ffn_w_up)
    nr = D_FF // LANES
    wd = pl.pallas_call(
        functools.partial(_prep_down_kernel, n_real=nr),
        grid=(L, D_FF_PAD // LANES),
        in_specs=[pl.BlockSpec((None, LANES, D), lambda l, j: (l, jnp.minimum(j, nr - 1), 0))],
        out_specs=pl.BlockSpec((None, LANES, D), lambda l, j: (l, j, 0)),
        out_shape=jax.ShapeDtypeStruct((L, D_FF_PAD, D), BF16), compiler_params=cp(2), name="prep_ffn_down",
    )(ffn_w_down)
    tr2 = 256
    n_in = w_in.shape[-1]
    w_ext = pl.pallas_call(
        _prep_in_kernel,
        grid=(L, D // tr2),
        in_specs=[pl.BlockSpec((None, tr2, n_in), lambda l, i: (l, i, 0)),
                  pl.BlockSpec((None, 2, tr2, LORA), lambda l, i: (l, 0, i, 0)),
                  pl.BlockSpec((None, 2, tr2, LORA), lambda l, i: (l, 0, i, 0))],
        out_specs=pl.BlockSpec((None, tr2, IN_W_EXT), lambda l, i: (l, i, 0)),
        out_shape=jax.ShapeDtypeStruct((L, D, IN_W_EXT), BF16), compiler_params=cp(2), name="prep_in",
    )(w_in, rwkv_w1, rwkv_a1)
    wo = pl.pallas_call(
        _prep_out_kernel,
        grid=(L,),
        in_specs=[pl.BlockSpec((None, D, D), lambda l: (l, 0, 0))],
        out_specs=pl.BlockSpec((None, D, D), lambda l: (l, 0, 0)),
        out_shape=jax.ShapeDtypeStruct((L, D, D), BF16), compiler_params=cp(1), name="prep_out",
    )(w_out)
    return w_ext, wo, wa, wb, wd


def kernel(x_prompt, x_sample, c, state_rwkv, cache_k, cache_v, c_ctx, w_mod, b_mod, w_in, rwkv_w0, rwkv_w1, rwkv_w2, rwkv_a0, rwkv_a1, rwkv_a2, rwkv_k_k, rwkv_k_a, rwkv_r_k, rwkv_lnx_g, rwkv_lnx_b, hy_short_w, hy_f_w1, hy_f_b1, hy_f_freq1, hy_f_w2, hy_f_b2, hy_f_freq2, hy_f_w3, hy_bias, attn_sink, w_out, ln1_g, ln1_b, ffn_w_up, ffn_conv_w, ffn_w_down, ln2_g, ln2_b):
    params = dict(rwkv_w0=rwkv_w0, rwkv_w2=rwkv_w2, rwkv_a0=rwkv_a0, rwkv_a2=rwkv_a2,
                  rwkv_k_k=rwkv_k_k, rwkv_k_a=rwkv_k_a, rwkv_r_k=rwkv_r_k,
                  rwkv_lnx_g=rwkv_lnx_g, rwkv_lnx_b=rwkv_lnx_b,
                  hy_short_w=hy_short_w, hy_f_w1=hy_f_w1, hy_f_b1=hy_f_b1, hy_f_freq1=hy_f_freq1,
                  hy_f_w2=hy_f_w2, hy_f_b2=hy_f_b2, hy_f_freq2=hy_f_freq2, hy_f_w3=hy_f_w3,
                  hy_bias=hy_bias)
    L = w_in.shape[0]
    Bc, Sc, D = x_prompt.shape
    Bl, Tl, _ = x_sample.shape
    P = cache_k.shape[2]
    n_ctx = Bc * Sc
    n_lat = Bl * Tl
    N = n_ctx + n_lat
    assert Sc % MOD_BLK == 0 and Tl % MOD_BLK == 0 and n_ctx % Tl == 0

    n_cond = 1 + Bl
    cond = jnp.zeros((16, D), F32).at[0].set(c_ctx).at[1:n_cond].set(c)
    mod_all = _modulation(cond, w_mod, b_mod)
    mod_tab = mod_all.reshape(L, 16, 6, D)

    cm_c, sm_c = (jnp.asarray(m, F32) for m in _dft_mats(Sc))
    cm_l, sm_l = (jnp.asarray(m, F32) for m in _dft_mats(Tl))
    sink_pad = jnp.zeros((L, 1, LANES), F32).at[:, 0, :ATTN_HEADS].set(attn_sink)

    w_ext, wo, wa, wb, wd = _prep_weights(w_in, rwkv_w1, rwkv_a1, w_out, ffn_w_up, ffn_w_down)
    x_src = (x_prompt.reshape(n_ctx, D), x_sample.reshape(n_lat, D), n_ctx, n_lat, 0, Tl)
    states, keys, vals = [], [], []
    for l in range(L):
        p = {n: a[l] for n, a in params.items()}
        proj = _in_projection(x_src, mod_tab, w_ext, l)
        keys.append(proj[:n_ctx, C_AK:C_AK + KV_W].reshape(Bc, Sc, ATTN_KV_HEADS, HEAD_DIM))
        vals.append(proj[:n_ctx, C_AV:C_AV + KV_W].reshape(Bc, Sc, ATTN_KV_HEADS, HEAD_DIM))

        ya_c, s_fin = _rwkv(proj, p, None, T=Sc, n_seq=Bc, row_blk0=0, nsq=RWKV_CTX_SEQS)
        ya_l, _ = _rwkv(proj, p, state_rwkv[:, l], T=Tl, n_seq=Bl, row_blk0=n_ctx // Tl)
        states.append(s_fin)

        yb_c = _hyena(proj, p, _hyena_filter(p, Sc, cm_c, sm_c), cm_c, sm_c, L=Sc, n_seq=Bc, row_blk0=0,
                      nsq=HYENA_CTX_SEQS)
        yb_l = _hyena(proj, p, _hyena_filter(p, Tl, cm_l, sm_l), cm_l, sm_l, L=Tl, n_seq=Bl,
                      row_blk0=n_ctx // Tl)

        yc_c = _ctx_attention(proj, sink_pad[l], S=Sc, n_seq=Bc)
        yc_l = _lat_attention(proj, cache_k[:, l].reshape(Bl, P, KV_W), cache_v[:, l].reshape(Bl, P, KV_W),
                              sink_pad[l], T=Tl, n_seq=Bl, row_blk0=n_ctx // Tl)

        x1, h2 = _out_projection((ya_c, yb_c, yc_c), (ya_l, yb_l, yc_l), x_src, mod_tab,
                                 wo, l, ln1_g[l], ln1_b[l])
        ffn = functools.partial(_ffn, h2, x1, mod_tab, wa, wb, ffn_conv_w[l], wd, l, ln2_g[l], ln2_b[l],
                                n_ctx=n_ctx, s_ctx=Sc, s_lat=Tl)
        if l < L - 1:
            x = ffn()
            x_src = (x, x, n_ctx, n_lat, n_ctx, Tl)
        else:
            y_prompt = ffn(row0=0, n_rows=n_ctx).reshape(Bc, Sc, D)
            y_sample = ffn(row0=n_ctx, n_rows=n_lat).reshape(Bl, Tl, D)

    new_state = jnp.stack(states, 1)
    new_k = jnp.stack(keys, 1)
    new_v = jnp.stack(vals, 1)
    return (y_prompt, y_sample, new_state, new_k, new_v)
```
